```python
import math
import jax, jax.numpy as jnp
from jax import lax
import numpy as np

D_MODEL = 1024
BATCH = 16
SEQ = 4096
DEPTH = 1

ATTN_HEADS = D_MODEL // 256
ATTN_HEAD_DIM = 64
ATTN_V_DIM = 2 * ATTN_HEAD_DIM
ATTN_QK_WIDTH = ATTN_HEADS * 2 * ATTN_HEAD_DIM
ATTN_V_WIDTH = ATTN_HEADS * ATTN_V_DIM
Q_BLOCK = 128
NEG_INF = -1e30
REL_BUCKETS = 32
REL_MAX_DISTANCE = 128
SSM_GROUP_CH = 16
SSM_WIDTH = D_MODEL // 2
SSM_GROUPS = SSM_WIDTH // SSM_GROUP_CH
SSM_STATE = 64
SSM_DT_MIN = 1e-3
SSM_DT_MAX = 1e-1
SSM_EIG_MAX_RE = -1e-4
N_BRANCHES = 2
SPLIT_Q = ATTN_QK_WIDTH
SPLIT_K = SPLIT_Q + ATTN_QK_WIDTH
SPLIT_V = SPLIT_K + ATTN_V_WIDTH
SPLIT_U = SPLIT_V + SSM_WIDTH
IN_WIDTH = SPLIT_U + N_BRANCHES * D_MODEL
MOE_GROUPS = 4
MOE_EXPERTS_PER_GROUP = 8
MOE_EXPERTS = MOE_GROUPS * MOE_EXPERTS_PER_GROUP
MOE_TOP_K = 2
MOE_D_FF = D_MODEL // 2
MOE_BLOCK = 256
NORM_EPS = 1e-6
SUBLN_EPS = 1e-5

kernel_name = 'hybrid_diffattn_s5_hiermoe_block'


def rms_norm(x, g, eps=NORM_EPS):
    xf = x.astype(jnp.float32)
    y = xf * lax.rsqrt(jnp.mean(xf * xf, axis=-1, keepdims=True) + eps)
    return (y * g.astype(jnp.float32)).astype(x.dtype)


def modulate(h, shift, scale):
    return h * (1 + scale[:, None, :]) + shift[:, None, :]


def rel_bucket(dist):
    max_exact = REL_BUCKETS // 2
    n = jnp.maximum(dist, 0)
    log_ratio = jnp.log(jnp.maximum(n, 1).astype(jnp.float32) / max_exact) / math.log(REL_MAX_DISTANCE / max_exact)
    large = max_exact + (log_ratio * (REL_BUCKETS - max_exact)).astype(jnp.int32)
    large = jnp.minimum(large, REL_BUCKETS - 1)
    return jnp.where(n < max_exact, n, large)


def diff_attention(q, k, v, rel_bias, lq1, lk1, lq2, lk2, subln_g, lambda_init):
    B_, S_ = q.shape[0], q.shape[1]
    f32 = jnp.float32
    lam = (jnp.exp(jnp.sum(lq1.astype(f32) * lk1.astype(f32)))
           - jnp.exp(jnp.sum(lq2.astype(f32) * lk2.astype(f32))) + lambda_init)
    qT = jnp.transpose(q, (0, 2, 3, 1, 4)) * (ATTN_HEAD_DIM ** -0.5)
    kT = jnp.transpose(k, (0, 2, 3, 1, 4))
    vT = jnp.transpose(v, (0, 2, 1, 3))
    outs = []
    for blk in range(S_ // Q_BLOCK):
        q0 = blk * Q_BLOCK
        q1 = q0 + Q_BLOCK
        dist = jnp.arange(q0, q1)[:, None] - jnp.arange(q1)[None, :]
        bias = jnp.transpose(rel_bias[rel_bucket(dist)], (2, 0, 1)).astype(f32)
        s = jnp.einsum('bhmqd,bhmkd->bhmqk', qT[:, :, :, q0:q1], kT[:, :, :, :q1]).astype(f32)
        s = jnp.where(dist >= 0, s + bias[None, :, None], NEG_INF)
        p = jax.nn.softmax(s, axis=-1)
        w = p[:, :, 0] - lam * p[:, :, 1]
        outs.append(jnp.einsum('bhqk,bhkd->bhqd', w.astype(vT.dtype), vT[:, :, :q1]))
    o = jnp.concatenate(outs, axis=2)
    o = rms_norm(o, subln_g, SUBLN_EPS) * (1 - lambda_init)
    return jnp.transpose(o, (0, 2, 1, 3)).reshape(B_, S_, ATTN_V_WIDTH)


def _complex_linear_combine(e1, e2):
    a1r, a1i, b1r, b1i = e1
    a2r, a2i, b2r, b2i = e2
    return (a2r * a1r - a2i * a1i,
            a2r * a1i + a2i * a1r,
            a2r * b1r - a2i * b1i + b2r,
            a2r * b1i + a2i * b1r + b2i)


def s5_branch(u, lam_re, lam_im, log_step, b_re, b_im, c_re, c_im, d_skip, w_glu):
    B_, S_ = u.shape[0], u.shape[1]
    f32 = jnp.float32
    ug = u.astype(f32).reshape(B_, S_, SSM_GROUPS, SSM_GROUP_CH)
    lr = jnp.minimum(lam_re.astype(f32), SSM_EIG_MAX_RE)
    li = lam_im.astype(f32)
    step = jnp.exp(log_step.astype(f32))[:, None]
    mag = jnp.exp(lr * step)
    ang = li * step
    a_re = mag * jnp.cos(ang)
    a_im = mag * jnp.sin(ang)
    den = lr * lr + li * li
    num_re = a_re - 1.0
    coef_re = (num_re * lr + a_im * li) / den
    coef_im = (a_im * lr - num_re * li) / den
    br = b_re.astype(f32)
    bi = b_im.astype(f32)
    bb_re = coef_re[..., None] * br - coef_im[..., None] * bi
    bb_im = coef_re[..., None] * bi + coef_im[..., None] * br
    bu_re = jnp.einsum('bsgh,gph->bsgp', ug, bb_re)
    bu_im = jnp.einsum('bsgh,gph->bsgp', ug, bb_im)
    a_re_s = jnp.broadcast_to(a_re, (S_,) + a_re.shape)
    a_im_s = jnp.broadcast_to(a_im, (S_,) + a_im.shape)

    def scan_one(bre, bim):
        _, _, xr, xi = lax.associative_scan(_complex_linear_combine, (a_re_s, a_im_s, bre, bim), axis=0)
        return xr, xi

    xr, xi = jax.vmap(scan_one)(bu_re, bu_im)
    y = (jnp.einsum('bsgp,ghp->bsgh', xr, c_re.astype(f32))
         - jnp.einsum('bsgp,ghp->bsgh', xi, c_im.astype(f32))
         + d_skip.astype(f32) * ug)
    y = jax.nn.gelu(y.reshape(B_, S_, SSM_WIDTH)).astype(u.dtype)
    gl = y @ w_glu
    return gl[..., :SSM_WIDTH] * jax.nn.sigmoid(gl[..., SSM_WIDTH:])


def hier_moe(h, w_rg, b_rg, w_re, b_re, w_e_in, w_e_out):
    B_, S_, D = h.shape
    T = B_ * S_
    xt = h.reshape(T, D)
    lg = (xt @ w_rg + b_rg).astype(jnp.float32)
    pg = jax.nn.softmax(lg, axis=-1)
    grp = jnp.argmax(lg, axis=-1)
    p_grp = jnp.take_along_axis(pg, grp[:, None], axis=-1)
    le_all = (jnp.einsum('td,gde->tge', xt, w_re) + b_re).astype(jnp.float32)
    le = jnp.take_along_axis(le_all, grp[:, None, None], axis=1)[:, 0]
    top_v, top_i = lax.top_k(le, MOE_TOP_K)
    comb = p_grp * jax.nn.softmax(top_v, axis=-1)
    eid = grp[:, None] * MOE_EXPERTS_PER_GROUP + top_i
    n_assign = T * MOE_TOP_K
    n_pad = (-(-n_assign // MOE_BLOCK) + MOE_EXPERTS) * MOE_BLOCK
    e_flat = eid.reshape(n_assign).astype(jnp.int32)
    w_flat = comb.reshape(n_assign)
    tok_flat = jnp.repeat(jnp.arange(T, dtype=jnp.int32), MOE_TOP_K)
    order = jnp.argsort(e_flat)
    e_sorted = e_flat[order]
    counts = jax.ops.segment_sum(jnp.ones_like(e_flat), e_flat, num_segments=MOE_EXPERTS)
    starts = jnp.cumsum(counts) - counts
    padded = ((counts + MOE_BLOCK - 1) // MOE_BLOCK) * MOE_BLOCK
    pends = jnp.cumsum(padded)
    pstarts = pends - padded
    dest = pstarts[e_sorted] + (jnp.arange(n_assign, dtype=jnp.int32) - starts[e_sorted])
    row_tok = jnp.full((n_pad,), T, jnp.int32).at[dest].set(tok_flat[order])
    row_w = jnp.zeros((n_pad,), jnp.float32).at[dest].set(w_flat[order])
    n_blocks = n_pad // MOE_BLOCK
    block_start = jnp.arange(n_blocks, dtype=jnp.int32) * MOE_BLOCK
    block_e = jnp.minimum(jnp.searchsorted(pends, block_start, side='right'), MOE_EXPERTS - 1)
    x_pad = jnp.concatenate([xt, jnp.zeros((1, D), xt.dtype)], axis=0)
    xs = x_pad[row_tok].reshape(n_blocks, MOE_BLOCK, D)

    def expert_block(args):
        xb, e = args
        hid = xb @ w_e_in[e]
        a, g = jnp.split(hid, 2, axis=-1)
        return (jax.nn.silu(a) * g) @ w_e_out[e]

    ys = lax.map(expert_block, (xs, block_e)).reshape(n_pad, D)
    out = jnp.zeros((T + 1, D), h.dtype).at[row_tok].add(ys * row_w[:, None].astype(ys.dtype))
    return out[:T].reshape(B_, S_, D)


def setup_inputs(seed: int = 0) -> dict:
    key = jax.random.key(seed)
    ks = jax.random.split(key, 32)
    f32 = jnp.float32

    def nrm(k, shape, scale):
        return jax.random.normal(k, shape, f32) * scale

    G, P, H = SSM_GROUPS, SSM_STATE, SSM_GROUP_CH
    n = jnp.arange(SSM_STATE, dtype=f32)
    return {
        'x': nrm(ks[0], (BATCH, SEQ, D_MODEL), 1.0),
        'c': nrm(ks[1], (BATCH, D_MODEL), 1.0),
        'w_ada': nrm(ks[2], (DEPTH, D_MODEL, 6 * D_MODEL), 0.5 * D_MODEL ** -0.5),
        'b_ada': nrm(ks[3], (DEPTH, 6 * D_MODEL), 0.01),
        'norm1_g': 1.0 + nrm(ks[4], (DEPTH, D_MODEL), 0.01),
        'w_in': nrm(ks[5], (DEPTH, D_MODEL, IN_WIDTH), D_MODEL ** -0.5),
        'rel_bias': nrm(ks[6], (REL_BUCKETS, ATTN_HEADS), 0.5),
        'lambda_q1': nrm(ks[7], (DEPTH, ATTN_HEAD_DIM), 0.1),
        'lambda_k1': nrm(ks[8], (DEPTH, ATTN_HEAD_DIM), 0.1),
        'lambda_q2': nrm(ks[9], (DEPTH, ATTN_HEAD_DIM), 0.1),
        'lambda_k2': nrm(ks[10], (DEPTH, ATTN_HEAD_DIM), 0.1),
        'subln_g': 1.0 + nrm(ks[11], (DEPTH, ATTN_V_DIM), 0.01),
        'ssm_lambda_re': -0.5 + nrm(ks[12], (DEPTH, G, P), 0.01),
        'ssm_lambda_im': math.pi * n + nrm(ks[13], (DEPTH, G, P), 0.01),
        'ssm_log_step': jax.random.uniform(ks[14], (DEPTH, G), f32, math.log(SSM_DT_MIN), math.log(SSM_DT_MAX)),
        'ssm_b_re': nrm(ks[15], (DEPTH, G, P, H), (2 * H) ** -0.5),
        'ssm_b_im': nrm(ks[16], (DEPTH, G, P, H), (2 * H) ** -0.5),
        'ssm_c_re': nrm(ks[17], (DEPTH, G, H, P), P ** -0.5),
        'ssm_c_im': nrm(ks[18], (DEPTH, G, H, P), P ** -0.5),
        'ssm_d': nrm(ks[19], (DEPTH, G, H), 1.0),
        'w_glu': nrm(ks[20], (DEPTH, SSM_WIDTH, 2 * SSM_WIDTH), SSM_WIDTH ** -0.5),
        'w_proj_attn': nrm(ks[21], (DEPTH, ATTN_V_WIDTH, D_MODEL), ATTN_V_WIDTH ** -0.5),
        'w_proj_ssm': nrm(ks[22], (DEPTH, SSM_WIDTH, D_MODEL), SSM_WIDTH ** -0.5),
        'w_out': nrm(ks[23], (DEPTH, D_MODEL, D_MODEL), D_MODEL ** -0.5),
        'norm2_g': 1.0 + nrm(ks[24], (DEPTH, D_MODEL), 0.01),
        'w_router_group': nrm(ks[25], (DEPTH, D_MODEL, MOE_GROUPS), D_MODEL ** -0.5),
        'b_router_group': nrm(ks[26], (DEPTH, MOE_GROUPS), 0.01),
        'w_router_expert': nrm(ks[27], (DEPTH, MOE_GROUPS, D_MODEL, MOE_EXPERTS_PER_GROUP), D_MODEL ** -0.5),
        'b_router_expert': nrm(ks[28], (DEPTH, MOE_GROUPS, MOE_EXPERTS_PER_GROUP), 0.01),
        'w_expert_in': nrm(ks[29], (DEPTH, MOE_EXPERTS, D_MODEL, 2 * MOE_D_FF), D_MODEL ** -0.5),
        'w_expert_out': nrm(ks[30], (DEPTH, MOE_EXPERTS, MOE_D_FF, D_MODEL), MOE_D_FF ** -0.5),
        'final_g': 1.0 + nrm(ks[31], (D_MODEL,), 0.01),
    }


def reference(x, c, w_ada, b_ada, norm1_g, w_in, rel_bias, lambda_q1, lambda_k1, lambda_q2, lambda_k2,
              subln_g, ssm_lambda_re, ssm_lambda_im, ssm_log_step, ssm_b_re, ssm_b_im, ssm_c_re, ssm_c_im,
              ssm_d, w_glu, w_proj_attn, w_proj_ssm, w_out, norm2_g, w_router_group, b_router_group,
              w_router_expert, b_router_expert, w_expert_in, w_expert_out, final_g):
    B_, S_ = x.shape[0], x.shape[1]
    c_act = jax.nn.silu(c)
    for layer in range(DEPTH):
        lambda_init = 0.8 - 0.6 * math.exp(-0.3 * layer)
        mod = c_act @ w_ada[layer] + b_ada[layer]
        shift1, scale1, gate1, shift2, scale2, gate2 = jnp.split(mod, 6, axis=-1)
        h = modulate(rms_norm(x, norm1_g[layer]), shift1, scale1)
        proj = h @ w_in[layer]
        q, k, v, u, gates = jnp.split(proj, [SPLIT_Q, SPLIT_K, SPLIT_V, SPLIT_U], axis=-1)
        y_attn = diff_attention(
            q.reshape(B_, S_, ATTN_HEADS, 2, ATTN_HEAD_DIM),
            k.reshape(B_, S_, ATTN_HEADS, 2, ATTN_HEAD_DIM),
            v.reshape(B_, S_, ATTN_HEADS, ATTN_V_DIM),
            rel_bias, lambda_q1[layer], lambda_k1[layer], lambda_q2[layer], lambda_k2[layer],
            subln_g[layer], lambda_init)
        y_ssm = s5_branch(u, ssm_lambda_re[layer], ssm_lambda_im[layer], ssm_log_step[layer],
                          ssm_b_re[layer], ssm_b_im[layer], ssm_c_re[layer], ssm_c_im[layer],
                          ssm_d[layer], w_glu[layer])
        g_attn, g_ssm = jnp.split(jax.nn.sigmoid(gates), 2, axis=-1)
        merged = g_attn * (y_attn @ w_proj_attn[layer]) + g_ssm * (y_ssm @ w_proj_ssm[layer])
        x = x + gate1[:, None, :] * (merged @ w_out[layer])
        h = modulate(rms_norm(x, norm2_g[layer]), shift2, scale2)
        x = x + gate2[:, None, :] * hier_moe(h, w_router_group[layer], b_router_group[layer],
                                             w_router_expert[layer], b_router_expert[layer],
                                             w_expert_in[layer], w_expert_out[layer])
    return rms_norm(x, final_g)
```

```python
import functools
import math

import jax
import jax.numpy as jnp
from jax import lax
from jax.experimental import pallas as pl
from jax.experimental.pallas import tpu as pltpu

F32 = jnp.float32
BF16 = jnp.bfloat16
I32 = jnp.int32
HIGHEST = lax.Precision.HIGHEST

ATTN_HEADS = 4
ATTN_HEAD_DIM = 64
ATTN_V_DIM = 2 * ATTN_HEAD_DIM
ATTN_WIDTH = ATTN_HEADS * ATTN_V_DIM
NEG_INF = -1e30
REL_BUCKETS = 32
REL_MAX_DISTANCE = 128
SSM_GROUP_CH = 16
SSM_WIDTH = 512
SSM_GROUPS = SSM_WIDTH // SSM_GROUP_CH
SSM_STATE = 64
SSM_EIG_MAX_RE = -1e-4
SSM_CHUNK = 16
MOE_GROUPS = 4
MOE_EXPERTS_PER_GROUP = 8
MOE_EXPERTS = MOE_GROUPS * MOE_EXPERTS_PER_GROUP
MOE_BLOCK = 256
NORM_EPS = 1e-6
SUBLN_EPS = 1e-5
LAMBDA_INIT = 0.8 - 0.6 * math.exp(-0.3 * 0)

ATTN_BLOCK = 256
TOKEN_TILE = 512
ROUTER_ROWS = 40
VMEM_LIMIT = 56 << 20


def _params(*sem):
    return pltpu.CompilerParams(dimension_semantics=sem, vmem_limit_bytes=VMEM_LIMIT)


def _rms(x, eps):
    return x * lax.rsqrt(jnp.mean(x * x, axis=-1, keepdims=True) + eps)


def _mod_kernel(c_ref, w_ref, b_ref, o_ref):
    c = c_ref[...]
    c_act = c * jax.nn.sigmoid(c)
    o_ref[...] = jnp.dot(c_act, w_ref[...], preferred_element_type=F32, precision=HIGHEST) + b_ref[...]


def _ada_mod(c, w_ada, b_ada):
    B, D = c.shape
    N = w_ada.shape[1]
    tn = 1024
    return pl.pallas_call(
        _mod_kernel,
        grid=(N // tn,),
        in_specs=[pl.BlockSpec((B, D), lambda j: (0, 0)),
                  pl.BlockSpec((D, tn), lambda j: (0, j)),
                  pl.BlockSpec((1, tn), lambda j: (0, j))],
        out_specs=pl.BlockSpec((B, tn), lambda j: (0, j)),
        out_shape=jax.ShapeDtypeStruct((B, N), F32),
        compiler_params=_params("arbitrary"),
        name="ada_mod",
    )(c, w_ada, b_ada.reshape(1, N))


def _proj_kernel(x_ref, mod_ref, g_ref, w_ref, q_ref, k_ref, v_ref, u_ref, gs_ref):
    y = _rms(x_ref[...], NORM_EPS) * g_ref[...]
    h = (y * (1.0 + mod_ref[1:2, :]) + mod_ref[0:1, :]).astype(BF16)
    W = ATTN_WIDTH

    def proj(lo, hi):
        return jnp.dot(h, w_ref[:, lo:hi], preferred_element_type=F32)

    q_ref[...] = (proj(0, W) * (ATTN_HEAD_DIM ** -0.5)).astype(BF16)
    k_ref[...] = proj(W, 2 * W).astype(BF16)
    v_ref[...] = proj(2 * W, 3 * W).astype(BF16)
    u_ref[...] = proj(3 * W, 3 * W + SSM_WIDTH).astype(BF16)
    gs_ref[...] = jax.nn.sigmoid(proj(3 * W + SSM_WIDTH, w_ref.shape[1])).astype(BF16)


def _in_proj(x2, mod, norm_g, w_in, seq):
    T, D = x2.shape
    tm = TOKEN_TILE
    per_b = seq // tm
    n_gate = w_in.shape[1] - 3 * ATTN_WIDTH - SSM_WIDTH
    row = lambda i: (i, 0)
    return pl.pallas_call(
        _proj_kernel,
        grid=(T // tm,),
        in_specs=[pl.BlockSpec((tm, D), row),
                  pl.BlockSpec((None, 6, D), lambda i: (i // per_b, 0, 0)),
                  pl.BlockSpec((1, D), lambda i: (0, 0)),
                  pl.BlockSpec(w_in.shape, lambda i: (0, 0))],
        out_specs=[pl.BlockSpec((tm, ATTN_WIDTH), row)] * 3
        + [pl.BlockSpec((tm, SSM_WIDTH), row), pl.BlockSpec((tm, n_gate), row)],
        out_shape=[jax.ShapeDtypeStruct((T, ATTN_WIDTH), BF16)] * 3
        + [jax.ShapeDtypeStruct((T, SSM_WIDTH), BF16), jax.ShapeDtypeStruct((T, n_gate), BF16)],
        compiler_params=_params("parallel"),
        name="in_proj",
    )(x2, mod, norm_g.reshape(1, D), w_in.astype(BF16))


def _rel_bucket(dist):
    max_exact = REL_BUCKETS // 2
    n = jnp.maximum(dist, 0)
    log_ratio = jnp.log(jnp.maximum(n, 1).astype(F32) / max_exact) / math.log(REL_MAX_DISTANCE / max_exact)
    large = max_exact + (log_ratio * (REL_BUCKETS - max_exact)).astype(I32)
    large = jnp.minimum(large, REL_BUCKETS - 1)
    return jnp.where(n < max_exact, n, large)


def _attn_bias_tiles(rel_bias, blk):
    assert blk >= REL_MAX_DISTANCE
    r = jnp.arange(blk)[:, None]
    c = jnp.arange(blk)[None, :]
    far = rel_bias[REL_BUCKETS - 1].astype(F32)
    tiles = []
    for kind in range(2):
        dist = kind * blk + r - c
        b = jnp.transpose(rel_bias[_rel_bucket(dist)], (2, 0, 1)).astype(F32) - far[:, None, None]
        tiles.append(jnp.where(dist >= 0, b, NEG_INF))
    tiles.append(jnp.zeros_like(tiles[0]))
    return jnp.stack(tiles, axis=1)


def _attn_kernel(lam_ref, q_ref, k_ref, v_ref, bias_ref, g_ref, o_ref, m_scr, l_scr, acc_scr, *, blk):
    i = pl.program_id(2)
    q = q_ref[...]
    lane = lax.broadcasted_iota(I32, q.shape, 1)
    zero = jnp.zeros_like(q)
    q_maps = (jnp.where(lane < ATTN_HEAD_DIM, q, zero), jnp.where(lane >= ATTN_HEAD_DIM, q, zero))
    m_scr[...] = jnp.full(m_scr.shape, -jnp.inf, F32)
    l_scr[...] = jnp.zeros(l_scr.shape, F32)
    acc_scr[...] = jnp.zeros(acc_scr.shape, F32)

    def body(j, carry):
        off = pl.multiple_of(j * blk, blk)
        kj = k_ref[pl.ds(off, blk), :]
        vj = v_ref[pl.ds(off, blk), :]
        bias = bias_ref[jnp.minimum(i - j, 2)]
        for mi in range(2):
            s = lax.dot_general(q_maps[mi], kj, (((1,), (1,)), ((), ())), preferred_element_type=F32) + bias
            m_old = m_scr[mi]
            m_new = jnp.maximum(m_old, jnp.max(s, axis=-1, keepdims=True))
            alpha = jnp.exp(m_old - m_new)
            p = jnp.exp(s - m_new)
            l_scr[mi] = alpha * l_scr[mi] + jnp.sum(p, axis=-1, keepdims=True)
            acc_scr[mi] = alpha * acc_scr[mi] + jnp.dot(p.astype(BF16), vj, preferred_element_type=F32)
            m_scr[mi] = m_new
        return carry

    lax.fori_loop(0, i + 1, body, 0)

    lam = (jnp.exp(jnp.sum(lam_ref[0:1, :] * lam_ref[1:2, :], axis=-1, keepdims=True))
           - jnp.exp(jnp.sum(lam_ref[2:3, :] * lam_ref[3:4, :], axis=-1, keepdims=True)) + LAMBDA_INIT)
    o = acc_scr[0] / l_scr[0] - lam * (acc_scr[1] / l_scr[1])
    o_ref[...] = (_rms(o, SUBLN_EPS) * g_ref[...] * (1.0 - LAMBDA_INIT)).astype(BF16)


def _diff_attn(q, k, v, rel_bias, lam_vecs, subln_g, batch, seq):
    T = q.shape[0]
    blk = ATTN_BLOCK
    nq = seq // blk
    bias = _attn_bias_tiles(rel_bias, blk)
    return pl.pallas_call(
        functools.partial(_attn_kernel, blk=blk),
        grid=(batch, ATTN_HEADS, nq),
        in_specs=[pl.BlockSpec((4, ATTN_HEAD_DIM), lambda b, h, i: (0, 0)),
                  pl.BlockSpec((blk, ATTN_V_DIM), lambda b, h, i: (b * nq + i, h)),
                  pl.BlockSpec((seq, ATTN_V_DIM), lambda b, h, i: (b, h)),
                  pl.BlockSpec((seq, ATTN_V_DIM), lambda b, h, i: (b, h)),
                  pl.BlockSpec((None, 3, blk, blk), lambda b, h, i: (h, 0, 0, 0)),
                  pl.BlockSpec((1, ATTN_V_DIM), lambda b, h, i: (0, 0))],
        out_specs=pl.BlockSpec((blk, ATTN_V_DIM), lambda b, h, i: (b * nq + i, h)),
        out_shape=jax.ShapeDtypeStruct((T, ATTN_WIDTH), BF16),
        scratch_shapes=[pltpu.VMEM((2, blk, 1), F32), pltpu.VMEM((2, blk, 1), F32),
                        pltpu.VMEM((2, blk, ATTN_V_DIM), F32)],
        compiler_params=_params("parallel", "parallel", "arbitrary"),
        name="diff_attn",
    )(lam_vecs, q, k, v, bias, subln_g.reshape(1, ATTN_V_DIM))


def _s5_tables(lam_re, lam_im, log_step, b_re, b_im, c_re, c_im, d_skip):
    L = SSM_CHUNK
    G, P = lam_re.shape
    H = SSM_GROUP_CH
    lr = jnp.minimum(lam_re.astype(F32), SSM_EIG_MAX_RE)
    li = lam_im.astype(F32)
    step = jnp.exp(log_step.astype(F32))[:, None]
    mag = jnp.exp(lr * step)
    ang = li * step
    a_re = mag * jnp.cos(ang)
    a_im = mag * jnp.sin(ang)
    den = lr * lr + li * li
    num_re = a_re - 1.0
    coef_re = (num_re * lr + a_im * li) / den
    coef_im = (a_im * lr - num_re * li) / den
    br = b_re.astype(F32)
    bi = b_im.astype(F32)
    bb_re = coef_re[..., None] * br - coef_im[..., None] * bi
    bb_im = coef_re[..., None] * bi + coef_im[..., None] * br
    pw_re, pw_im = [jnp.ones_like(a_re)], [jnp.zeros_like(a_re)]
    for _ in range(L):
        pr, pi = pw_re[-1], pw_im[-1]
        pw_re.append(pr * a_re - pi * a_im)
        pw_im.append(pr * a_im + pi * a_re)
    pw_re = jnp.stack(pw_re)
    pw_im = jnp.stack(pw_im)
    cr = c_re.astype(F32)[None]
    ci = c_im.astype(F32)[None]
    cp_re = cr * pw_re[:, :, None, :] - ci * pw_im[:, :, None, :]
    cp_im = cr * pw_im[:, :, None, :] + ci * pw_re[:, :, None, :]
    kern = (jnp.einsum('tghp,gpk->tghk', cp_re[:L], bb_re, precision=HIGHEST)
            - jnp.einsum('tghp,gpk->tghk', cp_im[:L], bb_im, precision=HIGHEST))
    s_idx = jnp.arange(L)[:, None]
    t_idx = jnp.arange(L)[None, :]
    toep = jnp.where((t_idx >= s_idx)[:, :, None, None, None], kern[jnp.maximum(t_idx - s_idx, 0)], 0.0)
    m_tab = jnp.transpose(toep, (2, 0, 4, 1, 3)).reshape(G, L * H, L * H)
    rev_re = pw_re[L - 1::-1][:, :, None, :]
    rev_im = pw_im[L - 1::-1][:, :, None, :]
    bbt_re = jnp.transpose(bb_re, (0, 2, 1))[None]
    bbt_im = jnp.transpose(bb_im, (0, 2, 1))[None]
    bst_re = jnp.transpose(rev_re * bbt_re - rev_im * bbt_im, (1, 0, 2, 3)).reshape(G, L * H, P)
    bst_im = jnp.transpose(rev_re * bbt_im + rev_im * bbt_re, (1, 0, 2, 3)).reshape(G, L * H, P)
    cst_re = jnp.transpose(cp_re[1:], (1, 3, 0, 2)).reshape(G, P, L * H)
    cst_im = -jnp.transpose(cp_im[1:], (1, 3, 0, 2)).reshape(G, P, L * H)
    a_chunk = jnp.stack([pw_re[L], pw_im[L]], axis=1)
    d_tab = jnp.tile(d_skip.astype(F32), (1, L)).reshape(G, 1, L * H)
    return (m_tab.astype(BF16), bst_re.astype(BF16), bst_im.astype(BF16),
            cst_re.astype(BF16), cst_im.astype(BF16), a_chunk, d_tab)


def _gelu_tanh(x):
    return 0.5 * x * (1.0 + jnp.tanh(math.sqrt(2.0 / math.pi) * (x + 0.044715 * (x * x * x))))


def _s5_kernel(u_ref, m_ref, bre_ref, bim_ref, cre_ref, cim_ref, a_ref, d_ref, o_ref, xr_scr, xi_scr,
               *, batch, row_tile):
    rows = u_ref.shape[0]
    n_tiles = rows // row_tile
    for r in range(n_tiles):
        sl = pl.ds(r * row_tile, row_tile)
        u = u_ref[sl, :]
        xr_scr[sl, :] = jnp.dot(u, bre_ref[...], preferred_element_type=F32)
        xi_scr[sl, :] = jnp.dot(u, bim_ref[...], preferred_element_type=F32)
    ar = a_ref[0:1, :]
    ai = a_ref[1:2, :]

    def step(c, carry):
        xr, xi = carry
        sl = pl.ds(pl.multiple_of(c * batch, batch), batch)
        zr = xr_scr[sl, :]
        zi = xi_scr[sl, :]
        xr_scr[sl, :] = xr
        xi_scr[sl, :] = xi
        return ar * xr - ai * xi + zr, ar * xi + ai * xr + zi

    zero = jnp.zeros((batch, SSM_STATE), F32)
    lax.fori_loop(0, rows // batch, step, (zero, zero))
    for r in range(n_tiles):
        sl = pl.ds(r * row_tile, row_tile)
        u = u_ref[sl, :]
        y = (jnp.dot(u, m_ref[...], preferred_element_type=F32)
             + jnp.dot(xr_scr[sl, :].astype(BF16), cre_ref[...], preferred_element_type=F32)
             + jnp.dot(xi_scr[sl, :].astype(BF16), cim_ref[...], preferred_element_type=F32)
             + u.astype(F32) * d_ref[...])
        o_ref[sl, :] = _gelu_tanh(y).astype(BF16)


def _s5_branch(u, tables, batch, seq):
    L, G, H, P = SSM_CHUNK, SSM_GROUPS, SSM_GROUP_CH, SSM_STATE
    n_chunks = seq // L
    rows = n_chunks * batch
    LH = L * H
    ug = u.reshape(batch, n_chunks, L, G, H).transpose(3, 1, 0, 2, 4).reshape(G, rows, LH)
    m_tab, bst_re, bst_im, cst_re, cst_im, a_chunk, d_tab = tables
    grp = lambda g: (g, 0, 0)
    yg = pl.pallas_call(
        functools.partial(_s5_kernel, batch=batch, row_tile=min(rows, 1024)),
        grid=(G,),
        in_specs=[pl.BlockSpec((None, rows, LH), grp),
                  pl.BlockSpec((None, LH, LH), grp),
                  pl.BlockSpec((None, LH, P), grp),
                  pl.BlockSpec((None, LH, P), grp),
                  pl.BlockSpec((None, P, LH), grp),
                  pl.BlockSpec((None, P, LH), grp),
                  pl.BlockSpec((None, 2, P), grp),
                  pl.BlockSpec((None, 1, LH), grp)],
        out_specs=pl.BlockSpec((None, rows, LH), grp),
        out_shape=jax.ShapeDtypeStruct((G, rows, LH), BF16),
        scratch_shapes=[pltpu.VMEM((rows, P), F32), pltpu.VMEM((rows, P), F32)],
        compiler_params=_params("parallel"),
        name="s5",
    )(ug, m_tab, bst_re, bst_im, cst_re, cst_im, a_chunk, d_tab)
    return yg.reshape(G, n_chunks, batch, L, H).transpose(2, 1, 3, 0, 4).reshape(batch * seq, G * H)


def _merge_kernel(x_ref, ya_ref, ys_ref, gs_ref, mod_ref, wglu_ref, pa_ref, ps_ref, wout_ref, g2_ref,
                  wr_ref, br_ref, x1_ref, h2_ref, ri_ref, rw_ref, cnt_ref, base_scr):
    i = pl.program_id(0)
    tm, D = x_ref.shape

    @pl.when(i == 0)
    def _():
        base_scr[...] = jnp.zeros(base_scr.shape, F32)

    gl = jnp.dot(ys_ref[...], wglu_ref[...], preferred_element_type=F32)
    y_ssm = gl[:, :SSM_WIDTH] * jax.nn.sigmoid(gl[:, SSM_WIDTH:])
    p_attn = jnp.dot(ya_ref[...], pa_ref[...], preferred_element_type=F32)
    p_ssm = jnp.dot(y_ssm.astype(BF16), ps_ref[...], preferred_element_type=F32)
    merged = gs_ref[:, :D].astype(F32) * p_attn + gs_ref[:, D:].astype(F32) * p_ssm
    mixed = jnp.dot(merged.astype(BF16), wout_ref[...], preferred_element_type=F32)
    x1 = x_ref[...] + mod_ref[2:3, :] * mixed
    x1_ref[...] = x1
    h2 = _rms(x1, NORM_EPS) * g2_ref[...] * (1.0 + mod_ref[4:5, :]) + mod_ref[3:4, :]
    h2_ref[...] = h2

    logits = lax.dot_general(wr_ref[...], h2, (((1,), (1,)), ((), ())),
                             preferred_element_type=F32, precision=HIGHEST) + br_ref[...]
    NG, EPG = MOE_GROUPS, MOE_EXPERTS_PER_GROUP
    lg = logits[0:NG, :]
    g_iota = lax.broadcasted_iota(I32, lg.shape, 0)
    lg_max = jnp.max(lg, axis=0, keepdims=True)
    grp = jnp.min(jnp.where(lg == lg_max, g_iota, NG), axis=0, keepdims=True)
    p_grp = 1.0 / jnp.sum(jnp.exp(lg - lg_max), axis=0, keepdims=True)
    le = logits[NG:NG + EPG, :]
    for g in range(1, NG):
        le = jnp.where(grp == g, logits[NG + g * EPG:NG + (g + 1) * EPG, :], le)
    e_iota = lax.broadcasted_iota(I32, le.shape, 0)
    v1 = jnp.max(le, axis=0, keepdims=True)
    i1 = jnp.min(jnp.where(le == v1, e_iota, EPG), axis=0, keepdims=True)
    le2 = jnp.where(e_iota == i1, -jnp.inf, le)
    v2 = jnp.max(le2, axis=0, keepdims=True)
    i2 = jnp.min(jnp.where(le2 == v2, e_iota, EPG), axis=0, keepdims=True)
    e21 = jnp.exp(v2 - v1)
    w1 = p_grp / (1.0 + e21)
    w2 = p_grp * e21 / (1.0 + e21)
    eid1 = grp * EPG + i1
    eid2 = grp * EPG + i2

    x_iota = lax.broadcasted_iota(I32, (MOE_EXPERTS, tm), 0)
    hot1 = x_iota == eid1
    hot2 = x_iota == eid2
    hot = jnp.logical_or(hot1, hot2).astype(F32)
    before = (lax.broadcasted_iota(I32, (tm, tm), 0) < lax.broadcasted_iota(I32, (tm, tm), 1))
    prior = jnp.dot(hot.astype(BF16), before.astype(BF16), preferred_element_type=F32) + base_scr[...]
    rank1 = jnp.sum(jnp.where(hot1, prior, 0.0), axis=0, keepdims=True)
    rank2 = jnp.sum(jnp.where(hot2, prior, 0.0), axis=0, keepdims=True)
    base_scr[...] = base_scr[...] + jnp.sum(hot, axis=1, keepdims=True)
    cnt_ref[...] = base_scr[...].astype(I32)

    zi = jnp.zeros((4, tm), I32)
    ri_ref[...] = jnp.concatenate([eid1, eid2, rank1.astype(I32), rank2.astype(I32), zi], axis=0)
    rw_ref[...] = jnp.concatenate([w1, w2, jnp.zeros((6, tm), F32)], axis=0)


def _merge_route(x2, ya, ys, gs, mod, w_glu, w_pa, w_ps, w_out, norm2_g, w_rg, b_rg, w_re, b_re, seq):
    T, D = x2.shape
    tm = TOKEN_TILE
    per_b = seq // tm
    wr = jnp.concatenate([w_rg.T, jnp.transpose(w_re, (0, 2, 1)).reshape(MOE_EXPERTS, D),
                          jnp.zeros((ROUTER_ROWS - MOE_GROUPS - MOE_EXPERTS, D), F32)], axis=0).astype(F32)
    br = jnp.concatenate([b_rg, b_re.reshape(-1),
                          jnp.zeros((ROUTER_ROWS - MOE_GROUPS - MOE_EXPERTS,), F32)]).reshape(ROUTER_ROWS, 1)
    row = lambda i: (i, 0)
    col = lambda i: (0, i)
    full = lambda i: (0, 0)
    return pl.pallas_call(
        _merge_kernel,
        grid=(T // tm,),
        in_specs=[pl.BlockSpec((tm, D), row),
                  pl.BlockSpec((tm, ATTN_WIDTH), row),
                  pl.BlockSpec((tm, SSM_WIDTH), row),
                  pl.BlockSpec((tm, 2 * D), row),
                  pl.BlockSpec((None, 6, D), lambda i: (i // per_b, 0, 0)),
                  pl.BlockSpec(w_glu.shape, full),
                  pl.BlockSpec(w_pa.shape, full),
                  pl.BlockSpec(w_ps.shape, full),
                  pl.BlockSpec(w_out.shape, full),
                  pl.BlockSpec((1, D), full),
                  pl.BlockSpec((ROUTER_ROWS, D), full),
                  pl.BlockSpec((ROUTER_ROWS, 1), full)],
        out_specs=[pl.BlockSpec((tm, D), row), pl.BlockSpec((tm, D), row),
                   pl.BlockSpec((8, tm), col), pl.BlockSpec((8, tm), col),
                   pl.BlockSpec((MOE_EXPERTS, 1), full)],
        out_shape=[jax.ShapeDtypeStruct((T, D), F32), jax.ShapeDtypeStruct((T, D), F32),
                   jax.ShapeDtypeStruct((8, T), I32), jax.ShapeDtypeStruct((8, T), F32),
                   jax.ShapeDtypeStruct((MOE_EXPERTS, 1), I32)],
        scratch_shapes=[pltpu.VMEM((MOE_EXPERTS, 1), F32)],
        compiler_params=_params("arbitrary"),
        name="merge_route",
    )(x2, ya, ys, gs, mod, w_glu.astype(BF16), w_pa.astype(BF16), w_ps.astype(BF16), w_out.astype(BF16),
      norm2_g.reshape(1, D), wr, br)


def _row_permute_kernel(idx_ref, src_ref, init_ref, dst_ref, sem, *, tm, n_slots, scatter, n_tokens):
    del init_ref
    t0 = pl.program_id(0) * tm

    def body(r, carry):
        for k in range(n_slots):
            d = idx_ref[k * tm + r]
            if scatter:
                cp = pltpu.make_async_copy(src_ref.at[pl.ds(t0 + r, 1)], dst_ref.at[pl.ds(d, 1)], sem)
            else:
                cp = pltpu.make_async_copy(src_ref.at[pl.ds(d, 1)],
                                           dst_ref.at[pl.ds(k * n_tokens + t0 + r, 1)], sem)
            cp.start()
        return carry

    lax.fori_loop(0, tm, body, 0)
    for k in range(n_slots):
        pltpu.make_async_copy(src_ref.at[pl.ds(0, tm)], dst_ref.at[pl.ds(0, tm)], sem).wait()


def _row_permute(idx_tiles, src, init, *, scatter, n_tokens, tm, n_slots):
    return pl.pallas_call(
        functools.partial(_row_permute_kernel, tm=tm, n_slots=n_slots, scatter=scatter, n_tokens=n_tokens),
        grid=(n_tokens // tm,),
        in_specs=[pl.BlockSpec((n_slots * tm,), lambda i: (i,), memory_space=pltpu.SMEM),
                  pl.BlockSpec(memory_space=pl.ANY),
                  pl.BlockSpec(memory_space=pl.ANY)],
        out_specs=pl.BlockSpec(memory_space=pl.ANY),
        out_shape=jax.ShapeDtypeStruct(init.shape, init.dtype),
        scratch_shapes=[pltpu.SemaphoreType.DMA(())],
        input_output_aliases={2: 0},
        compiler_params=_params("arbitrary"),
        name="row_scatter" if scatter else "row_gather",
    )(idx_tiles, src, init)


def _expert_kernel(be_ref, x_ref, wi_ref, wo_ref, o_ref):
    del be_ref
    F = wo_ref.shape[0]
    hid = jnp.dot(x_ref[...].astype(BF16), wi_ref[...], preferred_element_type=F32)
    a = hid[:, :F]
    act = a * jax.nn.sigmoid(a) * hid[:, F:]
    o_ref[...] = jnp.dot(act.astype(BF16), wo_ref[...], preferred_element_type=F32)


def _experts(block_e, xs, w_e_in, w_e_out):
    n_pad, D = xs.shape
    F = w_e_out.shape[1]
    return pl.pallas_call(
        _expert_kernel,
        grid_spec=pltpu.PrefetchScalarGridSpec(
            num_scalar_prefetch=1,
            grid=(n_pad // MOE_BLOCK,),
            in_specs=[pl.BlockSpec((MOE_BLOCK, D), lambda i, be: (i, 0)),
                      pl.BlockSpec((None, D, 2 * F), lambda i, be: (be[i], 0, 0)),
                      pl.BlockSpec((None, F, D), lambda i, be: (be[i], 0, 0))],
            out_specs=pl.BlockSpec((MOE_BLOCK, D), lambda i, be: (i, 0))),
        out_shape=jax.ShapeDtypeStruct((n_pad, D), F32),
        compiler_params=_params("arbitrary"),
        name="experts",
    )(block_e, xs, w_e_in.astype(BF16), w_e_out.astype(BF16))


def _final_kernel(x1_ref, y0_ref, y1_ref, w_ref, mod_ref, g_ref, o_ref):
    moe = w_ref[:, 0:1] * y0_ref[...] + w_ref[:, 1:2] * y1_ref[...]
    x2 = x1_ref[...] + mod_ref[5:6, :] * moe
    o_ref[...] = _rms(x2, NORM_EPS) * g_ref[...]


def _final(x1, ypair, w_tok, mod, final_g, seq):
    T, D = x1.shape
    tm = TOKEN_TILE
    per_b = seq // tm
    nt = T // tm
    row = lambda i: (i, 0)
    return pl.pallas_call(
        _final_kernel,
        grid=(nt,),
        in_specs=[pl.BlockSpec((tm, D), row),
                  pl.BlockSpec((tm, D), row),
                  pl.BlockSpec((tm, D), lambda i: (nt + i, 0)),
                  pl.BlockSpec((tm, 8), row),
                  pl.BlockSpec((None, 6, D), lambda i: (i // per_b, 0, 0)),
                  pl.BlockSpec((1, D), lambda i: (0, 0))],
        out_specs=pl.BlockSpec((tm, D), row),
        out_shape=jax.ShapeDtypeStruct((T, D), F32),
        compiler_params=_params("parallel"),
        name="final",
    )(x1, ypair, ypair, w_tok, mod, final_g.reshape(1, D))


def _moe(h2, ri, rw, counts, w_e_in, w_e_out):
    T, D = h2.shape
    tm = TOKEN_TILE
    n_assign = 2 * T
    n_blocks = -(-n_assign // MOE_BLOCK) + MOE_EXPERTS
    n_pad = n_blocks * MOE_BLOCK
    counts = counts.reshape(MOE_EXPERTS)
    padded = ((counts + MOE_BLOCK - 1) // MOE_BLOCK) * MOE_BLOCK
    pends = jnp.cumsum(padded)
    pstarts = pends - padded
    dest = pstarts[ri[0:2]] + ri[2:4]
    dest_tiles = dest.reshape(2, T // tm, tm).transpose(1, 0, 2).reshape(-1).astype(I32)
    block_start = jnp.arange(n_blocks, dtype=I32) * MOE_BLOCK
    block_e = jnp.minimum(jnp.searchsorted(pends, block_start, side='right'), MOE_EXPERTS - 1).astype(I32)
    xs = _row_permute(dest_tiles, h2, jnp.zeros((n_pad, D), F32), scatter=True, n_tokens=T, tm=tm, n_slots=2)
    ys = _experts(block_e, xs, w_e_in, w_e_out)
    return _row_permute(dest_tiles, ys, jnp.zeros((2 * T, D), F32), scatter=False, n_tokens=T, tm=tm, n_slots=2)


def kernel(x, c, w_ada, b_ada, norm1_g, w_in, rel_bias, lambda_q1, lambda_k1, lambda_q2, lambda_k2, subln_g, ssm_lambda_re, ssm_lambda_im, ssm_log_step, ssm_b_re, ssm_b_im, ssm_c_re, ssm_c_im, ssm_d, w_glu, w_proj_attn, w_proj_ssm, w_out, norm2_g, w_router_group, b_router_group, w_router_expert, b_router_expert, w_expert_in, w_expert_out, final_g):
    B, S, D = x.shape
    T = B * S
    x2 = x.reshape(T, D)
    mod = _ada_mod(c, w_ada[0], b_ada[0]).reshape(B, 6, D)
    q, k, v, u, gs = _in_proj(x2, mod, norm1_g[0], w_in[0], S)
    lam_vecs = jnp.stack([lambda_q1[0], lambda_k1[0], lambda_q2[0], lambda_k2[0]]).astype(F32)
    y_attn = _diff_attn(q, k, v, rel_bias, lam_vecs, subln_g[0], B, S)
    tables = _s5_tables(ssm_lambda_re[0], ssm_lambda_im[0], ssm_log_step[0], ssm_b_re[0], ssm_b_im[0],
                        ssm_c_re[0], ssm_c_im[0], ssm_d[0])
    y_s5 = _s5_branch(u, tables, B, S)
    x1, h2, ri, rw, counts = _merge_route(
        x2, y_attn, y_s5, gs, mod, w_glu[0], w_proj_attn[0], w_proj_ssm[0], w_out[0], norm2_g[0],
        w_router_group[0], b_router_group[0], w_router_expert[0], b_router_expert[0], S)
    ypair = _moe(h2, ri, rw, counts, w_expert_in[0], w_expert_out[0])
    out = _final(x1, ypair, rw.T, mod, final_g, S)
    return out.reshape(B, S, D)
```

```python
import functools
import math

import jax
import jax.numpy as jnp
from jax import lax
from jax.experimental import pallas as pl
from jax.experimental.pallas import tpu as pltpu

F32 = jnp.float32
BF16 = jnp.bfloat16
I32 = jnp.int32
HIGHEST = lax.Precision.HIGHEST

ATTN_HEADS = 4
ATTN_HEAD_DIM = 64
ATTN_V_DIM = 2 * ATTN_HEAD_DIM
ATTN_WIDTH = ATTN_HEADS * ATTN_V_DIM
NEG_INF = -1e30
REL_BUCKETS = 32
REL_MAX_DISTANCE = 128
SSM_GROUP_CH = 16
SSM_WIDTH = 512
SSM_GROUPS = SSM_WIDTH // SSM_GROUP_CH
SSM_STATE = 64
SSM_EIG_MAX_RE = -1e-4
SSM_CHUNK = 16
MOE_GROUPS = 4
MOE_EXPERTS_PER_GROUP = 8
MOE_EXPERTS = MOE_GROUPS * MOE_EXPERTS_PER_GROUP
MOE_BLOCK = 256
SEG_ALIGN = 8
NORM_EPS = 1e-6
SUBLN_EPS = 1e-5
LAMBDA_INIT = 0.8 - 0.6 * math.exp(-0.3 * 0)

ATTN_BLOCK = 256
ATTN_ROW_CHUNK = 32
TOKEN_TILE = 512
ROUTER_ROWS = 40
TILE_SLOTS = -(-(2 * TOKEN_TILE + MOE_EXPERTS * (SEG_ALIGN - 1)) // 256) * 256
VMEM_LIMIT = 56 << 20


def _params(*sem):
    return pltpu.CompilerParams(dimension_semantics=sem, vmem_limit_bytes=VMEM_LIMIT)


def _rms(x, eps):
    return x * lax.rsqrt(jnp.mean(x * x, axis=-1, keepdims=True) + eps)


def _mod_kernel(c_ref, w_ref, b_ref, o_ref):
    c = c_ref[...]
    c_act = c * jax.nn.sigmoid(c)
    o_ref[...] = jnp.dot(c_act, w_ref[...], preferred_element_type=F32, precision=HIGHEST) + b_ref[...]


def _ada_mod(c, w_ada, b_ada):
    B, D = c.shape
    N = w_ada.shape[1]
    tn = 1024
    return pl.pallas_call(
        _mod_kernel,
        grid=(N // tn,),
        in_specs=[pl.BlockSpec((B, D), lambda j: (0, 0)),
                  pl.BlockSpec((D, tn), lambda j: (0, j)),
                  pl.BlockSpec((1, tn), lambda j: (0, j))],
        out_specs=pl.BlockSpec((B, tn), lambda j: (0, j)),
        out_shape=jax.ShapeDtypeStruct((B, N), F32),
        compiler_params=_params("arbitrary"),
        name="ada_mod",
    )(c, w_ada, b_ada.reshape(1, N))


def _proj_kernel(x_ref, mod_ref, g_ref, w_ref, q_ref, k_ref, v_ref, u_ref, gs_ref):
    y = _rms(x_ref[...], NORM_EPS) * g_ref[...]
    h = (y * (1.0 + mod_ref[1:2, :]) + mod_ref[0:1, :]).astype(BF16)
    W = ATTN_WIDTH

    def proj(lo, hi):
        return jnp.dot(h, w_ref[:, lo:hi], preferred_element_type=F32)

    q_ref[...] = (proj(0, W) * (ATTN_HEAD_DIM ** -0.5)).astype(BF16)
    k_ref[...] = proj(W, 2 * W).astype(BF16)
    v_ref[...] = proj(2 * W, 3 * W).astype(BF16)
    u_ref[...] = proj(3 * W, 3 * W + SSM_WIDTH).astype(BF16)
    gs_ref[...] = jax.nn.sigmoid(proj(3 * W + SSM_WIDTH, w_ref.shape[1])).astype(BF16)


def _in_proj(x2, mod, norm_g, w_in, seq):
    T, D = x2.shape
    tm = TOKEN_TILE
    per_b = seq // tm
    n_gate = w_in.shape[1] - 3 * ATTN_WIDTH - SSM_WIDTH
    row = lambda i: (i, 0)
    return pl.pallas_call(
        _proj_kernel,
        grid=(T // tm,),
        in_specs=[pl.BlockSpec((tm, D), row),
                  pl.BlockSpec((None, 6, D), lambda i: (i // per_b, 0, 0)),
                  pl.BlockSpec((1, D), lambda i: (0, 0)),
                  pl.BlockSpec(w_in.shape, lambda i: (0, 0))],
        out_specs=[pl.BlockSpec((tm, ATTN_WIDTH), row)] * 3
        + [pl.BlockSpec((tm, SSM_WIDTH), row), pl.BlockSpec((tm, n_gate), row)],
        out_shape=[jax.ShapeDtypeStruct((T, ATTN_WIDTH), BF16)] * 3
        + [jax.ShapeDtypeStruct((T, SSM_WIDTH), BF16), jax.ShapeDtypeStruct((T, n_gate), BF16)],
        compiler_params=_params("parallel"),
        name="in_proj",
    )(x2, mod, norm_g.reshape(1, D), w_in.astype(BF16))


def _rel_bucket(dist):
    max_exact = REL_BUCKETS // 2
    n = jnp.maximum(dist, 0)
    log_ratio = jnp.log(jnp.maximum(n, 1).astype(F32) / max_exact) / math.log(REL_MAX_DISTANCE / max_exact)
    large = max_exact + (log_ratio * (REL_BUCKETS - max_exact)).astype(I32)
    large = jnp.minimum(large, REL_BUCKETS - 1)
    return jnp.where(n < max_exact, n, large)


def _attn_bias_tiles(rel_bias, blk):
    assert blk >= REL_MAX_DISTANCE
    kk = jnp.arange(blk)[:, None]
    qq = jnp.arange(blk)[None, :]
    far = rel_bias[REL_BUCKETS - 1].astype(F32)
    tiles = []
    for kind in range(2):
        dist = kind * blk + qq - kk
        b = jnp.transpose(rel_bias[_rel_bucket(dist)], (2, 0, 1)).astype(F32) - far[:, None, None]
        tiles.append(jnp.where(dist >= 0, b, NEG_INF))
    return jnp.stack(tiles, axis=1)


def _attn_kernel(lam_ref, q_ref, k_ref, v_ref, bias_ref, g_ref, o_ref, vt_scr,
                 acc0, acc1, cur0, cur1, nxt0, nxt1, p0, p1, *, blk):
    i = pl.program_id(2)
    n_kv = vt_scr.shape[0]
    acc, cur, nxt, pbuf = (acc0, acc1), (cur0, cur1), (nxt0, nxt1), (p0, p1)

    @pl.when(i == 0)
    def _():
        for jb in range(n_kv):
            vt_scr[jb] = v_ref[jb * blk:(jb + 1) * blk, :].astype(F32).T.astype(BF16)

    q = q_ref[...]
    lane = lax.broadcasted_iota(I32, q.shape, 1)
    zero = jnp.zeros_like(q)
    q_maps = (jnp.where(lane < ATTN_HEAD_DIM, q, zero), jnp.where(lane >= ATTN_HEAD_DIM, q, zero))

    def scores(j, mi):
        kj = k_ref[pl.ds(pl.multiple_of(j * blk, blk), blk), :]
        return lax.dot_general(kj, q_maps[mi], (((1,), (1,)), ((), ())), preferred_element_type=F32)

    n_chunks = blk // ATTN_ROW_CHUNK

    def rows(c):
        return slice(c * ATTN_ROW_CHUNK, (c + 1) * ATTN_ROW_CHUNK)

    def fold8(x):
        return x.reshape(ATTN_ROW_CHUNK // 8, 8, blk)

    def make_step(near):
        def step(j, carry):
            j_next = jnp.minimum(j + 1, i)
            vtj = vt_scr[j]
            out = []
            for mi in range(2):
                nxt[mi][...] = scores(j_next, mi)

                def chunk(c):
                    s = cur[mi][rows(c), :]
                    return s + bias_ref[i - j, rows(c), :] if near else s

                m_old, l_old = carry[2 * mi], carry[2 * mi + 1]
                m8 = jnp.max(fold8(chunk(0)), axis=0)
                for c in range(1, n_chunks):
                    m8 = jnp.maximum(m8, jnp.max(fold8(chunk(c)), axis=0))
                m_new = jnp.maximum(m_old, jnp.max(m8, axis=0, keepdims=True))
                alpha = jnp.exp(m_old - m_new)
                l8 = jnp.zeros((8, blk), F32)
                for c in range(n_chunks):
                    p = jnp.exp(chunk(c) - m_new)
                    l8 = l8 + jnp.sum(fold8(p), axis=0)
                    pbuf[mi][rows(c), :] = p.astype(BF16)
                l_new = alpha * l_old + jnp.sum(l8, axis=0, keepdims=True)
                acc[mi][...] = alpha * acc[mi][...] + jnp.dot(vtj, pbuf[mi][...], preferred_element_type=F32)
                out += [m_new, l_new]
            for mi in range(2):
                cur[mi][...] = nxt[mi][...]
            return tuple(out)
        return step

    for mi in range(2):
        acc[mi][...] = jnp.zeros(acc[mi].shape, F32)
        cur[mi][...] = scores(0, mi)
    m0 = jnp.full((1, blk), -jnp.inf, F32)
    l0 = jnp.zeros((1, blk), F32)
    n_far = jnp.maximum(i - 1, 0)
    carry = lax.fori_loop(0, n_far, make_step(False), (m0, l0, m0, l0))
    _, l1, _, l2 = lax.fori_loop(n_far, i + 1, make_step(True), carry)

    lam = (jnp.exp(jnp.sum(lam_ref[0:1, :] * lam_ref[1:2, :], axis=-1, keepdims=True))
           - jnp.exp(jnp.sum(lam_ref[2:3, :] * lam_ref[3:4, :], axis=-1, keepdims=True)) + LAMBDA_INIT)
    o = (acc0[...] / l1 - lam * (acc1[...] / l2)).T
    o_ref[...] = (_rms(o, SUBLN_EPS) * g_ref[...] * (1.0 - LAMBDA_INIT)).astype(BF16)


def _diff_attn(q, k, v, rel_bias, lam_vecs, subln_g, batch, seq):
    T = q.shape[0]
    blk = ATTN_BLOCK
    nq = seq // blk
    bias = _attn_bias_tiles(rel_bias, blk)
    return pl.pallas_call(
        functools.partial(_attn_kernel, blk=blk),
        grid=(batch, ATTN_HEADS, nq),
        in_specs=[pl.BlockSpec((4, ATTN_HEAD_DIM), lambda b, h, i: (0, 0)),
                  pl.BlockSpec((blk, ATTN_V_DIM), lambda b, h, i: (b * nq + i, h)),
                  pl.BlockSpec((seq, ATTN_V_DIM), lambda b, h, i: (b, h)),
                  pl.BlockSpec((seq, ATTN_V_DIM), lambda b, h, i: (b, h)),
                  pl.BlockSpec((None, 2, blk, blk), lambda b, h, i: (h, 0, 0, 0)),
                  pl.BlockSpec((1, ATTN_V_DIM), lambda b, h, i: (0, 0))],
        out_specs=pl.BlockSpec((blk, ATTN_V_DIM), lambda b, h, i: (b * nq + i, h)),
        out_shape=jax.ShapeDtypeStruct((T, ATTN_WIDTH), BF16),
        scratch_shapes=[pltpu.VMEM((nq, ATTN_V_DIM, blk), BF16)]
        + [pltpu.VMEM((ATTN_V_DIM, blk), F32)] * 2 + [pltpu.VMEM((blk, blk), F32)] * 4
        + [pltpu.VMEM((blk, blk), BF16)] * 2,
        compiler_params=_params("parallel", "parallel", "arbitrary"),
        name="diff_attn",
    )(lam_vecs, q, k, v, bias, subln_g.reshape(1, ATTN_V_DIM))


def _s5_tables(lam_re, lam_im, log_step, b_re, b_im, c_re, c_im, d_skip):
    L = SSM_CHUNK
    G, P = lam_re.shape
    H = SSM_GROUP_CH
    lr = jnp.minimum(lam_re.astype(F32), SSM_EIG_MAX_RE)
    li = lam_im.astype(F32)
    step = jnp.exp(log_step.astype(F32))[:, None]
    mag = jnp.exp(lr * step)
    ang = li * step
    a_re = mag * jnp.cos(ang)
    a_im = mag * jnp.sin(ang)
    den = lr * lr + li * li
    num_re = a_re - 1.0
    coef_re = (num_re * lr + a_im * li) / den
    coef_im = (a_im * lr - num_re * li) / den
    br = b_re.astype(F32)
    bi = b_im.astype(F32)
    bb_re = coef_re[..., None] * br - coef_im[..., None] * bi
    bb_im = coef_re[..., None] * bi + coef_im[..., None] * br
    pw_re, pw_im = [jnp.ones_like(a_re)], [jnp.zeros_like(a_re)]
    for _ in range(L):
        pr, pi = pw_re[-1], pw_im[-1]
        pw_re.append(pr * a_re - pi * a_im)
        pw_im.append(pr * a_im + pi * a_re)
    pw_re = jnp.stack(pw_re)
    pw_im = jnp.stack(pw_im)
    cr = c_re.astype(F32)[None]
    ci = c_im.astype(F32)[None]
    cp_re = cr * pw_re[:, :, None, :] - ci * pw_im[:, :, None, :]
    cp_im = cr * pw_im[:, :, None, :] + ci * pw_re[:, :, None, :]
    kern = (jnp.einsum('tghp,gpk->tghk', cp_re[:L], bb_re, precision=HIGHEST)
            - jnp.einsum('tghp,gpk->tghk', cp_im[:L], bb_im, precision=HIGHEST))
    s_idx = jnp.arange(L)[:, None]
    t_idx = jnp.arange(L)[None, :]
    toep = jnp.where((t_idx >= s_idx)[:, :, None, None, None], kern[jnp.maximum(t_idx - s_idx, 0)], 0.0)
    m_tab = jnp.transpose(toep, (2, 0, 4, 1, 3)).reshape(G, L * H, L * H)
    rev_re = pw_re[L - 1::-1][:, :, None, :]
    rev_im = pw_im[L - 1::-1][:, :, None, :]
    bbt_re = jnp.transpose(bb_re, (0, 2, 1))[None]
    bbt_im = jnp.transpose(bb_im, (0, 2, 1))[None]
    bst_re = jnp.transpose(rev_re * bbt_re - rev_im * bbt_im, (1, 0, 2, 3)).reshape(G, L * H, P)
    bst_im = jnp.transpose(rev_re * bbt_im + rev_im * bbt_re, (1, 0, 2, 3)).reshape(G, L * H, P)
    cst_re = jnp.transpose(cp_re[1:], (1, 3, 0, 2)).reshape(G, P, L * H)
    cst_im = -jnp.transpose(cp_im[1:], (1, 3, 0, 2)).reshape(G, P, L * H)
    a_chunk = jnp.stack([pw_re[L], pw_im[L]], axis=1)
    d_tab = jnp.tile(d_skip.astype(F32), (1, L)).reshape(G, 1, L * H)
    return (m_tab.astype(BF16), bst_re.astype(BF16), bst_im.astype(BF16),
            cst_re.astype(BF16), cst_im.astype(BF16), a_chunk, d_tab)


def _gelu_tanh(x):
    return 0.5 * x * (1.0 + jnp.tanh(math.sqrt(2.0 / math.pi) * (x + 0.044715 * (x * x * x))))


def _s5_kernel(u_ref, m_ref, bre_ref, bim_ref, cre_ref, cim_ref, a_ref, d_ref, o_ref, xr_scr, xi_scr,
               *, batch, row_tile):
    rows = u_ref.shape[0]
    n_tiles = rows // row_tile
    for r in range(n_tiles):
        sl = pl.ds(r * row_tile, row_tile)
        u = u_ref[sl, :]
        xr_scr[sl, :] = jnp.dot(u, bre_ref[...], preferred_element_type=F32)
        xi_scr[sl, :] = jnp.dot(u, bim_ref[...], preferred_element_type=F32)
    ar = a_ref[0:1, :]
    ai = a_ref[1:2, :]

    def step(c, carry):
        xr, xi = carry
        sl = pl.ds(pl.multiple_of(c * batch, batch), batch)
        zr = xr_scr[sl, :]
        zi = xi_scr[sl, :]
        xr_scr[sl, :] = xr
        xi_scr[sl, :] = xi
        return ar * xr - ai * xi + zr, ar * xi + ai * xr + zi

    zero = jnp.zeros((batch, SSM_STATE), F32)
    lax.fori_loop(0, rows // batch, step, (zero, zero))
    for r in range(n_tiles):
        sl = pl.ds(r * row_tile, row_tile)
        u = u_ref[sl, :]
        y = (jnp.dot(u, m_ref[...], preferred_element_type=F32)
             + jnp.dot(xr_scr[sl, :].astype(BF16), cre_ref[...], preferred_element_type=F32)
             + jnp.dot(xi_scr[sl, :].astype(BF16), cim_ref[...], preferred_element_type=F32)
             + u.astype(F32) * d_ref[...])
        o_ref[sl, :] = _gelu_tanh(y).astype(BF16)


def _s5_branch(u, tables, batch, seq):
    L, G, H, P = SSM_CHUNK, SSM_GROUPS, SSM_GROUP_CH, SSM_STATE
    n_chunks = seq // L
    rows = n_chunks * batch
    LH = L * H
    ug = u.reshape(batch, n_chunks, L, G, H).transpose(3, 1, 0, 2, 4).reshape(G, rows, LH)
    m_tab, bst_re, bst_im, cst_re, cst_im, a_chunk, d_tab = tables
    grp = lambda g: (g, 0, 0)
    yg = pl.pallas_call(
        functools.partial(_s5_kernel, batch=batch, row_tile=min(rows, 1024)),
        grid=(G,),
        in_specs=[pl.BlockSpec((None, rows, LH), grp),
                  pl.BlockSpec((None, LH, LH), grp),
                  pl.BlockSpec((None, LH, P), grp),
                  pl.BlockSpec((None, LH, P), grp),
                  pl.BlockSpec((None, P, LH), grp),
                  pl.BlockSpec((None, P, LH), grp),
                  pl.BlockSpec((None, 2, P), grp),
                  pl.BlockSpec((None, 1, LH), grp)],
        out_specs=pl.BlockSpec((None, rows, LH), grp),
        out_shape=jax.ShapeDtypeStruct((G, rows, LH), BF16),
        scratch_shapes=[pltpu.VMEM((rows, P), F32), pltpu.VMEM((rows, P), F32)],
        compiler_params=_params("parallel"),
        name="s5",
    )(ug, m_tab, bst_re, bst_im, cst_re, cst_im, a_chunk, d_tab)
    return yg.reshape(G, n_chunks, batch, L, H).transpose(2, 1, 3, 0, 4).reshape(batch * seq, G * H)


def _merge_kernel(x_ref, ya_ref, ys_ref, gs_ref, mod_ref, wglu_ref, pa_ref, ps_ref, wout_ref, g2_ref,
                  wr_ref, br_ref, x1_ref, h2_ref, ri_ref, rw_ref, cnt_ref):
    tm, D = x_ref.shape
    gl = jnp.dot(ys_ref[...], wglu_ref[...], preferred_element_type=F32)
    y_ssm = gl[:, :SSM_WIDTH] * jax.nn.sigmoid(gl[:, SSM_WIDTH:])
    p_attn = jnp.dot(ya_ref[...], pa_ref[...], preferred_element_type=F32)
    p_ssm = jnp.dot(y_ssm.astype(BF16), ps_ref[...], preferred_element_type=F32)
    merged = gs_ref[:, :D].astype(F32) * p_attn + gs_ref[:, D:].astype(F32) * p_ssm
    mixed = jnp.dot(merged.astype(BF16), wout_ref[...], preferred_element_type=F32)
    x1 = x_ref[...] + mod_ref[2:3, :] * mixed
    x1_ref[...] = x1
    h2 = _rms(x1, NORM_EPS) * g2_ref[...] * (1.0 + mod_ref[4:5, :]) + mod_ref[3:4, :]
    h2_ref[...] = h2.astype(BF16)

    logits = lax.dot_general(wr_ref[...], h2, (((1,), (1,)), ((), ())),
                             preferred_element_type=F32, precision=HIGHEST) + br_ref[...]
    NG, EPG = MOE_GROUPS, MOE_EXPERTS_PER_GROUP
    lg = logits[0:NG, :]
    g_iota = lax.broadcasted_iota(I32, lg.shape, 0)
    lg_max = jnp.max(lg, axis=0, keepdims=True)
    grp = jnp.min(jnp.where(lg == lg_max, g_iota, NG), axis=0, keepdims=True)
    p_grp = 1.0 / jnp.sum(jnp.exp(lg - lg_max), axis=0, keepdims=True)
    le = logits[NG:NG + EPG, :]
    for g in range(1, NG):
        le = jnp.where(grp == g, logits[NG + g * EPG:NG + (g + 1) * EPG, :], le)
    e_iota = lax.broadcasted_iota(I32, le.shape, 0)
    v1 = jnp.max(le, axis=0, keepdims=True)
    i1 = jnp.min(jnp.where(le == v1, e_iota, EPG), axis=0, keepdims=True)
    le2 = jnp.where(e_iota == i1, -jnp.inf, le)
    v2 = jnp.max(le2, axis=0, keepdims=True)
    i2 = jnp.min(jnp.where(le2 == v2, e_iota, EPG), axis=0, keepdims=True)
    e21 = jnp.exp(v2 - v1)
    w1 = p_grp / (1.0 + e21)
    w2 = p_grp * e21 / (1.0 + e21)
    eid1 = grp * EPG + i1
    eid2 = grp * EPG + i2

    x_iota = lax.broadcasted_iota(I32, (MOE_EXPERTS, tm), 0)
    hot1 = x_iota == eid1
    hot2 = x_iota == eid2
    hot = jnp.logical_or(hot1, hot2).astype(F32)
    before = (lax.broadcasted_iota(I32, (tm, tm), 0) < lax.broadcasted_iota(I32, (tm, tm), 1))
    prior = jnp.dot(hot.astype(BF16), before.astype(BF16), preferred_element_type=F32)
    rank1 = jnp.sum(jnp.where(hot1, prior, 0.0), axis=0, keepdims=True)
    rank2 = jnp.sum(jnp.where(hot2, prior, 0.0), axis=0, keepdims=True)
    cnt_ref[...] = jnp.sum(hot, axis=1, keepdims=True).astype(I32)

    zi = jnp.zeros((4, tm), I32)
    ri_ref[...] = jnp.concatenate([eid1, eid2, rank1.astype(I32), rank2.astype(I32), zi], axis=0)
    rw_ref[...] = jnp.concatenate([w1, w2, jnp.zeros((6, tm), F32)], axis=0)


def _merge_route(x2, ya, ys, gs, mod, w_glu, w_pa, w_ps, w_out, norm2_g, w_rg, b_rg, w_re, b_re, seq):
    T, D = x2.shape
    tm = TOKEN_TILE
    per_b = seq // tm
    wr = jnp.concatenate([w_rg.T, jnp.transpose(w_re, (0, 2, 1)).reshape(MOE_EXPERTS, D),
                          jnp.zeros((ROUTER_ROWS - MOE_GROUPS - MOE_EXPERTS, D), F32)], axis=0).astype(F32)
    br = jnp.concatenate([b_rg, b_re.reshape(-1),
                          jnp.zeros((ROUTER_ROWS - MOE_GROUPS - MOE_EXPERTS,), F32)]).reshape(ROUTER_ROWS, 1)
    row = lambda i: (i, 0)
    col = lambda i: (0, i)
    full = lambda i: (0, 0)
    return pl.pallas_call(
        _merge_kernel,
        grid=(T // tm,),
        in_specs=[pl.BlockSpec((tm, D), row),
                  pl.BlockSpec((tm, ATTN_WIDTH), row),
                  pl.BlockSpec((tm, SSM_WIDTH), row),
                  pl.BlockSpec((tm, 2 * D), row),
                  pl.BlockSpec((None, 6, D), lambda i: (i // per_b, 0, 0)),
                  pl.BlockSpec(w_glu.shape, full),
                  pl.BlockSpec(w_pa.shape, full),
                  pl.BlockSpec(w_ps.shape, full),
                  pl.BlockSpec(w_out.shape, full),
                  pl.BlockSpec((1, D), full),
                  pl.BlockSpec((ROUTER_ROWS, D), full),
                  pl.BlockSpec((ROUTER_ROWS, 1), full)],
        out_specs=[pl.BlockSpec((tm, D), row), pl.BlockSpec((tm, D), row),
                   pl.BlockSpec((8, tm), col), pl.BlockSpec((8, tm), col),
                   pl.BlockSpec((None, MOE_EXPERTS, 1), lambda i: (i, 0, 0))],
        out_shape=[jax.ShapeDtypeStruct((T, D), F32), jax.ShapeDtypeStruct((T, D), BF16),
                   jax.ShapeDtypeStruct((8, T), I32), jax.ShapeDtypeStruct((8, T), F32),
                   jax.ShapeDtypeStruct((T // tm, MOE_EXPERTS, 1), I32)],
        compiler_params=_params("parallel"),
        name="merge_route",
    )(x2, ya, ys, gs, mod, w_glu.astype(BF16), w_pa.astype(BF16), w_ps.astype(BF16), w_out.astype(BF16),
      norm2_g.reshape(1, D), wr, br)


def _tile_positions(ri_ref, seg_ref):
    tm = ri_ref.shape[1]
    x_iota = lax.broadcasted_iota(I32, (MOE_EXPERTS, tm), 0)
    seg = seg_ref[...].astype(F32)
    pos = []
    for k in range(2):
        start = jnp.sum(jnp.where(x_iota == ri_ref[k:k + 1, :], seg, 0.0), axis=0, keepdims=True)
        pos.append(start + ri_ref[2 + k:3 + k, :].astype(F32))
    return pos


def _segment_copies(meta, tile, make_copy):
    seg_row, dst_row, n_chunk = meta

    def per_expert(e, carry):
        idx = tile * MOE_EXPERTS + e
        src0 = seg_row[idx]
        dst0 = dst_row[idx]

        def per_chunk(c, carry2):
            off = c * SEG_ALIGN
            make_copy(pl.multiple_of(src0 + off, SEG_ALIGN), pl.multiple_of(dst0 + off, SEG_ALIGN))
            return carry2

        return lax.fori_loop(0, n_chunk[idx], per_chunk, carry)

    lax.fori_loop(0, MOE_EXPERTS, per_expert, 0)


def _dispatch_kernel(seg_row, dst_row, n_chunk, tile_chunks, pad_row, pad_chunks, n_used,
                     h_ref, ri_ref, rw_ref, seg_ref, xs_ref, pw_ref, zbuf, zeros_scr, sem, pad_sem, tail_sem):
    i = pl.program_id(0)
    n_tiles = pl.num_programs(0)
    slot = i % 2
    tm = h_ref.shape[0]
    pos1, pos2 = _tile_positions(ri_ref, seg_ref)
    pw_ref[...] = jnp.concatenate([pos1, pos2, rw_ref[0:2, :], jnp.zeros((4, tm), F32)], axis=0)
    r_iota = lax.broadcasted_iota(I32, (TILE_SLOTS, tm), 0).astype(F32)
    onehot = jnp.logical_or(r_iota == pos1, r_iota == pos2).astype(BF16)
    zbuf[slot] = jnp.dot(onehot, h_ref[...], preferred_element_type=F32)

    def chunk_copy(buf_slot, src, dst):
        return pltpu.make_async_copy(zbuf.at[buf_slot, pl.ds(src, SEG_ALIGN)],
                                     xs_ref.at[pl.ds(dst, SEG_ALIGN)], sem.at[buf_slot])

    _segment_copies((seg_row, dst_row, n_chunk), i, lambda src, dst: chunk_copy(slot, src, dst).start())

    def wait_tile(tile, buf_slot):
        def body(c, carry):
            chunk_copy(buf_slot, 0, 0).wait()
            return carry
        lax.fori_loop(0, tile_chunks[tile], body, 0)

    @pl.when(i > 0)
    def _():
        wait_tile(i - 1, 1 - slot)

    @pl.when(i == n_tiles - 1)
    def _():
        zeros_scr[...] = jnp.zeros(zeros_scr.shape, F32)

        def pad_copy(dst):
            return pltpu.make_async_copy(zeros_scr.at[pl.ds(0, SEG_ALIGN)], xs_ref.at[pl.ds(dst, SEG_ALIGN)], pad_sem)

        def tail_copy(dst):
            return pltpu.make_async_copy(zeros_scr, xs_ref.at[pl.ds(dst, MOE_BLOCK)], tail_sem)

        def per_expert(e, total):
            def per_chunk(c, carry):
                pad_copy(pl.multiple_of(pad_row[e] + c * SEG_ALIGN, SEG_ALIGN)).start()
                return carry
            lax.fori_loop(0, pad_chunks[e], per_chunk, 0)
            return total + pad_chunks[e]

        n_pad_copies = lax.fori_loop(0, MOE_EXPERTS, per_expert, 0)
        n_blocks = xs_ref.shape[0] // MOE_BLOCK

        def tail_start(b, carry):
            tail_copy(pl.multiple_of(b * MOE_BLOCK, MOE_BLOCK)).start()
            return carry
        lax.fori_loop(n_used[0], n_blocks, tail_start, 0)
        wait_tile(i, slot)

        def wait_pad(c, carry):
            pad_copy(0).wait()
            return carry
        lax.fori_loop(0, n_pad_copies, wait_pad, 0)

        def wait_tail(b, carry):
            tail_copy(0).wait()
            return carry
        lax.fori_loop(n_used[0], n_blocks, wait_tail, 0)


def _dispatch(meta, h2, ri, rw, seg_start, n_rows):
    T, D = h2.shape
    tm = TOKEN_TILE
    col = lambda i, *_: (0, i)
    return pl.pallas_call(
        _dispatch_kernel,
        grid_spec=pltpu.PrefetchScalarGridSpec(
            num_scalar_prefetch=7,
            grid=(T // tm,),
            in_specs=[pl.BlockSpec((tm, D), lambda i, *_: (i, 0)),
                      pl.BlockSpec((8, tm), col),
                      pl.BlockSpec((8, tm), col),
                      pl.BlockSpec((None, MOE_EXPERTS, 1), lambda i, *_: (i, 0, 0))],
            out_specs=[pl.BlockSpec(memory_space=pl.ANY), pl.BlockSpec((8, tm), col)],
            scratch_shapes=[pltpu.VMEM((2, TILE_SLOTS, D), F32), pltpu.VMEM((MOE_BLOCK, D), F32),
                            pltpu.SemaphoreType.DMA((2,)), pltpu.SemaphoreType.DMA(()),
                            pltpu.SemaphoreType.DMA(())]),
        out_shape=[jax.ShapeDtypeStruct((n_rows, D), F32), jax.ShapeDtypeStruct((8, T), F32)],
        compiler_params=_params("arbitrary"),
        name="moe_dispatch",
    )(*meta, h2, ri, rw, seg_start)


def _expert_kernel(be_ref, nb_ref, x_ref, wi_ref, wo_ref, o_ref):
    del be_ref
    F = wo_ref.shape[0]

    @pl.when(pl.program_id(0) < nb_ref[0])
    def _():
        hid = jnp.dot(x_ref[...].astype(BF16), wi_ref[...], preferred_element_type=F32)
        a = hid[:, :F]
        act = a * jax.nn.sigmoid(a) * hid[:, F:]
        o_ref[...] = jnp.dot(act.astype(BF16), wo_ref[...], preferred_element_type=F32)

    @pl.when(pl.program_id(0) >= nb_ref[0])
    def _():
        o_ref[...] = jnp.zeros(o_ref.shape, F32)


def _experts(block_e, n_used, xs, w_e_in, w_e_out):
    n_rows, D = xs.shape
    F = w_e_out.shape[1]
    blk = lambda i, be, nb: jnp.minimum(i, nb[0] - 1)
    return pl.pallas_call(
        _expert_kernel,
        grid_spec=pltpu.PrefetchScalarGridSpec(
            num_scalar_prefetch=2,
            grid=(n_rows // MOE_BLOCK,),
            in_specs=[pl.BlockSpec((MOE_BLOCK, D), lambda i, be, nb: (blk(i, be, nb), 0)),
                      pl.BlockSpec((None, D, 2 * F), lambda i, be, nb: (be[blk(i, be, nb)], 0, 0)),
                      pl.BlockSpec((None, F, D), lambda i, be, nb: (be[blk(i, be, nb)], 0, 0))],
            out_specs=pl.BlockSpec((MOE_BLOCK, D), lambda i, be, nb: (i, 0))),
        out_shape=jax.ShapeDtypeStruct((n_rows, D), F32),
        compiler_params=_params("arbitrary"),
        name="experts",
    )(block_e, n_used, xs, w_e_in.astype(BF16), w_e_out.astype(BF16))


def _final_kernel(seg_row, dst_row, n_chunk, tile_chunks, x1_ref, pw_ref, mod_ref, g_ref, ys_ref, o_ref,
                  ybuf, sem):
    i = pl.program_id(0)
    n_tiles = pl.num_programs(0)
    slot = i % 2
    tm = x1_ref.shape[0]

    def chunk_copy(buf_slot, src, dst):
        return pltpu.make_async_copy(ys_ref.at[pl.ds(dst, SEG_ALIGN)],
                                     ybuf.at[buf_slot, pl.ds(src, SEG_ALIGN)], sem.at[buf_slot])

    def fetch(tile, buf_slot):
        _segment_copies((seg_row, dst_row, n_chunk), tile, lambda src, dst: chunk_copy(buf_slot, src, dst).start())

    @pl.when(i == 0)
    def _():
        ybuf[...] = jnp.zeros(ybuf.shape, F32)
        fetch(0, 0)

    @pl.when(i + 1 < n_tiles)
    def _():
        fetch(i + 1, 1 - slot)

    def wait_one(c, carry):
        chunk_copy(slot, 0, 0).wait()
        return carry
    lax.fori_loop(0, tile_chunks[i], wait_one, 0)

    s_iota = lax.broadcasted_iota(I32, (tm, TILE_SLOTS), 1).astype(F32)
    comb = (jnp.where(s_iota == pw_ref[:, 0:1], pw_ref[:, 2:3], 0.0)
            + jnp.where(s_iota == pw_ref[:, 1:2], pw_ref[:, 3:4], 0.0))
    moe = jnp.dot(comb.astype(BF16), ybuf[slot].astype(BF16), preferred_element_type=F32)
    x2 = x1_ref[...] + mod_ref[5:6, :] * moe
    o_ref[...] = _rms(x2, NORM_EPS) * g_ref[...]


def _final(meta, x1, pw_tok, mod, final_g, ys, seq):
    T, D = x1.shape
    tm = TOKEN_TILE
    per_b = seq // tm
    row = lambda i, *_: (i, 0)
    return pl.pallas_call(
        _final_kernel,
        grid_spec=pltpu.PrefetchScalarGridSpec(
            num_scalar_prefetch=4,
            grid=(T // tm,),
            in_specs=[pl.BlockSpec((tm, D), row),
                      pl.BlockSpec((tm, 8), row),
                      pl.BlockSpec((None, 6, D), lambda i, *_: (i // per_b, 0, 0)),
                      pl.BlockSpec((1, D), lambda i, *_: (0, 0)),
                      pl.BlockSpec(memory_space=pl.ANY)],
            out_specs=pl.BlockSpec((tm, D), row),
            scratch_shapes=[pltpu.VMEM((2, TILE_SLOTS, D), F32), pltpu.SemaphoreType.DMA((2,))]),
        out_shape=jax.ShapeDtypeStruct((T, D), F32),
        compiler_params=_params("arbitrary"),
        name="final",
    )(*meta, x1, pw_tok, mod, final_g.reshape(1, D), ys)


def _round_up(x, m):
    return (x + m - 1) // m * m


def _moe_layout(tile_counts):
    n_tiles = tile_counts.shape[0]
    seg = _round_up(tile_counts, SEG_ALIGN)
    seg_start = jnp.cumsum(seg, axis=1) - seg
    tile_base = jnp.cumsum(seg, axis=0) - seg
    used = jnp.sum(seg, axis=0)
    region = _round_up(used, MOE_BLOCK)
    region_end = jnp.cumsum(region)
    region_start = region_end - region
    dst_row = region_start[None, :] + tile_base
    n_chunk = seg // SEG_ALIGN
    n_assign = 2 * n_tiles * TOKEN_TILE
    n_rows = _round_up(n_assign + n_tiles * MOE_EXPERTS * (SEG_ALIGN - 1) + MOE_EXPERTS * (MOE_BLOCK - 1), MOE_BLOCK)
    block_start = jnp.arange(n_rows // MOE_BLOCK, dtype=I32) * MOE_BLOCK
    block_e = jnp.minimum(jnp.searchsorted(region_end, block_start, side='right'), MOE_EXPERTS - 1)
    i32 = lambda a: a.reshape(-1).astype(I32)
    meta = (i32(seg_start), i32(dst_row), i32(n_chunk), i32(jnp.sum(n_chunk, axis=1)))
    pad = (i32(region_start + used), i32((region - used) // SEG_ALIGN))
    n_used = (region_end[-1:] // MOE_BLOCK).astype(I32)
    return meta, pad, seg_start.astype(I32)[:, :, None], block_e.astype(I32), n_used, n_rows


def kernel(x, c, w_ada, b_ada, norm1_g, w_in, rel_bias, lambda_q1, lambda_k1, lambda_q2, lambda_k2, subln_g, ssm_lambda_re, ssm_lambda_im, ssm_log_step, ssm_b_re, ssm_b_im, ssm_c_re, ssm_c_im, ssm_d, w_glu, w_proj_attn, w_proj_ssm, w_out, norm2_g, w_router_group, b_router_group, w_router_expert, b_router_expert, w_expert_in, w_expert_out, final_g):
    B, S, D = x.shape
    T = B * S
    x2 = x.reshape(T, D)
    mod = _ada_mod(c, w_ada[0], b_ada[0]).reshape(B, 6, D)
    q, k, v, u, gs = _in_proj(x2, mod, norm1_g[0], w_in[0], S)
    lam_vecs = jnp.stack([lambda_q1[0], lambda_k1[0], lambda_q2[0], lambda_k2[0]]).astype(F32)
    y_attn = _diff_attn(q, k, v, rel_bias, lam_vecs, subln_g[0], B, S)
    tables = _s5_tables(ssm_lambda_re[0], ssm_lambda_im[0], ssm_log_step[0], ssm_b_re[0], ssm_b_im[0],
                        ssm_c_re[0], ssm_c_im[0], ssm_d[0])
    y_s5 = _s5_branch(u, tables, B, S)
    x1, h2, ri, rw, tile_counts = _merge_route(
        x2, y_attn, y_s5, gs, mod, w_glu[0], w_proj_attn[0], w_proj_ssm[0], w_out[0], norm2_g[0],
        w_router_group[0], b_router_group[0], w_router_expert[0], b_router_expert[0], S)
    meta, pad, seg_start, block_e, n_used, n_rows = _moe_layout(tile_counts[:, :, 0])
    xs, pw = _dispatch(meta + pad + (n_used,), h2, ri, rw, seg_start, n_rows)
    ys = _experts(block_e, n_used, xs, w_expert_in[0], w_expert_out[0])
    out = _final(meta, x1, pw.T, mod, final_g, ys, S)
    return out.reshape(B, S, D)
```

```python
import functools
import math

import jax
import jax.numpy as jnp
from jax import lax
from jax.experimental import pallas as pl
from jax.experimental.pallas import tpu as pltpu

F32 = jnp.float32
BF16 = jnp.bfloat16
I32 = jnp.int32
HIGHEST = lax.Precision.HIGHEST

ATTN_HEADS = 4
ATTN_HEAD_DIM = 64
ATTN_V_DIM = 2 * ATTN_HEAD_DIM
ATTN_WIDTH = ATTN_HEADS * ATTN_V_DIM
NEG_INF = -1e30
REL_BUCKETS = 32
REL_MAX_DISTANCE = 128
SSM_GROUP_CH = 16
SSM_WIDTH = 512
SSM_GROUPS = SSM_WIDTH // SSM_GROUP_CH
SSM_STATE = 64
SSM_EIG_MAX_RE = -1e-4
SSM_CHUNK = 16
MOE_GROUPS = 4
MOE_EXPERTS_PER_GROUP = 8
MOE_EXPERTS = MOE_GROUPS * MOE_EXPERTS_PER_GROUP
MOE_BLOCK = 256
SEG_ALIGN = 8
NORM_EPS = 1e-6
SUBLN_EPS = 1e-5
LAMBDA_INIT = 0.8 - 0.6 * math.exp(-0.3 * 0)

ATTN_BLOCK = 256
ATTN_ROW_CHUNK = 32
ATTN_ONES_ROWS = 16
LOG2E = math.log2(math.e)
TOKEN_TILE = 512
ROUTER_ROWS = 40
TILE_SLOTS = -(-(2 * TOKEN_TILE + MOE_EXPERTS * (SEG_ALIGN - 1)) // 256) * 256
VMEM_LIMIT = 56 << 20


def _params(*sem):
    return pltpu.CompilerParams(dimension_semantics=sem, vmem_limit_bytes=VMEM_LIMIT)


def _rms(x, eps):
    return x * lax.rsqrt(jnp.mean(x * x, axis=-1, keepdims=True) + eps)


def _mod_kernel(c_ref, w_ref, b_ref, o_ref):
    c = c_ref[...]
    c_act = c * jax.nn.sigmoid(c)
    o_ref[...] = jnp.dot(c_act, w_ref[...], preferred_element_type=F32, precision=HIGHEST) + b_ref[...]


def _ada_mod(c, w_ada, b_ada):
    B, D = c.shape
    N = w_ada.shape[1]
    tn = 1024
    return pl.pallas_call(
        _mod_kernel,
        grid=(N // tn,),
        in_specs=[pl.BlockSpec((B, D), lambda j: (0, 0)),
                  pl.BlockSpec((D, tn), lambda j: (0, j)),
                  pl.BlockSpec((1, tn), lambda j: (0, j))],
        out_specs=pl.BlockSpec((B, tn), lambda j: (0, j)),
        out_shape=jax.ShapeDtypeStruct((B, N), F32),
        compiler_params=_params("arbitrary"),
        name="ada_mod",
    )(c, w_ada, b_ada.reshape(1, N))


def _proj_kernel(x_ref, mod_ref, g_ref, w_ref, q_ref, k_ref, v_ref, u_ref, gs_ref):
    y = _rms(x_ref[...], NORM_EPS) * g_ref[...]
    h = (y * (1.0 + mod_ref[1:2, :]) + mod_ref[0:1, :]).astype(BF16)
    W = ATTN_WIDTH

    def proj(lo, hi):
        return jnp.dot(h, w_ref[:, lo:hi], preferred_element_type=F32)

    q_ref[...] = (proj(0, W) * (ATTN_HEAD_DIM ** -0.5 * LOG2E)).astype(BF16)
    k_ref[...] = proj(W, 2 * W).astype(BF16)
    v_ref[...] = proj(2 * W, 3 * W).astype(BF16)
    u_ref[...] = proj(3 * W, 3 * W + SSM_WIDTH).astype(BF16)
    gs_ref[...] = jax.nn.sigmoid(proj(3 * W + SSM_WIDTH, w_ref.shape[1])).astype(BF16)


def _in_proj(x2, mod, norm_g, w_in, seq):
    T, D = x2.shape
    tm = TOKEN_TILE
    per_b = seq // tm
    n_gate = w_in.shape[1] - 3 * ATTN_WIDTH - SSM_WIDTH
    row = lambda i: (i, 0)
    return pl.pallas_call(
        _proj_kernel,
        grid=(T // tm,),
        in_specs=[pl.BlockSpec((tm, D), row),
                  pl.BlockSpec((None, 6, D), lambda i: (i // per_b, 0, 0)),
                  pl.BlockSpec((1, D), lambda i: (0, 0)),
                  pl.BlockSpec(w_in.shape, lambda i: (0, 0))],
        out_specs=[pl.BlockSpec((tm, ATTN_WIDTH), row)] * 3
        + [pl.BlockSpec((tm, SSM_WIDTH), row), pl.BlockSpec((tm, n_gate), row)],
        out_shape=[jax.ShapeDtypeStruct((T, ATTN_WIDTH), BF16)] * 3
        + [jax.ShapeDtypeStruct((T, SSM_WIDTH), BF16), jax.ShapeDtypeStruct((T, n_gate), BF16)],
        compiler_params=_params("parallel"),
        name="in_proj",
    )(x2, mod, norm_g.reshape(1, D), w_in.astype(BF16))


def _rel_bucket(dist):
    max_exact = REL_BUCKETS // 2
    n = jnp.maximum(dist, 0)
    log_ratio = jnp.log(jnp.maximum(n, 1).astype(F32) / max_exact) / math.log(REL_MAX_DISTANCE / max_exact)
    large = max_exact + (log_ratio * (REL_BUCKETS - max_exact)).astype(I32)
    large = jnp.minimum(large, REL_BUCKETS - 1)
    return jnp.where(n < max_exact, n, large)


def _attn_bias_tiles(rel_bias, blk):
    assert blk >= REL_MAX_DISTANCE
    n_heads = rel_bias.shape[1]
    far = rel_bias[REL_BUCKETS - 1].astype(F32)
    m = jnp.arange(2 * blk)
    signed = jnp.where(m < blk, m, m - 2 * blk)
    tiles = []
    for kind in range(2):
        dist = kind * blk + signed
        tab = jnp.where(dist >= 0, (rel_bias[_rel_bucket(dist)].astype(F32).T - far[:, None]) * LOG2E, NEG_INF)
        skew = jnp.tile(tab, (1, blk))[:, :blk * (2 * blk - 1)].reshape(n_heads, blk, 2 * blk - 1)
        tiles.append(skew[:, :, :blk])
    return jnp.stack(tiles, axis=1)


def _attn_kernel(lam_ref, q_ref, k_ref, v_ref, bias_ref, g_ref, o_ref, vt_scr,
                 acc0, acc1, *bufs, blk):
    i = pl.program_id(2)
    n_kv = vt_scr.shape[0]
    acc = (acc0, acc1)
    sbuf = (bufs[0:2], bufs[2:4], bufs[4:6])
    pbufs = (bufs[6:8], bufs[8:10], bufs[10:12])

    @pl.when(i == 0)
    def _():
        for jb in range(n_kv):
            vt_scr[jb, 0:ATTN_V_DIM, :] = v_ref[jb * blk:(jb + 1) * blk, :].astype(F32).T.astype(BF16)
            vt_scr[jb, ATTN_V_DIM:, :] = jnp.ones((ATTN_ONES_ROWS, blk), BF16)

    qt = q_ref[...].astype(F32).T
    feat = lax.broadcasted_iota(I32, qt.shape, 0)
    qt_maps = (jnp.where(feat < ATTN_HEAD_DIM, qt, 0.0).astype(BF16),
               jnp.where(feat >= ATTN_HEAD_DIM, qt, 0.0).astype(BF16))

    def scores(j, mi):
        kj = k_ref[pl.ds(pl.multiple_of(j * blk, blk), blk), :]
        return jnp.dot(kj, qt_maps[mi], preferred_element_type=F32)

    n_chunks = blk // ATTN_ROW_CHUNK

    def rows(c):
        return slice(c * ATTN_ROW_CHUNK, (c + 1) * ATTN_ROW_CHUNK)

    def fold8(x):
        return x.reshape(ATTN_ROW_CHUNK // 8, 8, blk)

    def block(j, carry, pos, lookahead, bias_kind):
        src, dst, pbuf = sbuf[pos], sbuf[(pos + 2) % 3], pbufs[pos]
        vtj = vt_scr[j]
        out = []
        for mi in range(2):
            if lookahead:
                dst[mi][...] = scores(j + 2, mi)

            def chunk(c):
                s = src[mi][rows(c), :]
                return s if bias_kind is None else s + bias_ref[bias_kind, rows(c), :]

            m_old = carry[mi]
            m8 = jnp.max(fold8(chunk(0)), axis=0)
            for c in range(1, n_chunks):
                m8 = jnp.maximum(m8, jnp.max(fold8(chunk(c)), axis=0))
            m_new = jnp.maximum(m_old, jnp.max(m8, axis=0, keepdims=True))
            alpha = jnp.exp2(m_old - m_new)
            for c in range(n_chunks):
                pbuf[mi][rows(c), :] = jnp.exp2(chunk(c) - m_new).astype(BF16)
            acc[mi][...] = alpha * acc[mi][...] + jnp.dot(vtj, pbuf[mi][...], preferred_element_type=F32)
            out.append(m_new)
        return tuple(out)

    def far_triple(t, carry):
        for r in range(3):
            carry = block(3 * t + r, carry, r, True, None)
        return carry

    def far_single(j, carry):
        carry = block(j, carry, 0, True, None)
        for mi in range(2):
            sbuf[0][mi][...] = sbuf[1][mi][...]
        for mi in range(2):
            sbuf[1][mi][...] = sbuf[2][mi][...]
        return carry

    def near_pair(_, carry):
        return block(i, block(i - 1, carry, 0, False, 1), 1, False, 0)

    def near_single(_, carry):
        return block(i, carry, 0, False, 0)

    for mi in range(2):
        acc[mi][...] = jnp.zeros(acc[mi].shape, F32)
        sbuf[0][mi][...] = scores(0, mi)
        sbuf[1][mi][...] = scores(jnp.minimum(i, 1), mi)
    m0 = jnp.full((1, blk), -jnp.inf, F32)
    n_far = jnp.maximum(i - 1, 0)
    carry = lax.fori_loop(0, n_far // 3, far_triple, (m0, m0))
    carry = lax.fori_loop(n_far - n_far % 3, n_far, far_single, carry)
    carry = lax.fori_loop(0, jnp.minimum(i, 1), near_pair, carry)
    lax.fori_loop(0, 1 - jnp.minimum(i, 1), near_single, carry)

    lam = (jnp.exp(jnp.sum(lam_ref[0:1, :] * lam_ref[1:2, :], axis=-1, keepdims=True))
           - jnp.exp(jnp.sum(lam_ref[2:3, :] * lam_ref[3:4, :], axis=-1, keepdims=True)) + LAMBDA_INIT)
    V = ATTN_V_DIM
    ot = acc0[0:V, :] / acc0[V:V + 1, :] - lam * (acc1[0:V, :] / acc1[V:V + 1, :])
    ot = ot * lax.rsqrt(jnp.mean(ot * ot, axis=0, keepdims=True) + SUBLN_EPS)
    o_ref[...] = (ot.T * (g_ref[...] * (1.0 - LAMBDA_INIT))).astype(BF16)


def _diff_attn(q, k, v, rel_bias, lam_vecs, subln_g, batch, seq):
    T = q.shape[0]
    blk = ATTN_BLOCK
    nq = seq // blk
    bias = _attn_bias_tiles(rel_bias, blk)
    return pl.pallas_call(
        functools.partial(_attn_kernel, blk=blk),
        grid=(batch, ATTN_HEADS, nq),
        in_specs=[pl.BlockSpec((4, ATTN_HEAD_DIM), lambda b, h, i: (0, 0)),
                  pl.BlockSpec((blk, ATTN_V_DIM), lambda b, h, i: (b * nq + i, h)),
                  pl.BlockSpec((seq, ATTN_V_DIM), lambda b, h, i: (b, h)),
                  pl.BlockSpec((seq, ATTN_V_DIM), lambda b, h, i: (b, h)),
                  pl.BlockSpec((None, 2, blk, blk), lambda b, h, i: (h, 0, 0, 0)),
                  pl.BlockSpec((1, ATTN_V_DIM), lambda b, h, i: (0, 0))],
        out_specs=pl.BlockSpec((blk, ATTN_V_DIM), lambda b, h, i: (b * nq + i, h)),
        out_shape=jax.ShapeDtypeStruct((T, ATTN_WIDTH), BF16),
        scratch_shapes=[pltpu.VMEM((nq, ATTN_V_DIM + ATTN_ONES_ROWS, blk), BF16)]
        + [pltpu.VMEM((ATTN_V_DIM + ATTN_ONES_ROWS, blk), F32)] * 2 + [pltpu.VMEM((blk, blk), F32)] * 6
        + [pltpu.VMEM((blk, blk), BF16)] * 6,
        compiler_params=_params("parallel", "parallel", "arbitrary"),
        name="diff_attn",
    )(lam_vecs, q, k, v, bias, subln_g.reshape(1, ATTN_V_DIM))


def _s5_tables(lam_re, lam_im, log_step, b_re, b_im, c_re, c_im, d_skip):
    L = SSM_CHUNK
    G, P = lam_re.shape
    H = SSM_GROUP_CH
    lr = jnp.minimum(lam_re.astype(F32), SSM_EIG_MAX_RE)
    li = lam_im.astype(F32)
    step = jnp.exp(log_step.astype(F32))[:, None]
    mag = jnp.exp(lr * step)
    ang = li * step
    a_re = mag * jnp.cos(ang)
    a_im = mag * jnp.sin(ang)
    den = lr * lr + li * li
    num_re = a_re - 1.0
    coef_re = (num_re * lr + a_im * li) / den
    coef_im = (a_im * lr - num_re * li) / den
    br = b_re.astype(F32)
    bi = b_im.astype(F32)
    bb_re = coef_re[..., None] * br - coef_im[..., None] * bi
    bb_im = coef_re[..., None] * bi + coef_im[..., None] * br
    pw_re, pw_im = [jnp.ones_like(a_re)], [jnp.zeros_like(a_re)]
    for _ in range(L):
        pr, pi = pw_re[-1], pw_im[-1]
        pw_re.append(pr * a_re - pi * a_im)
        pw_im.append(pr * a_im + pi * a_re)
    pw_re = jnp.stack(pw_re)
    pw_im = jnp.stack(pw_im)
    cr = c_re.astype(F32)[None]
    ci = c_im.astype(F32)[None]
    cp_re = cr * pw_re[:, :, None, :] - ci * pw_im[:, :, None, :]
    cp_im = cr * pw_im[:, :, None, :] + ci * pw_re[:, :, None, :]
    kern = (jnp.einsum('tghp,gpk->tghk', cp_re[:L], bb_re, precision=HIGHEST)
            - jnp.einsum('tghp,gpk->tghk', cp_im[:L], bb_im, precision=HIGHEST))
    s_idx = jnp.arange(L)[:, None]
    t_idx = jnp.arange(L)[None, :]
    toep = jnp.where((t_idx >= s_idx)[:, :, None, None, None], kern[jnp.maximum(t_idx - s_idx, 0)], 0.0)
    m_tab = jnp.transpose(toep, (2, 0, 4, 1, 3)).reshape(G, L * H, L * H)
    rev_re = pw_re[L - 1::-1][:, :, None, :]
    rev_im = pw_im[L - 1::-1][:, :, None, :]
    bbt_re = jnp.transpose(bb_re, (0, 2, 1))[None]
    bbt_im = jnp.transpose(bb_im, (0, 2, 1))[None]
    bst_re = jnp.transpose(rev_re * bbt_re - rev_im * bbt_im, (1, 0, 2, 3)).reshape(G, L * H, P)
    bst_im = jnp.transpose(rev_re * bbt_im + rev_im * bbt_re, (1, 0, 2, 3)).reshape(G, L * H, P)
    cst_re = jnp.transpose(cp_re[1:], (1, 3, 0, 2)).reshape(G, P, L * H)
    cst_im = -jnp.transpose(cp_im[1:], (1, 3, 0, 2)).reshape(G, P, L * H)
    a_chunk = jnp.stack([pw_re[L], pw_im[L]], axis=1)
    d_tab = jnp.tile(d_skip.astype(F32), (1, L)).reshape(G, 1, L * H)
    return (m_tab.astype(BF16), bst_re.astype(BF16), bst_im.astype(BF16),
            cst_re.astype(BF16), cst_im.astype(BF16), a_chunk, d_tab)


def _gelu_tanh(x):
    return 0.5 * x * (1.0 + jnp.tanh(math.sqrt(2.0 / math.pi) * (x + 0.044715 * (x * x * x))))


def _s5_kernel(u_ref, m_ref, bre_ref, bim_ref, cre_ref, cim_ref, a_ref, d_ref, o_ref, xr_scr, xi_scr,
               *, batch, row_tile):
    rows = u_ref.shape[0]
    n_tiles = rows // row_tile
    for r in range(n_tiles):
        sl = pl.ds(r * row_tile, row_tile)
        u = u_ref[sl, :]
        xr_scr[sl, :] = jnp.dot(u, bre_ref[...], preferred_element_type=F32)
        xi_scr[sl, :] = jnp.dot(u, bim_ref[...], preferred_element_type=F32)
    ar = a_ref[0:1, :]
    ai = a_ref[1:2, :]

    def step(c, carry):
        xr, xi = carry
        sl = pl.ds(pl.multiple_of(c * batch, batch), batch)
        zr = xr_scr[sl, :]
        zi = xi_scr[sl, :]
        xr_scr[sl, :] = xr
        xi_scr[sl, :] = xi
        return ar * xr - ai * xi + zr, ar * xi + ai * xr + zi

    zero = jnp.zeros((batch, SSM_STATE), F32)
    lax.fori_loop(0, rows // batch, step, (zero, zero))
    for r in range(n_tiles):
        sl = pl.ds(r * row_tile, row_tile)
        u = u_ref[sl, :]
        y = (jnp.dot(u, m_ref[...], preferred_element_type=F32)
             + jnp.dot(xr_scr[sl, :].astype(BF16), cre_ref[...], preferred_element_type=F32)
             + jnp.dot(xi_scr[sl, :].astype(BF16), cim_ref[...], preferred_element_type=F32)
             + u.astype(F32) * d_ref[...])
        o_ref[sl, :] = _gelu_tanh(y).astype(BF16)


def _s5_branch(u, tables, batch, seq):
    L, G, H, P = SSM_CHUNK, SSM_GROUPS, SSM_GROUP_CH, SSM_STATE
    n_chunks = seq // L
    rows = n_chunks * batch
    LH = L * H
    ug = u.reshape(batch, n_chunks, L, G, H).transpose(3, 1, 0, 2, 4).reshape(G, rows, LH)
    m_tab, bst_re, bst_im, cst_re, cst_im, a_chunk, d_tab = tables
    grp = lambda g: (g, 0, 0)
    yg = pl.pallas_call(
        functools.partial(_s5_kernel, batch=batch, row_tile=min(rows, 1024)),
        grid=(G,),
        in_specs=[pl.BlockSpec((None, rows, LH), grp),
                  pl.BlockSpec((None, LH, LH), grp),
                  pl.BlockSpec((None, LH, P), grp),
                  pl.BlockSpec((None, LH, P), grp),
                  pl.BlockSpec((None, P, LH), grp),
                  pl.BlockSpec((None, P, LH), grp),
                  pl.BlockSpec((None, 2, P), grp),
                  pl.BlockSpec((None, 1, LH), grp)],
        out_specs=pl.BlockSpec((None, rows, LH), grp),
        out_shape=jax.ShapeDtypeStruct((G, rows, LH), BF16),
        scratch_shapes=[pltpu.VMEM((rows, P), F32), pltpu.VMEM((rows, P), F32)],
        compiler_params=_params("parallel"),
        name="s5",
    )(ug, m_tab, bst_re, bst_im, cst_re, cst_im, a_chunk, d_tab)
    return yg.reshape(G, n_chunks, batch, L, H).transpose(2, 1, 3, 0, 4).reshape(batch * seq, G * H)


def _merge_kernel(x_ref, ya_ref, ys_ref, gs_ref, mod_ref, wglu_ref, pa_ref, ps_ref, wout_ref, g2_ref,
                  wr_ref, br_ref, x1_ref, h2_ref, ri_ref, rw_ref, cnt_ref):
    tm, D = x_ref.shape
    gl = jnp.dot(ys_ref[...], wglu_ref[...], preferred_element_type=F32)
    y_ssm = gl[:, :SSM_WIDTH] * jax.nn.sigmoid(gl[:, SSM_WIDTH:])
    p_attn = jnp.dot(ya_ref[...], pa_ref[...], preferred_element_type=F32)
    p_ssm = jnp.dot(y_ssm.astype(BF16), ps_ref[...], preferred_element_type=F32)
    merged = gs_ref[:, :D].astype(F32) * p_attn + gs_ref[:, D:].astype(F32) * p_ssm
    mixed = jnp.dot(merged.astype(BF16), wout_ref[...], preferred_element_type=F32)
    x1 = x_ref[...] + mod_ref[2:3, :] * mixed
    x1_ref[...] = x1
    h2 = _rms(x1, NORM_EPS) * g2_ref[...] * (1.0 + mod_ref[4:5, :]) + mod_ref[3:4, :]
    h2_ref[...] = h2.astype(BF16)

    logits = lax.dot_general(wr_ref[...], h2, (((1,), (1,)), ((), ())),
                             preferred_element_type=F32, precision=HIGHEST) + br_ref[...]
    NG, EPG = MOE_GROUPS, MOE_EXPERTS_PER_GROUP
    lg = logits[0:NG, :]
    g_iota = lax.broadcasted_iota(I32, lg.shape, 0)
    lg_max = jnp.max(lg, axis=0, keepdims=True)
    grp = jnp.min(jnp.where(lg == lg_max, g_iota, NG), axis=0, keepdims=True)
    p_grp = 1.0 / jnp.sum(jnp.exp(lg - lg_max), axis=0, keepdims=True)
    le = logits[NG:NG + EPG, :]
    for g in range(1, NG):
        le = jnp.where(grp == g, logits[NG + g * EPG:NG + (g + 1) * EPG, :], le)
    e_iota = lax.broadcasted_iota(I32, le.shape, 0)
    v1 = jnp.max(le, axis=0, keepdims=True)
    i1 = jnp.min(jnp.where(le == v1, e_iota, EPG), axis=0, keepdims=True)
    le2 = jnp.where(e_iota == i1, -jnp.inf, le)
    v2 = jnp.max(le2, axis=0, keepdims=True)
    i2 = jnp.min(jnp.where(le2 == v2, e_iota, EPG), axis=0, keepdims=True)
    e21 = jnp.exp(v2 - v1)
    w1 = p_grp / (1.0 + e21)
    w2 = p_grp * e21 / (1.0 + e21)
    eid1 = grp * EPG + i1
    eid2 = grp * EPG + i2

    x_iota = lax.broadcasted_iota(I32, (MOE_EXPERTS, tm), 0)
    hot1 = x_iota == eid1
    hot2 = x_iota == eid2
    hot = jnp.logical_or(hot1, hot2).astype(F32)
    before = (lax.broadcasted_iota(I32, (tm, tm), 0) < lax.broadcasted_iota(I32, (tm, tm), 1))
    prior = jnp.dot(hot.astype(BF16), before.astype(BF16), preferred_element_type=F32)
    rank1 = jnp.sum(jnp.where(hot1, prior, 0.0), axis=0, keepdims=True)
    rank2 = jnp.sum(jnp.where(hot2, prior, 0.0), axis=0, keepdims=True)
    cnt_ref[...] = jnp.sum(hot, axis=1, keepdims=True).astype(I32)

    zi = jnp.zeros((4, tm), I32)
    ri_ref[...] = jnp.concatenate([eid1, eid2, rank1.astype(I32), rank2.astype(I32), zi], axis=0)
    rw_ref[...] = jnp.concatenate([w1, w2, jnp.zeros((6, tm), F32)], axis=0)


def _merge_route(x2, ya, ys, gs, mod, w_glu, w_pa, w_ps, w_out, norm2_g, w_rg, b_rg, w_re, b_re, seq):
    T, D = x2.shape
    tm = TOKEN_TILE
    per_b = seq // tm
    wr = jnp.concatenate([w_rg.T, jnp.transpose(w_re, (0, 2, 1)).reshape(MOE_EXPERTS, D),
                          jnp.zeros((ROUTER_ROWS - MOE_GROUPS - MOE_EXPERTS, D), F32)], axis=0).astype(F32)
    br = jnp.concatenate([b_rg, b_re.reshape(-1),
                          jnp.zeros((ROUTER_ROWS - MOE_GROUPS - MOE_EXPERTS,), F32)]).reshape(ROUTER_ROWS, 1)
    row = lambda i: (i, 0)
    col = lambda i: (0, i)
    full = lambda i: (0, 0)
    return pl.pallas_call(
        _merge_kernel,
        grid=(T // tm,),
        in_specs=[pl.BlockSpec((tm, D), row),
                  pl.BlockSpec((tm, ATTN_WIDTH), row),
                  pl.BlockSpec((tm, SSM_WIDTH), row),
                  pl.BlockSpec((tm, 2 * D), row),
                  pl.BlockSpec((None, 6, D), lambda i: (i // per_b, 0, 0)),
                  pl.BlockSpec(w_glu.shape, full),
                  pl.BlockSpec(w_pa.shape, full),
                  pl.BlockSpec(w_ps.shape, full),
                  pl.BlockSpec(w_out.shape, full),
                  pl.BlockSpec((1, D), full),
                  pl.BlockSpec((ROUTER_ROWS, D), full),
                  pl.BlockSpec((ROUTER_ROWS, 1), full)],
        out_specs=[pl.BlockSpec((tm, D), row), pl.BlockSpec((tm, D), row),
                   pl.BlockSpec((8, tm), col), pl.BlockSpec((8, tm), col),
                   pl.BlockSpec((None, MOE_EXPERTS, 1), lambda i: (i, 0, 0))],
        out_shape=[jax.ShapeDtypeStruct((T, D), F32), jax.ShapeDtypeStruct((T, D), BF16),
                   jax.ShapeDtypeStruct((8, T), I32), jax.ShapeDtypeStruct((8, T), F32),
                   jax.ShapeDtypeStruct((T // tm, MOE_EXPERTS, 1), I32)],
        compiler_params=_params("parallel"),
        name="merge_route",
    )(x2, ya, ys, gs, mod, w_glu.astype(BF16), w_pa.astype(BF16), w_ps.astype(BF16), w_out.astype(BF16),
      norm2_g.reshape(1, D), wr, br)


def _tile_positions(ri_ref, seg_ref):
    tm = ri_ref.shape[1]
    x_iota = lax.broadcasted_iota(I32, (MOE_EXPERTS, tm), 0)
    seg = seg_ref[...].astype(F32)
    pos = []
    for k in range(2):
        start = jnp.sum(jnp.where(x_iota == ri_ref[k:k + 1, :], seg, 0.0), axis=0, keepdims=True)
        pos.append(start + ri_ref[2 + k:3 + k, :].astype(F32))
    return pos


def _segment_copies(meta, tile, make_copy):
    seg_row, dst_row, n_chunk = meta

    def per_expert(e, carry):
        idx = tile * MOE_EXPERTS + e
        src0 = seg_row[idx]
        dst0 = dst_row[idx]

        def per_chunk(c, carry2):
            off = c * SEG_ALIGN
            make_copy(pl.multiple_of(src0 + off, SEG_ALIGN), pl.multiple_of(dst0 + off, SEG_ALIGN))
            return carry2

        return lax.fori_loop(0, n_chunk[idx], per_chunk, carry)

    lax.fori_loop(0, MOE_EXPERTS, per_expert, 0)


def _dispatch_kernel(seg_row, dst_row, n_chunk, tile_chunks, pad_row, pad_chunks, n_used,
                     h_ref, ri_ref, rw_ref, seg_ref, xs_ref, pw_ref, zbuf, zeros_scr, sem, pad_sem, tail_sem):
    i = pl.program_id(0)
    n_tiles = pl.num_programs(0)
    slot = i % 2
    tm = h_ref.shape[0]
    pos1, pos2 = _tile_positions(ri_ref, seg_ref)
    pw_ref[...] = jnp.concatenate([pos1, pos2, rw_ref[0:2, :], jnp.zeros((4, tm), F32)], axis=0)
    r_iota = lax.broadcasted_iota(I32, (TILE_SLOTS, tm), 0).astype(F32)
    onehot = jnp.logical_or(r_iota == pos1, r_iota == pos2).astype(BF16)
    zbuf[slot] = jnp.dot(onehot, h_ref[...], preferred_element_type=F32)

    def chunk_copy(buf_slot, src, dst):
        return pltpu.make_async_copy(zbuf.at[buf_slot, pl.ds(src, SEG_ALIGN)],
                                     xs_ref.at[pl.ds(dst, SEG_ALIGN)], sem.at[buf_slot])

    _segment_copies((seg_row, dst_row, n_chunk), i, lambda src, dst: chunk_copy(slot, src, dst).start())

    def wait_tile(tile, buf_slot):
        def body(c, carry):
            chunk_copy(buf_slot, 0, 0).wait()
            return carry
        lax.fori_loop(0, tile_chunks[tile], body, 0)

    @pl.when(i > 0)
    def _():
        wait_tile(i - 1, 1 - slot)

    @pl.when(i == n_tiles - 1)
    def _():
        zeros_scr[...] = jnp.zeros(zeros_scr.shape, F32)

        def pad_copy(dst):
            return pltpu.make_async_copy(zeros_scr.at[pl.ds(0, SEG_ALIGN)], xs_ref.at[pl.ds(dst, SEG_ALIGN)], pad_sem)

        def tail_copy(dst):
            return pltpu.make_async_copy(zeros_scr, xs_ref.at[pl.ds(dst, MOE_BLOCK)], tail_sem)

        def per_expert(e, total):
            def per_chunk(c, carry):
                pad_copy(pl.multiple_of(pad_row[e] + c * SEG_ALIGN, SEG_ALIGN)).start()
                return carry
            lax.fori_loop(0, pad_chunks[e], per_chunk, 0)
            return total + pad_chunks[e]

        n_pad_copies = lax.fori_loop(0, MOE_EXPERTS, per_expert, 0)
        n_blocks = xs_ref.shape[0] // MOE_BLOCK

        def tail_start(b, carry):
            tail_copy(pl.multiple_of(b * MOE_BLOCK, MOE_BLOCK)).start()
            return carry
        lax.fori_loop(n_used[0], n_blocks, tail_start, 0)
        wait_tile(i, slot)

        def wait_pad(c, carry):
            pad_copy(0).wait()
            return carry
        lax.fori_loop(0, n_pad_copies, wait_pad, 0)

        def wait_tail(b, carry):
            tail_copy(0).wait()
            return carry
        lax.fori_loop(n_used[0], n_blocks, wait_tail, 0)


def _dispatch(meta, h2, ri, rw, seg_start, n_rows):
    T, D = h2.shape
    tm = TOKEN_TILE
    col = lambda i, *_: (0, i)
    return pl.pallas_call(
        _dispatch_kernel,
        grid_spec=pltpu.PrefetchScalarGridSpec(
            num_scalar_prefetch=7,
            grid=(T // tm,),
            in_specs=[pl.BlockSpec((tm, D), lambda i, *_: (i, 0)),
                      pl.BlockSpec((8, tm), col),
                      pl.BlockSpec((8, tm), col),
                      pl.BlockSpec((None, MOE_EXPERTS, 1), lambda i, *_: (i, 0, 0))],
            out_specs=[pl.BlockSpec(memory_space=pl.ANY), pl.BlockSpec((8, tm), col)],
            scratch_shapes=[pltpu.VMEM((2, TILE_SLOTS, D), F32), pltpu.VMEM((MOE_BLOCK, D), F32),
                            pltpu.SemaphoreType.DMA((2,)), pltpu.SemaphoreType.DMA(()),
                            pltpu.SemaphoreType.DMA(())]),
        out_shape=[jax.ShapeDtypeStruct((n_rows, D), F32), jax.ShapeDtypeStruct((8, T), F32)],
        compiler_params=_params("arbitrary"),
        name="moe_dispatch",
    )(*meta, h2, ri, rw, seg_start)


def _expert_kernel(be_ref, nb_ref, x_ref, wi_ref, wo_ref, o_ref):
    del be_ref
    F = wo_ref.shape[0]

    @pl.when(pl.program_id(0) < nb_ref[0])
    def _():
        hid = jnp.dot(x_ref[...].astype(BF16), wi_ref[...], preferred_element_type=F32)
        a = hid[:, :F]
        act = a * jax.nn.sigmoid(a) * hid[:, F:]
        o_ref[...] = jnp.dot(act.astype(BF16), wo_ref[...], preferred_element_type=F32)

    @pl.when(pl.program_id(0) >= nb_ref[0])
    def _():
        o_ref[...] = jnp.zeros(o_ref.shape, F32)


def _experts(block_e, n_used, xs, w_e_in, w_e_out):
    n_rows, D = xs.shape
    F = w_e_out.shape[1]
    blk = lambda i, be, nb: jnp.maximum(jnp.minimum(i, nb[0] - 1), 0)
    return pl.pallas_call(
        _expert_kernel,
        grid_spec=pltpu.PrefetchScalarGridSpec(
            num_scalar_prefetch=2,
            grid=(n_rows // MOE_BLOCK,),
            in_specs=[pl.BlockSpec((MOE_BLOCK, D), lambda i, be, nb: (blk(i, be, nb), 0)),
                      pl.BlockSpec((None, D, 2 * F), lambda i, be, nb: (be[blk(i, be, nb)], 0, 0)),
                      pl.BlockSpec((None, F, D), lambda i, be, nb: (be[blk(i, be, nb)], 0, 0))],
            out_specs=pl.BlockSpec((MOE_BLOCK, D), lambda i, be, nb: (i, 0))),
        out_shape=jax.ShapeDtypeStruct((n_rows, D), F32),
        compiler_params=_params("arbitrary"),
        name="experts",
    )(block_e, n_used, xs, w_e_in.astype(BF16), w_e_out.astype(BF16))


def _final_kernel(seg_row, dst_row, n_chunk, tile_chunks, x1_ref, pw_ref, mod_ref, g_ref, ys_ref, o_ref,
                  ybuf, sem):
    i = pl.program_id(0)
    n_tiles = pl.num_programs(0)
    slot = i % 2
    tm = x1_ref.shape[0]

    def chunk_copy(buf_slot, src, dst):
        return pltpu.make_async_copy(ys_ref.at[pl.ds(dst, SEG_ALIGN)],
                                     ybuf.at[buf_slot, pl.ds(src, SEG_ALIGN)], sem.at[buf_slot])

    def fetch(tile, buf_slot):
        _segment_copies((seg_row, dst_row, n_chunk), tile, lambda src, dst: chunk_copy(buf_slot, src, dst).start())

    @pl.when(i == 0)
    def _():
        ybuf[...] = jnp.zeros(ybuf.shape, F32)
        fetch(0, 0)

    @pl.when(i + 1 < n_tiles)
    def _():
        fetch(i + 1, 1 - slot)

    def wait_one(c, carry):
        chunk_copy(slot, 0, 0).wait()
        return carry
    lax.fori_loop(0, tile_chunks[i], wait_one, 0)

    s_iota = lax.broadcasted_iota(I32, (tm, TILE_SLOTS), 1).astype(F32)
    comb = (jnp.where(s_iota == pw_ref[:, 0:1], pw_ref[:, 2:3], 0.0)
            + jnp.where(s_iota == pw_ref[:, 1:2], pw_ref[:, 3:4], 0.0))
    moe = jnp.dot(comb.astype(BF16), ybuf[slot].astype(BF16), preferred_element_type=F32)
    x2 = x1_ref[...] + mod_ref[5:6, :] * moe
    o_ref[...] = _rms(x2, NORM_EPS) * g_ref[...]


def _final(meta, x1, pw_tok, mod, final_g, ys, seq):
    T, D = x1.shape
    tm = TOKEN_TILE
    per_b = seq // tm
    row = lambda i, *_: (i, 0)
    return pl.pallas_call(
        _final_kernel,
        grid_spec=pltpu.PrefetchScalarGridSpec(
            num_scalar_prefetch=4,
            grid=(T // tm,),
            in_specs=[pl.BlockSpec((tm, D), row),
                      pl.BlockSpec((tm, 8), row),
                      pl.BlockSpec((None, 6, D), lambda i, *_: (i // per_b, 0, 0)),
                      pl.BlockSpec((1, D), lambda i, *_: (0, 0)),
                      pl.BlockSpec(memory_space=pl.ANY)],
            out_specs=pl.BlockSpec((tm, D), row),
            scratch_shapes=[pltpu.VMEM((2, TILE_SLOTS, D), F32), pltpu.SemaphoreType.DMA((2,))]),
        out_shape=jax.ShapeDtypeStruct((T, D), F32),
        compiler_params=_params("arbitrary"),
        name="final",
    )(*meta, x1, pw_tok, mod, final_g.reshape(1, D), ys)


def _round_up(x, m):
    return (x + m - 1) // m * m


def _moe_layout(tile_counts):
    n_tiles = tile_counts.shape[0]
    seg = _round_up(tile_counts, SEG_ALIGN)
    seg_start = jnp.cumsum(seg, axis=1) - seg
    tile_base = jnp.cumsum(seg, axis=0) - seg
    used = jnp.sum(seg, axis=0)
    region = _round_up(used, MOE_BLOCK)
    region_end = jnp.cumsum(region)
    region_start = region_end - region
    dst_row = region_start[None, :] + tile_base
    n_chunk = seg // SEG_ALIGN
    n_assign = 2 * n_tiles * TOKEN_TILE
    n_rows = _round_up(n_assign + n_tiles * MOE_EXPERTS * (SEG_ALIGN - 1) + MOE_EXPERTS * (MOE_BLOCK - 1), MOE_BLOCK)
    block_start = jnp.arange(n_rows // MOE_BLOCK, dtype=I32) * MOE_BLOCK
    block_e = jnp.minimum(jnp.sum(block_start[:, None] >= region_end[None, :], axis=1), MOE_EXPERTS - 1)
    i32 = lambda a: a.reshape(-1).astype(I32)
    meta = (i32(seg_start), i32(dst_row), i32(n_chunk), i32(jnp.sum(n_chunk, axis=1)))
    pad = (i32(region_start + used), i32((region - used) // SEG_ALIGN))
    n_used = (region_end[-1:] // MOE_BLOCK).astype(I32)
    return meta, pad, seg_start.astype(I32)[:, :, None], block_e.astype(I32), n_used, n_rows


def kernel(x, c, w_ada, b_ada, norm1_g, w_in, rel_bias, lambda_q1, lambda_k1, lambda_q2, lambda_k2, subln_g, ssm_lambda_re, ssm_lambda_im, ssm_log_step, ssm_b_re, ssm_b_im, ssm_c_re, ssm_c_im, ssm_d, w_glu, w_proj_attn, w_proj_ssm, w_out, norm2_g, w_router_group, b_router_group, w_router_expert, b_router_expert, w_expert_in, w_expert_out, final_g):
    B, S, D = x.shape
    T = B * S
    x2 = x.reshape(T, D)
    mod = _ada_mod(c, w_ada[0], b_ada[0]).reshape(B, 6, D)
    q, k, v, u, gs = _in_proj(x2, mod, norm1_g[0], w_in[0], S)
    lam_vecs = jnp.stack([lambda_q1[0], lambda_k1[0], lambda_q2[0], lambda_k2[0]]).astype(F32)
    y_attn = _diff_attn(q, k, v, rel_bias, lam_vecs, subln_g[0], B, S)
    tables = _s5_tables(ssm_lambda_re[0], ssm_lambda_im[0], ssm_log_step[0], ssm_b_re[0], ssm_b_im[0],
                        ssm_c_re[0], ssm_c_im[0], ssm_d[0])
    y_s5 = _s5_branch(u, tables, B, S)
    x1, h2, ri, rw, tile_counts = _merge_route(
        x2, y_attn, y_s5, gs, mod, w_glu[0], w_proj_attn[0], w_proj_ssm[0], w_out[0], norm2_g[0],
        w_router_group[0], b_router_group[0], w_router_expert[0], b_router_expert[0], S)
    meta, pad, seg_start, block_e, n_used, n_rows = _moe_layout(tile_counts[:, :, 0])
    xs, pw = _dispatch(meta + pad + (n_used,), h2, ri, rw, seg_start, n_rows)
    ys = _experts(block_e, n_used, xs, w_expert_in[0], w_expert_out[0])
    out = _final(meta, x1, pw.T, mod, final_g, ys, S)
    return out.reshape(B, S, D)
```

```python
import functools
import math

import jax
import jax.numpy as jnp
from jax import lax
from jax.experimental import pallas as pl
from jax.experimental.pallas import tpu as pltpu

F32 = jnp.float32
BF16 = jnp.bfloat16
I32 = jnp.int32
HIGHEST = lax.Precision.HIGHEST

ATTN_HEADS = 4
ATTN_HEAD_DIM = 64
ATTN_V_DIM = 2 * ATTN_HEAD_DIM
ATTN_WIDTH = ATTN_HEADS * ATTN_V_DIM
NEG_INF = -1e30
REL_BUCKETS = 32
REL_MAX_DISTANCE = 128
SSM_GROUP_CH = 16
SSM_WIDTH = 512
SSM_GROUPS = SSM_WIDTH // SSM_GROUP_CH
SSM_STATE = 64
SSM_EIG_MAX_RE = -1e-4
SSM_CHUNK = 16
S5_CHUNKS_PER_STEP = 32
MOE_GROUPS = 4
MOE_EXPERTS_PER_GROUP = 8
MOE_EXPERTS = MOE_GROUPS * MOE_EXPERTS_PER_GROUP
MOE_BLOCK = 256
SEG_ALIGN = 8
NORM_EPS = 1e-6
SUBLN_EPS = 1e-5
LAMBDA_INIT = 0.8 - 0.6 * math.exp(-0.3 * 0)

ATTN_BLOCK = 256
ATTN_ROW_CHUNK = 32
ATTN_ONES_ROWS = 16
LOG2E = math.log2(math.e)
TOKEN_TILE = 512
ROUTER_ROWS = 40
TILE_SLOTS = -(-(2 * TOKEN_TILE + MOE_EXPERTS * (SEG_ALIGN - 1)) // 256) * 256
VMEM_LIMIT = 56 << 20


def _params(*sem):
    return pltpu.CompilerParams(dimension_semantics=sem, vmem_limit_bytes=VMEM_LIMIT)


def _rms(x, eps):
    return x * lax.rsqrt(jnp.mean(x * x, axis=-1, keepdims=True) + eps)


def _mod_kernel(c_ref, w_ref, b_ref, o_ref):
    c = c_ref[...]
    c_act = c * jax.nn.sigmoid(c)
    o_ref[...] = jnp.dot(c_act, w_ref[...], preferred_element_type=F32, precision=HIGHEST) + b_ref[...]


def _ada_mod(c, w_ada, b_ada):
    B, D = c.shape
    N = w_ada.shape[1]
    tn = 1024
    return pl.pallas_call(
        _mod_kernel,
        grid=(N // tn,),
        in_specs=[pl.BlockSpec((B, D), lambda j: (0, 0)),
                  pl.BlockSpec((D, tn), lambda j: (0, j)),
                  pl.BlockSpec((1, tn), lambda j: (0, j))],
        out_specs=pl.BlockSpec((B, tn), lambda j: (0, j)),
        out_shape=jax.ShapeDtypeStruct((B, N), F32),
        compiler_params=_params("arbitrary"),
        name="ada_mod",
    )(c, w_ada, b_ada.reshape(1, N))


def _proj_kernel(x_ref, mod_ref, g_ref, w_ref, q_ref, k_ref, v_ref, u_ref, gs_ref):
    y = _rms(x_ref[...], NORM_EPS) * g_ref[...]
    h = (y * (1.0 + mod_ref[1:2, :]) + mod_ref[0:1, :]).astype(BF16)
    W = ATTN_WIDTH

    def proj(lo, hi):
        return jnp.dot(h, w_ref[:, lo:hi], preferred_element_type=F32)

    q_ref[...] = (proj(0, W) * (ATTN_HEAD_DIM ** -0.5 * LOG2E)).astype(BF16)
    k_ref[...] = proj(W, 2 * W).astype(BF16)
    v_ref[...] = proj(2 * W, 3 * W).astype(BF16)
    u_ref[...] = proj(3 * W, 3 * W + SSM_WIDTH).reshape(u_ref.shape)
    gs_ref[...] = jax.nn.sigmoid(proj(3 * W + SSM_WIDTH, w_ref.shape[1])).astype(BF16)


def _chunk_major_spec(tm, per_b):
    return pl.BlockSpec((tm // SSM_CHUNK, None, SSM_CHUNK, SSM_WIDTH), lambda i: (i % per_b, i // per_b, 0, 0))


def _in_proj(x2, mod, norm_g, w_in, seq):
    T, D = x2.shape
    tm = TOKEN_TILE
    per_b = seq // tm
    n_gate = w_in.shape[1] - 3 * ATTN_WIDTH - SSM_WIDTH
    row = lambda i: (i, 0)
    return pl.pallas_call(
        _proj_kernel,
        grid=(T // tm,),
        in_specs=[pl.BlockSpec((tm, D), row),
                  pl.BlockSpec((None, 6, D), lambda i: (i // per_b, 0, 0)),
                  pl.BlockSpec((1, D), lambda i: (0, 0)),
                  pl.BlockSpec(w_in.shape, lambda i: (0, 0))],
        out_specs=[pl.BlockSpec((tm, ATTN_WIDTH), row)] * 3
        + [_chunk_major_spec(tm, per_b), pl.BlockSpec((tm, n_gate), row)],
        out_shape=[jax.ShapeDtypeStruct((T, ATTN_WIDTH), BF16)] * 3
        + [jax.ShapeDtypeStruct((seq // SSM_CHUNK, T // seq, SSM_CHUNK, SSM_WIDTH), F32),
           jax.ShapeDtypeStruct((T, n_gate), BF16)],
        compiler_params=_params("parallel"),
        name="in_proj",
    )(x2, mod, norm_g.reshape(1, D), w_in.astype(BF16))


def _rel_bucket(dist):
    max_exact = REL_BUCKETS // 2
    n = jnp.maximum(dist, 0)
    log_ratio = jnp.log(jnp.maximum(n, 1).astype(F32) / max_exact) / math.log(REL_MAX_DISTANCE / max_exact)
    large = max_exact + (log_ratio * (REL_BUCKETS - max_exact)).astype(I32)
    large = jnp.minimum(large, REL_BUCKETS - 1)
    return jnp.where(n < max_exact, n, large)


def _attn_bias_tiles(rel_bias, blk):
    assert blk >= REL_MAX_DISTANCE
    n_heads = rel_bias.shape[1]
    far = rel_bias[REL_BUCKETS - 1].astype(F32)
    m = jnp.arange(2 * blk)
    signed = jnp.where(m < blk, m, m - 2 * blk)
    tiles = []
    for kind in range(2):
        dist = kind * blk + signed
        tab = jnp.where(dist >= 0, (rel_bias[_rel_bucket(dist)].astype(F32).T - far[:, None]) * LOG2E, NEG_INF)
        skew = jnp.tile(tab, (1, blk))[:, :blk * (2 * blk - 1)].reshape(n_heads, blk, 2 * blk - 1)
        tiles.append(skew[:, :, :blk])
    return jnp.stack(tiles, axis=1)


def _attn_kernel(lam_ref, q_ref, k_ref, v_ref, bias_ref, g_ref, o_ref, vt_scr,
                 acc0, acc1, *bufs, blk):
    i = pl.program_id(2)
    n_kv = vt_scr.shape[0]
    acc = (acc0, acc1)
    sbuf = (bufs[0:2], bufs[2:4], bufs[4:6])
    pbufs = (bufs[6:8], bufs[8:10], bufs[10:12])

    @pl.when(i == 0)
    def _():
        for jb in range(n_kv):
            vt_scr[jb, 0:ATTN_V_DIM, :] = v_ref[jb * blk:(jb + 1) * blk, :].astype(F32).T.astype(BF16)
            vt_scr[jb, ATTN_V_DIM:, :] = jnp.ones((ATTN_ONES_ROWS, blk), BF16)

    qt = q_ref[...].astype(F32).T
    feat = lax.broadcasted_iota(I32, qt.shape, 0)
    qt_maps = (jnp.where(feat < ATTN_HEAD_DIM, qt, 0.0).astype(BF16),
               jnp.where(feat >= ATTN_HEAD_DIM, qt, 0.0).astype(BF16))

    def scores(j, mi):
        kj = k_ref[pl.ds(pl.multiple_of(j * blk, blk), blk), :]
        return jnp.dot(kj, qt_maps[mi], preferred_element_type=F32)

    n_chunks = blk // ATTN_ROW_CHUNK

    def rows(c):
        return slice(c * ATTN_ROW_CHUNK, (c + 1) * ATTN_ROW_CHUNK)

    def fold8(x):
        return x.reshape(ATTN_ROW_CHUNK // 8, 8, blk)

    def block(j, carry, pos, lookahead, bias_kind):
        src, dst, pbuf = sbuf[pos], sbuf[(pos + 2) % 3], pbufs[pos]
        vtj = vt_scr[j]
        out = []
        for mi in range(2):
            if lookahead:
                dst[mi][...] = scores(j + 2, mi)

            def chunk(c):
                s = src[mi][rows(c), :]
                return s if bias_kind is None else s + bias_ref[bias_kind, rows(c), :]

            m_old = carry[mi]
            m8 = jnp.max(fold8(chunk(0)), axis=0)
            for c in range(1, n_chunks):
                m8 = jnp.maximum(m8, jnp.max(fold8(chunk(c)), axis=0))
            m_new = jnp.maximum(m_old, jnp.max(m8, axis=0, keepdims=True))
            alpha = jnp.exp2(m_old - m_new)
            for c in range(n_chunks):
                pbuf[mi][rows(c), :] = jnp.exp2(chunk(c) - m_new).astype(BF16)
            acc[mi][...] = alpha * acc[mi][...] + jnp.dot(vtj, pbuf[mi][...], preferred_element_type=F32)
            out.append(m_new)
        return tuple(out)

    def far_triple(t, carry):
        for r in range(3):
            carry = block(3 * t + r, carry, r, True, None)
        return carry

    def far_single(j, carry):
        carry = block(j, carry, 0, True, None)
        for mi in range(2):
            sbuf[0][mi][...] = sbuf[1][mi][...]
        for mi in range(2):
            sbuf[1][mi][...] = sbuf[2][mi][...]
        return carry

    def near_pair(_, carry):
        return block(i, block(i - 1, carry, 0, False, 1), 1, False, 0)

    def near_single(_, carry):
        return block(i, carry, 0, False, 0)

    for mi in range(2):
        acc[mi][...] = jnp.zeros(acc[mi].shape, F32)
        sbuf[0][mi][...] = scores(0, mi)
        sbuf[1][mi][...] = scores(jnp.minimum(i, 1), mi)
    m0 = jnp.full((1, blk), -jnp.inf, F32)
    n_far = jnp.maximum(i - 1, 0)
    carry = lax.fori_loop(0, n_far // 3, far_triple, (m0, m0))
    carry = lax.fori_loop(n_far - n_far % 3, n_far, far_single, carry)
    carry = lax.fori_loop(0, jnp.minimum(i, 1), near_pair, carry)
    lax.fori_loop(0, 1 - jnp.minimum(i, 1), near_single, carry)

    lam = (jnp.exp(jnp.sum(lam_ref[0:1, :] * lam_ref[1:2, :], axis=-1, keepdims=True))
           - jnp.exp(jnp.sum(lam_ref[2:3, :] * lam_ref[3:4, :], axis=-1, keepdims=True)) + LAMBDA_INIT)
    V = ATTN_V_DIM
    ot = acc0[0:V, :] / acc0[V:V + 1, :] - lam * (acc1[0:V, :] / acc1[V:V + 1, :])
    ot = ot * lax.rsqrt(jnp.mean(ot * ot, axis=0, keepdims=True) + SUBLN_EPS)
    o_ref[...] = (ot.T * (g_ref[...] * (1.0 - LAMBDA_INIT))).astype(BF16)


def _diff_attn(q, k, v, rel_bias, lam_vecs, subln_g, batch, seq):
    T = q.shape[0]
    blk = ATTN_BLOCK
    nq = seq // blk
    bias = _attn_bias_tiles(rel_bias, blk)
    return pl.pallas_call(
        functools.partial(_attn_kernel, blk=blk),
        grid=(batch, ATTN_HEADS, nq),
        in_specs=[pl.BlockSpec((4, ATTN_HEAD_DIM), lambda b, h, i: (0, 0)),
                  pl.BlockSpec((blk, ATTN_V_DIM), lambda b, h, i: (b * nq + i, h)),
                  pl.BlockSpec((seq, ATTN_V_DIM), lambda b, h, i: (b, h)),
                  pl.BlockSpec((seq, ATTN_V_DIM), lambda b, h, i: (b, h)),
                  pl.BlockSpec((None, 2, blk, blk), lambda b, h, i: (h, 0, 0, 0)),
                  pl.BlockSpec((1, ATTN_V_DIM), lambda b, h, i: (0, 0))],
        out_specs=pl.BlockSpec((blk, ATTN_V_DIM), lambda b, h, i: (b * nq + i, h)),
        out_shape=jax.ShapeDtypeStruct((T, ATTN_WIDTH), BF16),
        scratch_shapes=[pltpu.VMEM((nq, ATTN_V_DIM + ATTN_ONES_ROWS, blk), BF16)]
        + [pltpu.VMEM((ATTN_V_DIM + ATTN_ONES_ROWS, blk), F32)] * 2 + [pltpu.VMEM((blk, blk), F32)] * 6
        + [pltpu.VMEM((blk, blk), BF16)] * 6,
        compiler_params=_params("parallel", "parallel", "arbitrary"),
        name="diff_attn",
    )(lam_vecs, q, k, v, bias, subln_g.reshape(1, ATTN_V_DIM))


def _s5_tables(lam_re, lam_im, log_step, b_re, b_im, c_re, c_im, d_skip):
    L = SSM_CHUNK
    G, P = lam_re.shape
    H = SSM_GROUP_CH
    lr = jnp.minimum(lam_re.astype(F32), SSM_EIG_MAX_RE)
    li = lam_im.astype(F32)
    step = jnp.exp(log_step.astype(F32))[:, None]
    mag = jnp.exp(lr * step)
    ang = li * step
    a_re = mag * jnp.cos(ang)
    a_im = mag * jnp.sin(ang)
    den = lr * lr + li * li
    num_re = a_re - 1.0
    coef_re = (num_re * lr + a_im * li) / den
    coef_im = (a_im * lr - num_re * li) / den
    br = b_re.astype(F32)
    bi = b_im.astype(F32)
    bb_re = coef_re[..., None] * br - coef_im[..., None] * bi
    bb_im = coef_re[..., None] * bi + coef_im[..., None] * br
    pw_re, pw_im = [jnp.ones_like(a_re)], [jnp.zeros_like(a_re)]
    for _ in range(L):
        pr, pi = pw_re[-1], pw_im[-1]
        pw_re.append(pr * a_re - pi * a_im)
        pw_im.append(pr * a_im + pi * a_re)
    pw_re = jnp.stack(pw_re)
    pw_im = jnp.stack(pw_im)
    cr = c_re.astype(F32)[None]
    ci = c_im.astype(F32)[None]
    cp_re = cr * pw_re[:, :, None, :] - ci * pw_im[:, :, None, :]
    cp_im = cr * pw_im[:, :, None, :] + ci * pw_re[:, :, None, :]
    kern = (jnp.einsum('tghp,gpk->tghk', cp_re[:L], bb_re, precision=HIGHEST)
            - jnp.einsum('tghp,gpk->tghk', cp_im[:L], bb_im, precision=HIGHEST))
    s_idx = jnp.arange(L)[:, None]
    t_idx = jnp.arange(L)[None, :]
    toep = jnp.where((t_idx >= s_idx)[:, :, None, None, None], kern[jnp.maximum(t_idx - s_idx, 0)], 0.0)
    m_tab = jnp.transpose(toep, (2, 0, 4, 1, 3)).reshape(G, L * H, L * H)
    rev_re = pw_re[L - 1::-1][:, :, None, :]
    rev_im = pw_im[L - 1::-1][:, :, None, :]
    bbt_re = jnp.transpose(bb_re, (0, 2, 1))[None]
    bbt_im = jnp.transpose(bb_im, (0, 2, 1))[None]
    bst_re = jnp.transpose(rev_re * bbt_re - rev_im * bbt_im, (1, 0, 2, 3)).reshape(G, L * H, P)
    bst_im = jnp.transpose(rev_re * bbt_im + rev_im * bbt_re, (1, 0, 2, 3)).reshape(G, L * H, P)
    cst_re = jnp.transpose(cp_re[1:], (1, 3, 0, 2)).reshape(G, P, L * H)
    cst_im = -jnp.transpose(cp_im[1:], (1, 3, 0, 2)).reshape(G, P, L * H)
    a_chunk = jnp.stack([jnp.concatenate([pw_re[L], pw_re[L]], axis=-1),
                         jnp.concatenate([-pw_im[L], pw_im[L]], axis=-1)], axis=1)
    d_tab = jnp.tile(d_skip.astype(F32), (1, L)).reshape(G, 1, L * H)
    bst = jnp.concatenate([bst_re, bst_im], axis=-1)
    cst = jnp.concatenate([cst_re, cst_im], axis=1)
    return m_tab.astype(BF16), bst.astype(BF16), cst.astype(BF16), a_chunk, d_tab


def _gelu_tanh(x):
    return 0.5 * x * (1.0 + jnp.tanh(math.sqrt(2.0 / math.pi) * (x + 0.044715 * (x * x * x))))


def _lane_block_transpose(arrs):
    n = len(arrs)
    width = arrs[0].shape[1]
    blk_id = lax.broadcasted_iota(I32, arrs[0].shape, 1) // SSM_GROUP_CH
    k = n // 2
    while k >= 1:
        keep = (blk_id & k) == 0
        nxt = list(arrs)
        for r in range(n):
            if r & k == 0:
                a, b = arrs[r], arrs[r + k]
                nxt[r] = jnp.where(keep, a, pltpu.roll(b, k * SSM_GROUP_CH, axis=1))
                nxt[r + k] = jnp.where(keep, pltpu.roll(a, width - k * SSM_GROUP_CH, axis=1), b)
        arrs = nxt
        k //= 2
    return arrs


def _s5_kernel(u_ref, m_ref, bst_ref, cst_ref, a_ref, d_ref, o_ref, us_scr, z_scr, y_scr, st_scr, *, batch):
    L, H = SSM_CHUNK, SSM_GROUP_CH
    n_grp = us_scr.shape[0]
    R = us_scr.shape[1]
    half = 128 // H

    @pl.when(pl.program_id(1) == 0)
    def _():
        st_scr[...] = jnp.zeros(st_scr.shape, F32)

    for hh in range(L // half):
        slabs = [u_ref[pl.ds(hh * half + s, R, stride=L), :] for s in range(half)]
        for gi, arr in enumerate(_lane_block_transpose(slabs)):
            us_scr[gi, :, hh * 128:(hh + 1) * 128] = arr.astype(BF16)
    for gi in range(n_grp):
        u = us_scr[gi]
        z_scr[gi] = jnp.dot(u, bst_ref[gi], preferred_element_type=F32)
        y_scr[gi] = jnp.dot(u, m_ref[gi], preferred_element_type=F32) + u.astype(F32) * d_ref[gi]

    def step(c, state):
        sl = pl.ds(pl.multiple_of(c * batch, batch), batch)
        out = []
        for gi in range(n_grp):
            x = state[gi]
            z = z_scr[gi, sl, :]
            z_scr[gi, sl, :] = x
            out.append(a_ref[gi, 0:1, :] * x + a_ref[gi, 1:2, :] * pltpu.roll(x, SSM_STATE, axis=1) + z)
        return tuple(out)

    state = lax.fori_loop(0, R // batch, step, tuple(st_scr[gi] for gi in range(n_grp)))
    for gi in range(n_grp):
        st_scr[gi] = state[gi]

    for gi in range(n_grp):
        y = y_scr[gi] + jnp.dot(z_scr[gi].astype(BF16), cst_ref[gi], preferred_element_type=F32)
        y_scr[gi] = _gelu_tanh(y)
    for hh in range(L // half):
        cols = [y_scr[gi, :, hh * 128:(hh + 1) * 128] for gi in range(n_grp)]
        for s, arr in enumerate(_lane_block_transpose(cols)):
            o_ref[pl.ds(hh * half + s, R, stride=L), :] = arr


def _s5_branch(u, tables, batch, seq):
    L, G, H, P = SSM_CHUNK, SSM_GROUPS, SSM_GROUP_CH, SSM_STATE
    n_chunks = seq // L
    gpt = 128 // H
    cr = S5_CHUNKS_PER_STEP
    R = cr * batch
    LH = L * H
    m_tab, bst, cst, a_chunk, d_tab = tables
    tile = lambda o, c: (o, 0, 0)
    return pl.pallas_call(
        functools.partial(_s5_kernel, batch=batch),
        grid=(G // gpt, n_chunks // cr),
        in_specs=[pl.BlockSpec((R * L, 128), lambda o, c: (c, o)),
                  pl.BlockSpec((gpt, LH, LH), tile),
                  pl.BlockSpec((gpt, LH, 2 * P), tile),
                  pl.BlockSpec((gpt, 2 * P, LH), tile),
                  pl.BlockSpec((gpt, 2, 2 * P), tile),
                  pl.BlockSpec((gpt, 1, LH), tile)],
        out_specs=pl.BlockSpec((R * L, 128), lambda o, c: (c, o)),
        out_shape=jax.ShapeDtypeStruct(u.shape, F32),
        scratch_shapes=[pltpu.VMEM((gpt, R, LH), BF16), pltpu.VMEM((gpt, R, 2 * P), F32),
                        pltpu.VMEM((gpt, R, LH), F32), pltpu.VMEM((gpt, batch, 2 * P), F32)],
        compiler_params=_params("parallel", "arbitrary"),
        name="s5",
    )(u, m_tab, bst, cst, a_chunk, d_tab)


def _merge_kernel(x_ref, ya_ref, ys_ref, gs_ref, mod_ref, wglu_ref, pa_ref, ps_ref, wout_ref, g2_ref,
                  wr_ref, br_ref, x1_ref, h2_ref, ri_ref, rw_ref, cnt_ref):
    tm, D = x_ref.shape
    ys = ys_ref[...].reshape(tm, SSM_WIDTH).astype(BF16)
    gl = jnp.dot(ys, wglu_ref[...], preferred_element_type=F32)
    y_ssm = gl[:, :SSM_WIDTH] * jax.nn.sigmoid(gl[:, SSM_WIDTH:])
    p_attn = jnp.dot(ya_ref[...], pa_ref[...], preferred_element_type=F32)
    p_ssm = jnp.dot(y_ssm.astype(BF16), ps_ref[...], preferred_element_type=F32)
    merged = gs_ref[:, :D].astype(F32) * p_attn + gs_ref[:, D:].astype(F32) * p_ssm
    mixed = jnp.dot(merged.astype(BF16), wout_ref[...], preferred_element_type=F32)
    x1 = x_ref[...] + mod_ref[2:3, :] * mixed
    x1_ref[...] = x1
    h2 = _rms(x1, NORM_EPS) * g2_ref[...] * (1.0 + mod_ref[4:5, :]) + mod_ref[3:4, :]
    h2_ref[...] = h2.astype(BF16)

    logits = lax.dot_general(wr_ref[...], h2, (((1,), (1,)), ((), ())),
                             preferred_element_type=F32, precision=HIGHEST) + br_ref[...]
    NG, EPG = MOE_GROUPS, MOE_EXPERTS_PER_GROUP
    lg = logits[0:NG, :]
    g_iota = lax.broadcasted_iota(I32, lg.shape, 0)
    lg_max = jnp.max(lg, axis=0, keepdims=True)
    grp = jnp.min(jnp.where(lg == lg_max, g_iota, NG), axis=0, keepdims=True)
    p_grp = 1.0 / jnp.sum(jnp.exp(lg - lg_max), axis=0, keepdims=True)
    le = logits[NG:NG + EPG, :]
    for g in range(1, NG):
        le = jnp.where(grp == g, logits[NG + g * EPG:NG + (g + 1) * EPG, :], le)
    e_iota = lax.broadcasted_iota(I32, le.shape, 0)
    v1 = jnp.max(le, axis=0, keepdims=True)
    i1 = jnp.min(jnp.where(le == v1, e_iota, EPG), axis=0, keepdims=True)
    le2 = jnp.where(e_iota == i1, -jnp.inf, le)
    v2 = jnp.max(le2, axis=0, keepdims=True)
    i2 = jnp.min(jnp.where(le2 == v2, e_iota, EPG), axis=0, keepdims=True)
    e21 = jnp.exp(v2 - v1)
    w1 = p_grp / (1.0 + e21)
    w2 = p_grp * e21 / (1.0 + e21)
    eid1 = grp * EPG + i1
    eid2 = grp * EPG + i2

    x_iota = lax.broadcasted_iota(I32, (MOE_EXPERTS, tm), 0)
    hot1 = x_iota == eid1
    hot2 = x_iota == eid2
    hot = jnp.logical_or(hot1, hot2).astype(F32)
    before = (lax.broadcasted_iota(I32, (tm, tm), 0) < lax.broadcasted_iota(I32, (tm, tm), 1))
    prior = jnp.dot(hot.astype(BF16), before.astype(BF16), preferred_element_type=F32)
    rank1 = jnp.sum(jnp.where(hot1, prior, 0.0), axis=0, keepdims=True)
    rank2 = jnp.sum(jnp.where(hot2, prior, 0.0), axis=0, keepdims=True)
    cnt_ref[...] = jnp.sum(hot, axis=1, keepdims=True).astype(I32)

    zi = jnp.zeros((4, tm), I32)
    ri_ref[...] = jnp.concatenate([eid1, eid2, rank1.astype(I32), rank2.astype(I32), zi], axis=0)
    rw_ref[...] = jnp.concatenate([w1, w2, jnp.zeros((6, tm), F32)], axis=0)


def _merge_route(x2, ya, ys, gs, mod, w_glu, w_pa, w_ps, w_out, norm2_g, w_rg, b_rg, w_re, b_re, seq):
    T, D = x2.shape
    tm = TOKEN_TILE
    per_b = seq // tm
    wr = jnp.concatenate([w_rg.T, jnp.transpose(w_re, (0, 2, 1)).reshape(MOE_EXPERTS, D),
                          jnp.zeros((ROUTER_ROWS - MOE_GROUPS - MOE_EXPERTS, D), F32)], axis=0).astype(F32)
    br = jnp.concatenate([b_rg, b_re.reshape(-1),
                          jnp.zeros((ROUTER_ROWS - MOE_GROUPS - MOE_EXPERTS,), F32)]).reshape(ROUTER_ROWS, 1)
    row = lambda i: (i, 0)
    col = lambda i: (0, i)
    full = lambda i: (0, 0)
    return pl.pallas_call(
        _merge_kernel,
        grid=(T // tm,),
        in_specs=[pl.BlockSpec((tm, D), row),
                  pl.BlockSpec((tm, ATTN_WIDTH), row),
                  _chunk_major_spec(tm, per_b),
                  pl.BlockSpec((tm, 2 * D), row),
                  pl.BlockSpec((None, 6, D), lambda i: (i // per_b, 0, 0)),
                  pl.BlockSpec(w_glu.shape, full),
                  pl.BlockSpec(w_pa.shape, full),
                  pl.BlockSpec(w_ps.shape, full),
                  pl.BlockSpec(w_out.shape, full),
                  pl.BlockSpec((1, D), full),
                  pl.BlockSpec((ROUTER_ROWS, D), full),
                  pl.BlockSpec((ROUTER_ROWS, 1), full)],
        out_specs=[pl.BlockSpec((tm, D), row), pl.BlockSpec((tm, D), row),
                   pl.BlockSpec((8, tm), col), pl.BlockSpec((8, tm), col),
                   pl.BlockSpec((None, MOE_EXPERTS, 1), lambda i: (i, 0, 0))],
        out_shape=[jax.ShapeDtypeStruct((T, D), F32), jax.ShapeDtypeStruct((T, D), BF16),
                   jax.ShapeDtypeStruct((8, T), I32), jax.ShapeDtypeStruct((8, T), F32),
                   jax.ShapeDtypeStruct((T // tm, MOE_EXPERTS, 1), I32)],
        compiler_params=_params("parallel"),
        name="merge_route",
    )(x2, ya, ys, gs, mod, w_glu.astype(BF16), w_pa.astype(BF16), w_ps.astype(BF16), w_out.astype(BF16),
      norm2_g.reshape(1, D), wr, br)


def _tile_positions(ri_ref, seg_ref):
    tm = ri_ref.shape[1]
    x_iota = lax.broadcasted_iota(I32, (MOE_EXPERTS, tm), 0)
    seg = seg_ref[...].astype(F32)
    pos = []
    for k in range(2):
        start = jnp.sum(jnp.where(x_iota == ri_ref[k:k + 1, :], seg, 0.0), axis=0, keepdims=True)
        pos.append(start + ri_ref[2 + k:3 + k, :].astype(F32))
    return pos


def _segment_copies(meta, tile, make_copy):
    seg_row, dst_row, n_chunk = meta

    def per_expert(e, carry):
        idx = tile * MOE_EXPERTS + e
        src0 = seg_row[idx]
        dst0 = dst_row[idx]

        def per_chunk(c, carry2):
            off = c * SEG_ALIGN
            make_copy(pl.multiple_of(src0 + off, SEG_ALIGN), pl.multiple_of(dst0 + off, SEG_ALIGN))
            return carry2

        return lax.fori_loop(0, n_chunk[idx], per_chunk, carry)

    lax.fori_loop(0, MOE_EXPERTS, per_expert, 0)


def _dispatch_kernel(seg_row, dst_row, n_chunk, tile_chunks, pad_row, pad_chunks, n_used,
                     h_ref, ri_ref, rw_ref, seg_ref, xs_ref, pw_ref, zbuf, zeros_scr, sem, pad_sem, tail_sem):
    i = pl.program_id(0)
    n_tiles = pl.num_programs(0)
    slot = i % 2
    tm = h_ref.shape[0]
    pos1, pos2 = _tile_positions(ri_ref, seg_ref)
    pw_ref[...] = jnp.concatenate([pos1, pos2, rw_ref[0:2, :], jnp.zeros((4, tm), F32)], axis=0)
    r_iota = lax.broadcasted_iota(I32, (TILE_SLOTS, tm), 0).astype(F32)
    onehot = jnp.logical_or(r_iota == pos1, r_iota == pos2).astype(BF16)
    zbuf[slot] = jnp.dot(onehot, h_ref[...], preferred_element_type=F32)

    def chunk_copy(buf_slot, src, dst):
        return pltpu.make_async_copy(zbuf.at[buf_slot, pl.ds(src, SEG_ALIGN)],
                                     xs_ref.at[pl.ds(dst, SEG_ALIGN)], sem.at[buf_slot])

    _segment_copies((seg_row, dst_row, n_chunk), i, lambda src, dst: chunk_copy(slot, src, dst).start())

    def wait_tile(tile, buf_slot):
        def body(c, carry):
            chunk_copy(buf_slot, 0, 0).wait()
            return carry
        lax.fori_loop(0, tile_chunks[tile], body, 0)

    @pl.when(i > 0)
    def _():
        wait_tile(i - 1, 1 - slot)

    @pl.when(i == n_tiles - 1)
    def _():
        zeros_scr[...] = jnp.zeros(zeros_scr.shape, F32)

        def pad_copy(dst):
            return pltpu.make_async_copy(zeros_scr.at[pl.ds(0, SEG_ALIGN)], xs_ref.at[pl.ds(dst, SEG_ALIGN)], pad_sem)

        def tail_copy(dst):
            return pltpu.make_async_copy(zeros_scr, xs_ref.at[pl.ds(dst, MOE_BLOCK)], tail_sem)

        def per_expert(e, total):
            def per_chunk(c, carry):
                pad_copy(pl.multiple_of(pad_row[e] + c * SEG_ALIGN, SEG_ALIGN)).start()
                return carry
            lax.fori_loop(0, pad_chunks[e], per_chunk, 0)
            return total + pad_chunks[e]

        n_pad_copies = lax.fori_loop(0, MOE_EXPERTS, per_expert, 0)
        n_blocks = xs_ref.shape[0] // MOE_BLOCK

        def tail_start(b, carry):
            tail_copy(pl.multiple_of(b * MOE_BLOCK, MOE_BLOCK)).start()
            return carry
        lax.fori_loop(n_used[0], n_blocks, tail_start, 0)
        wait_tile(i, slot)

        def wait_pad(c, carry):
            pad_copy(0).wait()
            return carry
        lax.fori_loop(0, n_pad_copies, wait_pad, 0)

        def wait_tail(b, carry):
            tail_copy(0).wait()
            return carry
        lax.fori_loop(n_used[0], n_blocks, wait_tail, 0)


def _dispatch(meta, h2, ri, rw, seg_start, n_rows):
    T, D = h2.shape
    tm = TOKEN_TILE
    col = lambda i, *_: (0, i)
    return pl.pallas_call(
        _dispatch_kernel,
        grid_spec=pltpu.PrefetchScalarGridSpec(
            num_scalar_prefetch=7,
            grid=(T // tm,),
            in_specs=[pl.BlockSpec((tm, D), lambda i, *_: (i, 0)),
                      pl.BlockSpec((8, tm), col),
                      pl.BlockSpec((8, tm), col),
                      pl.BlockSpec((None, MOE_EXPERTS, 1), lambda i, *_: (i, 0, 0))],
            out_specs=[pl.BlockSpec(memory_space=pl.ANY), pl.BlockSpec((8, tm), col)],
            scratch_shapes=[pltpu.VMEM((2, TILE_SLOTS, D), F32), pltpu.VMEM((MOE_BLOCK, D), F32),
                            pltpu.SemaphoreType.DMA((2,)), pltpu.SemaphoreType.DMA(()),
                            pltpu.SemaphoreType.DMA(())]),
        out_shape=[jax.ShapeDtypeStruct((n_rows, D), F32), jax.ShapeDtypeStruct((8, T), F32)],
        compiler_params=_params("arbitrary"),
        name="moe_dispatch",
    )(*meta, h2, ri, rw, seg_start)


def _expert_kernel(be_ref, nb_ref, x_ref, wi_ref, wo_ref, o_ref):
    del be_ref
    F = wo_ref.shape[0]

    @pl.when(pl.program_id(0) < nb_ref[0])
    def _():
        hid = jnp.dot(x_ref[...].astype(BF16), wi_ref[...], preferred_element_type=F32)
        a = hid[:, :F]
        act = a * jax.nn.sigmoid(a) * hid[:, F:]
        o_ref[...] = jnp.dot(act.astype(BF16), wo_ref[...], preferred_element_type=F32)

    @pl.when(pl.program_id(0) >= nb_ref[0])
    def _():
        o_ref[...] = jnp.zeros(o_ref.shape, F32)


def _experts(block_e, n_used, xs, w_e_in, w_e_out):
    n_rows, D = xs.shape
    F = w_e_out.shape[1]
    blk = lambda i, be, nb: jnp.maximum(jnp.minimum(i, nb[0] - 1), 0)
    return pl.pallas_call(
        _expert_kernel,
        grid_spec=pltpu.PrefetchScalarGridSpec(
            num_scalar_prefetch=2,
            grid=(n_rows // MOE_BLOCK,),
            in_specs=[pl.BlockSpec((MOE_BLOCK, D), lambda i, be, nb: (blk(i, be, nb), 0)),
                      pl.BlockSpec((None, D, 2 * F), lambda i, be, nb: (be[blk(i, be, nb)], 0, 0)),
                      pl.BlockSpec((None, F, D), lambda i, be, nb: (be[blk(i, be, nb)], 0, 0))],
            out_specs=pl.BlockSpec((MOE_BLOCK, D), lambda i, be, nb: (i, 0))),
        out_shape=jax.ShapeDtypeStruct((n_rows, D), F32),
        compiler_params=_params("arbitrary"),
        name="experts",
    )(block_e, n_used, xs, w_e_in.astype(BF16), w_e_out.astype(BF16))


def _final_kernel(seg_row, dst_row, n_chunk, tile_chunks, x1_ref, pw_ref, mod_ref, g_ref, ys_ref, o_ref,
                  ybuf, sem):
    i = pl.program_id(0)
    n_tiles = pl.num_programs(0)
    slot = i % 2
    tm = x1_ref.shape[0]

    def chunk_copy(buf_slot, src, dst):
        return pltpu.make_async_copy(ys_ref.at[pl.ds(dst, SEG_ALIGN)],
                                     ybuf.at[buf_slot, pl.ds(src, SEG_ALIGN)], sem.at[buf_slot])

    def fetch(tile, buf_slot):
        _segment_copies((seg_row, dst_row, n_chunk), tile, lambda src, dst: chunk_copy(buf_slot, src, dst).start())

    @pl.when(i == 0)
    def _():
        ybuf[...] = jnp.zeros(ybuf.shape, F32)
        fetch(0, 0)

    @pl.when(i + 1 < n_tiles)
    def _():
        fetch(i + 1, 1 - slot)

    def wait_one(c, carry):
        chunk_copy(slot, 0, 0).wait()
        return carry
    lax.fori_loop(0, tile_chunks[i], wait_one, 0)

    s_iota = lax.broadcasted_iota(I32, (tm, TILE_SLOTS), 1).astype(F32)
    comb = (jnp.where(s_iota == pw_ref[:, 0:1], pw_ref[:, 2:3], 0.0)
            + jnp.where(s_iota == pw_ref[:, 1:2], pw_ref[:, 3:4], 0.0))
    moe = jnp.dot(comb.astype(BF16), ybuf[slot].astype(BF16), preferred_element_type=F32)
    x2 = x1_ref[...] + mod_ref[5:6, :] * moe
    o_ref[...] = _rms(x2, NORM_EPS) * g_ref[...]


def _final(meta, x1, pw_tok, mod, final_g, ys, seq):
    T, D = x1.shape
    tm = TOKEN_TILE
    per_b = seq // tm
    row = lambda i, *_: (i, 0)
    return pl.pallas_call(
        _final_kernel,
        grid_spec=pltpu.PrefetchScalarGridSpec(
            num_scalar_prefetch=4,
            grid=(T // tm,),
            in_specs=[pl.BlockSpec((tm, D), row),
                      pl.BlockSpec((tm, 8), row),
                      pl.BlockSpec((None, 6, D), lambda i, *_: (i // per_b, 0, 0)),
                      pl.BlockSpec((1, D), lambda i, *_: (0, 0)),
                      pl.BlockSpec(memory_space=pl.ANY)],
            out_specs=pl.BlockSpec((tm, D), row),
            scratch_shapes=[pltpu.VMEM((2, TILE_SLOTS, D), F32), pltpu.SemaphoreType.DMA((2,))]),
        out_shape=jax.ShapeDtypeStruct((T, D), F32),
        compiler_params=_params("arbitrary"),
        name="final",
    )(*meta, x1, pw_tok, mod, final_g.reshape(1, D), ys)


def _round_up(x, m):
    return (x + m - 1) // m * m


def _moe_layout(tile_counts):
    n_tiles = tile_counts.shape[0]
    seg = _round_up(tile_counts, SEG_ALIGN)
    seg_start = jnp.cumsum(seg, axis=1) - seg
    tile_base = jnp.cumsum(seg, axis=0) - seg
    used = jnp.sum(seg, axis=0)
    region = _round_up(used, MOE_BLOCK)
    region_end = jnp.cumsum(region)
    region_start = region_end - region
    dst_row = region_start[None, :] + tile_base
    n_chunk = seg // SEG_ALIGN
    n_assign = 2 * n_tiles * TOKEN_TILE
    n_rows = _round_up(n_assign + n_tiles * MOE_EXPERTS * (SEG_ALIGN - 1) + MOE_EXPERTS * (MOE_BLOCK - 1), MOE_BLOCK)
    block_start = jnp.arange(n_rows // MOE_BLOCK, dtype=I32) * MOE_BLOCK
    block_e = jnp.minimum(jnp.sum(block_start[:, None] >= region_end[None, :], axis=1), MOE_EXPERTS - 1)
    i32 = lambda a: a.reshape(-1).astype(I32)
    meta = (i32(seg_start), i32(dst_row), i32(n_chunk), i32(jnp.sum(n_chunk, axis=1)))
    pad = (i32(region_start + used), i32((region - used) // SEG_ALIGN))
    n_used = (region_end[-1:] // MOE_BLOCK).astype(I32)
    return meta, pad, seg_start.astype(I32)[:, :, None], block_e.astype(I32), n_used, n_rows


def kernel(x, c, w_ada, b_ada, norm1_g, w_in, rel_bias, lambda_q1, lambda_k1, lambda_q2, lambda_k2, subln_g, ssm_lambda_re, ssm_lambda_im, ssm_log_step, ssm_b_re, ssm_b_im, ssm_c_re, ssm_c_im, ssm_d, w_glu, w_proj_attn, w_proj_ssm, w_out, norm2_g, w_router_group, b_router_group, w_router_expert, b_router_expert, w_expert_in, w_expert_out, final_g):
    B, S, D = x.shape
    T = B * S
    x2 = x.reshape(T, D)
    mod = _ada_mod(c, w_ada[0], b_ada[0]).reshape(B, 6, D)
    q, k, v, u, gs = _in_proj(x2, mod, norm1_g[0], w_in[0], S)
    lam_vecs = jnp.stack([lambda_q1[0], lambda_k1[0], lambda_q2[0], lambda_k2[0]]).astype(F32)
    y_attn = _diff_attn(q, k, v, rel_bias, lam_vecs, subln_g[0], B, S)
    tables = _s5_tables(ssm_lambda_re[0], ssm_lambda_im[0], ssm_log_step[0], ssm_b_re[0], ssm_b_im[0],
                        ssm_c_re[0], ssm_c_im[0], ssm_d[0])
    y_s5 = _s5_branch(u.reshape(-1, SSM_WIDTH), tables, B, S).reshape(u.shape)
    x1, h2, ri, rw, tile_counts = _merge_route(
        x2, y_attn, y_s5, gs, mod, w_glu[0], w_proj_attn[0], w_proj_ssm[0], w_out[0], norm2_g[0],
        w_router_group[0], b_router_group[0], w_router_expert[0], b_router_expert[0], S)
    meta, pad, seg_start, block_e, n_used, n_rows = _moe_layout(tile_counts[:, :, 0])
    xs, pw = _dispatch(meta + pad + (n_used,), h2, ri, rw, seg_start, n_rows)
    ys = _experts(block_e, n_used, xs, w_expert_in[0], w_expert_out[0])
    out = _final(meta, x1, pw.T, mod, final_g, ys, S)
    return out.reshape(B, S, D)
```

```python
import functools
import math

import jax
import jax.numpy as jnp
from jax import lax
from jax.experimental import pallas as pl
from jax.experimental.pallas import tpu as pltpu

F32 = jnp.float32
BF16 = jnp.bfloat16
I32 = jnp.int32
HIGHEST = lax.Precision.HIGHEST

ATTN_HEADS = 4
ATTN_HEAD_DIM = 64
ATTN_V_DIM = 2 * ATTN_HEAD_DIM
ATTN_WIDTH = ATTN_HEADS * ATTN_V_DIM
NEG_INF = -1e30
REL_BUCKETS = 32
REL_MAX_DISTANCE = 128
SSM_GROUP_CH = 16
SSM_WIDTH = 512
SSM_GROUPS = SSM_WIDTH // SSM_GROUP_CH
SSM_STATE = 64
SSM_EIG_MAX_RE = -1e-4
SSM_CHUNK = 16
S5_CHUNKS_PER_STEP = 32
MOE_GROUPS = 4
MOE_EXPERTS_PER_GROUP = 8
MOE_EXPERTS = MOE_GROUPS * MOE_EXPERTS_PER_GROUP
MOE_BLOCK = 256
SEG_ALIGN = 8
NORM_EPS = 1e-6
SUBLN_EPS = 1e-5
LAMBDA_INIT = 0.8 - 0.6 * math.exp(-0.3 * 0)

ATTN_BLOCK = 256
ATTN_HEADS_PER_STEP = 4
ATTN_ROW_CHUNK = 32
ATTN_ONES_ROWS = 16
LOG2E = math.log2(math.e)
TOKEN_TILE = 512
ROUTER_ROWS = 40
TILE_SLOTS = -(-(2 * TOKEN_TILE + MOE_EXPERTS * (SEG_ALIGN - 1)) // 256) * 256
VMEM_LIMIT = 56 << 20


def _params(*sem):
    return pltpu.CompilerParams(dimension_semantics=sem, vmem_limit_bytes=VMEM_LIMIT)


def _rms(x, eps):
    return x * lax.rsqrt(jnp.mean(x * x, axis=-1, keepdims=True) + eps)


def _mod_kernel(c_ref, w_ref, b_ref, o_ref):
    c = c_ref[...]
    c_act = c * jax.nn.sigmoid(c)
    o_ref[...] = jnp.dot(c_act, w_ref[...], preferred_element_type=F32, precision=HIGHEST) + b_ref[...]


def _ada_mod(c, w_ada, b_ada):
    B, D = c.shape
    N = w_ada.shape[1]
    tn = 1024
    return pl.pallas_call(
        _mod_kernel,
        grid=(N // tn,),
        in_specs=[pl.BlockSpec((B, D), lambda j: (0, 0)),
                  pl.BlockSpec((D, tn), lambda j: (0, j)),
                  pl.BlockSpec((1, tn), lambda j: (0, j))],
        out_specs=pl.BlockSpec((B, tn), lambda j: (0, j)),
        out_shape=jax.ShapeDtypeStruct((B, N), F32),
        compiler_params=_params("arbitrary"),
        name="ada_mod",
    )(c, w_ada, b_ada.reshape(1, N))


def _proj_kernel(x_ref, mod_ref, g_ref, w_ref, q_ref, k_ref, v_ref, u_ref, gs_ref):
    y = _rms(x_ref[...], NORM_EPS) * g_ref[...]
    h = (y * (1.0 + mod_ref[1:2, :]) + mod_ref[0:1, :]).astype(BF16)
    W = ATTN_WIDTH

    def proj(lo, hi):
        return jnp.dot(h, w_ref[:, lo:hi], preferred_element_type=F32)

    q_ref[...] = (proj(0, W) * (ATTN_HEAD_DIM ** -0.5 * LOG2E)).astype(BF16)
    k_ref[...] = proj(W, 2 * W).astype(BF16)
    v_ref[...] = proj(2 * W, 3 * W).astype(BF16)
    u_ref[...] = proj(3 * W, 3 * W + SSM_WIDTH).reshape(u_ref.shape)
    gs_ref[...] = jax.nn.sigmoid(proj(3 * W + SSM_WIDTH, w_ref.shape[1])).astype(BF16)


def _chunk_major_spec(tm, per_b):
    return pl.BlockSpec((tm // SSM_CHUNK, None, SSM_CHUNK, SSM_WIDTH), lambda i: (i % per_b, i // per_b, 0, 0))


def _in_proj(x2, mod, norm_g, w_in, seq):
    T, D = x2.shape
    tm = TOKEN_TILE
    per_b = seq // tm
    n_gate = w_in.shape[1] - 3 * ATTN_WIDTH - SSM_WIDTH
    row = lambda i: (i, 0)
    return pl.pallas_call(
        _proj_kernel,
        grid=(T // tm,),
        in_specs=[pl.BlockSpec((tm, D), row),
                  pl.BlockSpec((None, 6, D), lambda i: (i // per_b, 0, 0)),
                  pl.BlockSpec((1, D), lambda i: (0, 0)),
                  pl.BlockSpec(w_in.shape, lambda i: (0, 0))],
        out_specs=[pl.BlockSpec((tm, ATTN_WIDTH), row)] * 3
        + [_chunk_major_spec(tm, per_b), pl.BlockSpec((tm, n_gate), row)],
        out_shape=[jax.ShapeDtypeStruct((T, ATTN_WIDTH), BF16)] * 3
        + [jax.ShapeDtypeStruct((seq // SSM_CHUNK, T // seq, SSM_CHUNK, SSM_WIDTH), F32),
           jax.ShapeDtypeStruct((T, n_gate), BF16)],
        compiler_params=_params("parallel"),
        name="in_proj",
    )(x2, mod, norm_g.reshape(1, D), w_in.astype(BF16))


def _rel_bucket(dist):
    max_exact = REL_BUCKETS // 2
    n = jnp.maximum(dist, 0)
    log_ratio = jnp.log(jnp.maximum(n, 1).astype(F32) / max_exact) / math.log(REL_MAX_DISTANCE / max_exact)
    large = max_exact + (log_ratio * (REL_BUCKETS - max_exact)).astype(I32)
    large = jnp.minimum(large, REL_BUCKETS - 1)
    return jnp.where(n < max_exact, n, large)


def _attn_bias_tiles(rel_bias, blk):
    assert blk >= REL_MAX_DISTANCE
    n_heads = rel_bias.shape[1]
    far = rel_bias[REL_BUCKETS - 1].astype(F32)
    m = jnp.arange(2 * blk)
    signed = jnp.where(m < blk, m, m - 2 * blk)
    tiles = []
    for kind in range(2):
        dist = kind * blk + signed
        tab = jnp.where(dist >= 0, (rel_bias[_rel_bucket(dist)].astype(F32).T - far[:, None]) * LOG2E, NEG_INF)
        skew = jnp.tile(tab, (1, blk))[:, :blk * (2 * blk - 1)].reshape(n_heads, blk, 2 * blk - 1)
        tiles.append(skew[:, :, :blk])
    return jnp.stack(tiles, axis=1)


def _attn_kernel(lam_ref, q_ref, k_ref, v_ref, bias_ref, g_ref, o_ref, vt_scr, *bufs, blk, heads):
    i = pl.program_id(2)
    n_kv = vt_scr.shape[1]
    V = ATTN_V_DIM
    ns = 2 * heads
    acc = bufs[0:ns]
    sbuf = tuple(bufs[(1 + r) * ns:(2 + r) * ns] for r in range(3))
    pbufs = tuple(bufs[(4 + r) * ns:(5 + r) * ns] for r in range(3))

    @pl.when(i == 0)
    def _():
        for hd in range(heads):
            for jb in range(n_kv):
                vt_scr[hd, jb, 0:V, :] = v_ref[jb * blk:(jb + 1) * blk, hd * V:(hd + 1) * V].astype(F32).T.astype(BF16)
                vt_scr[hd, jb, V:, :] = jnp.ones((ATTN_ONES_ROWS, blk), BF16)

    qt_maps = []
    for hd in range(heads):
        qt = q_ref[:, hd * V:(hd + 1) * V].astype(F32).T
        feat = lax.broadcasted_iota(I32, qt.shape, 0)
        qt_maps += [jnp.where(feat < ATTN_HEAD_DIM, qt, 0.0).astype(BF16),
                    jnp.where(feat >= ATTN_HEAD_DIM, qt, 0.0).astype(BF16)]

    def scores(j, st):
        hd = st // 2
        kj = k_ref[pl.ds(pl.multiple_of(j * blk, blk), blk), hd * V:(hd + 1) * V]
        return jnp.dot(kj, qt_maps[st], preferred_element_type=F32)

    n_chunks = blk // ATTN_ROW_CHUNK

    def rows(c):
        return slice(c * ATTN_ROW_CHUNK, (c + 1) * ATTN_ROW_CHUNK)

    def fold8(x):
        return x.reshape(ATTN_ROW_CHUNK // 8, 8, blk)

    def block(j, carry, pos, lookahead, bias_kind):
        src, dst, pbuf = sbuf[pos], sbuf[(pos + 2) % 3], pbufs[pos]
        out = []
        for st in range(ns):
            hd = st // 2
            if lookahead:
                dst[st][...] = scores(j + 2, st)

            def chunk(c):
                s = src[st][rows(c), :]
                return s if bias_kind is None else s + bias_ref[hd, bias_kind, rows(c), :]

            m_old = carry[st]
            m8 = jnp.max(fold8(chunk(0)), axis=0)
            for c in range(1, n_chunks):
                m8 = jnp.maximum(m8, jnp.max(fold8(chunk(c)), axis=0))
            m_new = jnp.maximum(m_old, jnp.max(m8, axis=0, keepdims=True))
            alpha = jnp.exp2(m_old - m_new)
            for c in range(n_chunks):
                pbuf[st][rows(c), :] = jnp.exp2(chunk(c) - m_new).astype(BF16)
            acc[st][...] = alpha * acc[st][...] + jnp.dot(vt_scr[hd, j], pbuf[st][...],
                                                          preferred_element_type=F32)
            out.append(m_new)
        return tuple(out)

    def far_triple(t, carry):
        for r in range(3):
            carry = block(3 * t + r, carry, r, True, None)
        return carry

    def far_single(j, carry):
        carry = block(j, carry, 0, True, None)
        for st in range(ns):
            sbuf[0][st][...] = sbuf[1][st][...]
        for st in range(ns):
            sbuf[1][st][...] = sbuf[2][st][...]
        return carry

    def near_pair(_, carry):
        return block(i, block(i - 1, carry, 0, False, 1), 1, False, 0)

    def near_single(_, carry):
        return block(i, carry, 0, False, 0)

    for st in range(ns):
        acc[st][...] = jnp.zeros(acc[st].shape, F32)
        sbuf[0][st][...] = scores(0, st)
        sbuf[1][st][...] = scores(jnp.minimum(i, 1), st)
    m0 = jnp.full((1, blk), -jnp.inf, F32)
    n_far = jnp.maximum(i - 1, 0)
    carry = lax.fori_loop(0, n_far // 3, far_triple, (m0,) * ns)
    carry = lax.fori_loop(n_far - n_far % 3, n_far, far_single, carry)
    carry = lax.fori_loop(0, jnp.minimum(i, 1), near_pair, carry)
    lax.fori_loop(0, 1 - jnp.minimum(i, 1), near_single, carry)

    lam = (jnp.exp(jnp.sum(lam_ref[0:1, :] * lam_ref[1:2, :], axis=-1, keepdims=True))
           - jnp.exp(jnp.sum(lam_ref[2:3, :] * lam_ref[3:4, :], axis=-1, keepdims=True)) + LAMBDA_INIT)
    for hd in range(heads):
        a1, a2 = acc[2 * hd], acc[2 * hd + 1]
        ot = a1[0:V, :] / a1[V:V + 1, :] - lam * (a2[0:V, :] / a2[V:V + 1, :])
        ot = ot * lax.rsqrt(jnp.mean(ot * ot, axis=0, keepdims=True) + SUBLN_EPS)
        o_ref[:, hd * V:(hd + 1) * V] = (ot.T * (g_ref[...] * (1.0 - LAMBDA_INIT))).astype(BF16)


def _diff_attn(q, k, v, rel_bias, lam_vecs, subln_g, batch, seq):
    T = q.shape[0]
    blk = ATTN_BLOCK
    nq = seq // blk
    bias = _attn_bias_tiles(rel_bias, blk)
    hp = ATTN_HEADS_PER_STEP
    ns = 2 * hp
    width = hp * ATTN_V_DIM
    acc_rows = ATTN_V_DIM + ATTN_ONES_ROWS
    return pl.pallas_call(
        functools.partial(_attn_kernel, blk=blk, heads=hp),
        grid=(batch, ATTN_HEADS // hp, nq),
        in_specs=[pl.BlockSpec((4, ATTN_HEAD_DIM), lambda b, h, i: (0, 0)),
                  pl.BlockSpec((blk, width), lambda b, h, i: (b * nq + i, h)),
                  pl.BlockSpec((seq, width), lambda b, h, i: (b, h)),
                  pl.BlockSpec((seq, width), lambda b, h, i: (b, h)),
                  pl.BlockSpec((hp, 2, blk, blk), lambda b, h, i: (h, 0, 0, 0)),
                  pl.BlockSpec((1, ATTN_V_DIM), lambda b, h, i: (0, 0))],
        out_specs=pl.BlockSpec((blk, width), lambda b, h, i: (b * nq + i, h)),
        out_shape=jax.ShapeDtypeStruct((T, ATTN_WIDTH), BF16),
        scratch_shapes=[pltpu.VMEM((hp, nq, acc_rows, blk), BF16)]
        + [pltpu.VMEM((acc_rows, blk), F32)] * ns + [pltpu.VMEM((blk, blk), F32)] * (3 * ns)
        + [pltpu.VMEM((blk, blk), BF16)] * (3 * ns),
        compiler_params=_params("parallel", "parallel", "arbitrary"),
        name="diff_attn",
    )(lam_vecs, q, k, v, bias, subln_g.reshape(1, ATTN_V_DIM))


def _s5_tables(lam_re, lam_im, log_step, b_re, b_im, c_re, c_im, d_skip):
    L = SSM_CHUNK
    G, P = lam_re.shape
    H = SSM_GROUP_CH
    lr = jnp.minimum(lam_re.astype(F32), SSM_EIG_MAX_RE)
    li = lam_im.astype(F32)
    step = jnp.exp(log_step.astype(F32))[:, None]
    mag = jnp.exp(lr * step)
    ang = li * step
    a_re = mag * jnp.cos(ang)
    a_im = mag * jnp.sin(ang)
    den = lr * lr + li * li
    num_re = a_re - 1.0
    coef_re = (num_re * lr + a_im * li) / den
    coef_im = (a_im * lr - num_re * li) / den
    br = b_re.astype(F32)
    bi = b_im.astype(F32)
    bb_re = coef_re[..., None] * br - coef_im[..., None] * bi
    bb_im = coef_re[..., None] * bi + coef_im[..., None] * br
    pw_re, pw_im = [jnp.ones_like(a_re)], [jnp.zeros_like(a_re)]
    for _ in range(L):
        pr, pi = pw_re[-1], pw_im[-1]
        pw_re.append(pr * a_re - pi * a_im)
        pw_im.append(pr * a_im + pi * a_re)
    pw_re = jnp.stack(pw_re)
    pw_im = jnp.stack(pw_im)
    cr = c_re.astype(F32)[None]
    ci = c_im.astype(F32)[None]
    cp_re = cr * pw_re[:, :, None, :] - ci * pw_im[:, :, None, :]
    cp_im = cr * pw_im[:, :, None, :] + ci * pw_re[:, :, None, :]
    kern = (jnp.einsum('tghp,gpk->tghk', cp_re[:L], bb_re, precision=HIGHEST)
            - jnp.einsum('tghp,gpk->tghk', cp_im[:L], bb_im, precision=HIGHEST))
    s_idx = jnp.arange(L)[:, None]
    t_idx = jnp.arange(L)[None, :]
    toep = jnp.where((t_idx >= s_idx)[:, :, None, None, None], kern[jnp.maximum(t_idx - s_idx, 0)], 0.0)
    m_tab = jnp.transpose(toep, (2, 0, 4, 1, 3)).reshape(G, L * H, L * H)
    rev_re = pw_re[L - 1::-1][:, :, None, :]
    rev_im = pw_im[L - 1::-1][:, :, None, :]
    bbt_re = jnp.transpose(bb_re, (0, 2, 1))[None]
    bbt_im = jnp.transpose(bb_im, (0, 2, 1))[None]
    bst_re = jnp.transpose(rev_re * bbt_re - rev_im * bbt_im, (1, 0, 2, 3)).reshape(G, L * H, P)
    bst_im = jnp.transpose(rev_re * bbt_im + rev_im * bbt_re, (1, 0, 2, 3)).reshape(G, L * H, P)
    cst_re = jnp.transpose(cp_re[1:], (1, 3, 0, 2)).reshape(G, P, L * H)
    cst_im = -jnp.transpose(cp_im[1:], (1, 3, 0, 2)).reshape(G, P, L * H)
    a_chunk = jnp.stack([jnp.concatenate([pw_re[L], pw_re[L]], axis=-1),
                         jnp.concatenate([-pw_im[L], pw_im[L]], axis=-1)], axis=1)
    d_tab = jnp.tile(d_skip.astype(F32), (1, L)).reshape(G, 1, L * H)
    bst = jnp.concatenate([bst_re, bst_im], axis=-1)
    cst = jnp.concatenate([cst_re, cst_im], axis=1)
    return m_tab.astype(BF16), bst.astype(BF16), cst.astype(BF16), a_chunk, d_tab


def _gelu_tanh(x):
    return 0.5 * x * (1.0 + jnp.tanh(math.sqrt(2.0 / math.pi) * (x + 0.044715 * (x * x * x))))


def _lane_block_transpose(arrs):
    n = len(arrs)
    width = arrs[0].shape[1]
    blk_id = lax.broadcasted_iota(I32, arrs[0].shape, 1) // SSM_GROUP_CH
    k = n // 2
    while k >= 1:
        keep = (blk_id & k) == 0
        nxt = list(arrs)
        for r in range(n):
            if r & k == 0:
                a, b = arrs[r], arrs[r + k]
                nxt[r] = jnp.where(keep, a, pltpu.roll(b, k * SSM_GROUP_CH, axis=1))
                nxt[r + k] = jnp.where(keep, pltpu.roll(a, width - k * SSM_GROUP_CH, axis=1), b)
        arrs = nxt
        k //= 2
    return arrs


def _s5_kernel(u_ref, m_ref, bst_ref, cst_ref, a_ref, d_ref, o_ref, us_scr, z_scr, y_scr, st_scr, *, batch):
    L, H = SSM_CHUNK, SSM_GROUP_CH
    n_grp = us_scr.shape[0]
    R = us_scr.shape[1]
    half = 128 // H

    @pl.when(pl.program_id(1) == 0)
    def _():
        st_scr[...] = jnp.zeros(st_scr.shape, F32)

    for hh in range(L // half):
        slabs = [u_ref[pl.ds(hh * half + s, R, stride=L), :] for s in range(half)]
        for gi, arr in enumerate(_lane_block_transpose(slabs)):
            us_scr[gi, :, hh * 128:(hh + 1) * 128] = arr.astype(BF16)
    for gi in range(n_grp):
        u = us_scr[gi]
        z_scr[gi] = jnp.dot(u, bst_ref[gi], preferred_element_type=F32)
        y_scr[gi] = jnp.dot(u, m_ref[gi], preferred_element_type=F32) + u.astype(F32) * d_ref[gi]

    def step(c, state):
        sl = pl.ds(pl.multiple_of(c * batch, batch), batch)
        out = []
        for gi in range(n_grp):
            x = state[gi]
            z = z_scr[gi, sl, :]
            z_scr[gi, sl, :] = x
            out.append(a_ref[gi, 0:1, :] * x + a_ref[gi, 1:2, :] * pltpu.roll(x, SSM_STATE, axis=1) + z)
        return tuple(out)

    state = lax.fori_loop(0, R // batch, step, tuple(st_scr[gi] for gi in range(n_grp)))
    for gi in range(n_grp):
        st_scr[gi] = state[gi]

    for gi in range(n_grp):
        y = y_scr[gi] + jnp.dot(z_scr[gi].astype(BF16), cst_ref[gi], preferred_element_type=F32)
        y_scr[gi] = _gelu_tanh(y)
    for hh in range(L // half):
        cols = [y_scr[gi, :, hh * 128:(hh + 1) * 128] for gi in range(n_grp)]
        for s, arr in enumerate(_lane_block_transpose(cols)):
            o_ref[pl.ds(hh * half + s, R, stride=L), :] = arr


def _s5_branch(u, tables, batch, seq):
    L, G, H, P = SSM_CHUNK, SSM_GROUPS, SSM_GROUP_CH, SSM_STATE
    n_chunks = seq // L
    gpt = 128 // H
    cr = S5_CHUNKS_PER_STEP
    R = cr * batch
    LH = L * H
    m_tab, bst, cst, a_chunk, d_tab = tables
    tile = lambda o, c: (o, 0, 0)
    return pl.pallas_call(
        functools.partial(_s5_kernel, batch=batch),
        grid=(G // gpt, n_chunks // cr),
        in_specs=[pl.BlockSpec((R * L, 128), lambda o, c: (c, o)),
                  pl.BlockSpec((gpt, LH, LH), tile),
                  pl.BlockSpec((gpt, LH, 2 * P), tile),
                  pl.BlockSpec((gpt, 2 * P, LH), tile),
                  pl.BlockSpec((gpt, 2, 2 * P), tile),
                  pl.BlockSpec((gpt, 1, LH), tile)],
        out_specs=pl.BlockSpec((R * L, 128), lambda o, c: (c, o)),
        out_shape=jax.ShapeDtypeStruct(u.shape, F32),
        scratch_shapes=[pltpu.VMEM((gpt, R, LH), BF16), pltpu.VMEM((gpt, R, 2 * P), F32),
                        pltpu.VMEM((gpt, R, LH), F32), pltpu.VMEM((gpt, batch, 2 * P), F32)],
        compiler_params=_params("parallel", "arbitrary"),
        name="s5",
    )(u, m_tab, bst, cst, a_chunk, d_tab)


def _merge_kernel(x_ref, ya_ref, ys_ref, gs_ref, mod_ref, wglu_ref, pa_ref, ps_ref, wout_ref, g2_ref,
                  wr_ref, br_ref, x1_ref, h2_ref, ri_ref, rw_ref, cnt_ref):
    tm, D = x_ref.shape
    ys = ys_ref[...].reshape(tm, SSM_WIDTH).astype(BF16)
    gl = jnp.dot(ys, wglu_ref[...], preferred_element_type=F32)
    y_ssm = gl[:, :SSM_WIDTH] * jax.nn.sigmoid(gl[:, SSM_WIDTH:])
    p_attn = jnp.dot(ya_ref[...], pa_ref[...], preferred_element_type=F32)
    p_ssm = jnp.dot(y_ssm.astype(BF16), ps_ref[...], preferred_element_type=F32)
    merged = gs_ref[:, :D].astype(F32) * p_attn + gs_ref[:, D:].astype(F32) * p_ssm
    mixed = jnp.dot(merged.astype(BF16), wout_ref[...], preferred_element_type=F32)
    x1 = x_ref[...] + mod_ref[2:3, :] * mixed
    x1_ref[...] = x1
    h2 = _rms(x1, NORM_EPS) * g2_ref[...] * (1.0 + mod_ref[4:5, :]) + mod_ref[3:4, :]
    h2_ref[...] = h2.astype(BF16)

    logits = lax.dot_general(wr_ref[...], h2, (((1,), (1,)), ((), ())),
                             preferred_element_type=F32, precision=HIGHEST) + br_ref[...]
    NG, EPG = MOE_GROUPS, MOE_EXPERTS_PER_GROUP
    lg = logits[0:NG, :]
    g_iota = lax.broadcasted_iota(I32, lg.shape, 0)
    lg_max = jnp.max(lg, axis=0, keepdims=True)
    grp = jnp.min(jnp.where(lg == lg_max, g_iota, NG), axis=0, keepdims=True)
    p_grp = 1.0 / jnp.sum(jnp.exp(lg - lg_max), axis=0, keepdims=True)
    le = logits[NG:NG + EPG, :]
    for g in range(1, NG):
        le = jnp.where(grp == g, logits[NG + g * EPG:NG + (g + 1) * EPG, :], le)
    e_iota = lax.broadcasted_iota(I32, le.shape, 0)
    v1 = jnp.max(le, axis=0, keepdims=True)
    i1 = jnp.min(jnp.where(le == v1, e_iota, EPG), axis=0, keepdims=True)
    le2 = jnp.where(e_iota == i1, -jnp.inf, le)
    v2 = jnp.max(le2, axis=0, keepdims=True)
    i2 = jnp.min(jnp.where(le2 == v2, e_iota, EPG), axis=0, keepdims=True)
    e21 = jnp.exp(v2 - v1)
    w1 = p_grp / (1.0 + e21)
    w2 = p_grp * e21 / (1.0 + e21)
    eid1 = grp * EPG + i1
    eid2 = grp * EPG + i2

    x_iota = lax.broadcasted_iota(I32, (MOE_EXPERTS, tm), 0)
    hot1 = x_iota == eid1
    hot2 = x_iota == eid2
    hot = jnp.logical_or(hot1, hot2).astype(F32)
    before = (lax.broadcasted_iota(I32, (tm, tm), 0) < lax.broadcasted_iota(I32, (tm, tm), 1))
    prior = jnp.dot(hot.astype(BF16), before.astype(BF16), preferred_element_type=F32)
    rank1 = jnp.sum(jnp.where(hot1, prior, 0.0), axis=0, keepdims=True)
    rank2 = jnp.sum(jnp.where(hot2, prior, 0.0), axis=0, keepdims=True)
    cnt_ref[...] = jnp.sum(hot, axis=1, keepdims=True).astype(I32)

    zi = jnp.zeros((4, tm), I32)
    ri_ref[...] = jnp.concatenate([eid1, eid2, rank1.astype(I32), rank2.astype(I32), zi], axis=0)
    rw_ref[...] = jnp.concatenate([w1, w2, jnp.zeros((6, tm), F32)], axis=0)


def _merge_route(x2, ya, ys, gs, mod, w_glu, w_pa, w_ps, w_out, norm2_g, w_rg, b_rg, w_re, b_re, seq):
    T, D = x2.shape
    tm = TOKEN_TILE
    per_b = seq // tm
    wr = jnp.concatenate([w_rg.T, jnp.transpose(w_re, (0, 2, 1)).reshape(MOE_EXPERTS, D),
                          jnp.zeros((ROUTER_ROWS - MOE_GROUPS - MOE_EXPERTS, D), F32)], axis=0).astype(F32)
    br = jnp.concatenate([b_rg, b_re.reshape(-1),
                          jnp.zeros((ROUTER_ROWS - MOE_GROUPS - MOE_EXPERTS,), F32)]).reshape(ROUTER_ROWS, 1)
    row = lambda i: (i, 0)
    col = lambda i: (0, i)
    full = lambda i: (0, 0)
    return pl.pallas_call(
        _merge_kernel,
        grid=(T // tm,),
        in_specs=[pl.BlockSpec((tm, D), row),
                  pl.BlockSpec((tm, ATTN_WIDTH), row),
                  _chunk_major_spec(tm, per_b),
                  pl.BlockSpec((tm, 2 * D), row),
                  pl.BlockSpec((None, 6, D), lambda i: (i // per_b, 0, 0)),
                  pl.BlockSpec(w_glu.shape, full),
                  pl.BlockSpec(w_pa.shape, full),
                  pl.BlockSpec(w_ps.shape, full),
                  pl.BlockSpec(w_out.shape, full),
                  pl.BlockSpec((1, D), full),
                  pl.BlockSpec((ROUTER_ROWS, D), full),
                  pl.BlockSpec((ROUTER_ROWS, 1), full)],
        out_specs=[pl.BlockSpec((tm, D), row), pl.BlockSpec((tm, D), row),
                   pl.BlockSpec((8, tm), col), pl.BlockSpec((8, tm), col),
                   pl.BlockSpec((None, MOE_EXPERTS, 1), lambda i: (i, 0, 0))],
        out_shape=[jax.ShapeDtypeStruct((T, D), F32), jax.ShapeDtypeStruct((T, D), BF16),
                   jax.ShapeDtypeStruct((8, T), I32), jax.ShapeDtypeStruct((8, T), F32),
                   jax.ShapeDtypeStruct((T // tm, MOE_EXPERTS, 1), I32)],
        compiler_params=_params("parallel"),
        name="merge_route",
    )(x2, ya, ys, gs, mod, w_glu.astype(BF16), w_pa.astype(BF16), w_ps.astype(BF16), w_out.astype(BF16),
      norm2_g.reshape(1, D), wr, br)


def _tile_positions(ri_ref, seg_ref):
    tm = ri_ref.shape[1]
    x_iota = lax.broadcasted_iota(I32, (MOE_EXPERTS, tm), 0)
    seg = seg_ref[...].astype(F32)
    pos = []
    for k in range(2):
        start = jnp.sum(jnp.where(x_iota == ri_ref[k:k + 1, :], seg, 0.0), axis=0, keepdims=True)
        pos.append(start + ri_ref[2 + k:3 + k, :].astype(F32))
    return pos


def _segment_copies(meta, tile, make_copy):
    seg_row, dst_row, n_chunk = meta

    def per_expert(e, carry):
        idx = tile * MOE_EXPERTS + e
        src0 = seg_row[idx]
        dst0 = dst_row[idx]

        def per_chunk(c, carry2):
            off = c * SEG_ALIGN
            make_copy(pl.multiple_of(src0 + off, SEG_ALIGN), pl.multiple_of(dst0 + off, SEG_ALIGN))
            return carry2

        return lax.fori_loop(0, n_chunk[idx], per_chunk, carry)

    lax.fori_loop(0, MOE_EXPERTS, per_expert, 0)


def _dispatch_kernel(seg_row, dst_row, n_chunk, tile_chunks, pad_row, pad_chunks, n_used,
                     h_ref, ri_ref, rw_ref, seg_ref, xs_ref, pw_ref, zbuf, zeros_scr, sem, pad_sem, tail_sem):
    i = pl.program_id(0)
    n_tiles = pl.num_programs(0)
    slot = i % 2
    tm = h_ref.shape[0]
    pos1, pos2 = _tile_positions(ri_ref, seg_ref)
    pw_ref[...] = jnp.concatenate([pos1, pos2, rw_ref[0:2, :], jnp.zeros((4, tm), F32)], axis=0)
    r_iota = lax.broadcasted_iota(I32, (TILE_SLOTS, tm), 0).astype(F32)
    onehot = jnp.logical_or(r_iota == pos1, r_iota == pos2).astype(BF16)
    zbuf[slot] = jnp.dot(onehot, h_ref[...], preferred_element_type=F32)

    def chunk_copy(buf_slot, src, dst):
        return pltpu.make_async_copy(zbuf.at[buf_slot, pl.ds(src, SEG_ALIGN)],
                                     xs_ref.at[pl.ds(dst, SEG_ALIGN)], sem.at[buf_slot])

    _segment_copies((seg_row, dst_row, n_chunk), i, lambda src, dst: chunk_copy(slot, src, dst).start())

    def wait_tile(tile, buf_slot):
        def body(c, carry):
            chunk_copy(buf_slot, 0, 0).wait()
            return carry
        lax.fori_loop(0, tile_chunks[tile], body, 0)

    @pl.when(i > 0)
    def _():
        wait_tile(i - 1, 1 - slot)

    @pl.when(i == n_tiles - 1)
    def _():
        zeros_scr[...] = jnp.zeros(zeros_scr.shape, F32)

        def pad_copy(dst):
            return pltpu.make_async_copy(zeros_scr.at[pl.ds(0, SEG_ALIGN)], xs_ref.at[pl.ds(dst, SEG_ALIGN)], pad_sem)

        def tail_copy(dst):
            return pltpu.make_async_copy(zeros_scr, xs_ref.at[pl.ds(dst, MOE_BLOCK)], tail_sem)

        def per_expert(e, total):
            def per_chunk(c, carry):
                pad_copy(pl.multiple_of(pad_row[e] + c * SEG_ALIGN, SEG_ALIGN)).start()
                return carry
            lax.fori_loop(0, pad_chunks[e], per_chunk, 0)
            return total + pad_chunks[e]

        n_pad_copies = lax.fori_loop(0, MOE_EXPERTS, per_expert, 0)
        n_blocks = xs_ref.shape[0] // MOE_BLOCK

        def tail_start(b, carry):
            tail_copy(pl.multiple_of(b * MOE_BLOCK, MOE_BLOCK)).start()
            return carry
        lax.fori_loop(n_used[0], n_blocks, tail_start, 0)
        wait_tile(i, slot)

        def wait_pad(c, carry):
            pad_copy(0).wait()
            return carry
        lax.fori_loop(0, n_pad_copies, wait_pad, 0)

        def wait_tail(b, carry):
            tail_copy(0).wait()
            return carry
        lax.fori_loop(n_used[0], n_blocks, wait_tail, 0)


def _dispatch(meta, h2, ri, rw, seg_start, n_rows):
    T, D = h2.shape
    tm = TOKEN_TILE
    col = lambda i, *_: (0, i)
    return pl.pallas_call(
        _dispatch_kernel,
        grid_spec=pltpu.PrefetchScalarGridSpec(
            num_scalar_prefetch=7,
            grid=(T // tm,),
            in_specs=[pl.BlockSpec((tm, D), lambda i, *_: (i, 0)),
                      pl.BlockSpec((8, tm), col),
                      pl.BlockSpec((8, tm), col),
                      pl.BlockSpec((None, MOE_EXPERTS, 1), lambda i, *_: (i, 0, 0))],
            out_specs=[pl.BlockSpec(memory_space=pl.ANY), pl.BlockSpec((8, tm), col)],
            scratch_shapes=[pltpu.VMEM((2, TILE_SLOTS, D), F32), pltpu.VMEM((MOE_BLOCK, D), F32),
                            pltpu.SemaphoreType.DMA((2,)), pltpu.SemaphoreType.DMA(()),
                            pltpu.SemaphoreType.DMA(())]),
        out_shape=[jax.ShapeDtypeStruct((n_rows, D), F32), jax.ShapeDtypeStruct((8, T), F32)],
        compiler_params=_params("arbitrary"),
        name="moe_dispatch",
    )(*meta, h2, ri, rw, seg_start)


def _expert_kernel(be_ref, nb_ref, x_ref, wi_ref, wo_ref, o_ref):
    del be_ref
    F = wo_ref.shape[0]

    @pl.when(pl.program_id(0) < nb_ref[0])
    def _():
        hid = jnp.dot(x_ref[...].astype(BF16), wi_ref[...], preferred_element_type=F32)
        a = hid[:, :F]
        act = a * jax.nn.sigmoid(a) * hid[:, F:]
        o_ref[...] = jnp.dot(act.astype(BF16), wo_ref[...], preferred_element_type=F32)

    @pl.when(pl.program_id(0) >= nb_ref[0])
    def _():
        o_ref[...] = jnp.zeros(o_ref.shape, F32)


def _experts(block_e, n_used, xs, w_e_in, w_e_out):
    n_rows, D = xs.shape
    F = w_e_out.shape[1]
    blk = lambda i, be, nb: jnp.maximum(jnp.minimum(i, nb[0] - 1), 0)
    return pl.pallas_call(
        _expert_kernel,
        grid_spec=pltpu.PrefetchScalarGridSpec(
            num_scalar_prefetch=2,
            grid=(n_rows // MOE_BLOCK,),
            in_specs=[pl.BlockSpec((MOE_BLOCK, D), lambda i, be, nb: (blk(i, be, nb), 0)),
                      pl.BlockSpec((None, D, 2 * F), lambda i, be, nb: (be[blk(i, be, nb)], 0, 0)),
                      pl.BlockSpec((None, F, D), lambda i, be, nb: (be[blk(i, be, nb)], 0, 0))],
            out_specs=pl.BlockSpec((MOE_BLOCK, D), lambda i, be, nb: (i, 0))),
        out_shape=jax.ShapeDtypeStruct((n_rows, D), F32),
        compiler_params=_params("arbitrary"),
        name="experts",
    )(block_e, n_used, xs, w_e_in.astype(BF16), w_e_out.astype(BF16))


def _final_kernel(seg_row, dst_row, n_chunk, tile_chunks, x1_ref, pw_ref, mod_ref, g_ref, ys_ref, o_ref,
                  ybuf, sem):
    i = pl.program_id(0)
    n_tiles = pl.num_programs(0)
    slot = i % 2
    tm = x1_ref.shape[0]

    def chunk_copy(buf_slot, src, dst):
        return pltpu.make_async_copy(ys_ref.at[pl.ds(dst, SEG_ALIGN)],
                                     ybuf.at[buf_slot, pl.ds(src, SEG_ALIGN)], sem.at[buf_slot])

    def fetch(tile, buf_slot):
        _segment_copies((seg_row, dst_row, n_chunk), tile, lambda src, dst: chunk_copy(buf_slot, src, dst).start())

    @pl.when(i == 0)
    def _():
        ybuf[...] = jnp.zeros(ybuf.shape, F32)
        fetch(0, 0)

    @pl.when(i + 1 < n_tiles)
    def _():
        fetch(i + 1, 1 - slot)

    def wait_one(c, carry):
        chunk_copy(slot, 0, 0).wait()
        return carry
    lax.fori_loop(0, tile_chunks[i], wait_one, 0)

    s_iota = lax.broadcasted_iota(I32, (tm, TILE_SLOTS), 1).astype(F32)
    comb = (jnp.where(s_iota == pw_ref[:, 0:1], pw_ref[:, 2:3], 0.0)
            + jnp.where(s_iota == pw_ref[:, 1:2], pw_ref[:, 3:4], 0.0))
    moe = jnp.dot(comb.astype(BF16), ybuf[slot].astype(BF16), preferred_element_type=F32)
    x2 = x1_ref[...] + mod_ref[5:6, :] * moe
    o_ref[...] = _rms(x2, NORM_EPS) * g_ref[...]


def _final(meta, x1, pw_tok, mod, final_g, ys, seq):
    T, D = x1.shape
    tm = TOKEN_TILE
    per_b = seq // tm
    row = lambda i, *_: (i, 0)
    return pl.pallas_call(
        _final_kernel,
        grid_spec=pltpu.PrefetchScalarGridSpec(
            num_scalar_prefetch=4,
            grid=(T // tm,),
            in_specs=[pl.BlockSpec((tm, D), row),
                      pl.BlockSpec((tm, 8), row),
                      pl.BlockSpec((None, 6, D), lambda i, *_: (i // per_b, 0, 0)),
                      pl.BlockSpec((1, D), lambda i, *_: (0, 0)),
                      pl.BlockSpec(memory_space=pl.ANY)],
            out_specs=pl.BlockSpec((tm, D), row),
            scratch_shapes=[pltpu.VMEM((2, TILE_SLOTS, D), F32), pltpu.SemaphoreType.DMA((2,))]),
        out_shape=jax.ShapeDtypeStruct((T, D), F32),
        compiler_params=_params("arbitrary"),
        name="final",
    )(*meta, x1, pw_tok, mod, final_g.reshape(1, D), ys)


def _round_up(x, m):
    return (x + m - 1) // m * m


def _moe_layout(tile_counts):
    n_tiles = tile_counts.shape[0]
    seg = _round_up(tile_counts, SEG_ALIGN)
    seg_start = jnp.cumsum(seg, axis=1) - seg
    tile_base = jnp.cumsum(seg, axis=0) - seg
    used = jnp.sum(seg, axis=0)
    region = _round_up(used, MOE_BLOCK)
    region_end = jnp.cumsum(region)
    region_start = region_end - region
    dst_row = region_start[None, :] + tile_base
    n_chunk = seg // SEG_ALIGN
    n_assign = 2 * n_tiles * TOKEN_TILE
    n_rows = _round_up(n_assign + n_tiles * MOE_EXPERTS * (SEG_ALIGN - 1) + MOE_EXPERTS * (MOE_BLOCK - 1), MOE_BLOCK)
    block_start = jnp.arange(n_rows // MOE_BLOCK, dtype=I32) * MOE_BLOCK
    block_e = jnp.minimum(jnp.sum(block_start[:, None] >= region_end[None, :], axis=1), MOE_EXPERTS - 1)
    i32 = lambda a: a.reshape(-1).astype(I32)
    meta = (i32(seg_start), i32(dst_row), i32(n_chunk), i32(jnp.sum(n_chunk, axis=1)))
    pad = (i32(region_start + used), i32((region - used) // SEG_ALIGN))
    n_used = (region_end[-1:] // MOE_BLOCK).astype(I32)
    return meta, pad, seg_start.astype(I32)[:, :, None], block_e.astype(I32), n_used, n_rows


def kernel(x, c, w_ada, b_ada, norm1_g, w_in, rel_bias, lambda_q1, lambda_k1, lambda_q2, lambda_k2, subln_g, ssm_lambda_re, ssm_lambda_im, ssm_log_step, ssm_b_re, ssm_b_im, ssm_c_re, ssm_c_im, ssm_d, w_glu, w_proj_attn, w_proj_ssm, w_out, norm2_g, w_router_group, b_router_group, w_router_expert, b_router_expert, w_expert_in, w_expert_out, final_g):
    B, S, D = x.shape
    T = B * S
    x2 = x.reshape(T, D)
    mod = _ada_mod(c, w_ada[0], b_ada[0]).reshape(B, 6, D)
    q, k, v, u, gs = _in_proj(x2, mod, norm1_g[0], w_in[0], S)
    lam_vecs = jnp.stack([lambda_q1[0], lambda_k1[0], lambda_q2[0], lambda_k2[0]]).astype(F32)
    y_attn = _diff_attn(q, k, v, rel_bias, lam_vecs, subln_g[0], B, S)
    tables = _s5_tables(ssm_lambda_re[0], ssm_lambda_im[0], ssm_log_step[0], ssm_b_re[0], ssm_b_im[0],
                        ssm_c_re[0], ssm_c_im[0], ssm_d[0])
    y_s5 = _s5_branch(u.reshape(-1, SSM_WIDTH), tables, B, S).reshape(u.shape)
    x1, h2, ri, rw, tile_counts = _merge_route(
        x2, y_attn, y_s5, gs, mod, w_glu[0], w_proj_attn[0], w_proj_ssm[0], w_out[0], norm2_g[0],
        w_router_group[0], b_router_group[0], w_router_expert[0], b_router_expert[0], S)
    meta, pad, seg_start, block_e, n_used, n_rows = _moe_layout(tile_counts[:, :, 0])
    xs, pw = _dispatch(meta + pad + (n_used,), h2, ri, rw, seg_start, n_rows)
    ys = _experts(block_e, n_used, xs, w_expert_in[0], w_expert_out[0])
    out = _final(meta, x1, pw.T, mod, final_g, ys, S)
    return out.reshape(B, S, D)
```

```python
import functools
import math

import jax
import jax.numpy as jnp
from jax import lax
from jax.experimental import pallas as pl
from jax.experimental.pallas import tpu as pltpu

F32 = jnp.float32
BF16 = jnp.bfloat16
I32 = jnp.int32
HIGHEST = lax.Precision.HIGHEST

ATTN_HEADS = 4
ATTN_HEAD_DIM = 64
ATTN_V_DIM = 2 * ATTN_HEAD_DIM
ATTN_WIDTH = ATTN_HEADS * ATTN_V_DIM
NEG_INF = -1e30
REL_BUCKETS = 32
REL_MAX_DISTANCE = 128
SSM_GROUP_CH = 16
SSM_WIDTH = 512
SSM_GROUPS = SSM_WIDTH // SSM_GROUP_CH
SSM_STATE = 64
SSM_EIG_MAX_RE = -1e-4
SSM_CHUNK = 16
S5_CHUNKS_PER_STEP = 32
MOE_GROUPS = 4
MOE_EXPERTS_PER_GROUP = 8
MOE_EXPERTS = MOE_GROUPS * MOE_EXPERTS_PER_GROUP
MOE_BLOCK = 512
SEG_ALIGN = 8
NORM_EPS = 1e-6
SUBLN_EPS = 1e-5
LAMBDA_INIT = 0.8 - 0.6 * math.exp(-0.3 * 0)

ATTN_BLOCK = 256
ATTN_HEADS_PER_STEP = 4
ATTN_ROW_CHUNK = 32
ATTN_ONES_ROWS = 16
LOG2E = math.log2(math.e)
TOKEN_TILE = 512
ROUTER_ROWS = 40
TILE_SLOTS = -(-(2 * TOKEN_TILE + MOE_EXPERTS * (SEG_ALIGN - 1)) // 256) * 256
TILE_CHUNKS = TILE_SLOTS // SEG_ALIGN
MIN_TILE_CHUNKS = 2 * TOKEN_TILE // SEG_ALIGN
CHUNK_SLOT_BITS = 8
assert TILE_CHUNKS <= 2 ** CHUNK_SLOT_BITS
VMEM_LIMIT = 56 << 20


def _params(*sem):
    return pltpu.CompilerParams(dimension_semantics=sem, vmem_limit_bytes=VMEM_LIMIT)


def _rms(x, eps):
    return x * lax.rsqrt(jnp.mean(x * x, axis=-1, keepdims=True) + eps)


def _mod_kernel(c_ref, w_ref, b_ref, o_ref):
    c = c_ref[...]
    c_act = c * jax.nn.sigmoid(c)
    o_ref[...] = jnp.dot(c_act, w_ref[...], preferred_element_type=F32, precision=HIGHEST) + b_ref[...]


def _ada_mod(c, w_ada, b_ada):
    B, D = c.shape
    N = w_ada.shape[1]
    tn = 1024
    return pl.pallas_call(
        _mod_kernel,
        grid=(N // tn,),
        in_specs=[pl.BlockSpec((B, D), lambda j: (0, 0)),
                  pl.BlockSpec((D, tn), lambda j: (0, j)),
                  pl.BlockSpec((1, tn), lambda j: (0, j))],
        out_specs=pl.BlockSpec((B, tn), lambda j: (0, j)),
        out_shape=jax.ShapeDtypeStruct((B, N), F32),
        compiler_params=_params("arbitrary"),
        name="ada_mod",
    )(c, w_ada, b_ada.reshape(1, N))


def _proj_kernel(x_ref, mod_ref, g_ref, w_ref, q_ref, k_ref, v_ref, u_ref, gs_ref):
    y = _rms(x_ref[...], NORM_EPS) * g_ref[...]
    h = (y * (1.0 + mod_ref[1:2, :]) + mod_ref[0:1, :]).astype(BF16)
    W = ATTN_WIDTH

    def proj(lo, hi):
        return jnp.dot(h, w_ref[:, lo:hi], preferred_element_type=F32)

    q_ref[...] = (proj(0, W) * (ATTN_HEAD_DIM ** -0.5 * LOG2E)).astype(BF16)
    k_ref[...] = proj(W, 2 * W).astype(BF16)
    v_ref[...] = proj(2 * W, 3 * W).astype(BF16)
    u_ref[...] = proj(3 * W, 3 * W + SSM_WIDTH).reshape(u_ref.shape)
    gs_ref[...] = jax.nn.sigmoid(proj(3 * W + SSM_WIDTH, w_ref.shape[1])).astype(BF16)


def _chunk_major_spec(tm, per_b):
    return pl.BlockSpec((tm // SSM_CHUNK, None, SSM_CHUNK, SSM_WIDTH), lambda i: (i % per_b, i // per_b, 0, 0))


def _in_proj(x2, mod, norm_g, w_in, seq):
    T, D = x2.shape
    tm = TOKEN_TILE
    per_b = seq // tm
    n_gate = w_in.shape[1] - 3 * ATTN_WIDTH - SSM_WIDTH
    row = lambda i: (i, 0)
    return pl.pallas_call(
        _proj_kernel,
        grid=(T // tm,),
        in_specs=[pl.BlockSpec((tm, D), row),
                  pl.BlockSpec((None, 6, D), lambda i: (i // per_b, 0, 0)),
                  pl.BlockSpec((1, D), lambda i: (0, 0)),
                  pl.BlockSpec(w_in.shape, lambda i: (0, 0))],
        out_specs=[pl.BlockSpec((tm, ATTN_WIDTH), row)] * 3
        + [_chunk_major_spec(tm, per_b), pl.BlockSpec((tm, n_gate), row)],
        out_shape=[jax.ShapeDtypeStruct((T, ATTN_WIDTH), BF16)] * 3
        + [jax.ShapeDtypeStruct((seq // SSM_CHUNK, T // seq, SSM_CHUNK, SSM_WIDTH), F32),
           jax.ShapeDtypeStruct((T, n_gate), BF16)],
        compiler_params=_params("parallel"),
        name="in_proj",
    )(x2, mod, norm_g.reshape(1, D), w_in.astype(BF16))


def _rel_bucket(dist):
    max_exact = REL_BUCKETS // 2
    n = jnp.maximum(dist, 0)
    log_ratio = jnp.log(jnp.maximum(n, 1).astype(F32) / max_exact) / math.log(REL_MAX_DISTANCE / max_exact)
    large = max_exact + (log_ratio * (REL_BUCKETS - max_exact)).astype(I32)
    large = jnp.minimum(large, REL_BUCKETS - 1)
    return jnp.where(n < max_exact, n, large)


def _attn_bias_tiles(rel_bias, blk):
    assert blk >= REL_MAX_DISTANCE
    n_heads = rel_bias.shape[1]
    far = rel_bias[REL_BUCKETS - 1].astype(F32)
    m = jnp.arange(2 * blk)
    signed = jnp.where(m < blk, m, m - 2 * blk)
    tiles = []
    for kind in range(2):
        dist = kind * blk + signed
        tab = jnp.where(dist >= 0, (rel_bias[_rel_bucket(dist)].astype(F32).T - far[:, None]) * LOG2E, NEG_INF)
        skew = jnp.tile(tab, (1, blk))[:, :blk * (2 * blk - 1)].reshape(n_heads, blk, 2 * blk - 1)
        tiles.append(skew[:, :, :blk])
    return jnp.stack(tiles, axis=1)


def _attn_kernel(lam_ref, q_ref, k_ref, v_ref, bias_ref, g_ref, o_ref, vt_scr, *bufs, blk, heads):
    i = pl.program_id(2)
    n_kv = vt_scr.shape[1]
    V = ATTN_V_DIM
    ns = 2 * heads
    acc = bufs[0:ns]
    sbuf = tuple(bufs[(1 + r) * ns:(2 + r) * ns] for r in range(3))
    pbufs = tuple(bufs[(4 + r) * ns:(5 + r) * ns] for r in range(3))

    @pl.when(i == 0)
    def _():
        for hd in range(heads):
            for jb in range(n_kv):
                vt_scr[hd, jb, 0:V, :] = v_ref[jb * blk:(jb + 1) * blk, hd * V:(hd + 1) * V].astype(F32).T.astype(BF16)
                vt_scr[hd, jb, V:, :] = jnp.ones((ATTN_ONES_ROWS, blk), BF16)

    qt_maps = []
    for hd in range(heads):
        qt = q_ref[:, hd * V:(hd + 1) * V].astype(F32).T
        feat = lax.broadcasted_iota(I32, qt.shape, 0)
        qt_maps += [jnp.where(feat < ATTN_HEAD_DIM, qt, 0.0).astype(BF16),
                    jnp.where(feat >= ATTN_HEAD_DIM, qt, 0.0).astype(BF16)]

    def scores(j, st):
        hd = st // 2
        kj = k_ref[pl.ds(pl.multiple_of(j * blk, blk), blk), hd * V:(hd + 1) * V]
        return jnp.dot(kj, qt_maps[st], preferred_element_type=F32)

    n_chunks = blk // ATTN_ROW_CHUNK

    def rows(c):
        return slice(c * ATTN_ROW_CHUNK, (c + 1) * ATTN_ROW_CHUNK)

    def fold8(x):
        return x.reshape(ATTN_ROW_CHUNK // 8, 8, blk)

    def block(j, carry, pos, lookahead, bias_kind):
        src, dst, pbuf = sbuf[pos], sbuf[(pos + 2) % 3], pbufs[pos]
        out = []
        for st in range(ns):
            hd = st // 2
            if lookahead:
                dst[st][...] = scores(j + 2, st)

            def chunk(c):
                s = src[st][rows(c), :]
                return s if bias_kind is None else s + bias_ref[hd, bias_kind, rows(c), :]

            m_old = carry[st]
            m8 = jnp.max(fold8(chunk(0)), axis=0)
            for c in range(1, n_chunks):
                m8 = jnp.maximum(m8, jnp.max(fold8(chunk(c)), axis=0))
            m_new = jnp.maximum(m_old, jnp.max(m8, axis=0, keepdims=True))
            alpha = jnp.exp2(m_old - m_new)
            for c in range(n_chunks):
                pbuf[st][rows(c), :] = jnp.exp2(chunk(c) - m_new).astype(BF16)
            acc[st][...] = alpha * acc[st][...] + jnp.dot(vt_scr[hd, j], pbuf[st][...],
                                                          preferred_element_type=F32)
            out.append(m_new)
        return tuple(out)

    def far_triple(t, carry):
        for r in range(3):
            carry = block(3 * t + r, carry, r, True, None)
        return carry

    def far_single(j, carry):
        carry = block(j, carry, 0, True, None)
        for st in range(ns):
            sbuf[0][st][...] = sbuf[1][st][...]
        for st in range(ns):
            sbuf[1][st][...] = sbuf[2][st][...]
        return carry

    def near_pair(_, carry):
        return block(i, block(i - 1, carry, 0, False, 1), 1, False, 0)

    def near_single(_, carry):
        return block(i, carry, 0, False, 0)

    for st in range(ns):
        acc[st][...] = jnp.zeros(acc[st].shape, F32)
        sbuf[0][st][...] = scores(0, st)
        sbuf[1][st][...] = scores(jnp.minimum(i, 1), st)
    m0 = jnp.full((1, blk), -jnp.inf, F32)
    n_far = jnp.maximum(i - 1, 0)
    carry = lax.fori_loop(0, n_far // 3, far_triple, (m0,) * ns)
    carry = lax.fori_loop(n_far - n_far % 3, n_far, far_single, carry)
    carry = lax.fori_loop(0, jnp.minimum(i, 1), near_pair, carry)
    lax.fori_loop(0, 1 - jnp.minimum(i, 1), near_single, carry)

    lam = (jnp.exp(jnp.sum(lam_ref[0:1, :] * lam_ref[1:2, :], axis=-1, keepdims=True))
           - jnp.exp(jnp.sum(lam_ref[2:3, :] * lam_ref[3:4, :], axis=-1, keepdims=True)) + LAMBDA_INIT)
    for hd in range(heads):
        a1, a2 = acc[2 * hd], acc[2 * hd + 1]
        ot = a1[0:V, :] / a1[V:V + 1, :] - lam * (a2[0:V, :] / a2[V:V + 1, :])
        ot = ot * lax.rsqrt(jnp.mean(ot * ot, axis=0, keepdims=True) + SUBLN_EPS)
        o_ref[:, hd * V:(hd + 1) * V] = (ot.T * (g_ref[...] * (1.0 - LAMBDA_INIT))).astype(BF16)


def _diff_attn(q, k, v, rel_bias, lam_vecs, subln_g, batch, seq):
    T = q.shape[0]
    blk = ATTN_BLOCK
    nq = seq // blk
    bias = _attn_bias_tiles(rel_bias, blk)
    hp = ATTN_HEADS_PER_STEP
    ns = 2 * hp
    width = hp * ATTN_V_DIM
    acc_rows = ATTN_V_DIM + ATTN_ONES_ROWS
    return pl.pallas_call(
        functools.partial(_attn_kernel, blk=blk, heads=hp),
        grid=(batch, ATTN_HEADS // hp, nq),
        in_specs=[pl.BlockSpec((4, ATTN_HEAD_DIM), lambda b, h, i: (0, 0)),
                  pl.BlockSpec((blk, width), lambda b, h, i: (b * nq + i, h)),
                  pl.BlockSpec((seq, width), lambda b, h, i: (b, h)),
                  pl.BlockSpec((seq, width), lambda b, h, i: (b, h)),
                  pl.BlockSpec((hp, 2, blk, blk), lambda b, h, i: (h, 0, 0, 0)),
                  pl.BlockSpec((1, ATTN_V_DIM), lambda b, h, i: (0, 0))],
        out_specs=pl.BlockSpec((blk, width), lambda b, h, i: (b * nq + i, h)),
        out_shape=jax.ShapeDtypeStruct((T, ATTN_WIDTH), BF16),
        scratch_shapes=[pltpu.VMEM((hp, nq, acc_rows, blk), BF16)]
        + [pltpu.VMEM((acc_rows, blk), F32)] * ns + [pltpu.VMEM((blk, blk), F32)] * (3 * ns)
        + [pltpu.VMEM((blk, blk), BF16)] * (3 * ns),
        compiler_params=_params("parallel", "parallel", "arbitrary"),
        name="diff_attn",
    )(lam_vecs, q, k, v, bias, subln_g.reshape(1, ATTN_V_DIM))


def _s5_tables(lam_re, lam_im, log_step, b_re, b_im, c_re, c_im, d_skip):
    L = SSM_CHUNK
    G, P = lam_re.shape
    H = SSM_GROUP_CH
    lr = jnp.minimum(lam_re.astype(F32), SSM_EIG_MAX_RE)
    li = lam_im.astype(F32)
    step = jnp.exp(log_step.astype(F32))[:, None]
    mag = jnp.exp(lr * step)
    ang = li * step
    a_re = mag * jnp.cos(ang)
    a_im = mag * jnp.sin(ang)
    den = lr * lr + li * li
    num_re = a_re - 1.0
    coef_re = (num_re * lr + a_im * li) / den
    coef_im = (a_im * lr - num_re * li) / den
    br = b_re.astype(F32)
    bi = b_im.astype(F32)
    bb_re = coef_re[..., None] * br - coef_im[..., None] * bi
    bb_im = coef_re[..., None] * bi + coef_im[..., None] * br
    pw_re, pw_im = [jnp.ones_like(a_re)], [jnp.zeros_like(a_re)]
    for _ in range(L):
        pr, pi = pw_re[-1], pw_im[-1]
        pw_re.append(pr * a_re - pi * a_im)
        pw_im.append(pr * a_im + pi * a_re)
    pw_re = jnp.stack(pw_re)
    pw_im = jnp.stack(pw_im)
    cr = c_re.astype(F32)[None]
    ci = c_im.astype(F32)[None]
    cp_re = cr * pw_re[:, :, None, :] - ci * pw_im[:, :, None, :]
    cp_im = cr * pw_im[:, :, None, :] + ci * pw_re[:, :, None, :]
    kern = (jnp.einsum('tghp,gpk->tghk', cp_re[:L], bb_re, precision=HIGHEST)
            - jnp.einsum('tghp,gpk->tghk', cp_im[:L], bb_im, precision=HIGHEST))
    s_idx = jnp.arange(L)[:, None]
    t_idx = jnp.arange(L)[None, :]
    toep = jnp.where((t_idx >= s_idx)[:, :, None, None, None], kern[jnp.maximum(t_idx - s_idx, 0)], 0.0)
    m_tab = jnp.transpose(toep, (2, 0, 4, 1, 3)).reshape(G, L * H, L * H)
    rev_re = pw_re[L - 1::-1][:, :, None, :]
    rev_im = pw_im[L - 1::-1][:, :, None, :]
    bbt_re = jnp.transpose(bb_re, (0, 2, 1))[None]
    bbt_im = jnp.transpose(bb_im, (0, 2, 1))[None]
    bst_re = jnp.transpose(rev_re * bbt_re - rev_im * bbt_im, (1, 0, 2, 3)).reshape(G, L * H, P)
    bst_im = jnp.transpose(rev_re * bbt_im + rev_im * bbt_re, (1, 0, 2, 3)).reshape(G, L * H, P)
    cst_re = jnp.transpose(cp_re[1:], (1, 3, 0, 2)).reshape(G, P, L * H)
    cst_im = -jnp.transpose(cp_im[1:], (1, 3, 0, 2)).reshape(G, P, L * H)
    a_chunk = jnp.stack([jnp.concatenate([pw_re[L], pw_re[L]], axis=-1),
                         jnp.concatenate([-pw_im[L], pw_im[L]], axis=-1)], axis=1)
    d_tab = jnp.tile(d_skip.astype(F32), (1, L)).reshape(G, 1, L * H)
    bst = jnp.concatenate([bst_re, bst_im], axis=-1)
    cst = jnp.concatenate([cst_re, cst_im], axis=1)
    return m_tab.astype(BF16), bst.astype(BF16), cst.astype(BF16), a_chunk, d_tab


def _gelu_tanh(x):
    return 0.5 * x * (1.0 + jnp.tanh(math.sqrt(2.0 / math.pi) * (x + 0.044715 * (x * x * x))))


def _lane_block_transpose(arrs):
    n = len(arrs)
    width = arrs[0].shape[1]
    blk_id = lax.broadcasted_iota(I32, arrs[0].shape, 1) // SSM_GROUP_CH
    k = n // 2
    while k >= 1:
        keep = (blk_id & k) == 0
        nxt = list(arrs)
        for r in range(n):
            if r & k == 0:
                a, b = arrs[r], arrs[r + k]
                nxt[r] = jnp.where(keep, a, pltpu.roll(b, k * SSM_GROUP_CH, axis=1))
                nxt[r + k] = jnp.where(keep, pltpu.roll(a, width - k * SSM_GROUP_CH, axis=1), b)
        arrs = nxt
        k //= 2
    return arrs


def _s5_kernel(u_ref, m_ref, bst_ref, cst_ref, a_ref, d_ref, o_ref, us_scr, z_scr, y_scr, st_scr, *, batch):
    L, H = SSM_CHUNK, SSM_GROUP_CH
    n_grp = us_scr.shape[0]
    R = us_scr.shape[1]
    half = 128 // H

    @pl.when(pl.program_id(1) == 0)
    def _():
        st_scr[...] = jnp.zeros(st_scr.shape, F32)

    for hh in range(L // half):
        slabs = [u_ref[pl.ds(hh * half + s, R, stride=L), :] for s in range(half)]
        for gi, arr in enumerate(_lane_block_transpose(slabs)):
            us_scr[gi, :, hh * 128:(hh + 1) * 128] = arr.astype(BF16)
    for gi in range(n_grp):
        u = us_scr[gi]
        z_scr[gi] = jnp.dot(u, bst_ref[gi], preferred_element_type=F32)
        y_scr[gi] = jnp.dot(u, m_ref[gi], preferred_element_type=F32) + u.astype(F32) * d_ref[gi]

    def step(c, state):
        sl = pl.ds(pl.multiple_of(c * batch, batch), batch)
        out = []
        for gi in range(n_grp):
            x = state[gi]
            z = z_scr[gi, sl, :]
            z_scr[gi, sl, :] = x
            out.append(a_ref[gi, 0:1, :] * x + a_ref[gi, 1:2, :] * pltpu.roll(x, SSM_STATE, axis=1) + z)
        return tuple(out)

    state = lax.fori_loop(0, R // batch, step, tuple(st_scr[gi] for gi in range(n_grp)))
    for gi in range(n_grp):
        st_scr[gi] = state[gi]

    for gi in range(n_grp):
        y = y_scr[gi] + jnp.dot(z_scr[gi].astype(BF16), cst_ref[gi], preferred_element_type=F32)
        y_scr[gi] = _gelu_tanh(y)
    for hh in range(L // half):
        cols = [y_scr[gi, :, hh * 128:(hh + 1) * 128] for gi in range(n_grp)]
        for s, arr in enumerate(_lane_block_transpose(cols)):
            o_ref[pl.ds(hh * half + s, R, stride=L), :] = arr


def _s5_branch(u, tables, batch, seq):
    L, G, H, P = SSM_CHUNK, SSM_GROUPS, SSM_GROUP_CH, SSM_STATE
    n_chunks = seq // L
    gpt = 128 // H
    cr = S5_CHUNKS_PER_STEP
    R = cr * batch
    LH = L * H
    m_tab, bst, cst, a_chunk, d_tab = tables
    tile = lambda o, c: (o, 0, 0)
    return pl.pallas_call(
        functools.partial(_s5_kernel, batch=batch),
        grid=(G // gpt, n_chunks // cr),
        in_specs=[pl.BlockSpec((R * L, 128), lambda o, c: (c, o)),
                  pl.BlockSpec((gpt, LH, LH), tile),
                  pl.BlockSpec((gpt, LH, 2 * P), tile),
                  pl.BlockSpec((gpt, 2 * P, LH), tile),
                  pl.BlockSpec((gpt, 2, 2 * P), tile),
                  pl.BlockSpec((gpt, 1, LH), tile)],
        out_specs=pl.BlockSpec((R * L, 128), lambda o, c: (c, o)),
        out_shape=jax.ShapeDtypeStruct(u.shape, F32),
        scratch_shapes=[pltpu.VMEM((gpt, R, LH), BF16), pltpu.VMEM((gpt, R, 2 * P), F32),
                        pltpu.VMEM((gpt, R, LH), F32), pltpu.VMEM((gpt, batch, 2 * P), F32)],
        compiler_params=_params("parallel", "arbitrary"),
        name="s5",
    )(u, m_tab, bst, cst, a_chunk, d_tab)


def _merge_kernel(x_ref, ya_ref, ys_ref, gs_ref, mod_ref, wglu_ref, pa_ref, ps_ref, wout_ref, g2_ref,
                  wr_ref, br_ref, x1_ref, h2_ref, ri_ref, rw_ref, cnt_ref):
    tm, D = x_ref.shape
    ys = ys_ref[...].reshape(tm, SSM_WIDTH).astype(BF16)
    gl = jnp.dot(ys, wglu_ref[...], preferred_element_type=F32)
    y_ssm = gl[:, :SSM_WIDTH] * jax.nn.sigmoid(gl[:, SSM_WIDTH:])
    p_attn = jnp.dot(ya_ref[...], pa_ref[...], preferred_element_type=F32)
    p_ssm = jnp.dot(y_ssm.astype(BF16), ps_ref[...], preferred_element_type=F32)
    merged = gs_ref[:, :D].astype(F32) * p_attn + gs_ref[:, D:].astype(F32) * p_ssm
    mixed = jnp.dot(merged.astype(BF16), wout_ref[...], preferred_element_type=F32)
    x1 = x_ref[...] + mod_ref[2:3, :] * mixed
    x1_ref[...] = x1
    h2 = _rms(x1, NORM_EPS) * g2_ref[...] * (1.0 + mod_ref[4:5, :]) + mod_ref[3:4, :]
    h2_ref[...] = h2.astype(BF16)

    logits = lax.dot_general(wr_ref[...], h2, (((1,), (1,)), ((), ())),
                             preferred_element_type=F32, precision=HIGHEST) + br_ref[...]
    NG, EPG = MOE_GROUPS, MOE_EXPERTS_PER_GROUP
    lg = logits[0:NG, :]
    g_iota = lax.broadcasted_iota(I32, lg.shape, 0)
    lg_max = jnp.max(lg, axis=0, keepdims=True)
    grp = jnp.min(jnp.where(lg == lg_max, g_iota, NG), axis=0, keepdims=True)
    p_grp = 1.0 / jnp.sum(jnp.exp(lg - lg_max), axis=0, keepdims=True)
    le = logits[NG:NG + EPG, :]
    for g in range(1, NG):
        le = jnp.where(grp == g, logits[NG + g * EPG:NG + (g + 1) * EPG, :], le)
    e_iota = lax.broadcasted_iota(I32, le.shape, 0)
    v1 = jnp.max(le, axis=0, keepdims=True)
    i1 = jnp.min(jnp.where(le == v1, e_iota, EPG), axis=0, keepdims=True)
    le2 = jnp.where(e_iota == i1, -jnp.inf, le)
    v2 = jnp.max(le2, axis=0, keepdims=True)
    i2 = jnp.min(jnp.where(le2 == v2, e_iota, EPG), axis=0, keepdims=True)
    e21 = jnp.exp(v2 - v1)
    w1 = p_grp / (1.0 + e21)
    w2 = p_grp * e21 / (1.0 + e21)
    eid1 = grp * EPG + i1
    eid2 = grp * EPG + i2

    x_iota = lax.broadcasted_iota(I32, (MOE_EXPERTS, tm), 0)
    hot1 = x_iota == eid1
    hot2 = x_iota == eid2
    hot = jnp.logical_or(hot1, hot2).astype(F32)
    before = (lax.broadcasted_iota(I32, (tm, tm), 0) < lax.broadcasted_iota(I32, (tm, tm), 1))
    prior = jnp.dot(hot.astype(BF16), before.astype(BF16), preferred_element_type=F32)
    rank1 = jnp.sum(jnp.where(hot1, prior, 0.0), axis=0, keepdims=True)
    rank2 = jnp.sum(jnp.where(hot2, prior, 0.0), axis=0, keepdims=True)
    cnt_ref[...] = jnp.sum(hot, axis=1, keepdims=True).astype(I32)

    zi = jnp.zeros((4, tm), I32)
    ri_ref[...] = jnp.concatenate([eid1, eid2, rank1.astype(I32), rank2.astype(I32), zi], axis=0)
    rw_ref[...] = jnp.concatenate([w1, w2, jnp.zeros((6, tm), F32)], axis=0)


def _merge_route(x2, ya, ys, gs, mod, w_glu, w_pa, w_ps, w_out, norm2_g, w_rg, b_rg, w_re, b_re, seq):
    T, D = x2.shape
    tm = TOKEN_TILE
    per_b = seq // tm
    wr = jnp.concatenate([w_rg.T, jnp.transpose(w_re, (0, 2, 1)).reshape(MOE_EXPERTS, D),
                          jnp.zeros((ROUTER_ROWS - MOE_GROUPS - MOE_EXPERTS, D), F32)], axis=0).astype(F32)
    br = jnp.concatenate([b_rg, b_re.reshape(-1),
                          jnp.zeros((ROUTER_ROWS - MOE_GROUPS - MOE_EXPERTS,), F32)]).reshape(ROUTER_ROWS, 1)
    row = lambda i: (i, 0)
    col = lambda i: (0, i)
    full = lambda i: (0, 0)
    return pl.pallas_call(
        _merge_kernel,
        grid=(T // tm,),
        in_specs=[pl.BlockSpec((tm, D), row),
                  pl.BlockSpec((tm, ATTN_WIDTH), row),
                  _chunk_major_spec(tm, per_b),
                  pl.BlockSpec((tm, 2 * D), row),
                  pl.BlockSpec((None, 6, D), lambda i: (i // per_b, 0, 0)),
                  pl.BlockSpec(w_glu.shape, full),
                  pl.BlockSpec(w_pa.shape, full),
                  pl.BlockSpec(w_ps.shape, full),
                  pl.BlockSpec(w_out.shape, full),
                  pl.BlockSpec((1, D), full),
                  pl.BlockSpec((ROUTER_ROWS, D), full),
                  pl.BlockSpec((ROUTER_ROWS, 1), full)],
        out_specs=[pl.BlockSpec((tm, D), row), pl.BlockSpec((tm, D), row),
                   pl.BlockSpec((8, tm), col), pl.BlockSpec((8, tm), col),
                   pl.BlockSpec((None, MOE_EXPERTS, 1), lambda i: (i, 0, 0))],
        out_shape=[jax.ShapeDtypeStruct((T, D), F32), jax.ShapeDtypeStruct((T, D), BF16),
                   jax.ShapeDtypeStruct((8, T), I32), jax.ShapeDtypeStruct((8, T), F32),
                   jax.ShapeDtypeStruct((T // tm, MOE_EXPERTS, 1), I32)],
        compiler_params=_params("parallel"),
        name="merge_route",
    )(x2, ya, ys, gs, mod, w_glu.astype(BF16), w_pa.astype(BF16), w_ps.astype(BF16), w_out.astype(BF16),
      norm2_g.reshape(1, D), wr, br)


def _tile_positions(ri_ref, seg_ref):
    tm = ri_ref.shape[1]
    x_iota = lax.broadcasted_iota(I32, (MOE_EXPERTS, tm), 0)
    seg = seg_ref[...].astype(F32)
    pos = []
    for k in range(2):
        start = jnp.sum(jnp.where(x_iota == ri_ref[k:k + 1, :], seg, 0.0), axis=0, keepdims=True)
        pos.append(start + ri_ref[2 + k:3 + k, :].astype(F32))
    return pos


def _segment_copies(meta, tile, make_copy):
    chunks, tile_chunks = meta
    base = tile * TILE_CHUNKS

    def body(k, carry):
        word = chunks[base + k]
        src = (word & (2 ** CHUNK_SLOT_BITS - 1)) * SEG_ALIGN
        dst = (word >> CHUNK_SLOT_BITS) * SEG_ALIGN
        make_copy(pl.multiple_of(src, SEG_ALIGN), pl.multiple_of(dst, SEG_ALIGN))
        return carry

    lax.fori_loop(0, tile_chunks[tile], body, 0)


def _wait_chunks(n, wait_chunk, wait_bulk):
    has_bulk = n >= MIN_TILE_CHUNKS

    @pl.when(has_bulk)
    def _():
        wait_bulk()

    def body(c, carry):
        wait_chunk()
        return carry
    lax.fori_loop(jnp.where(has_bulk, MIN_TILE_CHUNKS, 0), n, body, 0)


def _dispatch_kernel(chunks, tile_chunks, pad_row, pad_chunks, n_used,
                     h_ref, ri_ref, rw_ref, seg_ref, xs_ref, pw_ref, zbuf, zeros_scr, sem, pad_sem, tail_sem):
    i = pl.program_id(0)
    n_tiles = pl.num_programs(0)
    slot = i % 2
    tm = h_ref.shape[0]
    pos1, pos2 = _tile_positions(ri_ref, seg_ref)
    pw_ref[...] = jnp.concatenate([pos1, pos2, rw_ref[0:2, :], jnp.zeros((4, tm), F32)], axis=0)
    r_iota = lax.broadcasted_iota(I32, (TILE_SLOTS, tm), 0).astype(F32)
    onehot = jnp.logical_or(r_iota == pos1, r_iota == pos2).astype(BF16)
    zbuf[slot] = jnp.dot(onehot, h_ref[...], preferred_element_type=F32)

    def chunk_copy(buf_slot, src, dst):
        return pltpu.make_async_copy(zbuf.at[buf_slot, pl.ds(src, SEG_ALIGN)],
                                     xs_ref.at[pl.ds(dst, SEG_ALIGN)], sem.at[buf_slot])

    _segment_copies((chunks, tile_chunks), i, lambda src, dst: chunk_copy(slot, src, dst).start())

    def wait_tile(tile, buf_slot):
        bulk = pltpu.make_async_copy(zbuf.at[buf_slot, pl.ds(0, MIN_TILE_CHUNKS * SEG_ALIGN)],
                                     xs_ref.at[pl.ds(0, MIN_TILE_CHUNKS * SEG_ALIGN)], sem.at[buf_slot])
        _wait_chunks(tile_chunks[tile], chunk_copy(buf_slot, 0, 0).wait, bulk.wait)

    @pl.when(i > 0)
    def _():
        wait_tile(i - 1, 1 - slot)

    @pl.when(i == n_tiles - 1)
    def _():
        zeros_scr[...] = jnp.zeros(zeros_scr.shape, F32)

        def pad_copy(dst):
            return pltpu.make_async_copy(zeros_scr.at[pl.ds(0, SEG_ALIGN)], xs_ref.at[pl.ds(dst, SEG_ALIGN)], pad_sem)

        def tail_copy(dst):
            return pltpu.make_async_copy(zeros_scr, xs_ref.at[pl.ds(dst, MOE_BLOCK)], tail_sem)

        def per_expert(e, total):
            def per_chunk(c, carry):
                pad_copy(pl.multiple_of(pad_row[e] + c * SEG_ALIGN, SEG_ALIGN)).start()
                return carry
            lax.fori_loop(0, pad_chunks[e], per_chunk, 0)
            return total + pad_chunks[e]

        n_pad_copies = lax.fori_loop(0, MOE_EXPERTS, per_expert, 0)
        n_blocks = xs_ref.shape[0] // MOE_BLOCK

        def tail_start(b, carry):
            tail_copy(pl.multiple_of(b * MOE_BLOCK, MOE_BLOCK)).start()
            return carry
        lax.fori_loop(n_used[0], n_blocks, tail_start, 0)
        wait_tile(i, slot)

        def wait_pad(c, carry):
            pad_copy(0).wait()
            return carry
        lax.fori_loop(0, n_pad_copies, wait_pad, 0)

        def wait_tail(b, carry):
            tail_copy(0).wait()
            return carry
        lax.fori_loop(n_used[0], n_blocks, wait_tail, 0)


def _dispatch(meta, h2, ri, rw, seg_start, n_rows):
    T, D = h2.shape
    tm = TOKEN_TILE
    col = lambda i, *_: (0, i)
    return pl.pallas_call(
        _dispatch_kernel,
        grid_spec=pltpu.PrefetchScalarGridSpec(
            num_scalar_prefetch=5,
            grid=(T // tm,),
            in_specs=[pl.BlockSpec((tm, D), lambda i, *_: (i, 0)),
                      pl.BlockSpec((8, tm), col),
                      pl.BlockSpec((8, tm), col),
                      pl.BlockSpec((None, MOE_EXPERTS, 1), lambda i, *_: (i, 0, 0))],
            out_specs=[pl.BlockSpec(memory_space=pl.ANY), pl.BlockSpec((8, tm), col)],
            scratch_shapes=[pltpu.VMEM((2, TILE_SLOTS, D), F32), pltpu.VMEM((MOE_BLOCK, D), F32),
                            pltpu.SemaphoreType.DMA((2,)), pltpu.SemaphoreType.DMA(()),
                            pltpu.SemaphoreType.DMA(())]),
        out_shape=[jax.ShapeDtypeStruct((n_rows, D), F32), jax.ShapeDtypeStruct((8, T), F32)],
        compiler_params=_params("arbitrary"),
        name="moe_dispatch",
    )(*meta, h2, ri, rw, seg_start)


def _expert_kernel(be_ref, nb_ref, x_ref, wi_ref, wo_ref, o_ref):
    del be_ref
    F = wo_ref.shape[0]

    @pl.when(pl.program_id(0) < nb_ref[0])
    def _():
        hid = jnp.dot(x_ref[...].astype(BF16), wi_ref[...], preferred_element_type=F32)
        a = hid[:, :F]
        act = a * jax.nn.sigmoid(a) * hid[:, F:]
        o_ref[...] = jnp.dot(act.astype(BF16), wo_ref[...], preferred_element_type=F32)

    @pl.when(pl.program_id(0) >= nb_ref[0])
    def _():
        o_ref[...] = jnp.zeros(o_ref.shape, F32)


def _experts(block_e, n_used, xs, w_e_in, w_e_out):
    n_rows, D = xs.shape
    F = w_e_out.shape[1]
    blk = lambda i, be, nb: jnp.maximum(jnp.minimum(i, nb[0] - 1), 0)
    return pl.pallas_call(
        _expert_kernel,
        grid_spec=pltpu.PrefetchScalarGridSpec(
            num_scalar_prefetch=2,
            grid=(n_rows // MOE_BLOCK,),
            in_specs=[pl.BlockSpec((MOE_BLOCK, D), lambda i, be, nb: (blk(i, be, nb), 0)),
                      pl.BlockSpec((None, D, 2 * F), lambda i, be, nb: (be[blk(i, be, nb)], 0, 0)),
                      pl.BlockSpec((None, F, D), lambda i, be, nb: (be[blk(i, be, nb)], 0, 0))],
            out_specs=pl.BlockSpec((MOE_BLOCK, D), lambda i, be, nb: (i, 0))),
        out_shape=jax.ShapeDtypeStruct((n_rows, D), F32),
        compiler_params=_params("arbitrary"),
        name="experts",
    )(block_e, n_used, xs, w_e_in.astype(BF16), w_e_out.astype(BF16))


def _final_kernel(chunks, tile_chunks, x1_ref, pw_ref, mod_ref, g_ref, ys_ref, o_ref, ybuf, sem):
    i = pl.program_id(0)
    n_tiles = pl.num_programs(0)
    slot = i % 2
    tm = x1_ref.shape[0]

    def chunk_copy(buf_slot, src, dst):
        return pltpu.make_async_copy(ys_ref.at[pl.ds(dst, SEG_ALIGN)],
                                     ybuf.at[buf_slot, pl.ds(src, SEG_ALIGN)], sem.at[buf_slot])

    def fetch(tile, buf_slot):
        _segment_copies((chunks, tile_chunks), tile, lambda src, dst: chunk_copy(buf_slot, src, dst).start())

    @pl.when(i == 0)
    def _():
        ybuf[...] = jnp.zeros(ybuf.shape, F32)
        fetch(0, 0)

    @pl.when(i + 1 < n_tiles)
    def _():
        fetch(i + 1, 1 - slot)

    bulk = pltpu.make_async_copy(ys_ref.at[pl.ds(0, MIN_TILE_CHUNKS * SEG_ALIGN)],
                                 ybuf.at[slot, pl.ds(0, MIN_TILE_CHUNKS * SEG_ALIGN)], sem.at[slot])
    _wait_chunks(tile_chunks[i], chunk_copy(slot, 0, 0).wait, bulk.wait)

    s_iota = lax.broadcasted_iota(I32, (tm, TILE_SLOTS), 1).astype(F32)
    comb = (jnp.where(s_iota == pw_ref[:, 0:1], pw_ref[:, 2:3], 0.0)
            + jnp.where(s_iota == pw_ref[:, 1:2], pw_ref[:, 3:4], 0.0))
    moe = jnp.dot(comb.astype(BF16), ybuf[slot].astype(BF16), preferred_element_type=F32)
    x2 = x1_ref[...] + mod_ref[5:6, :] * moe
    o_ref[...] = _rms(x2, NORM_EPS) * g_ref[...]


def _final(meta, x1, pw_tok, mod, final_g, ys, seq):
    T, D = x1.shape
    tm = TOKEN_TILE
    per_b = seq // tm
    row = lambda i, *_: (i, 0)
    return pl.pallas_call(
        _final_kernel,
        grid_spec=pltpu.PrefetchScalarGridSpec(
            num_scalar_prefetch=2,
            grid=(T // tm,),
            in_specs=[pl.BlockSpec((tm, D), row),
                      pl.BlockSpec((tm, 8), row),
                      pl.BlockSpec((None, 6, D), lambda i, *_: (i // per_b, 0, 0)),
                      pl.BlockSpec((1, D), lambda i, *_: (0, 0)),
                      pl.BlockSpec(memory_space=pl.ANY)],
            out_specs=pl.BlockSpec((tm, D), row),
            scratch_shapes=[pltpu.VMEM((2, TILE_SLOTS, D), F32), pltpu.SemaphoreType.DMA((2,))]),
        out_shape=jax.ShapeDtypeStruct((T, D), F32),
        compiler_params=_params("arbitrary"),
        name="final",
    )(*meta, x1, pw_tok, mod, final_g.reshape(1, D), ys)


def _round_up(x, m):
    return (x + m - 1) // m * m


def _moe_layout(tile_counts):
    n_tiles = tile_counts.shape[0]
    seg = _round_up(tile_counts, SEG_ALIGN)
    seg_start = jnp.cumsum(seg, axis=1) - seg
    tile_base = jnp.cumsum(seg, axis=0) - seg
    used = jnp.sum(seg, axis=0)
    region = _round_up(used, MOE_BLOCK)
    region_end = jnp.cumsum(region)
    region_start = region_end - region
    dst_row = region_start[None, :] + tile_base
    n_chunk = seg // SEG_ALIGN
    n_assign = 2 * n_tiles * TOKEN_TILE
    n_rows = _round_up(n_assign + n_tiles * MOE_EXPERTS * (SEG_ALIGN - 1) + MOE_EXPERTS * (MOE_BLOCK - 1), MOE_BLOCK)
    block_start = jnp.arange(n_rows // MOE_BLOCK, dtype=I32) * MOE_BLOCK
    block_e = jnp.minimum(jnp.sum(block_start[:, None] >= region_end[None, :], axis=1), MOE_EXPERTS - 1)
    i32 = lambda a: a.reshape(-1).astype(I32)
    chunk_end = jnp.cumsum(n_chunk, axis=1)
    k = jnp.arange(TILE_CHUNKS)
    e_of_k = jnp.minimum(jnp.sum(chunk_end[:, None, :] <= k[None, :, None], axis=2), MOE_EXPERTS - 1)
    pick = lambda a: jnp.take_along_axis(a, e_of_k, axis=1)
    c_of_k = k[None, :] - pick(chunk_end - n_chunk)
    word = (pick(seg_start) // SEG_ALIGN + c_of_k) | ((pick(dst_row) // SEG_ALIGN + c_of_k) << CHUNK_SLOT_BITS)
    meta = (i32(word), i32(chunk_end[:, -1]))
    pad = (i32(region_start + used), i32((region - used) // SEG_ALIGN))
    n_used = (region_end[-1:] // MOE_BLOCK).astype(I32)
    return meta, pad, seg_start.astype(I32)[:, :, None], block_e.astype(I32), n_used, n_rows


def kernel(x, c, w_ada, b_ada, norm1_g, w_in, rel_bias, lambda_q1, lambda_k1, lambda_q2, lambda_k2, subln_g, ssm_lambda_re, ssm_lambda_im, ssm_log_step, ssm_b_re, ssm_b_im, ssm_c_re, ssm_c_im, ssm_d, w_glu, w_proj_attn, w_proj_ssm, w_out, norm2_g, w_router_group, b_router_group, w_router_expert, b_router_expert, w_expert_in, w_expert_out, final_g):
    B, S, D = x.shape
    T = B * S
    x2 = x.reshape(T, D)
    mod = _ada_mod(c, w_ada[0], b_ada[0]).reshape(B, 6, D)
    q, k, v, u, gs = _in_proj(x2, mod, norm1_g[0], w_in[0], S)
    lam_vecs = jnp.stack([lambda_q1[0], lambda_k1[0], lambda_q2[0], lambda_k2[0]]).astype(F32)
    y_attn = _diff_attn(q, k, v, rel_bias, lam_vecs, subln_g[0], B, S)
    tables = _s5_tables(ssm_lambda_re[0], ssm_lambda_im[0], ssm_log_step[0], ssm_b_re[0], ssm_b_im[0],
                        ssm_c_re[0], ssm_c_im[0], ssm_d[0])
    y_s5 = _s5_branch(u.reshape(-1, SSM_WIDTH), tables, B, S).reshape(u.shape)
    x1, h2, ri, rw, tile_counts = _merge_route(
        x2, y_attn, y_s5, gs, mod, w_glu[0], w_proj_attn[0], w_proj_ssm[0], w_out[0], norm2_g[0],
        w_router_group[0], b_router_group[0], w_router_expert[0], b_router_expert[0], S)
    meta, pad, seg_start, block_e, n_used, n_rows = _moe_layout(tile_counts[:, :, 0])
    xs, pw = _dispatch(meta + pad + (n_used,), h2, ri, rw, seg_start, n_rows)
    ys = _experts(block_e, n_used, xs, w_expert_in[0], w_expert_out[0])
    out = _final(meta, x1, pw.T, mod, final_g, ys, S)
    return out.reshape(B, S, D)
```

```python
import functools
import math

import jax
import jax.numpy as jnp
from jax import lax
from jax.experimental import pallas as pl
from jax.experimental.pallas import tpu as pltpu

F32 = jnp.float32
BF16 = jnp.bfloat16
I32 = jnp.int32
HIGHEST = lax.Precision.HIGHEST

ATTN_HEADS = 4
ATTN_HEAD_DIM = 64
ATTN_V_DIM = 2 * ATTN_HEAD_DIM
ATTN_WIDTH = ATTN_HEADS * ATTN_V_DIM
NEG_INF = -1e30
REL_BUCKETS = 32
REL_MAX_DISTANCE = 128
SSM_GROUP_CH = 16
SSM_WIDTH = 512
SSM_GROUPS = SSM_WIDTH // SSM_GROUP_CH
SSM_STATE = 64
SSM_EIG_MAX_RE = -1e-4
SSM_CHUNK = 16
S5_CHUNKS_PER_STEP = 32
MOE_GROUPS = 4
MOE_EXPERTS_PER_GROUP = 8
MOE_EXPERTS = MOE_GROUPS * MOE_EXPERTS_PER_GROUP
MOE_BLOCK = 512
SEG_ALIGN = 8
NORM_EPS = 1e-6
SUBLN_EPS = 1e-5
LAMBDA_INIT = 0.8 - 0.6 * math.exp(-0.3 * 0)

ATTN_BLOCK = 256
ATTN_HEADS_PER_STEP = 4
ATTN_ROW_CHUNK = 32
ATTN_ONES_ROWS = 16
LOG2E = math.log2(math.e)
TOKEN_TILE = 512
ROUTER_ROWS = 40
TILE_SLOTS = -(-(2 * TOKEN_TILE + MOE_EXPERTS * (SEG_ALIGN - 1)) // 256) * 256
TILE_CHUNKS = TILE_SLOTS // SEG_ALIGN
MIN_TILE_CHUNKS = 2 * TOKEN_TILE // SEG_ALIGN
CHUNK_SLOT_BITS = 8
assert TILE_CHUNKS <= 2 ** CHUNK_SLOT_BITS
VMEM_LIMIT = 56 << 20


def _params(*sem):
    return pltpu.CompilerParams(dimension_semantics=sem, vmem_limit_bytes=VMEM_LIMIT)


def _rms(x, eps):
    return x * lax.rsqrt(jnp.mean(x * x, axis=-1, keepdims=True) + eps)


def _mod_kernel(c_ref, w_ref, b_ref, o_ref):
    c = c_ref[...]
    c_act = c * jax.nn.sigmoid(c)
    o_ref[...] = jnp.dot(c_act, w_ref[...], preferred_element_type=F32, precision=HIGHEST) + b_ref[...]


def _ada_mod(c, w_ada, b_ada):
    B, D = c.shape
    N = w_ada.shape[1]
    tn = 1024
    return pl.pallas_call(
        _mod_kernel,
        grid=(N // tn,),
        in_specs=[pl.BlockSpec((B, D), lambda j: (0, 0)),
                  pl.BlockSpec((D, tn), lambda j: (0, j)),
                  pl.BlockSpec((1, tn), lambda j: (0, j))],
        out_specs=pl.BlockSpec((B, tn), lambda j: (0, j)),
        out_shape=jax.ShapeDtypeStruct((B, N), F32),
        compiler_params=_params("arbitrary"),
        name="ada_mod",
    )(c, w_ada, b_ada.reshape(1, N))


def _proj_kernel(x_ref, mod_ref, g_ref, w_ref, q_ref, k_ref, v_ref, u_ref, gs_ref):
    y = _rms(x_ref[...], NORM_EPS) * g_ref[...]
    h = (y * (1.0 + mod_ref[1:2, :]) + mod_ref[0:1, :]).astype(BF16)
    W = ATTN_WIDTH

    def proj(lo, hi):
        return jnp.dot(h, w_ref[:, lo:hi], preferred_element_type=F32)

    q_ref[...] = (proj(0, W) * (ATTN_HEAD_DIM ** -0.5 * LOG2E)).astype(BF16)
    k_ref[...] = proj(W, 2 * W).astype(BF16)
    v_ref[...] = proj(2 * W, 3 * W).astype(BF16)
    u_ref[...] = proj(3 * W, 3 * W + SSM_WIDTH).reshape(u_ref.shape)
    gs_ref[...] = jax.nn.sigmoid(proj(3 * W + SSM_WIDTH, w_ref.shape[1])).astype(BF16)


def _chunk_major_spec(tm, per_b):
    return pl.BlockSpec((tm // SSM_CHUNK, None, SSM_CHUNK, SSM_WIDTH), lambda i: (i % per_b, i // per_b, 0, 0))


def _in_proj(x2, mod, norm_g, w_in, seq):
    T, D = x2.shape
    tm = TOKEN_TILE
    per_b = seq // tm
    n_gate = w_in.shape[1] - 3 * ATTN_WIDTH - SSM_WIDTH
    row = lambda i: (i, 0)
    return pl.pallas_call(
        _proj_kernel,
        grid=(T // tm,),
        in_specs=[pl.BlockSpec((tm, D), row),
                  pl.BlockSpec((None, 6, D), lambda i: (i // per_b, 0, 0)),
                  pl.BlockSpec((1, D), lambda i: (0, 0)),
                  pl.BlockSpec(w_in.shape, lambda i: (0, 0))],
        out_specs=[pl.BlockSpec((tm, ATTN_WIDTH), row)] * 3
        + [_chunk_major_spec(tm, per_b), pl.BlockSpec((tm, n_gate), row)],
        out_shape=[jax.ShapeDtypeStruct((T, ATTN_WIDTH), BF16)] * 3
        + [jax.ShapeDtypeStruct((seq // SSM_CHUNK, T // seq, SSM_CHUNK, SSM_WIDTH), F32),
           jax.ShapeDtypeStruct((T, n_gate), BF16)],
        compiler_params=_params("parallel"),
        name="in_proj",
    )(x2, mod, norm_g.reshape(1, D), w_in.astype(BF16))


def _rel_bucket(dist):
    max_exact = REL_BUCKETS // 2
    n = jnp.maximum(dist, 0)
    log_ratio = jnp.log(jnp.maximum(n, 1).astype(F32) / max_exact) / math.log(REL_MAX_DISTANCE / max_exact)
    large = max_exact + (log_ratio * (REL_BUCKETS - max_exact)).astype(I32)
    large = jnp.minimum(large, REL_BUCKETS - 1)
    return jnp.where(n < max_exact, n, large)


def _attn_bias_tiles(rel_bias, blk):
    assert blk >= REL_MAX_DISTANCE
    n_heads = rel_bias.shape[1]
    far = rel_bias[REL_BUCKETS - 1].astype(F32)
    m = jnp.arange(2 * blk)
    signed = jnp.where(m < blk, m, m - 2 * blk)
    tiles = []
    for kind in range(2):
        dist = kind * blk + signed
        tab = jnp.where(dist >= 0, (rel_bias[_rel_bucket(dist)].astype(F32).T - far[:, None]) * LOG2E, NEG_INF)
        skew = jnp.tile(tab, (1, blk))[:, :blk * (2 * blk - 1)].reshape(n_heads, blk, 2 * blk - 1)
        tiles.append(skew[:, :, :blk])
    return jnp.stack(tiles, axis=1)


def _attn_kernel(lam_ref, q_ref, k_ref, v_ref, bias_ref, g_ref, o_ref, vt_scr, *bufs, blk, heads):
    i = pl.program_id(2)
    n_kv = vt_scr.shape[1]
    V = ATTN_V_DIM
    ns = 2 * heads
    acc = bufs[0:ns]
    sbuf = tuple(bufs[(1 + r) * ns:(2 + r) * ns] for r in range(3))
    pbufs = tuple(bufs[(4 + r) * ns:(5 + r) * ns] for r in range(3))

    @pl.when(i == 0)
    def _():
        for hd in range(heads):
            for jb in range(n_kv):
                vt_scr[hd, jb, 0:V, :] = v_ref[jb * blk:(jb + 1) * blk, hd * V:(hd + 1) * V].astype(F32).T.astype(BF16)
                vt_scr[hd, jb, V:, :] = jnp.ones((ATTN_ONES_ROWS, blk), BF16)

    qt_maps = []
    for hd in range(heads):
        qt = q_ref[:, hd * V:(hd + 1) * V].astype(F32).T
        feat = lax.broadcasted_iota(I32, qt.shape, 0)
        qt_maps += [jnp.where(feat < ATTN_HEAD_DIM, qt, 0.0).astype(BF16),
                    jnp.where(feat >= ATTN_HEAD_DIM, qt, 0.0).astype(BF16)]

    def scores(j, st):
        hd = st // 2
        kj = k_ref[pl.ds(pl.multiple_of(j * blk, blk), blk), hd * V:(hd + 1) * V]
        return jnp.dot(kj, qt_maps[st], preferred_element_type=F32)

    n_chunks = blk // ATTN_ROW_CHUNK

    def rows(c):
        return slice(c * ATTN_ROW_CHUNK, (c + 1) * ATTN_ROW_CHUNK)

    def fold8(x):
        return x.reshape(ATTN_ROW_CHUNK // 8, 8, blk)

    def block(j, carry, pos, lookahead, bias_kind):
        src, dst, pbuf = sbuf[pos], sbuf[(pos + 2) % 3], pbufs[pos]
        out = []
        for st in range(ns):
            hd = st // 2
            if lookahead:
                dst[st][...] = scores(j + 2, st)

            def chunk(c):
                s = src[st][rows(c), :]
                return s if bias_kind is None else s + bias_ref[hd, bias_kind, rows(c), :]

            m_old = carry[st]
            m8 = jnp.max(fold8(chunk(0)), axis=0)
            for c in range(1, n_chunks):
                m8 = jnp.maximum(m8, jnp.max(fold8(chunk(c)), axis=0))
            m_new = jnp.maximum(m_old, jnp.max(m8, axis=0, keepdims=True))
            alpha = jnp.exp2(m_old - m_new)
            for c in range(n_chunks):
                pbuf[st][rows(c), :] = jnp.exp2(chunk(c) - m_new).astype(BF16)
            acc[st][...] = alpha * acc[st][...] + jnp.dot(vt_scr[hd, j], pbuf[st][...],
                                                          preferred_element_type=F32)
            out.append(m_new)
        return tuple(out)

    def far_triple(t, carry):
        for r in range(3):
            carry = block(3 * t + r, carry, r, True, None)
        return carry

    def far_single(j, carry):
        carry = block(j, carry, 0, True, None)
        for st in range(ns):
            sbuf[0][st][...] = sbuf[1][st][...]
        for st in range(ns):
            sbuf[1][st][...] = sbuf[2][st][...]
        return carry

    def near_pair(_, carry):
        return block(i, block(i - 1, carry, 0, False, 1), 1, False, 0)

    def near_single(_, carry):
        return block(i, carry, 0, False, 0)

    for st in range(ns):
        acc[st][...] = jnp.zeros(acc[st].shape, F32)
        sbuf[0][st][...] = scores(0, st)
        sbuf[1][st][...] = scores(jnp.minimum(i, 1), st)
    m0 = jnp.full((1, blk), -jnp.inf, F32)
    n_far = jnp.maximum(i - 1, 0)
    carry = lax.fori_loop(0, n_far // 3, far_triple, (m0,) * ns)
    carry = lax.fori_loop(n_far - n_far % 3, n_far, far_single, carry)
    carry = lax.fori_loop(0, jnp.minimum(i, 1), near_pair, carry)
    lax.fori_loop(0, 1 - jnp.minimum(i, 1), near_single, carry)

    lam = (jnp.exp(jnp.sum(lam_ref[0:1, :] * lam_ref[1:2, :], axis=-1, keepdims=True))
           - jnp.exp(jnp.sum(lam_ref[2:3, :] * lam_ref[3:4, :], axis=-1, keepdims=True)) + LAMBDA_INIT)
    for hd in range(heads):
        a1, a2 = acc[2 * hd], acc[2 * hd + 1]
        ot = a1[0:V, :] / a1[V:V + 1, :] - lam * (a2[0:V, :] / a2[V:V + 1, :])
        ot = ot * lax.rsqrt(jnp.mean(ot * ot, axis=0, keepdims=True) + SUBLN_EPS)
        o_ref[:, hd * V:(hd + 1) * V] = (ot.T * (g_ref[...] * (1.0 - LAMBDA_INIT))).astype(BF16)


def _diff_attn(q, k, v, rel_bias, lam_vecs, subln_g, batch, seq):
    T = q.shape[0]
    blk = ATTN_BLOCK
    nq = seq // blk
    bias = _attn_bias_tiles(rel_bias, blk)
    hp = ATTN_HEADS_PER_STEP
    ns = 2 * hp
    width = hp * ATTN_V_DIM
    acc_rows = ATTN_V_DIM + ATTN_ONES_ROWS
    return pl.pallas_call(
        functools.partial(_attn_kernel, blk=blk, heads=hp),
        grid=(batch, ATTN_HEADS // hp, nq),
        in_specs=[pl.BlockSpec((4, ATTN_HEAD_DIM), lambda b, h, i: (0, 0)),
                  pl.BlockSpec((blk, width), lambda b, h, i: (b * nq + i, h)),
                  pl.BlockSpec((seq, width), lambda b, h, i: (b, h)),
                  pl.BlockSpec((seq, width), lambda b, h, i: (b, h)),
                  pl.BlockSpec((hp, 2, blk, blk), lambda b, h, i: (h, 0, 0, 0)),
                  pl.BlockSpec((1, ATTN_V_DIM), lambda b, h, i: (0, 0))],
        out_specs=pl.BlockSpec((blk, width), lambda b, h, i: (b * nq + i, h)),
        out_shape=jax.ShapeDtypeStruct((T, ATTN_WIDTH), BF16),
        scratch_shapes=[pltpu.VMEM((hp, nq, acc_rows, blk), BF16)]
        + [pltpu.VMEM((acc_rows, blk), F32)] * ns + [pltpu.VMEM((blk, blk), F32)] * (3 * ns)
        + [pltpu.VMEM((blk, blk), BF16)] * (3 * ns),
        compiler_params=_params("parallel", "parallel", "arbitrary"),
        name="diff_attn",
    )(lam_vecs, q, k, v, bias, subln_g.reshape(1, ATTN_V_DIM))


def _s5_tables(lam_re, lam_im, log_step, b_re, b_im, c_re, c_im, d_skip):
    L = SSM_CHUNK
    G, P = lam_re.shape
    H = SSM_GROUP_CH
    lr = jnp.minimum(lam_re.astype(F32), SSM_EIG_MAX_RE)
    li = lam_im.astype(F32)
    step = jnp.exp(log_step.astype(F32))[:, None]
    mag = jnp.exp(lr * step)
    ang = li * step
    a_re = mag * jnp.cos(ang)
    a_im = mag * jnp.sin(ang)
    den = lr * lr + li * li
    num_re = a_re - 1.0
    coef_re = (num_re * lr + a_im * li) / den
    coef_im = (a_im * lr - num_re * li) / den
    br = b_re.astype(F32)
    bi = b_im.astype(F32)
    bb_re = coef_re[..., None] * br - coef_im[..., None] * bi
    bb_im = coef_re[..., None] * bi + coef_im[..., None] * br
    pw_re, pw_im = [jnp.ones_like(a_re)], [jnp.zeros_like(a_re)]
    for _ in range(L):
        pr, pi = pw_re[-1], pw_im[-1]
        pw_re.append(pr * a_re - pi * a_im)
        pw_im.append(pr * a_im + pi * a_re)
    pw_re = jnp.stack(pw_re)
    pw_im = jnp.stack(pw_im)
    cr = c_re.astype(F32)[None]
    ci = c_im.astype(F32)[None]
    cp_re = cr * pw_re[:, :, None, :] - ci * pw_im[:, :, None, :]
    cp_im = cr * pw_im[:, :, None, :] + ci * pw_re[:, :, None, :]
    kern = (jnp.einsum('tghp,gpk->tghk', cp_re[:L], bb_re, precision=HIGHEST)
            - jnp.einsum('tghp,gpk->tghk', cp_im[:L], bb_im, precision=HIGHEST))
    s_idx = jnp.arange(L)[:, None]
    t_idx = jnp.arange(L)[None, :]
    toep = jnp.where((t_idx >= s_idx)[:, :, None, None, None], kern[jnp.maximum(t_idx - s_idx, 0)], 0.0)
    m_tab = jnp.transpose(toep, (2, 0, 4, 1, 3)).reshape(G, L * H, L * H)
    rev_re = pw_re[L - 1::-1][:, :, None, :]
    rev_im = pw_im[L - 1::-1][:, :, None, :]
    bbt_re = jnp.transpose(bb_re, (0, 2, 1))[None]
    bbt_im = jnp.transpose(bb_im, (0, 2, 1))[None]
    bst_re = jnp.transpose(rev_re * bbt_re - rev_im * bbt_im, (1, 0, 2, 3)).reshape(G, L * H, P)
    bst_im = jnp.transpose(rev_re * bbt_im + rev_im * bbt_re, (1, 0, 2, 3)).reshape(G, L * H, P)
    cst_re = jnp.transpose(cp_re[1:], (1, 3, 0, 2)).reshape(G, P, L * H)
    cst_im = -jnp.transpose(cp_im[1:], (1, 3, 0, 2)).reshape(G, P, L * H)
    a_chunk = jnp.stack([jnp.concatenate([pw_re[L], pw_re[L]], axis=-1),
                         jnp.concatenate([-pw_im[L], pw_im[L]], axis=-1)], axis=1)
    d_tab = jnp.tile(d_skip.astype(F32), (1, L)).reshape(G, 1, L * H)
    bst = jnp.concatenate([bst_re, bst_im], axis=-1)
    cst = jnp.concatenate([cst_re, cst_im], axis=1)
    return m_tab.astype(BF16), bst.astype(BF16), cst.astype(BF16), a_chunk, d_tab


def _gelu_tanh(x):
    return 0.5 * x * (1.0 + jnp.tanh(math.sqrt(2.0 / math.pi) * (x + 0.044715 * (x * x * x))))


def _lane_block_transpose(arrs):
    n = len(arrs)
    width = arrs[0].shape[1]
    blk_id = lax.broadcasted_iota(I32, arrs[0].shape, 1) // SSM_GROUP_CH
    k = n // 2
    while k >= 1:
        keep = (blk_id & k) == 0
        nxt = list(arrs)
        for r in range(n):
            if r & k == 0:
                a, b = arrs[r], arrs[r + k]
                nxt[r] = jnp.where(keep, a, pltpu.roll(b, k * SSM_GROUP_CH, axis=1))
                nxt[r + k] = jnp.where(keep, pltpu.roll(a, width - k * SSM_GROUP_CH, axis=1), b)
        arrs = nxt
        k //= 2
    return arrs


def _s5_kernel(u_ref, m_ref, bst_ref, cst_ref, a_ref, d_ref, o_ref, us_scr, z_scr, y_scr, st_scr, *, batch):
    L, H = SSM_CHUNK, SSM_GROUP_CH
    n_grp = us_scr.shape[0]
    R = us_scr.shape[1]
    half = 128 // H

    @pl.when(pl.program_id(1) == 0)
    def _():
        st_scr[...] = jnp.zeros(st_scr.shape, F32)

    for hh in range(L // half):
        slabs = [u_ref[pl.ds(hh * half + s, R, stride=L), :] for s in range(half)]
        for gi, arr in enumerate(_lane_block_transpose(slabs)):
            us_scr[gi, :, hh * 128:(hh + 1) * 128] = arr.astype(BF16)
    for gi in range(n_grp):
        u = us_scr[gi]
        z_scr[gi] = jnp.dot(u, bst_ref[gi], preferred_element_type=F32)
        y_scr[gi] = jnp.dot(u, m_ref[gi], preferred_element_type=F32) + u.astype(F32) * d_ref[gi]

    def step(c, state):
        sl = pl.ds(pl.multiple_of(c * batch, batch), batch)
        out = []
        for gi in range(n_grp):
            x = state[gi]
            z = z_scr[gi, sl, :]
            z_scr[gi, sl, :] = x
            out.append(a_ref[gi, 0:1, :] * x + a_ref[gi, 1:2, :] * pltpu.roll(x, SSM_STATE, axis=1) + z)
        return tuple(out)

    state = lax.fori_loop(0, R // batch, step, tuple(st_scr[gi] for gi in range(n_grp)))
    for gi in range(n_grp):
        st_scr[gi] = state[gi]

    for gi in range(n_grp):
        y = y_scr[gi] + jnp.dot(z_scr[gi].astype(BF16), cst_ref[gi], preferred_element_type=F32)
        y_scr[gi] = _gelu_tanh(y)
    for hh in range(L // half):
        cols = [y_scr[gi, :, hh * 128:(hh + 1) * 128] for gi in range(n_grp)]
        for s, arr in enumerate(_lane_block_transpose(cols)):
            o_ref[pl.ds(hh * half + s, R, stride=L), :] = arr


def _s5_branch(u, tables, batch, seq):
    L, G, H, P = SSM_CHUNK, SSM_GROUPS, SSM_GROUP_CH, SSM_STATE
    n_chunks = seq // L
    gpt = 128 // H
    cr = S5_CHUNKS_PER_STEP
    R = cr * batch
    LH = L * H
    m_tab, bst, cst, a_chunk, d_tab = tables
    tile = lambda o, c: (o, 0, 0)
    return pl.pallas_call(
        functools.partial(_s5_kernel, batch=batch),
        grid=(G // gpt, n_chunks // cr),
        in_specs=[pl.BlockSpec((R * L, 128), lambda o, c: (c, o)),
                  pl.BlockSpec((gpt, LH, LH), tile),
                  pl.BlockSpec((gpt, LH, 2 * P), tile),
                  pl.BlockSpec((gpt, 2 * P, LH), tile),
                  pl.BlockSpec((gpt, 2, 2 * P), tile),
                  pl.BlockSpec((gpt, 1, LH), tile)],
        out_specs=pl.BlockSpec((R * L, 128), lambda o, c: (c, o)),
        out_shape=jax.ShapeDtypeStruct(u.shape, F32),
        scratch_shapes=[pltpu.VMEM((gpt, R, LH), BF16), pltpu.VMEM((gpt, R, 2 * P), F32),
                        pltpu.VMEM((gpt, R, LH), F32), pltpu.VMEM((gpt, batch, 2 * P), F32)],
        compiler_params=_params("parallel", "arbitrary"),
        name="s5",
    )(u, m_tab, bst, cst, a_chunk, d_tab)


def _merge_kernel(x_ref, ya_ref, ys_ref, gs_ref, mod_ref, wglu_ref, pa_ref, ps_ref, wout_ref, g2_ref,
                  wr_ref, br_ref, x1_ref, h2_ref, ri_ref, rw_ref, cnt_ref):
    tm, D = x_ref.shape
    ys = ys_ref[...].reshape(tm, SSM_WIDTH).astype(BF16)
    gl = jnp.dot(ys, wglu_ref[...], preferred_element_type=F32)
    y_ssm = gl[:, :SSM_WIDTH] * jax.nn.sigmoid(gl[:, SSM_WIDTH:])
    p_attn = jnp.dot(ya_ref[...], pa_ref[...], preferred_element_type=F32)
    p_ssm = jnp.dot(y_ssm.astype(BF16), ps_ref[...], preferred_element_type=F32)
    merged = gs_ref[:, :D].astype(F32) * p_attn + gs_ref[:, D:].astype(F32) * p_ssm
    mixed = jnp.dot(merged.astype(BF16), wout_ref[...], preferred_element_type=F32)
    x1 = x_ref[...] + mod_ref[2:3, :] * mixed
    x1_ref[...] = x1
    h2 = _rms(x1, NORM_EPS) * g2_ref[...] * (1.0 + mod_ref[4:5, :]) + mod_ref[3:4, :]
    h2_ref[...] = h2.astype(BF16)

    logits = lax.dot_general(wr_ref[...], h2, (((1,), (1,)), ((), ())),
                             preferred_element_type=F32, precision=HIGHEST) + br_ref[...]
    NG, EPG = MOE_GROUPS, MOE_EXPERTS_PER_GROUP
    lg = logits[0:NG, :]
    g_iota = lax.broadcasted_iota(I32, lg.shape, 0)
    lg_max = jnp.max(lg, axis=0, keepdims=True)
    grp = jnp.min(jnp.where(lg == lg_max, g_iota, NG), axis=0, keepdims=True)
    p_grp = 1.0 / jnp.sum(jnp.exp(lg - lg_max), axis=0, keepdims=True)
    le = logits[NG:NG + EPG, :]
    for g in range(1, NG):
        le = jnp.where(grp == g, logits[NG + g * EPG:NG + (g + 1) * EPG, :], le)
    e_iota = lax.broadcasted_iota(I32, le.shape, 0)
    v1 = jnp.max(le, axis=0, keepdims=True)
    i1 = jnp.min(jnp.where(le == v1, e_iota, EPG), axis=0, keepdims=True)
    le2 = jnp.where(e_iota == i1, -jnp.inf, le)
    v2 = jnp.max(le2, axis=0, keepdims=True)
    i2 = jnp.min(jnp.where(le2 == v2, e_iota, EPG), axis=0, keepdims=True)
    e21 = jnp.exp(v2 - v1)
    w1 = p_grp / (1.0 + e21)
    w2 = p_grp * e21 / (1.0 + e21)
    eid1 = grp * EPG + i1
    eid2 = grp * EPG + i2

    x_iota = lax.broadcasted_iota(I32, (MOE_EXPERTS, tm), 0)
    hot1 = x_iota == eid1
    hot2 = x_iota == eid2
    hot = jnp.logical_or(hot1, hot2).astype(F32)
    before = (lax.broadcasted_iota(I32, (tm, tm), 0) < lax.broadcasted_iota(I32, (tm, tm), 1))
    prior = jnp.dot(hot.astype(BF16), before.astype(BF16), preferred_element_type=F32)
    rank1 = jnp.sum(jnp.where(hot1, prior, 0.0), axis=0, keepdims=True)
    rank2 = jnp.sum(jnp.where(hot2, prior, 0.0), axis=0, keepdims=True)
    cnt_ref[...] = jnp.sum(hot, axis=1, keepdims=True).astype(I32)

    zi = jnp.zeros((4, tm), I32)
    ri_ref[...] = jnp.concatenate([eid1, eid2, rank1.astype(I32), rank2.astype(I32), zi], axis=0)
    rw_ref[...] = jnp.concatenate([w1, w2, jnp.zeros((6, tm), F32)], axis=0)


def _merge_route(x2, ya, ys, gs, mod, w_glu, w_pa, w_ps, w_out, norm2_g, w_rg, b_rg, w_re, b_re, seq):
    T, D = x2.shape
    tm = TOKEN_TILE
    per_b = seq // tm
    wr = jnp.concatenate([w_rg.T, jnp.transpose(w_re, (0, 2, 1)).reshape(MOE_EXPERTS, D),
                          jnp.zeros((ROUTER_ROWS - MOE_GROUPS - MOE_EXPERTS, D), F32)], axis=0).astype(F32)
    br = jnp.concatenate([b_rg, b_re.reshape(-1),
                          jnp.zeros((ROUTER_ROWS - MOE_GROUPS - MOE_EXPERTS,), F32)]).reshape(ROUTER_ROWS, 1)
    row = lambda i: (i, 0)
    col = lambda i: (0, i)
    full = lambda i: (0, 0)
    return pl.pallas_call(
        _merge_kernel,
        grid=(T // tm,),
        in_specs=[pl.BlockSpec((tm, D), row),
                  pl.BlockSpec((tm, ATTN_WIDTH), row),
                  _chunk_major_spec(tm, per_b),
                  pl.BlockSpec((tm, 2 * D), row),
                  pl.BlockSpec((None, 6, D), lambda i: (i // per_b, 0, 0)),
                  pl.BlockSpec(w_glu.shape, full),
                  pl.BlockSpec(w_pa.shape, full),
                  pl.BlockSpec(w_ps.shape, full),
                  pl.BlockSpec(w_out.shape, full),
                  pl.BlockSpec((1, D), full),
                  pl.BlockSpec((ROUTER_ROWS, D), full),
                  pl.BlockSpec((ROUTER_ROWS, 1), full)],
        out_specs=[pl.BlockSpec((tm, D), row), pl.BlockSpec((tm, D), row),
                   pl.BlockSpec((8, tm), col), pl.BlockSpec((8, tm), col),
                   pl.BlockSpec((None, MOE_EXPERTS, 1), lambda i: (i, 0, 0))],
        out_shape=[jax.ShapeDtypeStruct((T, D), F32), jax.ShapeDtypeStruct((T, D), BF16),
                   jax.ShapeDtypeStruct((8, T), I32), jax.ShapeDtypeStruct((8, T), F32),
                   jax.ShapeDtypeStruct((T // tm, MOE_EXPERTS, 1), I32)],
        compiler_params=_params("parallel"),
        name="merge_route",
    )(x2, ya, ys, gs, mod, w_glu.astype(BF16), w_pa.astype(BF16), w_ps.astype(BF16), w_out.astype(BF16),
      norm2_g.reshape(1, D), wr, br)


def _tile_positions(ri_ref, seg_ref):
    tm = ri_ref.shape[1]
    x_iota = lax.broadcasted_iota(I32, (MOE_EXPERTS, tm), 0)
    seg = seg_ref[...].astype(F32)
    pos = []
    for k in range(2):
        start = jnp.sum(jnp.where(x_iota == ri_ref[k:k + 1, :], seg, 0.0), axis=0, keepdims=True)
        pos.append(start + ri_ref[2 + k:3 + k, :].astype(F32))
    return pos


def _segment_copies(meta, tile, make_copy):
    chunks, tile_chunks = meta
    base = tile * TILE_CHUNKS

    def body(k, carry):
        word = chunks[base + k]
        src = (word & (2 ** CHUNK_SLOT_BITS - 1)) * SEG_ALIGN
        dst = (word >> CHUNK_SLOT_BITS) * SEG_ALIGN
        make_copy(pl.multiple_of(src, SEG_ALIGN), pl.multiple_of(dst, SEG_ALIGN))
        return carry

    lax.fori_loop(0, tile_chunks[tile], body, 0)


def _wait_chunks(n, wait_chunk, wait_bulk):
    has_bulk = n >= MIN_TILE_CHUNKS

    @pl.when(has_bulk)
    def _():
        wait_bulk()

    def body(c, carry):
        wait_chunk()
        return carry
    lax.fori_loop(jnp.where(has_bulk, MIN_TILE_CHUNKS, 0), n, body, 0)


def _dispatch_kernel(chunks, tile_chunks, pad_row, pad_chunks, n_used,
                     h_ref, ri_ref, rw_ref, seg_ref, xs_ref, pw_ref, zbuf, zeros_scr, sem, pad_sem, tail_sem):
    i = pl.program_id(0)
    n_tiles = pl.num_programs(0)
    slot = i % 2
    tm = h_ref.shape[0]
    pos1, pos2 = _tile_positions(ri_ref, seg_ref)
    pw_ref[...] = jnp.concatenate([pos1, pos2, rw_ref[0:2, :], jnp.zeros((4, tm), F32)], axis=0)
    r_iota = lax.broadcasted_iota(I32, (TILE_SLOTS, tm), 0).astype(F32)
    onehot = jnp.logical_or(r_iota == pos1, r_iota == pos2).astype(BF16)
    zbuf[slot] = jnp.dot(onehot, h_ref[...], preferred_element_type=F32)

    def chunk_copy(buf_slot, src, dst):
        return pltpu.make_async_copy(zbuf.at[buf_slot, pl.ds(src, SEG_ALIGN)],
                                     xs_ref.at[pl.ds(dst, SEG_ALIGN)], sem.at[buf_slot])

    _segment_copies((chunks, tile_chunks), i, lambda src, dst: chunk_copy(slot, src, dst).start())

    def wait_tile(tile, buf_slot):
        bulk = pltpu.make_async_copy(zbuf.at[buf_slot, pl.ds(0, MIN_TILE_CHUNKS * SEG_ALIGN)],
                                     xs_ref.at[pl.ds(0, MIN_TILE_CHUNKS * SEG_ALIGN)], sem.at[buf_slot])
        _wait_chunks(tile_chunks[tile], chunk_copy(buf_slot, 0, 0).wait, bulk.wait)

    @pl.when(i > 0)
    def _():
        wait_tile(i - 1, 1 - slot)

    @pl.when(i == n_tiles - 1)
    def _():
        zeros_scr[...] = jnp.zeros(zeros_scr.shape, F32)

        def pad_copy(dst):
            return pltpu.make_async_copy(zeros_scr.at[pl.ds(0, SEG_ALIGN)], xs_ref.at[pl.ds(dst, SEG_ALIGN)], pad_sem)

        def tail_copy(dst):
            return pltpu.make_async_copy(zeros_scr, xs_ref.at[pl.ds(dst, MOE_BLOCK)], tail_sem)

        def per_expert(e, total):
            def per_chunk(c, carry):
                pad_copy(pl.multiple_of(pad_row[e] + c * SEG_ALIGN, SEG_ALIGN)).start()
                return carry
            lax.fori_loop(0, pad_chunks[e], per_chunk, 0)
            return total + pad_chunks[e]

        n_pad_copies = lax.fori_loop(0, MOE_EXPERTS, per_expert, 0)
        n_blocks = xs_ref.shape[0] // MOE_BLOCK

        def tail_start(b, carry):
            tail_copy(pl.multiple_of(b * MOE_BLOCK, MOE_BLOCK)).start()
            return carry
        lax.fori_loop(n_used[0], n_blocks, tail_start, 0)
        wait_tile(i, slot)

        def wait_pad(c, carry):
            pad_copy(0).wait()
            return carry
        lax.fori_loop(0, n_pad_copies, wait_pad, 0)

        def wait_tail(b, carry):
            tail_copy(0).wait()
            return carry
        lax.fori_loop(n_used[0], n_blocks, wait_tail, 0)


def _dispatch(meta, h2, ri, rw, seg_start, n_rows):
    T, D = h2.shape
    tm = TOKEN_TILE
    col = lambda i, *_: (0, i)
    return pl.pallas_call(
        _dispatch_kernel,
        grid_spec=pltpu.PrefetchScalarGridSpec(
            num_scalar_prefetch=5,
            grid=(T // tm,),
            in_specs=[pl.BlockSpec((tm, D), lambda i, *_: (i, 0)),
                      pl.BlockSpec((8, tm), col),
                      pl.BlockSpec((8, tm), col),
                      pl.BlockSpec((None, MOE_EXPERTS, 1), lambda i, *_: (i, 0, 0))],
            out_specs=[pl.BlockSpec(memory_space=pl.ANY), pl.BlockSpec((8, tm), col)],
            scratch_shapes=[pltpu.VMEM((2, TILE_SLOTS, D), F32), pltpu.VMEM((MOE_BLOCK, D), F32),
                            pltpu.SemaphoreType.DMA((2,)), pltpu.SemaphoreType.DMA(()),
                            pltpu.SemaphoreType.DMA(())]),
        out_shape=[jax.ShapeDtypeStruct((n_rows, D), F32), jax.ShapeDtypeStruct((8, T), F32)],
        compiler_params=_params("arbitrary"),
        name="moe_dispatch",
    )(*meta, h2, ri, rw, seg_start)


def _expert_kernel(be_ref, nb_ref, x_ref, wi_ref, wo_ref, o_ref, wi_bf, wo_bf):
    i = pl.program_id(0)
    F = wo_ref.shape[0]
    in_use = i < nb_ref[0]

    @pl.when(jnp.logical_and(in_use, jnp.logical_or(i == 0, be_ref[i] != be_ref[jnp.maximum(i - 1, 0)])))
    def _():
        wi_bf[...] = wi_ref[...].astype(BF16)
        wo_bf[...] = wo_ref[...].astype(BF16)

    @pl.when(in_use)
    def _():
        hid = jnp.dot(x_ref[...].astype(BF16), wi_bf[...], preferred_element_type=F32)
        a = hid[:, :F]
        act = a * jax.nn.sigmoid(a) * hid[:, F:]
        o_ref[...] = jnp.dot(act.astype(BF16), wo_bf[...], preferred_element_type=F32)

    @pl.when(jnp.logical_not(in_use))
    def _():
        o_ref[...] = jnp.zeros(o_ref.shape, F32)


def _experts(block_e, n_used, xs, w_e_in, w_e_out):
    n_rows, D = xs.shape
    F = w_e_out.shape[1]
    blk = lambda i, be, nb: jnp.maximum(jnp.minimum(i, nb[0] - 1), 0)
    return pl.pallas_call(
        _expert_kernel,
        grid_spec=pltpu.PrefetchScalarGridSpec(
            num_scalar_prefetch=2,
            grid=(n_rows // MOE_BLOCK,),
            in_specs=[pl.BlockSpec((MOE_BLOCK, D), lambda i, be, nb: (blk(i, be, nb), 0)),
                      pl.BlockSpec((None, D, 2 * F), lambda i, be, nb: (be[blk(i, be, nb)], 0, 0)),
                      pl.BlockSpec((None, F, D), lambda i, be, nb: (be[blk(i, be, nb)], 0, 0))],
            out_specs=pl.BlockSpec((MOE_BLOCK, D), lambda i, be, nb: (i, 0)),
            scratch_shapes=[pltpu.VMEM((D, 2 * F), BF16), pltpu.VMEM((F, D), BF16)]),
        out_shape=jax.ShapeDtypeStruct((n_rows, D), F32),
        compiler_params=_params("arbitrary"),
        name="experts",
    )(block_e, n_used, xs, w_e_in, w_e_out)


def _final_kernel(chunks, tile_chunks, x1_ref, pw_ref, mod_ref, g_ref, ys_ref, o_ref, ybuf, sem):
    i = pl.program_id(0)
    n_tiles = pl.num_programs(0)
    slot = i % 2
    tm = x1_ref.shape[0]

    def chunk_copy(buf_slot, src, dst):
        return pltpu.make_async_copy(ys_ref.at[pl.ds(dst, SEG_ALIGN)],
                                     ybuf.at[buf_slot, pl.ds(src, SEG_ALIGN)], sem.at[buf_slot])

    def fetch(tile, buf_slot):
        _segment_copies((chunks, tile_chunks), tile, lambda src, dst: chunk_copy(buf_slot, src, dst).start())

    @pl.when(i == 0)
    def _():
        ybuf[...] = jnp.zeros(ybuf.shape, F32)
        fetch(0, 0)

    @pl.when(i + 1 < n_tiles)
    def _():
        fetch(i + 1, 1 - slot)

    bulk = pltpu.make_async_copy(ys_ref.at[pl.ds(0, MIN_TILE_CHUNKS * SEG_ALIGN)],
                                 ybuf.at[slot, pl.ds(0, MIN_TILE_CHUNKS * SEG_ALIGN)], sem.at[slot])
    _wait_chunks(tile_chunks[i], chunk_copy(slot, 0, 0).wait, bulk.wait)

    s_iota = lax.broadcasted_iota(I32, (tm, TILE_SLOTS), 1).astype(F32)
    comb = (jnp.where(s_iota == pw_ref[:, 0:1], pw_ref[:, 2:3], 0.0)
            + jnp.where(s_iota == pw_ref[:, 1:2], pw_ref[:, 3:4], 0.0))
    moe = jnp.dot(comb.astype(BF16), ybuf[slot].astype(BF16), preferred_element_type=F32)
    x2 = x1_ref[...] + mod_ref[5:6, :] * moe
    o_ref[...] = _rms(x2, NORM_EPS) * g_ref[...]


def _final(meta, x1, pw_tok, mod, final_g, ys, seq):
    T, D = x1.shape
    tm = TOKEN_TILE
    per_b = seq // tm
    row = lambda i, *_: (i, 0)
    return pl.pallas_call(
        _final_kernel,
        grid_spec=pltpu.PrefetchScalarGridSpec(
            num_scalar_prefetch=2,
            grid=(T // tm,),
            in_specs=[pl.BlockSpec((tm, D), row),
                      pl.BlockSpec((tm, 8), row),
                      pl.BlockSpec((None, 6, D), lambda i, *_: (i // per_b, 0, 0)),
                      pl.BlockSpec((1, D), lambda i, *_: (0, 0)),
                      pl.BlockSpec(memory_space=pl.ANY)],
            out_specs=pl.BlockSpec((tm, D), row),
            scratch_shapes=[pltpu.VMEM((2, TILE_SLOTS, D), F32), pltpu.SemaphoreType.DMA((2,))]),
        out_shape=jax.ShapeDtypeStruct((T, D), F32),
        compiler_params=_params("arbitrary"),
        name="final",
    )(*meta, x1, pw_tok, mod, final_g.reshape(1, D), ys)


def _round_up(x, m):
    return (x + m - 1) // m * m


def _moe_layout(tile_counts):
    n_tiles = tile_counts.shape[0]
    seg = _round_up(tile_counts, SEG_ALIGN)
    seg_start = jnp.cumsum(seg, axis=1) - seg
    tile_base = jnp.cumsum(seg, axis=0) - seg
    used = jnp.sum(seg, axis=0)
    region = _round_up(used, MOE_BLOCK)
    region_end = jnp.cumsum(region)
    region_start = region_end - region
    dst_row = region_start[None, :] + tile_base
    n_chunk = seg // SEG_ALIGN
    n_assign = 2 * n_tiles * TOKEN_TILE
    n_rows = _round_up(n_assign + n_tiles * MOE_EXPERTS * (SEG_ALIGN - 1) + MOE_EXPERTS * (MOE_BLOCK - 1), MOE_BLOCK)
    block_start = jnp.arange(n_rows // MOE_BLOCK, dtype=I32) * MOE_BLOCK
    block_e = jnp.minimum(jnp.sum(block_start[:, None] >= region_end[None, :], axis=1), MOE_EXPERTS - 1)
    i32 = lambda a: a.reshape(-1).astype(I32)
    chunk_end = jnp.cumsum(n_chunk, axis=1)
    chunk_start = chunk_end - n_chunk
    k = jnp.arange(TILE_CHUNKS)[None, :, None]
    mine = (chunk_start[:, None, :] <= k) & (k < chunk_end[:, None, :])
    pick = lambda a: jnp.sum(jnp.where(mine, (a // SEG_ALIGN - chunk_start)[:, None, :] + k, 0), axis=2)
    word = pick(seg_start) | (pick(dst_row) << CHUNK_SLOT_BITS)
    meta = (i32(word), i32(chunk_end[:, -1]))
    pad = (i32(region_start + used), i32((region - used) // SEG_ALIGN))
    n_used = (region_end[-1:] // MOE_BLOCK).astype(I32)
    return meta, pad, seg_start.astype(I32)[:, :, None], block_e.astype(I32), n_used, n_rows


def kernel(x, c, w_ada, b_ada, norm1_g, w_in, rel_bias, lambda_q1, lambda_k1, lambda_q2, lambda_k2, subln_g, ssm_lambda_re, ssm_lambda_im, ssm_log_step, ssm_b_re, ssm_b_im, ssm_c_re, ssm_c_im, ssm_d, w_glu, w_proj_attn, w_proj_ssm, w_out, norm2_g, w_router_group, b_router_group, w_router_expert, b_router_expert, w_expert_in, w_expert_out, final_g):
    B, S, D = x.shape
    T = B * S
    x2 = x.reshape(T, D)
    mod = _ada_mod(c, w_ada[0], b_ada[0]).reshape(B, 6, D)
    q, k, v, u, gs = _in_proj(x2, mod, norm1_g[0], w_in[0], S)
    lam_vecs = jnp.stack([lambda_q1[0], lambda_k1[0], lambda_q2[0], lambda_k2[0]]).astype(F32)
    y_attn = _diff_attn(q, k, v, rel_bias, lam_vecs, subln_g[0], B, S)
    tables = _s5_tables(ssm_lambda_re[0], ssm_lambda_im[0], ssm_log_step[0], ssm_b_re[0], ssm_b_im[0],
                        ssm_c_re[0], ssm_c_im[0], ssm_d[0])
    y_s5 = _s5_branch(u.reshape(-1, SSM_WIDTH), tables, B, S).reshape(u.shape)
    x1, h2, ri, rw, tile_counts = _merge_route(
        x2, y_attn, y_s5, gs, mod, w_glu[0], w_proj_attn[0], w_proj_ssm[0], w_out[0], norm2_g[0],
        w_router_group[0], b_router_group[0], w_router_expert[0], b_router_expert[0], S)
    meta, pad, seg_start, block_e, n_used, n_rows = _moe_layout(tile_counts[:, :, 0])
    xs, pw = _dispatch(meta + pad + (n_used,), h2, ri, rw, seg_start, n_rows)
    ys = _experts(block_e, n_used, xs, w_expert_in[0], w_expert_out[0])
    out = _final(meta, x1, pw.T, mod, final_g, ys, S)
    return out.reshape(B, S, D)
```

```python
import functools
import math

import jax
import jax.numpy as jnp
from jax import lax
from jax.experimental import pallas as pl
from jax.experimental.pallas import tpu as pltpu

F32 = jnp.float32
BF16 = jnp.bfloat16
I32 = jnp.int32
HIGHEST = lax.Precision.HIGHEST

ATTN_HEADS = 4
ATTN_HEAD_DIM = 64
ATTN_V_DIM = 2 * ATTN_HEAD_DIM
ATTN_WIDTH = ATTN_HEADS * ATTN_V_DIM
NEG_INF = -1e30
REL_BUCKETS = 32
REL_MAX_DISTANCE = 128
SSM_GROUP_CH = 16
SSM_WIDTH = 512
SSM_GROUPS = SSM_WIDTH // SSM_GROUP_CH
SSM_STATE = 64
SSM_EIG_MAX_RE = -1e-4
SSM_CHUNK = 16
S5_CHUNKS_PER_STEP = 32
MOE_GROUPS = 4
MOE_EXPERTS_PER_GROUP = 8
MOE_EXPERTS = MOE_GROUPS * MOE_EXPERTS_PER_GROUP
MOE_BLOCK = 512
SEG_ALIGN = 8
NORM_EPS = 1e-6
SUBLN_EPS = 1e-5
LAMBDA_INIT = 0.8 - 0.6 * math.exp(-0.3 * 0)

ATTN_BLOCK = 256
ATTN_HEADS_PER_STEP = 4
ATTN_ROW_CHUNK = 32
ATTN_ONES_ROWS = 16
LOG2E = math.log2(math.e)
TOKEN_TILE = 512
ROUTER_ROWS = 40
ROUTER_LANES = 128
TILE_SLOTS = -(-(2 * TOKEN_TILE + MOE_EXPERTS * (SEG_ALIGN - 1)) // 256) * 256
TILE_CHUNKS = TILE_SLOTS // SEG_ALIGN
MIN_TILE_CHUNKS = 2 * TOKEN_TILE // SEG_ALIGN
CHUNK_SLOT_BITS = 8
assert TILE_CHUNKS <= 2 ** CHUNK_SLOT_BITS
VMEM_LIMIT = 56 << 20


def _params(*sem):
    return pltpu.CompilerParams(dimension_semantics=sem, vmem_limit_bytes=VMEM_LIMIT)


def _rms(x, eps):
    return x * lax.rsqrt(jnp.mean(x * x, axis=-1, keepdims=True) + eps)


def _mod_kernel(c_ref, w_ref, b_ref, o_ref):
    c = c_ref[...]
    c_act = c * jax.nn.sigmoid(c)
    o_ref[...] = jnp.dot(c_act, w_ref[...], preferred_element_type=F32, precision=HIGHEST) + b_ref[...]


def _ada_mod(c, w_ada, b_ada):
    B, D = c.shape
    N = w_ada.shape[1]
    tn = 1024
    return pl.pallas_call(
        _mod_kernel,
        grid=(N // tn,),
        in_specs=[pl.BlockSpec((B, D), lambda j: (0, 0)),
                  pl.BlockSpec((D, tn), lambda j: (0, j)),
                  pl.BlockSpec((1, tn), lambda j: (0, j))],
        out_specs=pl.BlockSpec((B, tn), lambda j: (0, j)),
        out_shape=jax.ShapeDtypeStruct((B, N), F32),
        compiler_params=_params("arbitrary"),
        name="ada_mod",
    )(c, w_ada, b_ada.reshape(1, N))


def _proj_kernel(x_ref, mod_ref, g_ref, w_ref, q_ref, k_ref, v_ref, u_ref, gs_ref):
    y = _rms(x_ref[...], NORM_EPS) * g_ref[...]
    h = (y * (1.0 + mod_ref[1:2, :]) + mod_ref[0:1, :]).astype(BF16)
    W = ATTN_WIDTH

    def proj(lo, hi):
        return jnp.dot(h, w_ref[:, lo:hi], preferred_element_type=F32)

    q_ref[...] = (proj(0, W) * (ATTN_HEAD_DIM ** -0.5 * LOG2E)).astype(BF16)
    k_ref[...] = proj(W, 2 * W).astype(BF16)
    v_ref[...] = proj(2 * W, 3 * W).astype(BF16)
    u_ref[...] = proj(3 * W, 3 * W + SSM_WIDTH).reshape(u_ref.shape)
    gs_ref[...] = jax.nn.sigmoid(proj(3 * W + SSM_WIDTH, w_ref.shape[1])).astype(BF16)


def _chunk_major_spec(tm, per_b):
    return pl.BlockSpec((tm // SSM_CHUNK, None, SSM_CHUNK, SSM_WIDTH), lambda i: (i % per_b, i // per_b, 0, 0))


def _in_proj(x2, mod, norm_g, w_in, seq):
    T, D = x2.shape
    tm = TOKEN_TILE
    per_b = seq // tm
    n_gate = w_in.shape[1] - 3 * ATTN_WIDTH - SSM_WIDTH
    row = lambda i: (i, 0)
    return pl.pallas_call(
        _proj_kernel,
        grid=(T // tm,),
        in_specs=[pl.BlockSpec((tm, D), row),
                  pl.BlockSpec((None, 6, D), lambda i: (i // per_b, 0, 0)),
                  pl.BlockSpec((1, D), lambda i: (0, 0)),
                  pl.BlockSpec(w_in.shape, lambda i: (0, 0))],
        out_specs=[pl.BlockSpec((tm, ATTN_WIDTH), row)] * 3
        + [_chunk_major_spec(tm, per_b), pl.BlockSpec((tm, n_gate), row)],
        out_shape=[jax.ShapeDtypeStruct((T, ATTN_WIDTH), BF16)] * 3
        + [jax.ShapeDtypeStruct((seq // SSM_CHUNK, T // seq, SSM_CHUNK, SSM_WIDTH), F32),
           jax.ShapeDtypeStruct((T, n_gate), BF16)],
        compiler_params=_params("parallel"),
        name="in_proj",
    )(x2, mod, norm_g.reshape(1, D), w_in.astype(BF16))


def _rel_bucket(dist):
    max_exact = REL_BUCKETS // 2
    n = jnp.maximum(dist, 0)
    log_ratio = jnp.log(jnp.maximum(n, 1).astype(F32) / max_exact) / math.log(REL_MAX_DISTANCE / max_exact)
    large = max_exact + (log_ratio * (REL_BUCKETS - max_exact)).astype(I32)
    large = jnp.minimum(large, REL_BUCKETS - 1)
    return jnp.where(n < max_exact, n, large)


def _attn_bias_tiles(rel_bias, blk):
    assert blk >= REL_MAX_DISTANCE
    n_heads = rel_bias.shape[1]
    far = rel_bias[REL_BUCKETS - 1].astype(F32)
    m = jnp.arange(2 * blk)
    signed = jnp.where(m < blk, m, m - 2 * blk)
    tiles = []
    for kind in range(2):
        dist = kind * blk + signed
        tab = jnp.where(dist >= 0, (rel_bias[_rel_bucket(dist)].astype(F32).T - far[:, None]) * LOG2E, NEG_INF)
        skew = jnp.tile(tab, (1, blk))[:, :blk * (2 * blk - 1)].reshape(n_heads, blk, 2 * blk - 1)
        tiles.append(skew[:, :, :blk])
    return jnp.stack(tiles, axis=1)


def _attn_kernel(lam_ref, q_ref, k_ref, v_ref, bias_ref, g_ref, o_ref, vt_scr, *bufs, blk, heads):
    i = pl.program_id(2)
    n_kv = vt_scr.shape[1]
    V = ATTN_V_DIM
    ns = 2 * heads
    acc = bufs[0:ns]
    sbuf = tuple(bufs[(1 + r) * ns:(2 + r) * ns] for r in range(3))
    pbufs = tuple(bufs[(4 + r) * ns:(5 + r) * ns] for r in range(3))

    @pl.when(i == 0)
    def _():
        for hd in range(heads):
            for jb in range(n_kv):
                vt_scr[hd, jb, 0:V, :] = v_ref[jb * blk:(jb + 1) * blk, hd * V:(hd + 1) * V].astype(F32).T.astype(BF16)
                vt_scr[hd, jb, V:, :] = jnp.ones((ATTN_ONES_ROWS, blk), BF16)

    qt_maps = []
    for hd in range(heads):
        qt = q_ref[:, hd * V:(hd + 1) * V].astype(F32).T
        feat = lax.broadcasted_iota(I32, qt.shape, 0)
        qt_maps += [jnp.where(feat < ATTN_HEAD_DIM, qt, 0.0).astype(BF16),
                    jnp.where(feat >= ATTN_HEAD_DIM, qt, 0.0).astype(BF16)]

    def scores(j, st):
        hd = st // 2
        kj = k_ref[pl.ds(pl.multiple_of(j * blk, blk), blk), hd * V:(hd + 1) * V]
        return jnp.dot(kj, qt_maps[st], preferred_element_type=F32)

    n_chunks = blk // ATTN_ROW_CHUNK

    def rows(c):
        return slice(c * ATTN_ROW_CHUNK, (c + 1) * ATTN_ROW_CHUNK)

    def fold8(x):
        return x.reshape(ATTN_ROW_CHUNK // 8, 8, blk)

    def block(j, carry, pos, lookahead, bias_kind):
        src, dst, pbuf = sbuf[pos], sbuf[(pos + 2) % 3], pbufs[pos]
        out = []
        for st in range(ns):
            hd = st // 2
            if lookahead:
                dst[st][...] = scores(j + 2, st)

            def chunk(c):
                s = src[st][rows(c), :]
                return s if bias_kind is None else s + bias_ref[hd, bias_kind, rows(c), :]

            m_old = carry[st]
            m8 = jnp.max(fold8(chunk(0)), axis=0)
            for c in range(1, n_chunks):
                m8 = jnp.maximum(m8, jnp.max(fold8(chunk(c)), axis=0))
            m_new = jnp.maximum(m_old, jnp.max(m8, axis=0, keepdims=True))
            alpha = jnp.exp2(m_old - m_new)
            for c in range(n_chunks):
                pbuf[st][rows(c), :] = jnp.exp2(chunk(c) - m_new).astype(BF16)
            acc[st][...] = alpha * acc[st][...] + jnp.dot(vt_scr[hd, j], pbuf[st][...],
                                                          preferred_element_type=F32)
            out.append(m_new)
        return tuple(out)

    def far_triple(t, carry):
        for r in range(3):
            carry = block(3 * t + r, carry, r, True, None)
        return carry

    def far_single(j, carry):
        carry = block(j, carry, 0, True, None)
        for st in range(ns):
            sbuf[0][st][...] = sbuf[1][st][...]
        for st in range(ns):
            sbuf[1][st][...] = sbuf[2][st][...]
        return carry

    def near_pair(_, carry):
        return block(i, block(i - 1, carry, 0, False, 1), 1, False, 0)

    def near_single(_, carry):
        return block(i, carry, 0, False, 0)

    for st in range(ns):
        acc[st][...] = jnp.zeros(acc[st].shape, F32)
        sbuf[0][st][...] = scores(0, st)
        sbuf[1][st][...] = scores(jnp.minimum(i, 1), st)
    m0 = jnp.full((1, blk), -jnp.inf, F32)
    n_far = jnp.maximum(i - 1, 0)
    carry = lax.fori_loop(0, n_far // 3, far_triple, (m0,) * ns)
    carry = lax.fori_loop(n_far - n_far % 3, n_far, far_single, carry)
    carry = lax.fori_loop(0, jnp.minimum(i, 1), near_pair, carry)
    lax.fori_loop(0, 1 - jnp.minimum(i, 1), near_single, carry)

    lam = (jnp.exp(jnp.sum(lam_ref[0:1, :] * lam_ref[1:2, :], axis=-1, keepdims=True))
           - jnp.exp(jnp.sum(lam_ref[2:3, :] * lam_ref[3:4, :], axis=-1, keepdims=True)) + LAMBDA_INIT)
    for hd in range(heads):
        a1, a2 = acc[2 * hd], acc[2 * hd + 1]
        ot = a1[0:V, :] / a1[V:V + 1, :] - lam * (a2[0:V, :] / a2[V:V + 1, :])
        ot = ot * lax.rsqrt(jnp.mean(ot * ot, axis=0, keepdims=True) + SUBLN_EPS)
        o_ref[:, hd * V:(hd + 1) * V] = (ot.T * (g_ref[...] * (1.0 - LAMBDA_INIT))).astype(BF16)


def _diff_attn(q, k, v, rel_bias, lam_vecs, subln_g, batch, seq):
    T = q.shape[0]
    blk = ATTN_BLOCK
    nq = seq // blk
    bias = _attn_bias_tiles(rel_bias, blk)
    hp = ATTN_HEADS_PER_STEP
    ns = 2 * hp
    width = hp * ATTN_V_DIM
    acc_rows = ATTN_V_DIM + ATTN_ONES_ROWS
    return pl.pallas_call(
        functools.partial(_attn_kernel, blk=blk, heads=hp),
        grid=(batch, ATTN_HEADS // hp, nq),
        in_specs=[pl.BlockSpec((4, ATTN_HEAD_DIM), lambda b, h, i: (0, 0)),
                  pl.BlockSpec((blk, width), lambda b, h, i: (b * nq + i, h)),
                  pl.BlockSpec((seq, width), lambda b, h, i: (b, h)),
                  pl.BlockSpec((seq, width), lambda b, h, i: (b, h)),
                  pl.BlockSpec((hp, 2, blk, blk), lambda b, h, i: (h, 0, 0, 0)),
                  pl.BlockSpec((1, ATTN_V_DIM), lambda b, h, i: (0, 0))],
        out_specs=pl.BlockSpec((blk, width), lambda b, h, i: (b * nq + i, h)),
        out_shape=jax.ShapeDtypeStruct((T, ATTN_WIDTH), BF16),
        scratch_shapes=[pltpu.VMEM((hp, nq, acc_rows, blk), BF16)]
        + [pltpu.VMEM((acc_rows, blk), F32)] * ns + [pltpu.VMEM((blk, blk), F32)] * (3 * ns)
        + [pltpu.VMEM((blk, blk), BF16)] * (3 * ns),
        compiler_params=_params("parallel", "parallel", "arbitrary"),
        name="diff_attn",
    )(lam_vecs, q, k, v, bias, subln_g.reshape(1, ATTN_V_DIM))


def _s5_tables(lam_re, lam_im, log_step, b_re, b_im, c_re, c_im, d_skip):
    L = SSM_CHUNK
    G, P = lam_re.shape
    H = SSM_GROUP_CH
    lr = jnp.minimum(lam_re.astype(F32), SSM_EIG_MAX_RE)
    li = lam_im.astype(F32)
    step = jnp.exp(log_step.astype(F32))[:, None]
    mag = jnp.exp(lr * step)
    ang = li * step
    a_re = mag * jnp.cos(ang)
    a_im = mag * jnp.sin(ang)
    den = lr * lr + li * li
    num_re = a_re - 1.0
    coef_re = (num_re * lr + a_im * li) / den
    coef_im = (a_im * lr - num_re * li) / den
    br = b_re.astype(F32)
    bi = b_im.astype(F32)
    bb_re = coef_re[..., None] * br - coef_im[..., None] * bi
    bb_im = coef_re[..., None] * bi + coef_im[..., None] * br
    pw_re, pw_im = [jnp.ones_like(a_re)], [jnp.zeros_like(a_re)]
    for _ in range(L):
        pr, pi = pw_re[-1], pw_im[-1]
        pw_re.append(pr * a_re - pi * a_im)
        pw_im.append(pr * a_im + pi * a_re)
    pw_re = jnp.stack(pw_re)
    pw_im = jnp.stack(pw_im)
    cr = c_re.astype(F32)[None]
    ci = c_im.astype(F32)[None]
    cp_re = cr * pw_re[:, :, None, :] - ci * pw_im[:, :, None, :]
    cp_im = cr * pw_im[:, :, None, :] + ci * pw_re[:, :, None, :]
    kern = (jnp.einsum('tghp,gpk->tghk', cp_re[:L], bb_re, precision=HIGHEST)
            - jnp.einsum('tghp,gpk->tghk', cp_im[:L], bb_im, precision=HIGHEST))
    s_idx = jnp.arange(L)[:, None]
    t_idx = jnp.arange(L)[None, :]
    toep = jnp.where((t_idx >= s_idx)[:, :, None, None, None], kern[jnp.maximum(t_idx - s_idx, 0)], 0.0)
    m_tab = jnp.transpose(toep, (2, 0, 4, 1, 3)).reshape(G, L * H, L * H)
    rev_re = pw_re[L - 1::-1][:, :, None, :]
    rev_im = pw_im[L - 1::-1][:, :, None, :]
    bbt_re = jnp.transpose(bb_re, (0, 2, 1))[None]
    bbt_im = jnp.transpose(bb_im, (0, 2, 1))[None]
    bst_re = jnp.transpose(rev_re * bbt_re - rev_im * bbt_im, (1, 0, 2, 3)).reshape(G, L * H, P)
    bst_im = jnp.transpose(rev_re * bbt_im + rev_im * bbt_re, (1, 0, 2, 3)).reshape(G, L * H, P)
    cst_re = jnp.transpose(cp_re[1:], (1, 3, 0, 2)).reshape(G, P, L * H)
    cst_im = -jnp.transpose(cp_im[1:], (1, 3, 0, 2)).reshape(G, P, L * H)
    a_chunk = jnp.stack([jnp.concatenate([pw_re[L], pw_re[L]], axis=-1),
                         jnp.concatenate([-pw_im[L], pw_im[L]], axis=-1)], axis=1)
    d_tab = jnp.tile(d_skip.astype(F32), (1, L)).reshape(G, 1, L * H)
    bst = jnp.concatenate([bst_re, bst_im], axis=-1)
    cst = jnp.concatenate([cst_re, cst_im], axis=1)
    return m_tab.astype(BF16), bst.astype(BF16), cst.astype(BF16), a_chunk, d_tab


def _gelu_tanh(x):
    return 0.5 * x * (1.0 + jnp.tanh(math.sqrt(2.0 / math.pi) * (x + 0.044715 * (x * x * x))))


def _lane_block_transpose(arrs):
    n = len(arrs)
    width = arrs[0].shape[1]
    blk_id = lax.broadcasted_iota(I32, arrs[0].shape, 1) // SSM_GROUP_CH
    k = n // 2
    while k >= 1:
        keep = (blk_id & k) == 0
        nxt = list(arrs)
        for r in range(n):
            if r & k == 0:
                a, b = arrs[r], arrs[r + k]
                nxt[r] = jnp.where(keep, a, pltpu.roll(b, k * SSM_GROUP_CH, axis=1))
                nxt[r + k] = jnp.where(keep, pltpu.roll(a, width - k * SSM_GROUP_CH, axis=1), b)
        arrs = nxt
        k //= 2
    return arrs


def _s5_kernel(u_ref, m_ref, bst_ref, cst_ref, a_ref, d_ref, o_ref, us_scr, z_scr, y_scr, st_scr, *, batch):
    L, H = SSM_CHUNK, SSM_GROUP_CH
    n_grp = us_scr.shape[0]
    R = us_scr.shape[1]
    half = 128 // H

    @pl.when(pl.program_id(1) == 0)
    def _():
        st_scr[...] = jnp.zeros(st_scr.shape, F32)

    for hh in range(L // half):
        slabs = [u_ref[pl.ds(hh * half + s, R, stride=L), :] for s in range(half)]
        for gi, arr in enumerate(_lane_block_transpose(slabs)):
            us_scr[gi, :, hh * 128:(hh + 1) * 128] = arr.astype(BF16)
    for gi in range(n_grp):
        u = us_scr[gi]
        z_scr[gi] = jnp.dot(u, bst_ref[gi], preferred_element_type=F32)
        y_scr[gi] = jnp.dot(u, m_ref[gi], preferred_element_type=F32) + u.astype(F32) * d_ref[gi]

    def step(c, state):
        sl = pl.ds(pl.multiple_of(c * batch, batch), batch)
        out = []
        for gi in range(n_grp):
            x = state[gi]
            z = z_scr[gi, sl, :]
            z_scr[gi, sl, :] = x
            out.append(a_ref[gi, 0:1, :] * x + a_ref[gi, 1:2, :] * pltpu.roll(x, SSM_STATE, axis=1) + z)
        return tuple(out)

    state = lax.fori_loop(0, R // batch, step, tuple(st_scr[gi] for gi in range(n_grp)))
    for gi in range(n_grp):
        st_scr[gi] = state[gi]

    for gi in range(n_grp):
        y = y_scr[gi] + jnp.dot(z_scr[gi].astype(BF16), cst_ref[gi], preferred_element_type=F32)
        y_scr[gi] = _gelu_tanh(y)
    for hh in range(L // half):
        cols = [y_scr[gi, :, hh * 128:(hh + 1) * 128] for gi in range(n_grp)]
        for s, arr in enumerate(_lane_block_transpose(cols)):
            o_ref[pl.ds(hh * half + s, R, stride=L), :] = arr


def _s5_branch(u, tables, batch, seq):
    L, G, H, P = SSM_CHUNK, SSM_GROUPS, SSM_GROUP_CH, SSM_STATE
    n_chunks = seq // L
    gpt = 128 // H
    cr = S5_CHUNKS_PER_STEP
    R = cr * batch
    LH = L * H
    m_tab, bst, cst, a_chunk, d_tab = tables
    tile = lambda o, c: (o, 0, 0)
    return pl.pallas_call(
        functools.partial(_s5_kernel, batch=batch),
        grid=(G // gpt, n_chunks // cr),
        in_specs=[pl.BlockSpec((R * L, 128), lambda o, c: (c, o)),
                  pl.BlockSpec((gpt, LH, LH), tile),
                  pl.BlockSpec((gpt, LH, 2 * P), tile),
                  pl.BlockSpec((gpt, 2 * P, LH), tile),
                  pl.BlockSpec((gpt, 2, 2 * P), tile),
                  pl.BlockSpec((gpt, 1, LH), tile)],
        out_specs=pl.BlockSpec((R * L, 128), lambda o, c: (c, o)),
        out_shape=jax.ShapeDtypeStruct(u.shape, F32),
        scratch_shapes=[pltpu.VMEM((gpt, R, LH), BF16), pltpu.VMEM((gpt, R, 2 * P), F32),
                        pltpu.VMEM((gpt, R, LH), F32), pltpu.VMEM((gpt, batch, 2 * P), F32)],
        compiler_params=_params("parallel", "arbitrary"),
        name="s5",
    )(u, m_tab, bst, cst, a_chunk, d_tab)


def _merge_kernel(x_ref, ya_ref, ys_ref, gs_ref, mod_ref, wglu_ref, pa_ref, ps_ref, wout_ref, g2_ref,
                  wr_ref, br_ref, x1_ref, h2_ref, ri_ref, rw_ref, cnt_ref):
    tm, D = x_ref.shape
    ys = ys_ref[...].reshape(tm, SSM_WIDTH).astype(BF16)
    gl = jnp.dot(ys, wglu_ref[...], preferred_element_type=F32)
    y_ssm = gl[:, :SSM_WIDTH] * jax.nn.sigmoid(gl[:, SSM_WIDTH:])
    p_attn = jnp.dot(ya_ref[...], pa_ref[...], preferred_element_type=F32)
    p_ssm = jnp.dot(y_ssm.astype(BF16), ps_ref[...], preferred_element_type=F32)
    merged = gs_ref[:, :D].astype(F32) * p_attn + gs_ref[:, D:].astype(F32) * p_ssm
    mixed = jnp.dot(merged.astype(BF16), wout_ref[...], preferred_element_type=F32)
    x1 = x_ref[...] + mod_ref[2:3, :] * mixed
    x1_ref[...] = x1
    h2 = _rms(x1, NORM_EPS) * g2_ref[...] * (1.0 + mod_ref[4:5, :]) + mod_ref[3:4, :]
    h2_hi = h2.astype(BF16)
    h2_ref[...] = h2_hi

    h2_lo = (h2 - h2_hi.astype(F32)).astype(BF16)
    hi_prod = jnp.dot(h2_hi, wr_ref[...], preferred_element_type=F32)
    lo_prod = jnp.dot(h2_lo, wr_ref[:, 0:ROUTER_LANES], preferred_element_type=F32)
    lg_tok = hi_prod[:, 0:ROUTER_LANES] + hi_prod[:, ROUTER_LANES:] + lo_prod
    logits = lg_tok.T[0:ROUTER_ROWS, :] + br_ref[...]
    NG, EPG = MOE_GROUPS, MOE_EXPERTS_PER_GROUP
    lg = logits[0:NG, :]
    g_iota = lax.broadcasted_iota(I32, lg.shape, 0)
    lg_max = jnp.max(lg, axis=0, keepdims=True)
    grp = jnp.min(jnp.where(lg == lg_max, g_iota, NG), axis=0, keepdims=True)
    p_grp = 1.0 / jnp.sum(jnp.exp(lg - lg_max), axis=0, keepdims=True)
    le = logits[NG:NG + EPG, :]
    for g in range(1, NG):
        le = jnp.where(grp == g, logits[NG + g * EPG:NG + (g + 1) * EPG, :], le)
    e_iota = lax.broadcasted_iota(I32, le.shape, 0)
    v1 = jnp.max(le, axis=0, keepdims=True)
    i1 = jnp.min(jnp.where(le == v1, e_iota, EPG), axis=0, keepdims=True)
    le2 = jnp.where(e_iota == i1, -jnp.inf, le)
    v2 = jnp.max(le2, axis=0, keepdims=True)
    i2 = jnp.min(jnp.where(le2 == v2, e_iota, EPG), axis=0, keepdims=True)
    e21 = jnp.exp(v2 - v1)
    w1 = p_grp / (1.0 + e21)
    w2 = p_grp * e21 / (1.0 + e21)
    eid1 = grp * EPG + i1
    eid2 = grp * EPG + i2

    x_iota = lax.broadcasted_iota(I32, (MOE_EXPERTS, tm), 0)
    hot1 = x_iota == eid1
    hot2 = x_iota == eid2
    hot = jnp.logical_or(hot1, hot2).astype(F32)
    before = (lax.broadcasted_iota(I32, (tm, tm), 0) < lax.broadcasted_iota(I32, (tm, tm), 1))
    prior = jnp.dot(hot.astype(BF16), before.astype(BF16), preferred_element_type=F32)
    rank1 = jnp.sum(jnp.where(hot1, prior, 0.0), axis=0, keepdims=True)
    rank2 = jnp.sum(jnp.where(hot2, prior, 0.0), axis=0, keepdims=True)
    cnt_ref[...] = jnp.sum(hot, axis=1, keepdims=True).astype(I32)

    zi = jnp.zeros((4, tm), I32)
    ri_ref[...] = jnp.concatenate([eid1, eid2, rank1.astype(I32), rank2.astype(I32), zi], axis=0)
    rw_ref[...] = jnp.concatenate([w1, w2, jnp.zeros((6, tm), F32)], axis=0)


def _merge_route(x2, ya, ys, gs, mod, w_glu, w_pa, w_ps, w_out, norm2_g, w_rg, b_rg, w_re, b_re, seq):
    T, D = x2.shape
    tm = TOKEN_TILE
    per_b = seq // tm
    wr = jnp.concatenate([w_rg, jnp.transpose(w_re, (1, 0, 2)).reshape(D, MOE_EXPERTS),
                          jnp.zeros((D, ROUTER_LANES - MOE_GROUPS - MOE_EXPERTS), F32)], axis=1).astype(F32)
    wr_hi = wr.astype(BF16)
    wr_lo = (wr - wr_hi.astype(F32)).astype(BF16)
    br = jnp.concatenate([b_rg, b_re.reshape(-1),
                          jnp.zeros((ROUTER_ROWS - MOE_GROUPS - MOE_EXPERTS,), F32)]).reshape(ROUTER_ROWS, 1)
    row = lambda i: (i, 0)
    col = lambda i: (0, i)
    full = lambda i: (0, 0)
    return pl.pallas_call(
        _merge_kernel,
        grid=(T // tm,),
        in_specs=[pl.BlockSpec((tm, D), row),
                  pl.BlockSpec((tm, ATTN_WIDTH), row),
                  _chunk_major_spec(tm, per_b),
                  pl.BlockSpec((tm, 2 * D), row),
                  pl.BlockSpec((None, 6, D), lambda i: (i // per_b, 0, 0)),
                  pl.BlockSpec(w_glu.shape, full),
                  pl.BlockSpec(w_pa.shape, full),
                  pl.BlockSpec(w_ps.shape, full),
                  pl.BlockSpec(w_out.shape, full),
                  pl.BlockSpec((1, D), full),
                  pl.BlockSpec((D, 2 * ROUTER_LANES), full),
                  pl.BlockSpec((ROUTER_ROWS, 1), full)],
        out_specs=[pl.BlockSpec((tm, D), row), pl.BlockSpec((tm, D), row),
                   pl.BlockSpec((8, tm), col), pl.BlockSpec((8, tm), col),
                   pl.BlockSpec((None, MOE_EXPERTS, 1), lambda i: (i, 0, 0))],
        out_shape=[jax.ShapeDtypeStruct((T, D), F32), jax.ShapeDtypeStruct((T, D), BF16),
                   jax.ShapeDtypeStruct((8, T), I32), jax.ShapeDtypeStruct((8, T), F32),
                   jax.ShapeDtypeStruct((T // tm, MOE_EXPERTS, 1), I32)],
        compiler_params=_params("parallel"),
        name="merge_route",
    )(x2, ya, ys, gs, mod, w_glu.astype(BF16), w_pa.astype(BF16), w_ps.astype(BF16), w_out.astype(BF16),
      norm2_g.reshape(1, D), jnp.concatenate([wr_hi, wr_lo], axis=1), br)


def _pack_pairs(x):
    W = x.shape[1] // 2
    lo = lax.bitcast_convert_type(x[:, :W], I32)
    hi = lax.bitcast_convert_type(x[:, W:], I32)
    return lax.shift_right_logical(lo, 16) | hi


def _unpack_pairs(w):
    lo = lax.bitcast_convert_type(lax.shift_left(w, 16), F32)
    hi = lax.bitcast_convert_type(w & jnp.int32(-65536), F32)
    return jnp.concatenate([lo.astype(BF16), hi.astype(BF16)], axis=1)


def _tile_positions(ri_ref, seg_ref):
    tm = ri_ref.shape[1]
    x_iota = lax.broadcasted_iota(I32, (MOE_EXPERTS, tm), 0)
    seg = seg_ref[...].astype(F32)
    pos = []
    for k in range(2):
        start = jnp.sum(jnp.where(x_iota == ri_ref[k:k + 1, :], seg, 0.0), axis=0, keepdims=True)
        pos.append(start + ri_ref[2 + k:3 + k, :].astype(F32))
    return pos


def _segment_copies(meta, tile, make_copy):
    chunks, tile_chunks = meta
    base = tile * TILE_CHUNKS

    def body(k, carry):
        word = chunks[base + k]
        src = (word & (2 ** CHUNK_SLOT_BITS - 1)) * SEG_ALIGN
        dst = (word >> CHUNK_SLOT_BITS) * SEG_ALIGN
        make_copy(pl.multiple_of(src, SEG_ALIGN), pl.multiple_of(dst, SEG_ALIGN))
        return carry

    lax.fori_loop(0, tile_chunks[tile], body, 0)


def _wait_chunks(n, wait_chunk, wait_bulk):
    has_bulk = n >= MIN_TILE_CHUNKS

    @pl.when(has_bulk)
    def _():
        wait_bulk()

    def body(c, carry):
        wait_chunk()
        return carry
    lax.fori_loop(jnp.where(has_bulk, MIN_TILE_CHUNKS, 0), n, body, 0)


def _dispatch_kernel(chunks, tile_chunks, pad_row, pad_chunks, n_used,
                     h_ref, ri_ref, rw_ref, seg_ref, xs_ref, pw_ref, zbuf, zeros_scr, sem, pad_sem, tail_sem):
    i = pl.program_id(0)
    n_tiles = pl.num_programs(0)
    slot = i % 2
    tm = h_ref.shape[0]
    pos1, pos2 = _tile_positions(ri_ref, seg_ref)
    pw_ref[...] = jnp.concatenate([pos1, pos2, rw_ref[0:2, :], jnp.zeros((4, tm), F32)], axis=0)
    r_iota = lax.broadcasted_iota(I32, (TILE_SLOTS, tm), 0).astype(F32)
    onehot = jnp.logical_or(r_iota == pos1, r_iota == pos2).astype(BF16)
    zbuf[slot] = _pack_pairs(jnp.dot(onehot, h_ref[...], preferred_element_type=F32))

    def chunk_copy(buf_slot, src, dst):
        return pltpu.make_async_copy(zbuf.at[buf_slot, pl.ds(src, SEG_ALIGN)],
                                     xs_ref.at[pl.ds(dst, SEG_ALIGN)], sem.at[buf_slot])

    _segment_copies((chunks, tile_chunks), i, lambda src, dst: chunk_copy(slot, src, dst).start())

    def wait_tile(tile, buf_slot):
        bulk = pltpu.make_async_copy(zbuf.at[buf_slot, pl.ds(0, MIN_TILE_CHUNKS * SEG_ALIGN)],
                                     xs_ref.at[pl.ds(0, MIN_TILE_CHUNKS * SEG_ALIGN)], sem.at[buf_slot])
        _wait_chunks(tile_chunks[tile], chunk_copy(buf_slot, 0, 0).wait, bulk.wait)

    @pl.when(i > 0)
    def _():
        wait_tile(i - 1, 1 - slot)

    @pl.when(i == n_tiles - 1)
    def _():
        zeros_scr[...] = jnp.zeros(zeros_scr.shape, I32)

        def pad_copy(dst):
            return pltpu.make_async_copy(zeros_scr.at[pl.ds(0, SEG_ALIGN)], xs_ref.at[pl.ds(dst, SEG_ALIGN)], pad_sem)

        def tail_copy(dst):
            return pltpu.make_async_copy(zeros_scr, xs_ref.at[pl.ds(dst, MOE_BLOCK)], tail_sem)

        def per_expert(e, total):
            def per_chunk(c, carry):
                pad_copy(pl.multiple_of(pad_row[e] + c * SEG_ALIGN, SEG_ALIGN)).start()
                return carry
            lax.fori_loop(0, pad_chunks[e], per_chunk, 0)
            return total + pad_chunks[e]

        n_pad_copies = lax.fori_loop(0, MOE_EXPERTS, per_expert, 0)
        n_blocks = xs_ref.shape[0] // MOE_BLOCK

        def tail_start(b, carry):
            tail_copy(pl.multiple_of(b * MOE_BLOCK, MOE_BLOCK)).start()
            return carry
        lax.fori_loop(n_used[0], n_blocks, tail_start, 0)
        wait_tile(i, slot)

        def wait_pad(c, carry):
            pad_copy(0).wait()
            return carry
        lax.fori_loop(0, n_pad_copies, wait_pad, 0)

        def wait_tail(b, carry):
            tail_copy(0).wait()
            return carry
        lax.fori_loop(n_used[0], n_blocks, wait_tail, 0)


def _dispatch(meta, h2, ri, rw, seg_start, n_rows):
    T, D = h2.shape
    tm = TOKEN_TILE
    col = lambda i, *_: (0, i)
    return pl.pallas_call(
        _dispatch_kernel,
        grid_spec=pltpu.PrefetchScalarGridSpec(
            num_scalar_prefetch=5,
            grid=(T // tm,),
            in_specs=[pl.BlockSpec((tm, D), lambda i, *_: (i, 0)),
                      pl.BlockSpec((8, tm), col),
                      pl.BlockSpec((8, tm), col),
                      pl.BlockSpec((None, MOE_EXPERTS, 1), lambda i, *_: (i, 0, 0))],
            out_specs=[pl.BlockSpec(memory_space=pl.ANY), pl.BlockSpec((8, tm), col)],
            scratch_shapes=[pltpu.VMEM((2, TILE_SLOTS, D // 2), I32), pltpu.VMEM((MOE_BLOCK, D // 2), I32),
                            pltpu.SemaphoreType.DMA((2,)), pltpu.SemaphoreType.DMA(()),
                            pltpu.SemaphoreType.DMA(())]),
        out_shape=[jax.ShapeDtypeStruct((n_rows, D // 2), I32), jax.ShapeDtypeStruct((8, T), F32)],
        compiler_params=_params("arbitrary"),
        name="moe_dispatch",
    )(*meta, h2, ri, rw, seg_start)


def _expert_kernel(be_ref, nb_ref, x_ref, wi_ref, wo_ref, o_ref, wi_bf, wo_bf):
    i = pl.program_id(0)
    F = wo_ref.shape[0]
    in_use = i < nb_ref[0]

    @pl.when(jnp.logical_and(in_use, jnp.logical_or(i == 0, be_ref[i] != be_ref[jnp.maximum(i - 1, 0)])))
    def _():
        wi_bf[...] = wi_ref[...].astype(BF16)
        wo_bf[...] = wo_ref[...].astype(BF16)

    @pl.when(in_use)
    def _():
        hid = jnp.dot(_unpack_pairs(x_ref[...]), wi_bf[...], preferred_element_type=F32)
        a = hid[:, :F]
        act = a * jax.nn.sigmoid(a) * hid[:, F:]
        y = jnp.dot(act.astype(BF16), wo_bf[...], preferred_element_type=F32)
        o_ref[...] = _pack_pairs(y.astype(BF16).astype(F32))

    @pl.when(jnp.logical_not(in_use))
    def _():
        o_ref[...] = jnp.zeros(o_ref.shape, I32)


def _experts(block_e, n_used, xs, w_e_in, w_e_out):
    n_rows, half_d = xs.shape
    D = 2 * half_d
    F = w_e_out.shape[1]
    blk = lambda i, be, nb: jnp.maximum(jnp.minimum(i, nb[0] - 1), 0)
    return pl.pallas_call(
        _expert_kernel,
        grid_spec=pltpu.PrefetchScalarGridSpec(
            num_scalar_prefetch=2,
            grid=(n_rows // MOE_BLOCK,),
            in_specs=[pl.BlockSpec((MOE_BLOCK, half_d), lambda i, be, nb: (blk(i, be, nb), 0)),
                      pl.BlockSpec((None, D, 2 * F), lambda i, be, nb: (be[blk(i, be, nb)], 0, 0)),
                      pl.BlockSpec((None, F, D), lambda i, be, nb: (be[blk(i, be, nb)], 0, 0))],
            out_specs=pl.BlockSpec((MOE_BLOCK, half_d), lambda i, be, nb: (i, 0)),
            scratch_shapes=[pltpu.VMEM((D, 2 * F), BF16), pltpu.VMEM((F, D), BF16)]),
        out_shape=jax.ShapeDtypeStruct((n_rows, half_d), I32),
        compiler_params=_params("arbitrary"),
        name="experts",
    )(block_e, n_used, xs, w_e_in, w_e_out)


def _final_kernel(chunks, tile_chunks, x1_ref, pw_ref, mod_ref, g_ref, ys_ref, o_ref, ybuf, sem):
    i = pl.program_id(0)
    n_tiles = pl.num_programs(0)
    slot = i % 2
    tm = x1_ref.shape[0]

    def chunk_copy(buf_slot, src, dst):
        return pltpu.make_async_copy(ys_ref.at[pl.ds(dst, SEG_ALIGN)],
                                     ybuf.at[buf_slot, pl.ds(src, SEG_ALIGN)], sem.at[buf_slot])

    def fetch(tile, buf_slot):
        _segment_copies((chunks, tile_chunks), tile, lambda src, dst: chunk_copy(buf_slot, src, dst).start())

    @pl.when(i == 0)
    def _():
        ybuf[...] = jnp.zeros(ybuf.shape, I32)
        fetch(0, 0)

    @pl.when(i + 1 < n_tiles)
    def _():
        fetch(i + 1, 1 - slot)

    bulk = pltpu.make_async_copy(ys_ref.at[pl.ds(0, MIN_TILE_CHUNKS * SEG_ALIGN)],
                                 ybuf.at[slot, pl.ds(0, MIN_TILE_CHUNKS * SEG_ALIGN)], sem.at[slot])
    _wait_chunks(tile_chunks[i], chunk_copy(slot, 0, 0).wait, bulk.wait)

    s_iota = lax.broadcasted_iota(I32, (tm, TILE_SLOTS), 1).astype(F32)
    comb = (jnp.where(s_iota == pw_ref[:, 0:1], pw_ref[:, 2:3], 0.0)
            + jnp.where(s_iota == pw_ref[:, 1:2], pw_ref[:, 3:4], 0.0))
    moe = jnp.dot(comb.astype(BF16), _unpack_pairs(ybuf[slot]), preferred_element_type=F32)
    x2 = x1_ref[...] + mod_ref[5:6, :] * moe
    o_ref[...] = _rms(x2, NORM_EPS) * g_ref[...]


def _final(meta, x1, pw_tok, mod, final_g, ys, seq):
    T, D = x1.shape
    tm = TOKEN_TILE
    per_b = seq // tm
    row = lambda i, *_: (i, 0)
    return pl.pallas_call(
        _final_kernel,
        grid_spec=pltpu.PrefetchScalarGridSpec(
            num_scalar_prefetch=2,
            grid=(T // tm,),
            in_specs=[pl.BlockSpec((tm, D), row),
                      pl.BlockSpec((tm, 8), row),
                      pl.BlockSpec((None, 6, D), lambda i, *_: (i // per_b, 0, 0)),
                      pl.BlockSpec((1, D), lambda i, *_: (0, 0)),
                      pl.BlockSpec(memory_space=pl.ANY)],
            out_specs=pl.BlockSpec((tm, D), row),
            scratch_shapes=[pltpu.VMEM((2, TILE_SLOTS, D // 2), I32), pltpu.SemaphoreType.DMA((2,))]),
        out_shape=jax.ShapeDtypeStruct((T, D), F32),
        compiler_params=_params("arbitrary"),
        name="final",
    )(*meta, x1, pw_tok, mod, final_g.reshape(1, D), ys)


def _round_up(x, m):
    return (x + m - 1) // m * m


def _moe_layout(tile_counts):
    n_tiles = tile_counts.shape[0]
    seg = _round_up(tile_counts, SEG_ALIGN)
    seg_start = jnp.cumsum(seg, axis=1) - seg
    tile_base = jnp.cumsum(seg, axis=0) - seg
    used = jnp.sum(seg, axis=0)
    region = _round_up(used, MOE_BLOCK)
    region_end = jnp.cumsum(region)
    region_start = region_end - region
    dst_row = region_start[None, :] + tile_base
    n_chunk = seg // SEG_ALIGN
    n_assign = 2 * n_tiles * TOKEN_TILE
    n_rows = _round_up(n_assign + n_tiles * MOE_EXPERTS * (SEG_ALIGN - 1) + MOE_EXPERTS * (MOE_BLOCK - 1), MOE_BLOCK)
    block_start = jnp.arange(n_rows // MOE_BLOCK, dtype=I32) * MOE_BLOCK
    block_e = jnp.minimum(jnp.sum(block_start[:, None] >= region_end[None, :], axis=1), MOE_EXPERTS - 1)
    i32 = lambda a: a.reshape(-1).astype(I32)
    chunk_end = jnp.cumsum(n_chunk, axis=1)
    chunk_start = chunk_end - n_chunk
    k = jnp.arange(TILE_CHUNKS)[None, :, None]
    mine = (chunk_start[:, None, :] <= k) & (k < chunk_end[:, None, :])
    pick = lambda a: jnp.sum(jnp.where(mine, (a // SEG_ALIGN - chunk_start)[:, None, :] + k, 0), axis=2)
    word = pick(seg_start) | (pick(dst_row) << CHUNK_SLOT_BITS)
    meta = (i32(word), i32(chunk_end[:, -1]))
    pad = (i32(region_start + used), i32((region - used) // SEG_ALIGN))
    n_used = (region_end[-1:] // MOE_BLOCK).astype(I32)
    return meta, pad, seg_start.astype(I32)[:, :, None], block_e.astype(I32), n_used, n_rows


def kernel(x, c, w_ada, b_ada, norm1_g, w_in, rel_bias, lambda_q1, lambda_k1, lambda_q2, lambda_k2, subln_g, ssm_lambda_re, ssm_lambda_im, ssm_log_step, ssm_b_re, ssm_b_im, ssm_c_re, ssm_c_im, ssm_d, w_glu, w_proj_attn, w_proj_ssm, w_out, norm2_g, w_router_group, b_router_group, w_router_expert, b_router_expert, w_expert_in, w_expert_out, final_g):
    B, S, D = x.shape
    T = B * S
    x2 = x.reshape(T, D)
    mod = _ada_mod(c, w_ada[0], b_ada[0]).reshape(B, 6, D)
    q, k, v, u, gs = _in_proj(x2, mod, norm1_g[0], w_in[0], S)
    lam_vecs = jnp.stack([lambda_q1[0], lambda_k1[0], lambda_q2[0], lambda_k2[0]]).astype(F32)
    y_attn = _diff_attn(q, k, v, rel_bias, lam_vecs, subln_g[0], B, S)
    tables = _s5_tables(ssm_lambda_re[0], ssm_lambda_im[0], ssm_log_step[0], ssm_b_re[0], ssm_b_im[0],
                        ssm_c_re[0], ssm_c_im[0], ssm_d[0])
    y_s5 = _s5_branch(u.reshape(-1, SSM_WIDTH), tables, B, S).reshape(u.shape)
    x1, h2, ri, rw, tile_counts = _merge_route(
        x2, y_attn, y_s5, gs, mod, w_glu[0], w_proj_attn[0], w_proj_ssm[0], w_out[0], norm2_g[0],
        w_router_group[0], b_router_group[0], w_router_expert[0], b_router_expert[0], S)
    meta, pad, seg_start, block_e, n_used, n_rows = _moe_layout(tile_counts[:, :, 0])
    xs, pw = _dispatch(meta + pad + (n_used,), h2, ri, rw, seg_start, n_rows)
    ys = _experts(block_e, n_used, xs, w_expert_in[0], w_expert_out[0])
    out = _final(meta, x1, pw.T, mod, final_g, ys, S)
    return out.reshape(B, S, D)
```

```python
import functools
import math

import jax
import jax.numpy as jnp
from jax import lax
from jax.experimental import pallas as pl
from jax.experimental.pallas import tpu as pltpu

F32 = jnp.float32
BF16 = jnp.bfloat16
I32 = jnp.int32
HIGHEST = lax.Precision.HIGHEST

ATTN_HEADS = 4
ATTN_HEAD_DIM = 64
ATTN_V_DIM = 2 * ATTN_HEAD_DIM
ATTN_WIDTH = ATTN_HEADS * ATTN_V_DIM
NEG_INF = -1e30
REL_BUCKETS = 32
REL_MAX_DISTANCE = 128
SSM_GROUP_CH = 16
SSM_WIDTH = 512
SSM_GROUPS = SSM_WIDTH // SSM_GROUP_CH
SSM_STATE = 64
SSM_EIG_MAX_RE = -1e-4
SSM_CHUNK = 16
S5_CHUNKS_PER_STEP = 32
MOE_GROUPS = 4
MOE_EXPERTS_PER_GROUP = 8
MOE_EXPERTS = MOE_GROUPS * MOE_EXPERTS_PER_GROUP
MOE_BLOCK = 512
SEG_ALIGN = 8
NORM_EPS = 1e-6
SUBLN_EPS = 1e-5
LAMBDA_INIT = 0.8 - 0.6 * math.exp(-0.3 * 0)

ATTN_BLOCK = 256
ATTN_HEADS_PER_STEP = 4
ATTN_ROW_CHUNK = 32
ATTN_ONES_ROWS = 16
LOG2E = math.log2(math.e)
TOKEN_TILE = 512
ROUTER_ROWS = 40
ROUTER_LANES = 128
TILE_SLOTS = -(-(2 * TOKEN_TILE + MOE_EXPERTS * (SEG_ALIGN - 1)) // 256) * 256
TILE_CHUNKS = TILE_SLOTS // SEG_ALIGN
MIN_TILE_CHUNKS = 2 * TOKEN_TILE // SEG_ALIGN
CHUNK_ISSUE_UNROLL = 8
CHUNK_SLOT_BITS = 8
assert TILE_CHUNKS <= 2 ** CHUNK_SLOT_BITS
VMEM_LIMIT = 56 << 20


def _params(*sem):
    return pltpu.CompilerParams(dimension_semantics=sem, vmem_limit_bytes=VMEM_LIMIT)


def _rms(x, eps):
    return x * lax.rsqrt(jnp.mean(x * x, axis=-1, keepdims=True) + eps)


def _mod_kernel(c_ref, w_ref, b_ref, o_ref):
    c = c_ref[...]
    c_act = c * jax.nn.sigmoid(c)
    o_ref[...] = jnp.dot(c_act, w_ref[...], preferred_element_type=F32, precision=HIGHEST) + b_ref[...]


def _ada_mod(c, w_ada, b_ada):
    B, D = c.shape
    N = w_ada.shape[1]
    tn = 1024
    return pl.pallas_call(
        _mod_kernel,
        grid=(N // tn,),
        in_specs=[pl.BlockSpec((B, D), lambda j: (0, 0)),
                  pl.BlockSpec((D, tn), lambda j: (0, j)),
                  pl.BlockSpec((1, tn), lambda j: (0, j))],
        out_specs=pl.BlockSpec((B, tn), lambda j: (0, j)),
        out_shape=jax.ShapeDtypeStruct((B, N), F32),
        compiler_params=_params("arbitrary"),
        name="ada_mod",
    )(c, w_ada, b_ada.reshape(1, N))


def _proj_kernel(x_ref, mod_ref, g_ref, w_ref, q_ref, k_ref, v_ref, u_ref, gs_ref):
    y = _rms(x_ref[...], NORM_EPS) * g_ref[...]
    h = (y * (1.0 + mod_ref[1:2, :]) + mod_ref[0:1, :]).astype(BF16)
    W = ATTN_WIDTH

    def proj(lo, hi):
        return jnp.dot(h, w_ref[:, lo:hi], preferred_element_type=F32)

    q_ref[...] = (proj(0, W) * (ATTN_HEAD_DIM ** -0.5 * LOG2E)).astype(BF16)
    k_ref[...] = proj(W, 2 * W).astype(BF16)
    v_ref[...] = proj(2 * W, 3 * W).astype(BF16)
    u_ref[...] = proj(3 * W, 3 * W + SSM_WIDTH).reshape(u_ref.shape)
    gs_ref[...] = jax.nn.sigmoid(proj(3 * W + SSM_WIDTH, w_ref.shape[1])).astype(BF16)


def _chunk_major_spec(tm, per_b):
    return pl.BlockSpec((tm // SSM_CHUNK, None, SSM_CHUNK, SSM_WIDTH), lambda i: (i % per_b, i // per_b, 0, 0))


def _in_proj(x2, mod, norm_g, w_in, seq):
    T, D = x2.shape
    tm = TOKEN_TILE
    per_b = seq // tm
    n_gate = w_in.shape[1] - 3 * ATTN_WIDTH - SSM_WIDTH
    row = lambda i: (i, 0)
    return pl.pallas_call(
        _proj_kernel,
        grid=(T // tm,),
        in_specs=[pl.BlockSpec((tm, D), row),
                  pl.BlockSpec((None, 6, D), lambda i: (i // per_b, 0, 0)),
                  pl.BlockSpec((1, D), lambda i: (0, 0)),
                  pl.BlockSpec(w_in.shape, lambda i: (0, 0))],
        out_specs=[pl.BlockSpec((tm, ATTN_WIDTH), row)] * 3
        + [_chunk_major_spec(tm, per_b), pl.BlockSpec((tm, n_gate), row)],
        out_shape=[jax.ShapeDtypeStruct((T, ATTN_WIDTH), BF16)] * 3
        + [jax.ShapeDtypeStruct((seq // SSM_CHUNK, T // seq, SSM_CHUNK, SSM_WIDTH), F32),
           jax.ShapeDtypeStruct((T, n_gate), BF16)],
        compiler_params=_params("parallel"),
        name="in_proj",
    )(x2, mod, norm_g.reshape(1, D), w_in.astype(BF16))


def _rel_bucket(dist):
    max_exact = REL_BUCKETS // 2
    n = jnp.maximum(dist, 0)
    log_ratio = jnp.log(jnp.maximum(n, 1).astype(F32) / max_exact) / math.log(REL_MAX_DISTANCE / max_exact)
    large = max_exact + (log_ratio * (REL_BUCKETS - max_exact)).astype(I32)
    large = jnp.minimum(large, REL_BUCKETS - 1)
    return jnp.where(n < max_exact, n, large)


def _attn_bias_tiles(rel_bias, blk):
    assert blk >= REL_MAX_DISTANCE
    n_heads = rel_bias.shape[1]
    far = rel_bias[REL_BUCKETS - 1].astype(F32)
    m = jnp.arange(2 * blk)
    signed = jnp.where(m < blk, m, m - 2 * blk)
    tiles = []
    for kind in range(2):
        dist = kind * blk + signed
        tab = jnp.where(dist >= 0, (rel_bias[_rel_bucket(dist)].astype(F32).T - far[:, None]) * LOG2E, NEG_INF)
        skew = jnp.tile(tab, (1, blk))[:, :blk * (2 * blk - 1)].reshape(n_heads, blk, 2 * blk - 1)
        tiles.append(skew[:, :, :blk])
    return jnp.stack(tiles, axis=1)


def _attn_kernel(lam_ref, q_ref, k_ref, v_ref, bias_ref, g_ref, o_ref, vt_scr, *bufs, blk, heads):
    i = pl.program_id(2)
    n_kv = vt_scr.shape[1]
    V = ATTN_V_DIM
    ns = 2 * heads
    acc = bufs[0:ns]
    sbuf = tuple(bufs[(1 + r) * ns:(2 + r) * ns] for r in range(3))
    pbufs = tuple(bufs[(4 + r) * ns:(5 + r) * ns] for r in range(3))

    @pl.when(i == 0)
    def _():
        for hd in range(heads):
            for jb in range(n_kv):
                vt_scr[hd, jb, 0:V, :] = v_ref[jb * blk:(jb + 1) * blk, hd * V:(hd + 1) * V].astype(F32).T.astype(BF16)
                vt_scr[hd, jb, V:, :] = jnp.ones((ATTN_ONES_ROWS, blk), BF16)

    qt_maps = []
    for hd in range(heads):
        qt = q_ref[:, hd * V:(hd + 1) * V].astype(F32).T
        feat = lax.broadcasted_iota(I32, qt.shape, 0)
        qt_maps += [jnp.where(feat < ATTN_HEAD_DIM, qt, 0.0).astype(BF16),
                    jnp.where(feat >= ATTN_HEAD_DIM, qt, 0.0).astype(BF16)]

    def scores(j, st):
        hd = st // 2
        kj = k_ref[pl.ds(pl.multiple_of(j * blk, blk), blk), hd * V:(hd + 1) * V]
        return jnp.dot(kj, qt_maps[st], preferred_element_type=F32)

    n_chunks = blk // ATTN_ROW_CHUNK

    def rows(c):
        return slice(c * ATTN_ROW_CHUNK, (c + 1) * ATTN_ROW_CHUNK)

    def fold8(x):
        return x.reshape(ATTN_ROW_CHUNK // 8, 8, blk)

    def block(j, carry, pos, lookahead, bias_kind):
        src, dst, pbuf = sbuf[pos], sbuf[(pos + 2) % 3], pbufs[pos]
        out = []
        for st in range(ns):
            hd = st // 2
            if lookahead:
                dst[st][...] = scores(j + 2, st)

            def chunk(c):
                s = src[st][rows(c), :]
                return s if bias_kind is None else s + bias_ref[hd, bias_kind, rows(c), :]

            m_old = carry[st]
            m8 = jnp.max(fold8(chunk(0)), axis=0)
            for c in range(1, n_chunks):
                m8 = jnp.maximum(m8, jnp.max(fold8(chunk(c)), axis=0))
            m_new = jnp.maximum(m_old, jnp.max(m8, axis=0, keepdims=True))
            alpha = jnp.exp2(m_old - m_new)
            for c in range(n_chunks):
                pbuf[st][rows(c), :] = jnp.exp2(chunk(c) - m_new).astype(BF16)
            acc[st][...] = alpha * acc[st][...] + jnp.dot(vt_scr[hd, j], pbuf[st][...],
                                                          preferred_element_type=F32)
            out.append(m_new)
        return tuple(out)

    def far_triple(t, carry):
        for r in range(3):
            carry = block(3 * t + r, carry, r, True, None)
        return carry

    def far_single(j, carry):
        carry = block(j, carry, 0, True, None)
        for st in range(ns):
            sbuf[0][st][...] = sbuf[1][st][...]
        for st in range(ns):
            sbuf[1][st][...] = sbuf[2][st][...]
        return carry

    def near_pair(_, carry):
        return block(i, block(i - 1, carry, 0, False, 1), 1, False, 0)

    def near_single(_, carry):
        return block(i, carry, 0, False, 0)

    for st in range(ns):
        acc[st][...] = jnp.zeros(acc[st].shape, F32)
        sbuf[0][st][...] = scores(0, st)
        sbuf[1][st][...] = scores(jnp.minimum(i, 1), st)
    m0 = jnp.full((1, blk), -jnp.inf, F32)
    n_far = jnp.maximum(i - 1, 0)
    carry = lax.fori_loop(0, n_far // 3, far_triple, (m0,) * ns)
    carry = lax.fori_loop(n_far - n_far % 3, n_far, far_single, carry)
    carry = lax.fori_loop(0, jnp.minimum(i, 1), near_pair, carry)
    lax.fori_loop(0, 1 - jnp.minimum(i, 1), near_single, carry)

    lam = (jnp.exp(jnp.sum(lam_ref[0:1, :] * lam_ref[1:2, :], axis=-1, keepdims=True))
           - jnp.exp(jnp.sum(lam_ref[2:3, :] * lam_ref[3:4, :], axis=-1, keepdims=True)) + LAMBDA_INIT)
    for hd in range(heads):
        a1, a2 = acc[2 * hd], acc[2 * hd + 1]
        ot = a1[0:V, :] / a1[V:V + 1, :] - lam * (a2[0:V, :] / a2[V:V + 1, :])
        ot = ot * lax.rsqrt(jnp.mean(ot * ot, axis=0, keepdims=True) + SUBLN_EPS)
        o_ref[:, hd * V:(hd + 1) * V] = (ot.T * (g_ref[...] * (1.0 - LAMBDA_INIT))).astype(BF16)


def _diff_attn(q, k, v, rel_bias, lam_vecs, subln_g, batch, seq):
    T = q.shape[0]
    blk = ATTN_BLOCK
    nq = seq // blk
    bias = _attn_bias_tiles(rel_bias, blk)
    hp = ATTN_HEADS_PER_STEP
    ns = 2 * hp
    width = hp * ATTN_V_DIM
    acc_rows = ATTN_V_DIM + ATTN_ONES_ROWS
    return pl.pallas_call(
        functools.partial(_attn_kernel, blk=blk, heads=hp),
        grid=(batch, ATTN_HEADS // hp, nq),
        in_specs=[pl.BlockSpec((4, ATTN_HEAD_DIM), lambda b, h, i: (0, 0)),
                  pl.BlockSpec((blk, width), lambda b, h, i: (b * nq + i, h)),
                  pl.BlockSpec((seq, width), lambda b, h, i: (b, h)),
                  pl.BlockSpec((seq, width), lambda b, h, i: (b, h)),
                  pl.BlockSpec((hp, 2, blk, blk), lambda b, h, i: (h, 0, 0, 0)),
                  pl.BlockSpec((1, ATTN_V_DIM), lambda b, h, i: (0, 0))],
        out_specs=pl.BlockSpec((blk, width), lambda b, h, i: (b * nq + i, h)),
        out_shape=jax.ShapeDtypeStruct((T, ATTN_WIDTH), BF16),
        scratch_shapes=[pltpu.VMEM((hp, nq, acc_rows, blk), BF16)]
        + [pltpu.VMEM((acc_rows, blk), F32)] * ns + [pltpu.VMEM((blk, blk), F32)] * (3 * ns)
        + [pltpu.VMEM((blk, blk), BF16)] * (3 * ns),
        compiler_params=_params("parallel", "parallel", "arbitrary"),
        name="diff_attn",
    )(lam_vecs, q, k, v, bias, subln_g.reshape(1, ATTN_V_DIM))


def _s5_tables(lam_re, lam_im, log_step, b_re, b_im, c_re, c_im, d_skip):
    L = SSM_CHUNK
    G, P = lam_re.shape
    H = SSM_GROUP_CH
    lr = jnp.minimum(lam_re.astype(F32), SSM_EIG_MAX_RE)
    li = lam_im.astype(F32)
    step = jnp.exp(log_step.astype(F32))[:, None]
    mag = jnp.exp(lr * step)
    ang = li * step
    a_re = mag * jnp.cos(ang)
    a_im = mag * jnp.sin(ang)
    den = lr * lr + li * li
    num_re = a_re - 1.0
    coef_re = (num_re * lr + a_im * li) / den
    coef_im = (a_im * lr - num_re * li) / den
    br = b_re.astype(F32)
    bi = b_im.astype(F32)
    bb_re = coef_re[..., None] * br - coef_im[..., None] * bi
    bb_im = coef_re[..., None] * bi + coef_im[..., None] * br
    pw_re, pw_im = [jnp.ones_like(a_re)], [jnp.zeros_like(a_re)]
    for _ in range(L):
        pr, pi = pw_re[-1], pw_im[-1]
        pw_re.append(pr * a_re - pi * a_im)
        pw_im.append(pr * a_im + pi * a_re)
    pw_re = jnp.stack(pw_re)
    pw_im = jnp.stack(pw_im)
    cr = c_re.astype(F32)[None]
    ci = c_im.astype(F32)[None]
    cp_re = cr * pw_re[:, :, None, :] - ci * pw_im[:, :, None, :]
    cp_im = cr * pw_im[:, :, None, :] + ci * pw_re[:, :, None, :]
    kern = (jnp.einsum('tghp,gpk->tghk', cp_re[:L], bb_re, precision=HIGHEST)
            - jnp.einsum('tghp,gpk->tghk', cp_im[:L], bb_im, precision=HIGHEST))
    s_idx = jnp.arange(L)[:, None]
    t_idx = jnp.arange(L)[None, :]
    toep = jnp.where((t_idx >= s_idx)[:, :, None, None, None], kern[jnp.maximum(t_idx - s_idx, 0)], 0.0)
    m_tab = jnp.transpose(toep, (2, 0, 4, 1, 3)).reshape(G, L * H, L * H)
    rev_re = pw_re[L - 1::-1][:, :, None, :]
    rev_im = pw_im[L - 1::-1][:, :, None, :]
    bbt_re = jnp.transpose(bb_re, (0, 2, 1))[None]
    bbt_im = jnp.transpose(bb_im, (0, 2, 1))[None]
    bst_re = jnp.transpose(rev_re * bbt_re - rev_im * bbt_im, (1, 0, 2, 3)).reshape(G, L * H, P)
    bst_im = jnp.transpose(rev_re * bbt_im + rev_im * bbt_re, (1, 0, 2, 3)).reshape(G, L * H, P)
    cst_re = jnp.transpose(cp_re[1:], (1, 3, 0, 2)).reshape(G, P, L * H)
    cst_im = -jnp.transpose(cp_im[1:], (1, 3, 0, 2)).reshape(G, P, L * H)
    a_chunk = jnp.stack([jnp.concatenate([pw_re[L], pw_re[L]], axis=-1),
                         jnp.concatenate([-pw_im[L], pw_im[L]], axis=-1)], axis=1)
    d_tab = jnp.tile(d_skip.astype(F32), (1, L)).reshape(G, 1, L * H)
    bst = jnp.concatenate([bst_re, bst_im], axis=-1)
    cst = jnp.concatenate([cst_re, cst_im], axis=1)
    return m_tab.astype(BF16), bst.astype(BF16), cst.astype(BF16), a_chunk, d_tab


def _gelu_tanh(x):
    return 0.5 * x * (1.0 + jnp.tanh(math.sqrt(2.0 / math.pi) * (x + 0.044715 * (x * x * x))))


def _lane_block_transpose(arrs):
    n = len(arrs)
    width = arrs[0].shape[1]
    blk_id = lax.broadcasted_iota(I32, arrs[0].shape, 1) // SSM_GROUP_CH
    k = n // 2
    while k >= 1:
        keep = (blk_id & k) == 0
        nxt = list(arrs)
        for r in range(n):
            if r & k == 0:
                a, b = arrs[r], arrs[r + k]
                nxt[r] = jnp.where(keep, a, pltpu.roll(b, k * SSM_GROUP_CH, axis=1))
                nxt[r + k] = jnp.where(keep, pltpu.roll(a, width - k * SSM_GROUP_CH, axis=1), b)
        arrs = nxt
        k //= 2
    return arrs


def _s5_kernel(u_ref, m_ref, bst_ref, cst_ref, a_ref, d_ref, o_ref, us_scr, z_scr, y_scr, st_scr, *, batch):
    L, H = SSM_CHUNK, SSM_GROUP_CH
    n_grp = us_scr.shape[0]
    R = us_scr.shape[1]
    half = 128 // H

    @pl.when(pl.program_id(1) == 0)
    def _():
        st_scr[...] = jnp.zeros(st_scr.shape, F32)

    for hh in range(L // half):
        slabs = [u_ref[pl.ds(hh * half + s, R, stride=L), :] for s in range(half)]
        for gi, arr in enumerate(_lane_block_transpose(slabs)):
            us_scr[gi, :, hh * 128:(hh + 1) * 128] = arr.astype(BF16)
    for gi in range(n_grp):
        u = us_scr[gi]
        z_scr[gi] = jnp.dot(u, bst_ref[gi], preferred_element_type=F32)
        y_scr[gi] = jnp.dot(u, m_ref[gi], preferred_element_type=F32) + u.astype(F32) * d_ref[gi]

    def step(c, state):
        sl = pl.ds(pl.multiple_of(c * batch, batch), batch)
        out = []
        for gi in range(n_grp):
            x = state[gi]
            z = z_scr[gi, sl, :]
            z_scr[gi, sl, :] = x
            out.append(a_ref[gi, 0:1, :] * x + a_ref[gi, 1:2, :] * pltpu.roll(x, SSM_STATE, axis=1) + z)
        return tuple(out)

    state = lax.fori_loop(0, R // batch, step, tuple(st_scr[gi] for gi in range(n_grp)))
    for gi in range(n_grp):
        st_scr[gi] = state[gi]

    for gi in range(n_grp):
        y = y_scr[gi] + jnp.dot(z_scr[gi].astype(BF16), cst_ref[gi], preferred_element_type=F32)
        y_scr[gi] = _gelu_tanh(y)
    for hh in range(L // half):
        cols = [y_scr[gi, :, hh * 128:(hh + 1) * 128] for gi in range(n_grp)]
        for s, arr in enumerate(_lane_block_transpose(cols)):
            o_ref[pl.ds(hh * half + s, R, stride=L), :] = arr


def _s5_branch(u, tables, batch, seq):
    L, G, H, P = SSM_CHUNK, SSM_GROUPS, SSM_GROUP_CH, SSM_STATE
    n_chunks = seq // L
    gpt = 128 // H
    cr = S5_CHUNKS_PER_STEP
    R = cr * batch
    LH = L * H
    m_tab, bst, cst, a_chunk, d_tab = tables
    tile = lambda o, c: (o, 0, 0)
    return pl.pallas_call(
        functools.partial(_s5_kernel, batch=batch),
        grid=(G // gpt, n_chunks // cr),
        in_specs=[pl.BlockSpec((R * L, 128), lambda o, c: (c, o)),
                  pl.BlockSpec((gpt, LH, LH), tile),
                  pl.BlockSpec((gpt, LH, 2 * P), tile),
                  pl.BlockSpec((gpt, 2 * P, LH), tile),
                  pl.BlockSpec((gpt, 2, 2 * P), tile),
                  pl.BlockSpec((gpt, 1, LH), tile)],
        out_specs=pl.BlockSpec((R * L, 128), lambda o, c: (c, o)),
        out_shape=jax.ShapeDtypeStruct(u.shape, F32),
        scratch_shapes=[pltpu.VMEM((gpt, R, LH), BF16), pltpu.VMEM((gpt, R, 2 * P), F32),
                        pltpu.VMEM((gpt, R, LH), F32), pltpu.VMEM((gpt, batch, 2 * P), F32)],
        compiler_params=_params("parallel", "arbitrary"),
        name="s5",
    )(u, m_tab, bst, cst, a_chunk, d_tab)


def _merge_kernel(x_ref, ya_ref, ys_ref, gs_ref, mod_ref, wglu_ref, pa_ref, ps_ref, wout_ref, g2_ref,
                  wr_ref, br_ref, x1_ref, h2_ref, ri_ref, rw_ref, cnt_ref):
    tm, D = x_ref.shape
    ys = ys_ref[...].reshape(tm, SSM_WIDTH).astype(BF16)
    gl = jnp.dot(ys, wglu_ref[...], preferred_element_type=F32)
    y_ssm = gl[:, :SSM_WIDTH] * jax.nn.sigmoid(gl[:, SSM_WIDTH:])
    p_attn = jnp.dot(ya_ref[...], pa_ref[...], preferred_element_type=F32)
    p_ssm = jnp.dot(y_ssm.astype(BF16), ps_ref[...], preferred_element_type=F32)
    merged = gs_ref[:, :D].astype(F32) * p_attn + gs_ref[:, D:].astype(F32) * p_ssm
    mixed = jnp.dot(merged.astype(BF16), wout_ref[...], preferred_element_type=F32)
    x1 = x_ref[...] + mod_ref[2:3, :] * mixed
    x1_ref[...] = x1
    h2 = _rms(x1, NORM_EPS) * g2_ref[...] * (1.0 + mod_ref[4:5, :]) + mod_ref[3:4, :]
    h2_hi = h2.astype(BF16)
    h2_ref[...] = h2_hi

    h2_lo = (h2 - h2_hi.astype(F32)).astype(BF16)
    hi_prod = jnp.dot(h2_hi, wr_ref[...], preferred_element_type=F32)
    lo_prod = jnp.dot(h2_lo, wr_ref[:, 0:ROUTER_LANES], preferred_element_type=F32)
    lg_tok = hi_prod[:, 0:ROUTER_LANES] + hi_prod[:, ROUTER_LANES:] + lo_prod
    logits = lg_tok.T[0:ROUTER_ROWS, :] + br_ref[...]
    NG, EPG = MOE_GROUPS, MOE_EXPERTS_PER_GROUP
    lg = logits[0:NG, :]
    g_iota = lax.broadcasted_iota(I32, lg.shape, 0)
    lg_max = jnp.max(lg, axis=0, keepdims=True)
    grp = jnp.min(jnp.where(lg == lg_max, g_iota, NG), axis=0, keepdims=True)
    p_grp = 1.0 / jnp.sum(jnp.exp(lg - lg_max), axis=0, keepdims=True)
    le = logits[NG:NG + EPG, :]
    for g in range(1, NG):
        le = jnp.where(grp == g, logits[NG + g * EPG:NG + (g + 1) * EPG, :], le)
    e_iota = lax.broadcasted_iota(I32, le.shape, 0)
    v1 = jnp.max(le, axis=0, keepdims=True)
    i1 = jnp.min(jnp.where(le == v1, e_iota, EPG), axis=0, keepdims=True)
    le2 = jnp.where(e_iota == i1, -jnp.inf, le)
    v2 = jnp.max(le2, axis=0, keepdims=True)
    i2 = jnp.min(jnp.where(le2 == v2, e_iota, EPG), axis=0, keepdims=True)
    e21 = jnp.exp(v2 - v1)
    w1 = p_grp / (1.0 + e21)
    w2 = p_grp * e21 / (1.0 + e21)
    eid1 = grp * EPG + i1
    eid2 = grp * EPG + i2

    x_iota = lax.broadcasted_iota(I32, (MOE_EXPERTS, tm), 0)
    hot1 = x_iota == eid1
    hot2 = x_iota == eid2
    hot = jnp.logical_or(hot1, hot2).astype(F32)
    before = (lax.broadcasted_iota(I32, (tm, tm), 0) < lax.broadcasted_iota(I32, (tm, tm), 1))
    prior = jnp.dot(hot.astype(BF16), before.astype(BF16), preferred_element_type=F32)
    rank1 = jnp.sum(jnp.where(hot1, prior, 0.0), axis=0, keepdims=True)
    rank2 = jnp.sum(jnp.where(hot2, prior, 0.0), axis=0, keepdims=True)
    cnt_ref[...] = jnp.sum(hot, axis=1, keepdims=True).astype(I32)

    zi = jnp.zeros((4, tm), I32)
    ri_ref[...] = jnp.concatenate([eid1, eid2, rank1.astype(I32), rank2.astype(I32), zi], axis=0)
    rw_ref[...] = jnp.concatenate([w1, w2, jnp.zeros((6, tm), F32)], axis=0)


def _merge_route(x2, ya, ys, gs, mod, w_glu, w_pa, w_ps, w_out, norm2_g, w_rg, b_rg, w_re, b_re, seq):
    T, D = x2.shape
    tm = TOKEN_TILE
    per_b = seq // tm
    wr = jnp.concatenate([w_rg, jnp.transpose(w_re, (1, 0, 2)).reshape(D, MOE_EXPERTS),
                          jnp.zeros((D, ROUTER_LANES - MOE_GROUPS - MOE_EXPERTS), F32)], axis=1).astype(F32)
    wr_hi = wr.astype(BF16)
    wr_lo = (wr - wr_hi.astype(F32)).astype(BF16)
    br = jnp.concatenate([b_rg, b_re.reshape(-1),
                          jnp.zeros((ROUTER_ROWS - MOE_GROUPS - MOE_EXPERTS,), F32)]).reshape(ROUTER_ROWS, 1)
    row = lambda i: (i, 0)
    col = lambda i: (0, i)
    full = lambda i: (0, 0)
    return pl.pallas_call(
        _merge_kernel,
        grid=(T // tm,),
        in_specs=[pl.BlockSpec((tm, D), row),
                  pl.BlockSpec((tm, ATTN_WIDTH), row),
                  _chunk_major_spec(tm, per_b),
                  pl.BlockSpec((tm, 2 * D), row),
                  pl.BlockSpec((None, 6, D), lambda i: (i // per_b, 0, 0)),
                  pl.BlockSpec(w_glu.shape, full),
                  pl.BlockSpec(w_pa.shape, full),
                  pl.BlockSpec(w_ps.shape, full),
                  pl.BlockSpec(w_out.shape, full),
                  pl.BlockSpec((1, D), full),
                  pl.BlockSpec((D, 2 * ROUTER_LANES), full),
                  pl.BlockSpec((ROUTER_ROWS, 1), full)],
        out_specs=[pl.BlockSpec((tm, D), row), pl.BlockSpec((tm, D), row),
                   pl.BlockSpec((8, tm), col), pl.BlockSpec((8, tm), col),
                   pl.BlockSpec((None, MOE_EXPERTS, 1), lambda i: (i, 0, 0))],
        out_shape=[jax.ShapeDtypeStruct((T, D), F32), jax.ShapeDtypeStruct((T, D), BF16),
                   jax.ShapeDtypeStruct((8, T), I32), jax.ShapeDtypeStruct((8, T), F32),
                   jax.ShapeDtypeStruct((T // tm, MOE_EXPERTS, 1), I32)],
        compiler_params=_params("parallel"),
        name="merge_route",
    )(x2, ya, ys, gs, mod, w_glu.astype(BF16), w_pa.astype(BF16), w_ps.astype(BF16), w_out.astype(BF16),
      norm2_g.reshape(1, D), jnp.concatenate([wr_hi, wr_lo], axis=1), br)


def _pack_pairs(x):
    W = x.shape[1] // 2
    lo = lax.bitcast_convert_type(x[:, :W], I32)
    hi = lax.bitcast_convert_type(x[:, W:], I32)
    return lax.shift_right_logical(lo, 16) | hi


def _unpack_pairs(w):
    lo = lax.bitcast_convert_type(lax.shift_left(w, 16), F32)
    hi = lax.bitcast_convert_type(w & jnp.int32(-65536), F32)
    return jnp.concatenate([lo.astype(BF16), hi.astype(BF16)], axis=1)


def _tile_positions(ri_ref, seg_ref):
    tm = ri_ref.shape[1]
    x_iota = lax.broadcasted_iota(I32, (MOE_EXPERTS, tm), 0)
    seg = seg_ref[...].astype(F32)
    pos = []
    for k in range(2):
        start = jnp.sum(jnp.where(x_iota == ri_ref[k:k + 1, :], seg, 0.0), axis=0, keepdims=True)
        pos.append(start + ri_ref[2 + k:3 + k, :].astype(F32))
    return pos


def _segment_copies(meta, tile, make_copy):
    chunks, tile_chunks = meta
    base = tile * TILE_CHUNKS
    n = tile_chunks[tile]

    def issue(k):
        word = chunks[base + k]
        src = (word & (2 ** CHUNK_SLOT_BITS - 1)) * SEG_ALIGN
        dst = (word >> CHUNK_SLOT_BITS) * SEG_ALIGN
        make_copy(pl.multiple_of(src, SEG_ALIGN), pl.multiple_of(dst, SEG_ALIGN))

    def group(g, carry):
        for r in range(CHUNK_ISSUE_UNROLL):
            issue(g * CHUNK_ISSUE_UNROLL + r)
        return carry

    def single(k, carry):
        issue(k)
        return carry

    n_groups = n // CHUNK_ISSUE_UNROLL
    lax.fori_loop(0, n_groups, group, 0)
    lax.fori_loop(n_groups * CHUNK_ISSUE_UNROLL, n, single, 0)


def _wait_chunks(n, wait_chunk, wait_bulk):
    has_bulk = n >= MIN_TILE_CHUNKS

    @pl.when(has_bulk)
    def _():
        wait_bulk()

    def body(c, carry):
        wait_chunk()
        return carry
    lax.fori_loop(jnp.where(has_bulk, MIN_TILE_CHUNKS, 0), n, body, 0)


def _dispatch_kernel(chunks, tile_chunks, pad_row, pad_chunks, n_used,
                     h_ref, ri_ref, rw_ref, seg_ref, xs_ref, pw_ref, zbuf, zeros_scr, sem, pad_sem, tail_sem):
    i = pl.program_id(0)
    n_tiles = pl.num_programs(0)
    slot = i % 2
    tm = h_ref.shape[0]
    pos1, pos2 = _tile_positions(ri_ref, seg_ref)
    pw_ref[...] = jnp.concatenate([pos1, pos2, rw_ref[0:2, :], jnp.zeros((4, tm), F32)], axis=0)
    r_iota = lax.broadcasted_iota(I32, (TILE_SLOTS, tm), 0).astype(F32)
    onehot = jnp.logical_or(r_iota == pos1, r_iota == pos2).astype(BF16)
    zbuf[slot] = _pack_pairs(jnp.dot(onehot, h_ref[...], preferred_element_type=F32))

    def chunk_copy(buf_slot, src, dst):
        return pltpu.make_async_copy(zbuf.at[buf_slot, pl.ds(src, SEG_ALIGN)],
                                     xs_ref.at[pl.ds(dst, SEG_ALIGN)], sem.at[buf_slot])

    _segment_copies((chunks, tile_chunks), i, lambda src, dst: chunk_copy(slot, src, dst).start())

    def wait_tile(tile, buf_slot):
        bulk = pltpu.make_async_copy(zbuf.at[buf_slot, pl.ds(0, MIN_TILE_CHUNKS * SEG_ALIGN)],
                                     xs_ref.at[pl.ds(0, MIN_TILE_CHUNKS * SEG_ALIGN)], sem.at[buf_slot])
        _wait_chunks(tile_chunks[tile], chunk_copy(buf_slot, 0, 0).wait, bulk.wait)

    @pl.when(i > 0)
    def _():
        wait_tile(i - 1, 1 - slot)

    @pl.when(i == n_tiles - 1)
    def _():
        zeros_scr[...] = jnp.zeros(zeros_scr.shape, I32)

        def pad_copy(dst):
            return pltpu.make_async_copy(zeros_scr.at[pl.ds(0, SEG_ALIGN)], xs_ref.at[pl.ds(dst, SEG_ALIGN)], pad_sem)

        def tail_copy(dst):
            return pltpu.make_async_copy(zeros_scr, xs_ref.at[pl.ds(dst, MOE_BLOCK)], tail_sem)

        def per_expert(e, total):
            def per_chunk(c, carry):
                pad_copy(pl.multiple_of(pad_row[e] + c * SEG_ALIGN, SEG_ALIGN)).start()
                return carry
            lax.fori_loop(0, pad_chunks[e], per_chunk, 0)
            return total + pad_chunks[e]

        n_pad_copies = lax.fori_loop(0, MOE_EXPERTS, per_expert, 0)
        n_blocks = xs_ref.shape[0] // MOE_BLOCK

        def tail_start(b, carry):
            tail_copy(pl.multiple_of(b * MOE_BLOCK, MOE_BLOCK)).start()
            return carry
        lax.fori_loop(n_used[0], n_blocks, tail_start, 0)
        wait_tile(i, slot)

        def wait_pad(c, carry):
            pad_copy(0).wait()
            return carry
        lax.fori_loop(0, n_pad_copies, wait_pad, 0)

        def wait_tail(b, carry):
            tail_copy(0).wait()
            return carry
        lax.fori_loop(n_used[0], n_blocks, wait_tail, 0)


def _dispatch(meta, h2, ri, rw, seg_start, n_rows):
    T, D = h2.shape
    tm = TOKEN_TILE
    col = lambda i, *_: (0, i)
    return pl.pallas_call(
        _dispatch_kernel,
        grid_spec=pltpu.PrefetchScalarGridSpec(
            num_scalar_prefetch=5,
            grid=(T // tm,),
            in_specs=[pl.BlockSpec((tm, D), lambda i, *_: (i, 0)),
                      pl.BlockSpec((8, tm), col),
                      pl.BlockSpec((8, tm), col),
                      pl.BlockSpec((None, MOE_EXPERTS, 1), lambda i, *_: (i, 0, 0))],
            out_specs=[pl.BlockSpec(memory_space=pl.ANY), pl.BlockSpec((8, tm), col)],
            scratch_shapes=[pltpu.VMEM((2, TILE_SLOTS, D // 2), I32), pltpu.VMEM((MOE_BLOCK, D // 2), I32),
                            pltpu.SemaphoreType.DMA((2,)), pltpu.SemaphoreType.DMA(()),
                            pltpu.SemaphoreType.DMA(())]),
        out_shape=[jax.ShapeDtypeStruct((n_rows, D // 2), I32), jax.ShapeDtypeStruct((8, T), F32)],
        compiler_params=_params("arbitrary"),
        name="moe_dispatch",
    )(*meta, h2, ri, rw, seg_start)


def _expert_kernel(be_ref, nb_ref, x_ref, wi_ref, wo_ref, o_ref, wi_bf, wo_bf):
    i = pl.program_id(0)
    F = wo_ref.shape[0]
    in_use = i < nb_ref[0]

    @pl.when(jnp.logical_and(in_use, jnp.logical_or(i == 0, be_ref[i] != be_ref[jnp.maximum(i - 1, 0)])))
    def _():
        wi_bf[...] = wi_ref[...].astype(BF16)
        wo_bf[...] = wo_ref[...].astype(BF16)

    @pl.when(in_use)
    def _():
        hid = jnp.dot(_unpack_pairs(x_ref[...]), wi_bf[...], preferred_element_type=F32)
        a = hid[:, :F]
        act = a * jax.nn.sigmoid(a) * hid[:, F:]
        y = jnp.dot(act.astype(BF16), wo_bf[...], preferred_element_type=F32)
        o_ref[...] = _pack_pairs(y.astype(BF16).astype(F32))

    @pl.when(jnp.logical_not(in_use))
    def _():
        o_ref[...] = jnp.zeros(o_ref.shape, I32)


def _experts(block_e, n_used, xs, w_e_in, w_e_out):
    n_rows, half_d = xs.shape
    D = 2 * half_d
    F = w_e_out.shape[1]
    blk = lambda i, be, nb: jnp.maximum(jnp.minimum(i, nb[0] - 1), 0)
    return pl.pallas_call(
        _expert_kernel,
        grid_spec=pltpu.PrefetchScalarGridSpec(
            num_scalar_prefetch=2,
            grid=(n_rows // MOE_BLOCK,),
            in_specs=[pl.BlockSpec((MOE_BLOCK, half_d), lambda i, be, nb: (blk(i, be, nb), 0)),
                      pl.BlockSpec((None, D, 2 * F), lambda i, be, nb: (be[blk(i, be, nb)], 0, 0)),
                      pl.BlockSpec((None, F, D), lambda i, be, nb: (be[blk(i, be, nb)], 0, 0))],
            out_specs=pl.BlockSpec((MOE_BLOCK, half_d), lambda i, be, nb: (i, 0)),
            scratch_shapes=[pltpu.VMEM((D, 2 * F), BF16), pltpu.VMEM((F, D), BF16)]),
        out_shape=jax.ShapeDtypeStruct((n_rows, half_d), I32),
        compiler_params=_params("arbitrary"),
        name="experts",
    )(block_e, n_used, xs, w_e_in, w_e_out)


def _final_kernel(chunks, tile_chunks, x1_ref, pw_ref, mod_ref, g_ref, ys_ref, o_ref, ybuf, sem):
    i = pl.program_id(0)
    n_tiles = pl.num_programs(0)
    slot = i % 2
    tm = x1_ref.shape[0]

    def chunk_copy(buf_slot, src, dst):
        return pltpu.make_async_copy(ys_ref.at[pl.ds(dst, SEG_ALIGN)],
                                     ybuf.at[buf_slot, pl.ds(src, SEG_ALIGN)], sem.at[buf_slot])

    def fetch(tile, buf_slot):
        _segment_copies((chunks, tile_chunks), tile, lambda src, dst: chunk_copy(buf_slot, src, dst).start())

    @pl.when(i == 0)
    def _():
        ybuf[...] = jnp.zeros(ybuf.shape, I32)
        fetch(0, 0)

    @pl.when(i + 1 < n_tiles)
    def _():
        fetch(i + 1, 1 - slot)

    bulk = pltpu.make_async_copy(ys_ref.at[pl.ds(0, MIN_TILE_CHUNKS * SEG_ALIGN)],
                                 ybuf.at[slot, pl.ds(0, MIN_TILE_CHUNKS * SEG_ALIGN)], sem.at[slot])
    _wait_chunks(tile_chunks[i], chunk_copy(slot, 0, 0).wait, bulk.wait)

    s_iota = lax.broadcasted_iota(I32, (tm, TILE_SLOTS), 1).astype(F32)
    comb = (jnp.where(s_iota == pw_ref[:, 0:1], pw_ref[:, 2:3], 0.0)
            + jnp.where(s_iota == pw_ref[:, 1:2], pw_ref[:, 3:4], 0.0))
    moe = jnp.dot(comb.astype(BF16), _unpack_pairs(ybuf[slot]), preferred_element_type=F32)
    x2 = x1_ref[...] + mod_ref[5:6, :] * moe
    o_ref[...] = _rms(x2, NORM_EPS) * g_ref[...]


def _final(meta, x1, pw_tok, mod, final_g, ys, seq):
    T, D = x1.shape
    tm = TOKEN_TILE
    per_b = seq // tm
    row = lambda i, *_: (i, 0)
    return pl.pallas_call(
        _final_kernel,
        grid_spec=pltpu.PrefetchScalarGridSpec(
            num_scalar_prefetch=2,
            grid=(T // tm,),
            in_specs=[pl.BlockSpec((tm, D), row),
                      pl.BlockSpec((tm, 8), row),
                      pl.BlockSpec((None, 6, D), lambda i, *_: (i // per_b, 0, 0)),
                      pl.BlockSpec((1, D), lambda i, *_: (0, 0)),
                      pl.BlockSpec(memory_space=pl.ANY)],
            out_specs=pl.BlockSpec((tm, D), row),
            scratch_shapes=[pltpu.VMEM((2, TILE_SLOTS, D // 2), I32), pltpu.SemaphoreType.DMA((2,))]),
        out_shape=jax.ShapeDtypeStruct((T, D), F32),
        compiler_params=_params("arbitrary"),
        name="final",
    )(*meta, x1, pw_tok, mod, final_g.reshape(1, D), ys)


def _round_up(x, m):
    return (x + m - 1) // m * m


def _moe_layout(tile_counts):
    n_tiles = tile_counts.shape[0]
    seg = _round_up(tile_counts, SEG_ALIGN)
    seg_start = jnp.cumsum(seg, axis=1) - seg
    tile_base = jnp.cumsum(seg, axis=0) - seg
    used = jnp.sum(seg, axis=0)
    region = _round_up(used, MOE_BLOCK)
    region_end = jnp.cumsum(region)
    region_start = region_end - region
    dst_row = region_start[None, :] + tile_base
    n_chunk = seg // SEG_ALIGN
    n_assign = 2 * n_tiles * TOKEN_TILE
    n_rows = _round_up(n_assign + n_tiles * MOE_EXPERTS * (SEG_ALIGN - 1) + MOE_EXPERTS * (MOE_BLOCK - 1), MOE_BLOCK)
    block_start = jnp.arange(n_rows // MOE_BLOCK, dtype=I32) * MOE_BLOCK
    block_e = jnp.minimum(jnp.sum(block_start[:, None] >= region_end[None, :], axis=1), MOE_EXPERTS - 1)
    i32 = lambda a: a.reshape(-1).astype(I32)
    chunk_end = jnp.cumsum(n_chunk, axis=1)
    chunk_start = chunk_end - n_chunk
    k = jnp.arange(TILE_CHUNKS)[None, :, None]
    mine = (chunk_start[:, None, :] <= k) & (k < chunk_end[:, None, :])
    pick = lambda a: jnp.sum(jnp.where(mine, (a // SEG_ALIGN - chunk_start)[:, None, :] + k, 0), axis=2)
    word = pick(seg_start) | (pick(dst_row) << CHUNK_SLOT_BITS)
    meta = (i32(word), i32(chunk_end[:, -1]))
    pad = (i32(region_start + used), i32((region - used) // SEG_ALIGN))
    n_used = (region_end[-1:] // MOE_BLOCK).astype(I32)
    return meta, pad, seg_start.astype(I32)[:, :, None], block_e.astype(I32), n_used, n_rows


def kernel(x, c, w_ada, b_ada, norm1_g, w_in, rel_bias, lambda_q1, lambda_k1, lambda_q2, lambda_k2, subln_g, ssm_lambda_re, ssm_lambda_im, ssm_log_step, ssm_b_re, ssm_b_im, ssm_c_re, ssm_c_im, ssm_d, w_glu, w_proj_attn, w_proj_ssm, w_out, norm2_g, w_router_group, b_router_group, w_router_expert, b_router_expert, w_expert_in, w_expert_out, final_g):
    B, S, D = x.shape
    T = B * S
    x2 = x.reshape(T, D)
    mod = _ada_mod(c, w_ada[0], b_ada[0]).reshape(B, 6, D)
    q, k, v, u, gs = _in_proj(x2, mod, norm1_g[0], w_in[0], S)
    lam_vecs = jnp.stack([lambda_q1[0], lambda_k1[0], lambda_q2[0], lambda_k2[0]]).astype(F32)
    y_attn = _diff_attn(q, k, v, rel_bias, lam_vecs, subln_g[0], B, S)
    tables = _s5_tables(ssm_lambda_re[0], ssm_lambda_im[0], ssm_log_step[0], ssm_b_re[0], ssm_b_im[0],
                        ssm_c_re[0], ssm_c_im[0], ssm_d[0])
    y_s5 = _s5_branch(u.reshape(-1, SSM_WIDTH), tables, B, S).reshape(u.shape)
    x1, h2, ri, rw, tile_counts = _merge_route(
        x2, y_attn, y_s5, gs, mod, w_glu[0], w_proj_attn[0], w_proj_ssm[0], w_out[0], norm2_g[0],
        w_router_group[0], b_router_group[0], w_router_expert[0], b_router_expert[0], S)
    meta, pad, seg_start, block_e, n_used, n_rows = _moe_layout(tile_counts[:, :, 0])
    xs, pw = _dispatch(meta + pad + (n_used,), h2, ri, rw, seg_start, n_rows)
    ys = _experts(block_e, n_used, xs, w_expert_in[0], w_expert_out[0])
    out = _final(meta, x1, pw.T, mod, final_g, ys, S)
    return out.reshape(B, S, D)
```

```python
import functools
import math

import jax
import jax.numpy as jnp
from jax import lax
from jax.experimental import pallas as pl
from jax.experimental.pallas import tpu as pltpu

F32 = jnp.float32
BF16 = jnp.bfloat16
I32 = jnp.int32
HIGHEST = lax.Precision.HIGHEST

ATTN_HEADS = 4
ATTN_HEAD_DIM = 64
ATTN_V_DIM = 2 * ATTN_HEAD_DIM
ATTN_WIDTH = ATTN_HEADS * ATTN_V_DIM
NEG_INF = -1e30
REL_BUCKETS = 32
REL_MAX_DISTANCE = 128
SSM_GROUP_CH = 16
SSM_WIDTH = 512
SSM_GROUPS = SSM_WIDTH // SSM_GROUP_CH
SSM_STATE = 64
SSM_EIG_MAX_RE = -1e-4
SSM_CHUNK = 16
S5_CHUNKS_PER_STEP = 32
MOE_GROUPS = 4
MOE_EXPERTS_PER_GROUP = 8
MOE_EXPERTS = MOE_GROUPS * MOE_EXPERTS_PER_GROUP
MOE_BLOCK = 512
SEG_ALIGN = 8
NORM_EPS = 1e-6
SUBLN_EPS = 1e-5
LAMBDA_INIT = 0.8 - 0.6 * math.exp(-0.3 * 0)

ATTN_BLOCK = 256
ATTN_HEADS_PER_STEP = 4
ATTN_ROW_CHUNK = 32
ATTN_ONES_ROWS = 16
LOG2E = math.log2(math.e)
TOKEN_TILE = 512
ROUTER_ROWS = 40
ROUTER_LANES = 128
TILE_SLOTS = -(-(2 * TOKEN_TILE + MOE_EXPERTS * (SEG_ALIGN - 1)) // 256) * 256
TILE_CHUNKS = TILE_SLOTS // SEG_ALIGN
MIN_TILE_CHUNKS = 2 * TOKEN_TILE // SEG_ALIGN
CHUNK_ISSUE_UNROLL = 8
CHUNK_SLOT_BITS = 8
assert TILE_CHUNKS <= 2 ** CHUNK_SLOT_BITS
VMEM_LIMIT = 56 << 20


def _params(*sem):
    return pltpu.CompilerParams(dimension_semantics=sem, vmem_limit_bytes=VMEM_LIMIT)


def _rms(x, eps):
    return x * lax.rsqrt(jnp.mean(x * x, axis=-1, keepdims=True) + eps)


def _mod_kernel(c_ref, w_ref, b_ref, o_ref):
    c = c_ref[...]
    c_act = c * jax.nn.sigmoid(c)
    o_ref[...] = jnp.dot(c_act, w_ref[...], preferred_element_type=F32, precision=HIGHEST) + b_ref[...]


def _ada_mod(c, w_ada, b_ada):
    B, D = c.shape
    N = w_ada.shape[1]
    tn = 1024
    return pl.pallas_call(
        _mod_kernel,
        grid=(N // tn,),
        in_specs=[pl.BlockSpec((B, D), lambda j: (0, 0)),
                  pl.BlockSpec((D, tn), lambda j: (0, j)),
                  pl.BlockSpec((1, tn), lambda j: (0, j))],
        out_specs=pl.BlockSpec((B, tn), lambda j: (0, j)),
        out_shape=jax.ShapeDtypeStruct((B, N), F32),
        compiler_params=_params("arbitrary"),
        name="ada_mod",
    )(c, w_ada, b_ada.reshape(1, N))


def _proj_kernel(x_ref, mod_ref, g_ref, w_ref, q_ref, k_ref, v_ref, u_ref, gs_ref):
    y = _rms(x_ref[...], NORM_EPS) * g_ref[...]
    h = (y * (1.0 + mod_ref[1:2, :]) + mod_ref[0:1, :]).astype(BF16)
    W = ATTN_WIDTH

    def proj(lo, hi):
        return jnp.dot(h, w_ref[:, lo:hi], preferred_element_type=F32)

    q_ref[...] = (proj(0, W) * (ATTN_HEAD_DIM ** -0.5 * LOG2E)).astype(BF16)
    k_ref[...] = proj(W, 2 * W).astype(BF16)
    v_ref[...] = proj(2 * W, 3 * W).astype(BF16)
    u_ref[...] = proj(3 * W, 3 * W + SSM_WIDTH).reshape(u_ref.shape)
    gs_ref[...] = jax.nn.sigmoid(proj(3 * W + SSM_WIDTH, w_ref.shape[1])).astype(BF16)


def _chunk_major_spec(tm, per_b):
    return pl.BlockSpec((tm // SSM_CHUNK, None, SSM_CHUNK, SSM_WIDTH), lambda i: (i % per_b, i // per_b, 0, 0))


def _in_proj(x2, mod, norm_g, w_in, seq):
    T, D = x2.shape
    tm = TOKEN_TILE
    per_b = seq // tm
    n_gate = w_in.shape[1] - 3 * ATTN_WIDTH - SSM_WIDTH
    row = lambda i: (i, 0)
    return pl.pallas_call(
        _proj_kernel,
        grid=(T // tm,),
        in_specs=[pl.BlockSpec((tm, D), row),
                  pl.BlockSpec((None, 6, D), lambda i: (i // per_b, 0, 0)),
                  pl.BlockSpec((1, D), lambda i: (0, 0)),
                  pl.BlockSpec(w_in.shape, lambda i: (0, 0))],
        out_specs=[pl.BlockSpec((tm, ATTN_WIDTH), row)] * 3
        + [_chunk_major_spec(tm, per_b), pl.BlockSpec((tm, n_gate), row)],
        out_shape=[jax.ShapeDtypeStruct((T, ATTN_WIDTH), BF16)] * 3
        + [jax.ShapeDtypeStruct((seq // SSM_CHUNK, T // seq, SSM_CHUNK, SSM_WIDTH), F32),
           jax.ShapeDtypeStruct((T, n_gate), BF16)],
        compiler_params=_params("parallel"),
        name="in_proj",
    )(x2, mod, norm_g.reshape(1, D), w_in.astype(BF16))


def _rel_bucket(dist):
    max_exact = REL_BUCKETS // 2
    n = jnp.maximum(dist, 0)
    log_ratio = jnp.log(jnp.maximum(n, 1).astype(F32) / max_exact) / math.log(REL_MAX_DISTANCE / max_exact)
    large = max_exact + (log_ratio * (REL_BUCKETS - max_exact)).astype(I32)
    large = jnp.minimum(large, REL_BUCKETS - 1)
    return jnp.where(n < max_exact, n, large)


def _attn_bias_tiles(rel_bias, blk):
    assert blk >= REL_MAX_DISTANCE
    n_heads = rel_bias.shape[1]
    far = rel_bias[REL_BUCKETS - 1].astype(F32)
    m = jnp.arange(2 * blk)
    signed = jnp.where(m < blk, m, m - 2 * blk)
    tiles = []
    for kind in range(2):
        dist = kind * blk + signed
        tab = jnp.where(dist >= 0, (rel_bias[_rel_bucket(dist)].astype(F32).T - far[:, None]) * LOG2E, NEG_INF)
        skew = jnp.tile(tab, (1, blk))[:, :blk * (2 * blk - 1)].reshape(n_heads, blk, 2 * blk - 1)
        tiles.append(skew[:, :, :blk])
    return jnp.stack(tiles, axis=1)


def _attn_kernel(lam_ref, q_ref, k_ref, v_ref, bias_ref, g_ref, o_ref, vt_scr, *bufs, blk, heads):
    i = pl.program_id(2)
    n_kv = vt_scr.shape[1]
    V = ATTN_V_DIM
    ns = 2 * heads
    acc = bufs[0:ns]
    sbuf = tuple(bufs[(1 + r) * ns:(2 + r) * ns] for r in range(3))
    pbufs = tuple(bufs[(4 + r) * ns:(5 + r) * ns] for r in range(3))

    @pl.when(i == 0)
    def _():
        for hd in range(heads):
            for jb in range(n_kv):
                vt_scr[hd, jb, 0:V, :] = v_ref[jb * blk:(jb + 1) * blk, hd * V:(hd + 1) * V].astype(F32).T.astype(BF16)
                vt_scr[hd, jb, V:, :] = jnp.ones((ATTN_ONES_ROWS, blk), BF16)

    qt_maps = []
    for hd in range(heads):
        qt = q_ref[:, hd * V:(hd + 1) * V].astype(F32).T
        feat = lax.broadcasted_iota(I32, qt.shape, 0)
        qt_maps += [jnp.where(feat < ATTN_HEAD_DIM, qt, 0.0).astype(BF16),
                    jnp.where(feat >= ATTN_HEAD_DIM, qt, 0.0).astype(BF16)]

    def scores(j, st):
        hd = st // 2
        kj = k_ref[pl.ds(pl.multiple_of(j * blk, blk), blk), hd * V:(hd + 1) * V]
        return jnp.dot(kj, qt_maps[st], preferred_element_type=F32)

    n_chunks = blk // ATTN_ROW_CHUNK

    def rows(c):
        return slice(c * ATTN_ROW_CHUNK, (c + 1) * ATTN_ROW_CHUNK)

    def fold8(x):
        return x.reshape(ATTN_ROW_CHUNK // 8, 8, blk)

    def block(j, carry, pos, lookahead, bias_kind):
        src, dst, pbuf = sbuf[pos], sbuf[(pos + 2) % 3], pbufs[pos]
        out = []
        for st in range(ns):
            hd = st // 2
            if lookahead:
                dst[st][...] = scores(j + 2, st)

            def chunk(c):
                s = src[st][rows(c), :]
                return s if bias_kind is None else s + bias_ref[hd, bias_kind, rows(c), :]

            m_old = carry[st]
            m8 = jnp.max(fold8(chunk(0)), axis=0)
            for c in range(1, n_chunks):
                m8 = jnp.maximum(m8, jnp.max(fold8(chunk(c)), axis=0))
            m_new = jnp.maximum(m_old, jnp.max(m8, axis=0, keepdims=True))
            alpha = jnp.exp2(m_old - m_new)
            for c in range(n_chunks):
                pbuf[st][rows(c), :] = jnp.exp2(chunk(c) - m_new).astype(BF16)
            acc[st][...] = alpha * acc[st][...] + jnp.dot(vt_scr[hd, j], pbuf[st][...],
                                                          preferred_element_type=F32)
            out.append(m_new)
        return tuple(out)

    def far_triple(t, carry):
        for r in range(3):
            carry = block(3 * t + r, carry, r, True, None)
        return carry

    def far_single(j, carry):
        carry = block(j, carry, 0, True, None)
        for st in range(ns):
            sbuf[0][st][...] = sbuf[1][st][...]
        for st in range(ns):
            sbuf[1][st][...] = sbuf[2][st][...]
        return carry

    def near_pair(_, carry):
        return block(i, block(i - 1, carry, 0, False, 1), 1, False, 0)

    def near_single(_, carry):
        return block(i, carry, 0, False, 0)

    for st in range(ns):
        acc[st][...] = jnp.zeros(acc[st].shape, F32)
        sbuf[0][st][...] = scores(0, st)
        sbuf[1][st][...] = scores(jnp.minimum(i, 1), st)
    m0 = jnp.full((1, blk), -jnp.inf, F32)
    n_far = jnp.maximum(i - 1, 0)
    carry = lax.fori_loop(0, n_far // 3, far_triple, (m0,) * ns)
    carry = lax.fori_loop(n_far - n_far % 3, n_far, far_single, carry)
    carry = lax.fori_loop(0, jnp.minimum(i, 1), near_pair, carry)
    lax.fori_loop(0, 1 - jnp.minimum(i, 1), near_single, carry)

    lam = (jnp.exp(jnp.sum(lam_ref[0:1, :] * lam_ref[1:2, :], axis=-1, keepdims=True))
           - jnp.exp(jnp.sum(lam_ref[2:3, :] * lam_ref[3:4, :], axis=-1, keepdims=True)) + LAMBDA_INIT)
    for hd in range(heads):
        a1, a2 = acc[2 * hd], acc[2 * hd + 1]
        ot = a1[0:V, :] / a1[V:V + 1, :] - lam * (a2[0:V, :] / a2[V:V + 1, :])
        ot = ot * lax.rsqrt(jnp.mean(ot * ot, axis=0, keepdims=True) + SUBLN_EPS)
        o_ref[:, hd * V:(hd + 1) * V] = (ot.T * (g_ref[...] * (1.0 - LAMBDA_INIT))).astype(BF16)


def _diff_attn(q, k, v, rel_bias, lam_vecs, subln_g, batch, seq):
    T = q.shape[0]
    blk = ATTN_BLOCK
    nq = seq // blk
    bias = _attn_bias_tiles(rel_bias, blk)
    hp = ATTN_HEADS_PER_STEP
    ns = 2 * hp
    width = hp * ATTN_V_DIM
    acc_rows = ATTN_V_DIM + ATTN_ONES_ROWS
    return pl.pallas_call(
        functools.partial(_attn_kernel, blk=blk, heads=hp),
        grid=(batch, ATTN_HEADS // hp, nq),
        in_specs=[pl.BlockSpec((4, ATTN_HEAD_DIM), lambda b, h, i: (0, 0)),
                  pl.BlockSpec((blk, width), lambda b, h, i: (b * nq + i, h)),
                  pl.BlockSpec((seq, width), lambda b, h, i: (b, h)),
                  pl.BlockSpec((seq, width), lambda b, h, i: (b, h)),
                  pl.BlockSpec((hp, 2, blk, blk), lambda b, h, i: (h, 0, 0, 0)),
                  pl.BlockSpec((1, ATTN_V_DIM), lambda b, h, i: (0, 0))],
        out_specs=pl.BlockSpec((blk, width), lambda b, h, i: (b * nq + i, h)),
        out_shape=jax.ShapeDtypeStruct((T, ATTN_WIDTH), BF16),
        scratch_shapes=[pltpu.VMEM((hp, nq, acc_rows, blk), BF16)]
        + [pltpu.VMEM((acc_rows, blk), F32)] * ns + [pltpu.VMEM((blk, blk), F32)] * (3 * ns)
        + [pltpu.VMEM((blk, blk), BF16)] * (3 * ns),
        compiler_params=_params("parallel", "parallel", "arbitrary"),
        name="diff_attn",
    )(lam_vecs, q, k, v, bias, subln_g.reshape(1, ATTN_V_DIM))


def _s5_tables(lam_re, lam_im, log_step, b_re, b_im, c_re, c_im, d_skip):
    L = SSM_CHUNK
    G, P = lam_re.shape
    H = SSM_GROUP_CH
    lr = jnp.minimum(lam_re.astype(F32), SSM_EIG_MAX_RE)
    li = lam_im.astype(F32)
    step = jnp.exp(log_step.astype(F32))[:, None]
    mag = jnp.exp(lr * step)
    ang = li * step
    a_re = mag * jnp.cos(ang)
    a_im = mag * jnp.sin(ang)
    den = lr * lr + li * li
    num_re = a_re - 1.0
    coef_re = (num_re * lr + a_im * li) / den
    coef_im = (a_im * lr - num_re * li) / den
    br = b_re.astype(F32)
    bi = b_im.astype(F32)
    bb_re = coef_re[..., None] * br - coef_im[..., None] * bi
    bb_im = coef_re[..., None] * bi + coef_im[..., None] * br
    pw_re, pw_im = [jnp.ones_like(a_re)], [jnp.zeros_like(a_re)]
    for _ in range(L):
        pr, pi = pw_re[-1], pw_im[-1]
        pw_re.append(pr * a_re - pi * a_im)
        pw_im.append(pr * a_im + pi * a_re)
    pw_re = jnp.stack(pw_re)
    pw_im = jnp.stack(pw_im)
    cr = c_re.astype(F32)[None]
    ci = c_im.astype(F32)[None]
    cp_re = cr * pw_re[:, :, None, :] - ci * pw_im[:, :, None, :]
    cp_im = cr * pw_im[:, :, None, :] + ci * pw_re[:, :, None, :]
    kern = (jnp.einsum('tghp,gpk->tghk', cp_re[:L], bb_re, precision=HIGHEST)
            - jnp.einsum('tghp,gpk->tghk', cp_im[:L], bb_im, precision=HIGHEST))
    steps = jnp.arange(L)
    place = (steps[None, :, None] - steps[:, None, None] == steps[None, None, :]).astype(F32)
    m_tab = jnp.einsum('stu,ughk->gskth', place, kern, precision=HIGHEST).reshape(G, L * H, L * H)
    rev_re = pw_re[L - 1::-1][:, :, None, :]
    rev_im = pw_im[L - 1::-1][:, :, None, :]
    bbt_re = jnp.transpose(bb_re, (0, 2, 1))[None]
    bbt_im = jnp.transpose(bb_im, (0, 2, 1))[None]
    bst_re = jnp.transpose(rev_re * bbt_re - rev_im * bbt_im, (1, 0, 2, 3)).reshape(G, L * H, P)
    bst_im = jnp.transpose(rev_re * bbt_im + rev_im * bbt_re, (1, 0, 2, 3)).reshape(G, L * H, P)
    cst_re = jnp.transpose(cp_re[1:], (1, 3, 0, 2)).reshape(G, P, L * H)
    cst_im = -jnp.transpose(cp_im[1:], (1, 3, 0, 2)).reshape(G, P, L * H)
    a_chunk = jnp.stack([jnp.concatenate([pw_re[L], pw_re[L]], axis=-1),
                         jnp.concatenate([-pw_im[L], pw_im[L]], axis=-1)], axis=1)
    d_tab = jnp.tile(d_skip.astype(F32), (1, L)).reshape(G, 1, L * H)
    bst = jnp.concatenate([bst_re, bst_im], axis=-1)
    cst = jnp.concatenate([cst_re, cst_im], axis=1)
    return m_tab.astype(BF16), bst.astype(BF16), cst.astype(BF16), a_chunk, d_tab


def _gelu_tanh(x):
    return 0.5 * x * (1.0 + jnp.tanh(math.sqrt(2.0 / math.pi) * (x + 0.044715 * (x * x * x))))


def _lane_block_transpose(arrs):
    n = len(arrs)
    width = arrs[0].shape[1]
    blk_id = lax.broadcasted_iota(I32, arrs[0].shape, 1) // SSM_GROUP_CH
    k = n // 2
    while k >= 1:
        keep = (blk_id & k) == 0
        nxt = list(arrs)
        for r in range(n):
            if r & k == 0:
                a, b = arrs[r], arrs[r + k]
                nxt[r] = jnp.where(keep, a, pltpu.roll(b, k * SSM_GROUP_CH, axis=1))
                nxt[r + k] = jnp.where(keep, pltpu.roll(a, width - k * SSM_GROUP_CH, axis=1), b)
        arrs = nxt
        k //= 2
    return arrs


def _s5_kernel(u_ref, m_ref, bst_ref, cst_ref, a_ref, d_ref, o_ref, us_scr, z_scr, y_scr, st_scr, *, batch):
    L, H = SSM_CHUNK, SSM_GROUP_CH
    n_grp = us_scr.shape[0]
    R = us_scr.shape[1]
    half = 128 // H

    @pl.when(pl.program_id(1) == 0)
    def _():
        st_scr[...] = jnp.zeros(st_scr.shape, F32)

    for hh in range(L // half):
        slabs = [pltpu.bitcast(u_ref[pl.ds(hh * half + s, R, stride=L), :].astype(BF16), I32) for s in range(half)]
        for gi, arr in enumerate(_lane_block_transpose(slabs)):
            us_scr[gi, :, hh * 128:(hh + 1) * 128] = pltpu.bitcast(arr, BF16)
    for gi in range(n_grp):
        u = us_scr[gi]
        z_scr[gi] = jnp.dot(u, bst_ref[gi], preferred_element_type=F32)
        y_scr[gi] = jnp.dot(u, m_ref[gi], preferred_element_type=F32) + u.astype(F32) * d_ref[gi]

    def step(c, state):
        sl = pl.ds(pl.multiple_of(c * batch, batch), batch)
        out = []
        for gi in range(n_grp):
            x = state[gi]
            z = z_scr[gi, sl, :]
            z_scr[gi, sl, :] = x
            out.append(a_ref[gi, 0:1, :] * x + a_ref[gi, 1:2, :] * pltpu.roll(x, SSM_STATE, axis=1) + z)
        return tuple(out)

    state = lax.fori_loop(0, R // batch, step, tuple(st_scr[gi] for gi in range(n_grp)))
    for gi in range(n_grp):
        st_scr[gi] = state[gi]

    for gi in range(n_grp):
        y = y_scr[gi] + jnp.dot(z_scr[gi].astype(BF16), cst_ref[gi], preferred_element_type=F32)
        y_scr[gi] = _gelu_tanh(y)
    for hh in range(L // half):
        cols = [pltpu.bitcast(y_scr[gi, :, hh * 128:(hh + 1) * 128].astype(BF16), I32) for gi in range(n_grp)]
        for s, arr in enumerate(_lane_block_transpose(cols)):
            o_ref[pl.ds(hh * half + s, R, stride=L), :] = pltpu.bitcast(arr, BF16).astype(F32)


def _s5_branch(u, tables, batch, seq):
    L, G, H, P = SSM_CHUNK, SSM_GROUPS, SSM_GROUP_CH, SSM_STATE
    n_chunks = seq // L
    gpt = 128 // H
    cr = S5_CHUNKS_PER_STEP
    R = cr * batch
    LH = L * H
    m_tab, bst, cst, a_chunk, d_tab = tables
    tile = lambda o, c: (o, 0, 0)
    return pl.pallas_call(
        functools.partial(_s5_kernel, batch=batch),
        grid=(G // gpt, n_chunks // cr),
        in_specs=[pl.BlockSpec((R * L, 128), lambda o, c: (c, o)),
                  pl.BlockSpec((gpt, LH, LH), tile),
                  pl.BlockSpec((gpt, LH, 2 * P), tile),
                  pl.BlockSpec((gpt, 2 * P, LH), tile),
                  pl.BlockSpec((gpt, 2, 2 * P), tile),
                  pl.BlockSpec((gpt, 1, LH), tile)],
        out_specs=pl.BlockSpec((R * L, 128), lambda o, c: (c, o)),
        out_shape=jax.ShapeDtypeStruct(u.shape, F32),
        scratch_shapes=[pltpu.VMEM((gpt, R, LH), BF16), pltpu.VMEM((gpt, R, 2 * P), F32),
                        pltpu.VMEM((gpt, R, LH), F32), pltpu.VMEM((gpt, batch, 2 * P), F32)],
        compiler_params=_params("parallel", "arbitrary"),
        name="s5",
    )(u, m_tab, bst, cst, a_chunk, d_tab)


def _merge_kernel(x_ref, ya_ref, ys_ref, gs_ref, mod_ref, wglu_ref, pa_ref, ps_ref, wout_ref, g2_ref,
                  wr_ref, br_ref, x1_ref, h2_ref, ri_ref, rw_ref, cnt_ref):
    tm, D = x_ref.shape
    ys = ys_ref[...].reshape(tm, SSM_WIDTH).astype(BF16)
    gl = jnp.dot(ys, wglu_ref[...], preferred_element_type=F32)
    y_ssm = gl[:, :SSM_WIDTH] * jax.nn.sigmoid(gl[:, SSM_WIDTH:])
    p_attn = jnp.dot(ya_ref[...], pa_ref[...], preferred_element_type=F32)
    p_ssm = jnp.dot(y_ssm.astype(BF16), ps_ref[...], preferred_element_type=F32)
    merged = gs_ref[:, :D].astype(F32) * p_attn + gs_ref[:, D:].astype(F32) * p_ssm
    mixed = jnp.dot(merged.astype(BF16), wout_ref[...], preferred_element_type=F32)
    x1 = x_ref[...] + mod_ref[2:3, :] * mixed
    x1_ref[...] = x1
    h2 = _rms(x1, NORM_EPS) * g2_ref[...] * (1.0 + mod_ref[4:5, :]) + mod_ref[3:4, :]
    h2_hi = h2.astype(BF16)
    h2_ref[...] = h2_hi

    h2_lo = (h2 - h2_hi.astype(F32)).astype(BF16)
    hi_prod = jnp.dot(h2_hi, wr_ref[...], preferred_element_type=F32)
    lo_prod = jnp.dot(h2_lo, wr_ref[:, 0:ROUTER_LANES], preferred_element_type=F32)
    lg_tok = hi_prod[:, 0:ROUTER_LANES] + hi_prod[:, ROUTER_LANES:] + lo_prod
    logits = lg_tok.T[0:ROUTER_ROWS, :] + br_ref[...]
    NG, EPG = MOE_GROUPS, MOE_EXPERTS_PER_GROUP
    lg = logits[0:NG, :]
    g_iota = lax.broadcasted_iota(I32, lg.shape, 0)
    lg_max = jnp.max(lg, axis=0, keepdims=True)
    grp = jnp.min(jnp.where(lg == lg_max, g_iota, NG), axis=0, keepdims=True)
    p_grp = 1.0 / jnp.sum(jnp.exp(lg - lg_max), axis=0, keepdims=True)
    le = logits[NG:NG + EPG, :]
    for g in range(1, NG):
        le = jnp.where(grp == g, logits[NG + g * EPG:NG + (g + 1) * EPG, :], le)
    e_iota = lax.broadcasted_iota(I32, le.shape, 0)
    v1 = jnp.max(le, axis=0, keepdims=True)
    i1 = jnp.min(jnp.where(le == v1, e_iota, EPG), axis=0, keepdims=True)
    le2 = jnp.where(e_iota == i1, -jnp.inf, le)
    v2 = jnp.max(le2, axis=0, keepdims=True)
    i2 = jnp.min(jnp.where(le2 == v2, e_iota, EPG), axis=0, keepdims=True)
    e21 = jnp.exp(v2 - v1)
    w1 = p_grp / (1.0 + e21)
    w2 = p_grp * e21 / (1.0 + e21)
    eid1 = grp * EPG + i1
    eid2 = grp * EPG + i2

    x_iota = lax.broadcasted_iota(I32, (MOE_EXPERTS, tm), 0)
    hot1 = x_iota == eid1
    hot2 = x_iota == eid2
    hot = jnp.logical_or(hot1, hot2).astype(F32)
    before = (lax.broadcasted_iota(I32, (tm, tm), 0) < lax.broadcasted_iota(I32, (tm, tm), 1))
    prior = jnp.dot(hot.astype(BF16), before.astype(BF16), preferred_element_type=F32)
    rank1 = jnp.sum(jnp.where(hot1, prior, 0.0), axis=0, keepdims=True)
    rank2 = jnp.sum(jnp.where(hot2, prior, 0.0), axis=0, keepdims=True)
    cnt_ref[...] = jnp.sum(hot, axis=1, keepdims=True).astype(I32)

    zi = jnp.zeros((4, tm), I32)
    ri_ref[...] = jnp.concatenate([eid1, eid2, rank1.astype(I32), rank2.astype(I32), zi], axis=0)
    rw_ref[...] = jnp.concatenate([w1, w2, jnp.zeros((6, tm), F32)], axis=0)


def _merge_route(x2, ya, ys, gs, mod, w_glu, w_pa, w_ps, w_out, norm2_g, w_rg, b_rg, w_re, b_re, seq):
    T, D = x2.shape
    tm = TOKEN_TILE
    per_b = seq // tm
    wr = jnp.concatenate([w_rg, jnp.transpose(w_re, (1, 0, 2)).reshape(D, MOE_EXPERTS),
                          jnp.zeros((D, ROUTER_LANES - MOE_GROUPS - MOE_EXPERTS), F32)], axis=1).astype(F32)
    wr_hi = wr.astype(BF16)
    wr_lo = (wr - wr_hi.astype(F32)).astype(BF16)
    br = jnp.concatenate([b_rg, b_re.reshape(-1),
                          jnp.zeros((ROUTER_ROWS - MOE_GROUPS - MOE_EXPERTS,), F32)]).reshape(ROUTER_ROWS, 1)
    row = lambda i: (i, 0)
    col = lambda i: (0, i)
    full = lambda i: (0, 0)
    return pl.pallas_call(
        _merge_kernel,
        grid=(T // tm,),
        in_specs=[pl.BlockSpec((tm, D), row),
                  pl.BlockSpec((tm, ATTN_WIDTH), row),
                  _chunk_major_spec(tm, per_b),
                  pl.BlockSpec((tm, 2 * D), row),
                  pl.BlockSpec((None, 6, D), lambda i: (i // per_b, 0, 0)),
                  pl.BlockSpec(w_glu.shape, full),
                  pl.BlockSpec(w_pa.shape, full),
                  pl.BlockSpec(w_ps.shape, full),
                  pl.BlockSpec(w_out.shape, full),
                  pl.BlockSpec((1, D), full),
                  pl.BlockSpec((D, 2 * ROUTER_LANES), full),
                  pl.BlockSpec((ROUTER_ROWS, 1), full)],
        out_specs=[pl.BlockSpec((tm, D), row), pl.BlockSpec((tm, D), row),
                   pl.BlockSpec((8, tm), col), pl.BlockSpec((8, tm), col),
                   pl.BlockSpec((None, MOE_EXPERTS, 1), lambda i: (i, 0, 0))],
        out_shape=[jax.ShapeDtypeStruct((T, D), F32), jax.ShapeDtypeStruct((T, D), BF16),
                   jax.ShapeDtypeStruct((8, T), I32), jax.ShapeDtypeStruct((8, T), F32),
                   jax.ShapeDtypeStruct((T // tm, MOE_EXPERTS, 1), I32)],
        compiler_params=_params("parallel"),
        name="merge_route",
    )(x2, ya, ys, gs, mod, w_glu.astype(BF16), w_pa.astype(BF16), w_ps.astype(BF16), w_out.astype(BF16),
      norm2_g.reshape(1, D), jnp.concatenate([wr_hi, wr_lo], axis=1), br)


def _pack_pairs(x):
    W = x.shape[1] // 2
    lo = lax.bitcast_convert_type(x[:, :W], I32)
    hi = lax.bitcast_convert_type(x[:, W:], I32)
    return lax.shift_right_logical(lo, 16) | hi


def _unpack_pairs(w):
    lo = lax.bitcast_convert_type(lax.shift_left(w, 16), F32)
    hi = lax.bitcast_convert_type(w & jnp.int32(-65536), F32)
    return jnp.concatenate([lo.astype(BF16), hi.astype(BF16)], axis=1)


def _tile_positions(ri_ref, seg_ref):
    tm = ri_ref.shape[1]
    x_iota = lax.broadcasted_iota(I32, (MOE_EXPERTS, tm), 0)
    seg = seg_ref[...].astype(F32)
    pos = []
    for k in range(2):
        start = jnp.sum(jnp.where(x_iota == ri_ref[k:k + 1, :], seg, 0.0), axis=0, keepdims=True)
        pos.append(start + ri_ref[2 + k:3 + k, :].astype(F32))
    return pos


def _segment_copies(meta, tile, make_copy):
    chunks, tile_chunks = meta
    base = tile * TILE_CHUNKS
    n = tile_chunks[tile]

    def issue(k):
        word = chunks[base + k]
        src = (word & (2 ** CHUNK_SLOT_BITS - 1)) * SEG_ALIGN
        dst = (word >> CHUNK_SLOT_BITS) * SEG_ALIGN
        make_copy(pl.multiple_of(src, SEG_ALIGN), pl.multiple_of(dst, SEG_ALIGN))

    def group(g, carry):
        for r in range(CHUNK_ISSUE_UNROLL):
            issue(g * CHUNK_ISSUE_UNROLL + r)
        return carry

    def single(k, carry):
        issue(k)
        return carry

    n_groups = n // CHUNK_ISSUE_UNROLL
    lax.fori_loop(0, n_groups, group, 0)
    lax.fori_loop(n_groups * CHUNK_ISSUE_UNROLL, n, single, 0)


def _wait_chunks(n, wait_chunk, wait_bulk):
    has_bulk = n >= MIN_TILE_CHUNKS

    @pl.when(has_bulk)
    def _():
        wait_bulk()

    def body(c, carry):
        wait_chunk()
        return carry
    lax.fori_loop(jnp.where(has_bulk, MIN_TILE_CHUNKS, 0), n, body, 0)


def _dispatch_kernel(chunks, tile_chunks, pad_row, pad_chunks, n_used,
                     h_ref, ri_ref, rw_ref, seg_ref, xs_ref, pw_ref, zbuf, zeros_scr, sem, pad_sem, tail_sem):
    i = pl.program_id(0)
    n_tiles = pl.num_programs(0)
    slot = i % 2
    tm = h_ref.shape[0]
    pos1, pos2 = _tile_positions(ri_ref, seg_ref)
    pw_ref[...] = jnp.concatenate([pos1, pos2, rw_ref[0:2, :], jnp.zeros((4, tm), F32)], axis=0)
    r_iota = lax.broadcasted_iota(I32, (TILE_SLOTS, tm), 0).astype(F32)
    onehot = jnp.logical_or(r_iota == pos1, r_iota == pos2).astype(BF16)
    zbuf[slot] = _pack_pairs(jnp.dot(onehot, h_ref[...], preferred_element_type=F32))

    def chunk_copy(buf_slot, src, dst):
        return pltpu.make_async_copy(zbuf.at[buf_slot, pl.ds(src, SEG_ALIGN)],
                                     xs_ref.at[pl.ds(dst, SEG_ALIGN)], sem.at[buf_slot])

    _segment_copies((chunks, tile_chunks), i, lambda src, dst: chunk_copy(slot, src, dst).start())

    def wait_tile(tile, buf_slot):
        bulk = pltpu.make_async_copy(zbuf.at[buf_slot, pl.ds(0, MIN_TILE_CHUNKS * SEG_ALIGN)],
                                     xs_ref.at[pl.ds(0, MIN_TILE_CHUNKS * SEG_ALIGN)], sem.at[buf_slot])
        _wait_chunks(tile_chunks[tile], chunk_copy(buf_slot, 0, 0).wait, bulk.wait)

    @pl.when(i > 0)
    def _():
        wait_tile(i - 1, 1 - slot)

    @pl.when(i == n_tiles - 1)
    def _():
        zeros_scr[...] = jnp.zeros(zeros_scr.shape, I32)

        def pad_copy(dst):
            return pltpu.make_async_copy(zeros_scr.at[pl.ds(0, SEG_ALIGN)], xs_ref.at[pl.ds(dst, SEG_ALIGN)], pad_sem)

        def tail_copy(dst):
            return pltpu.make_async_copy(zeros_scr, xs_ref.at[pl.ds(dst, MOE_BLOCK)], tail_sem)

        def per_expert(e, total):
            def per_chunk(c, carry):
                pad_copy(pl.multiple_of(pad_row[e] + c * SEG_ALIGN, SEG_ALIGN)).start()
                return carry
            lax.fori_loop(0, pad_chunks[e], per_chunk, 0)
            return total + pad_chunks[e]

        n_pad_copies = lax.fori_loop(0, MOE_EXPERTS, per_expert, 0)
        n_blocks = xs_ref.shape[0] // MOE_BLOCK

        def tail_start(b, carry):
            tail_copy(pl.multiple_of(b * MOE_BLOCK, MOE_BLOCK)).start()
            return carry
        lax.fori_loop(n_used[0], n_blocks, tail_start, 0)
        wait_tile(i, slot)

        def wait_pad(c, carry):
            pad_copy(0).wait()
            return carry
        lax.fori_loop(0, n_pad_copies, wait_pad, 0)

        def wait_tail(b, carry):
            tail_copy(0).wait()
            return carry
        lax.fori_loop(n_used[0], n_blocks, wait_tail, 0)


def _dispatch(meta, h2, ri, rw, seg_start, n_rows):
    T, D = h2.shape
    tm = TOKEN_TILE
    col = lambda i, *_: (0, i)
    return pl.pallas_call(
        _dispatch_kernel,
        grid_spec=pltpu.PrefetchScalarGridSpec(
            num_scalar_prefetch=5,
            grid=(T // tm,),
            in_specs=[pl.BlockSpec((tm, D), lambda i, *_: (i, 0)),
                      pl.BlockSpec((8, tm), col),
                      pl.BlockSpec((8, tm), col),
                      pl.BlockSpec((None, MOE_EXPERTS, 1), lambda i, *_: (i, 0, 0))],
            out_specs=[pl.BlockSpec(memory_space=pl.ANY), pl.BlockSpec((8, tm), col)],
            scratch_shapes=[pltpu.VMEM((2, TILE_SLOTS, D // 2), I32), pltpu.VMEM((MOE_BLOCK, D // 2), I32),
                            pltpu.SemaphoreType.DMA((2,)), pltpu.SemaphoreType.DMA(()),
                            pltpu.SemaphoreType.DMA(())]),
        out_shape=[jax.ShapeDtypeStruct((n_rows, D // 2), I32), jax.ShapeDtypeStruct((8, T), F32)],
        compiler_params=_params("arbitrary"),
        name="moe_dispatch",
    )(*meta, h2, ri, rw, seg_start)


def _expert_kernel(be_ref, nb_ref, x_ref, wi_ref, wo_ref, o_ref, wi_bf, wo_bf):
    i = pl.program_id(0)
    F = wo_ref.shape[0]
    in_use = i < nb_ref[0]

    @pl.when(jnp.logical_and(in_use, jnp.logical_or(i == 0, be_ref[i] != be_ref[jnp.maximum(i - 1, 0)])))
    def _():
        wi_bf[...] = wi_ref[...].astype(BF16)
        wo_bf[...] = wo_ref[...].astype(BF16)

    @pl.when(in_use)
    def _():
        hid = jnp.dot(_unpack_pairs(x_ref[...]), wi_bf[...], preferred_element_type=F32)
        a = hid[:, :F]
        act = a * jax.nn.sigmoid(a) * hid[:, F:]
        y = jnp.dot(act.astype(BF16), wo_bf[...], preferred_element_type=F32)
        o_ref[...] = _pack_pairs(y.astype(BF16).astype(F32))

    @pl.when(jnp.logical_not(in_use))
    def _():
        o_ref[...] = jnp.zeros(o_ref.shape, I32)


def _experts(block_e, n_used, xs, w_e_in, w_e_out):
    n_rows, half_d = xs.shape
    D = 2 * half_d
    F = w_e_out.shape[1]
    blk = lambda i, be, nb: jnp.maximum(jnp.minimum(i, nb[0] - 1), 0)
    return pl.pallas_call(
        _expert_kernel,
        grid_spec=pltpu.PrefetchScalarGridSpec(
            num_scalar_prefetch=2,
            grid=(n_rows // MOE_BLOCK,),
            in_specs=[pl.BlockSpec((MOE_BLOCK, half_d), lambda i, be, nb: (blk(i, be, nb), 0)),
                      pl.BlockSpec((None, D, 2 * F), lambda i, be, nb: (be[blk(i, be, nb)], 0, 0)),
                      pl.BlockSpec((None, F, D), lambda i, be, nb: (be[blk(i, be, nb)], 0, 0))],
            out_specs=pl.BlockSpec((MOE_BLOCK, half_d), lambda i, be, nb: (i, 0)),
            scratch_shapes=[pltpu.VMEM((D, 2 * F), BF16), pltpu.VMEM((F, D), BF16)]),
        out_shape=jax.ShapeDtypeStruct((n_rows, half_d), I32),
        compiler_params=_params("arbitrary"),
        name="experts",
    )(block_e, n_used, xs, w_e_in, w_e_out)


def _final_kernel(chunks, tile_chunks, x1_ref, pw_ref, mod_ref, g_ref, ys_ref, o_ref, ybuf, sem):
    i = pl.program_id(0)
    n_tiles = pl.num_programs(0)
    slot = i % 2
    tm = x1_ref.shape[0]

    def chunk_copy(buf_slot, src, dst):
        return pltpu.make_async_copy(ys_ref.at[pl.ds(dst, SEG_ALIGN)],
                                     ybuf.at[buf_slot, pl.ds(src, SEG_ALIGN)], sem.at[buf_slot])

    def fetch(tile, buf_slot):
        _segment_copies((chunks, tile_chunks), tile, lambda src, dst: chunk_copy(buf_slot, src, dst).start())

    @pl.when(i == 0)
    def _():
        ybuf[...] = jnp.zeros(ybuf.shape, I32)
        fetch(0, 0)

    @pl.when(i + 1 < n_tiles)
    def _():
        fetch(i + 1, 1 - slot)

    bulk = pltpu.make_async_copy(ys_ref.at[pl.ds(0, MIN_TILE_CHUNKS * SEG_ALIGN)],
                                 ybuf.at[slot, pl.ds(0, MIN_TILE_CHUNKS * SEG_ALIGN)], sem.at[slot])
    _wait_chunks(tile_chunks[i], chunk_copy(slot, 0, 0).wait, bulk.wait)

    s_iota = lax.broadcasted_iota(I32, (tm, TILE_SLOTS), 1).astype(F32)
    comb = (jnp.where(s_iota == pw_ref[:, 0:1], pw_ref[:, 2:3], 0.0)
            + jnp.where(s_iota == pw_ref[:, 1:2], pw_ref[:, 3:4], 0.0))
    moe = jnp.dot(comb.astype(BF16), _unpack_pairs(ybuf[slot]), preferred_element_type=F32)
    x2 = x1_ref[...] + mod_ref[5:6, :] * moe
    o_ref[...] = _rms(x2, NORM_EPS) * g_ref[...]


def _final(meta, x1, pw_tok, mod, final_g, ys, seq):
    T, D = x1.shape
    tm = TOKEN_TILE
    per_b = seq // tm
    row = lambda i, *_: (i, 0)
    return pl.pallas_call(
        _final_kernel,
        grid_spec=pltpu.PrefetchScalarGridSpec(
            num_scalar_prefetch=2,
            grid=(T // tm,),
            in_specs=[pl.BlockSpec((tm, D), row),
                      pl.BlockSpec((tm, 8), row),
                      pl.BlockSpec((None, 6, D), lambda i, *_: (i // per_b, 0, 0)),
                      pl.BlockSpec((1, D), lambda i, *_: (0, 0)),
                      pl.BlockSpec(memory_space=pl.ANY)],
            out_specs=pl.BlockSpec((tm, D), row),
            scratch_shapes=[pltpu.VMEM((2, TILE_SLOTS, D // 2), I32), pltpu.SemaphoreType.DMA((2,))]),
        out_shape=jax.ShapeDtypeStruct((T, D), F32),
        compiler_params=_params("arbitrary"),
        name="final",
    )(*meta, x1, pw_tok, mod, final_g.reshape(1, D), ys)


def _round_up(x, m):
    return (x + m - 1) // m * m


def _moe_layout(tile_counts):
    n_tiles = tile_counts.shape[0]
    seg = _round_up(tile_counts, SEG_ALIGN)
    seg_start = jnp.cumsum(seg, axis=1) - seg
    tile_base = jnp.cumsum(seg, axis=0) - seg
    used = jnp.sum(seg, axis=0)
    region = _round_up(used, MOE_BLOCK)
    region_end = jnp.cumsum(region)
    region_start = region_end - region
    dst_row = region_start[None, :] + tile_base
    n_chunk = seg // SEG_ALIGN
    n_assign = 2 * n_tiles * TOKEN_TILE
    n_rows = _round_up(n_assign + n_tiles * MOE_EXPERTS * (SEG_ALIGN - 1) + MOE_EXPERTS * (MOE_BLOCK - 1), MOE_BLOCK)
    block_start = jnp.arange(n_rows // MOE_BLOCK, dtype=I32) * MOE_BLOCK
    block_e = jnp.minimum(jnp.sum(block_start[:, None] >= region_end[None, :], axis=1), MOE_EXPERTS - 1)
    i32 = lambda a: a.reshape(-1).astype(I32)
    chunk_end = jnp.cumsum(n_chunk, axis=1)
    chunk_start = chunk_end - n_chunk
    k = jnp.arange(TILE_CHUNKS)[None, :, None]
    mine = (chunk_start[:, None, :] <= k) & (k < chunk_end[:, None, :])
    pick = lambda a: jnp.sum(jnp.where(mine, (a // SEG_ALIGN - chunk_start)[:, None, :] + k, 0), axis=2)
    word = pick(seg_start) | (pick(dst_row) << CHUNK_SLOT_BITS)
    meta = (i32(word), i32(chunk_end[:, -1]))
    pad = (i32(region_start + used), i32((region - used) // SEG_ALIGN))
    n_used = (region_end[-1:] // MOE_BLOCK).astype(I32)
    return meta, pad, seg_start.astype(I32)[:, :, None], block_e.astype(I32), n_used, n_rows


def kernel(x, c, w_ada, b_ada, norm1_g, w_in, rel_bias, lambda_q1, lambda_k1, lambda_q2, lambda_k2, subln_g, ssm_lambda_re, ssm_lambda_im, ssm_log_step, ssm_b_re, ssm_b_im, ssm_c_re, ssm_c_im, ssm_d, w_glu, w_proj_attn, w_proj_ssm, w_out, norm2_g, w_router_group, b_router_group, w_router_expert, b_router_expert, w_expert_in, w_expert_out, final_g):
    B, S, D = x.shape
    T = B * S
    x2 = x.reshape(T, D)
    mod = _ada_mod(c, w_ada[0], b_ada[0]).reshape(B, 6, D)
    q, k, v, u, gs = _in_proj(x2, mod, norm1_g[0], w_in[0], S)
    lam_vecs = jnp.stack([lambda_q1[0], lambda_k1[0], lambda_q2[0], lambda_k2[0]]).astype(F32)
    y_attn = _diff_attn(q, k, v, rel_bias, lam_vecs, subln_g[0], B, S)
    tables = _s5_tables(ssm_lambda_re[0], ssm_lambda_im[0], ssm_log_step[0], ssm_b_re[0], ssm_b_im[0],
                        ssm_c_re[0], ssm_c_im[0], ssm_d[0])
    y_s5 = _s5_branch(u.reshape(-1, SSM_WIDTH), tables, B, S).reshape(u.shape)
    x1, h2, ri, rw, tile_counts = _merge_route(
        x2, y_attn, y_s5, gs, mod, w_glu[0], w_proj_attn[0], w_proj_ssm[0], w_out[0], norm2_g[0],
        w_router_group[0], b_router_group[0], w_router_expert[0], b_router_expert[0], S)
    meta, pad, seg_start, block_e, n_used, n_rows = _moe_layout(tile_counts[:, :, 0])
    xs, pw = _dispatch(meta + pad + (n_used,), h2, ri, rw, seg_start, n_rows)
    ys = _experts(block_e, n_used, xs, w_expert_in[0], w_expert_out[0])
    out = _final(meta, x1, pw.T, mod, final_g, ys, S)
    return out.reshape(B, S, D)
```

```python
import functools
import math

import jax
import jax.numpy as jnp
from jax import lax
from jax.experimental import pallas as pl
from jax.experimental.pallas import tpu as pltpu

F32 = jnp.float32
BF16 = jnp.bfloat16
I32 = jnp.int32
HIGHEST = lax.Precision.HIGHEST

LANES = 128
MXU_TILE = 256

ATTN_HEADS = 4
ATTN_HEAD_DIM = 64
ATTN_V_DIM = 2 * ATTN_HEAD_DIM
ATTN_WIDTH = ATTN_HEADS * ATTN_V_DIM
NEG_INF = -1e30
REL_BUCKETS = 32
REL_MAX_DISTANCE = 128
SSM_GROUP_CH = 16
SSM_WIDTH = 512
SSM_GROUPS = SSM_WIDTH // SSM_GROUP_CH
SSM_STATE = 64
SSM_EIG_MAX_RE = -1e-4
SSM_CHUNK = 16
S5_CHUNKS_PER_STEP = 32
MOE_GROUPS = 4
MOE_EXPERTS_PER_GROUP = 8
MOE_EXPERTS = MOE_GROUPS * MOE_EXPERTS_PER_GROUP
MOE_BLOCK = 512
SEG_ALIGN = 8
NORM_EPS = 1e-6
SUBLN_EPS = 1e-5
LAMBDA_INIT = 0.8 - 0.6 * math.exp(-0.3 * 0)

ATTN_BLOCK = 256
ATTN_HEADS_PER_STEP = 4
ATTN_ROW_CHUNK = 32
ATTN_ONES_ROWS = 16
LOG2E = math.log2(math.e)
TOKEN_TILE = 512
ROUTER_ROWS = 40
ROUTER_LANES = LANES
TILE_SLOTS = -(-(2 * TOKEN_TILE + MOE_EXPERTS * (SEG_ALIGN - 1)) // MXU_TILE) * MXU_TILE
TILE_CHUNKS = TILE_SLOTS // SEG_ALIGN
MIN_TILE_CHUNKS = 2 * TOKEN_TILE // SEG_ALIGN
CHUNK_ISSUE_UNROLL = 8
CHUNK_SLOT_BITS = 8
assert TILE_CHUNKS <= 2 ** CHUNK_SLOT_BITS
VMEM_LIMIT = 56 << 20


def _params(*sem):
    return pltpu.CompilerParams(dimension_semantics=sem, vmem_limit_bytes=VMEM_LIMIT)


def _rms(x, eps):
    return x * lax.rsqrt(jnp.mean(x * x, axis=-1, keepdims=True) + eps)


def _mod_kernel(c_ref, w_ref, b_ref, o_ref):
    c = c_ref[...]
    c_act = c * jax.nn.sigmoid(c)
    o_ref[...] = jnp.dot(c_act, w_ref[...], preferred_element_type=F32, precision=HIGHEST) + b_ref[...]


def _ada_mod(c, w_ada, b_ada):
    B, D = c.shape
    N = w_ada.shape[1]
    tn = 1024
    return pl.pallas_call(
        _mod_kernel,
        grid=(N // tn,),
        in_specs=[pl.BlockSpec((B, D), lambda j: (0, 0)),
                  pl.BlockSpec((D, tn), lambda j: (0, j)),
                  pl.BlockSpec((1, tn), lambda j: (0, j))],
        out_specs=pl.BlockSpec((B, tn), lambda j: (0, j)),
        out_shape=jax.ShapeDtypeStruct((B, N), F32),
        compiler_params=_params("arbitrary"),
        name="ada_mod",
    )(c, w_ada, b_ada.reshape(1, N))


def _proj_kernel(x_ref, mod_ref, g_ref, w_ref, q_ref, k_ref, v_ref, u_ref, gs_ref):
    y = _rms(x_ref[...], NORM_EPS) * g_ref[...]
    h = (y * (1.0 + mod_ref[1:2, :]) + mod_ref[0:1, :]).astype(BF16)
    W = ATTN_WIDTH

    def proj(lo, hi):
        return jnp.dot(h, w_ref[:, lo:hi], preferred_element_type=F32)

    q_ref[...] = (proj(0, W) * (ATTN_HEAD_DIM ** -0.5 * LOG2E)).astype(BF16)
    k_ref[...] = proj(W, 2 * W).astype(BF16)
    v_ref[...] = proj(2 * W, 3 * W).astype(BF16)
    u_ref[...] = proj(3 * W, 3 * W + SSM_WIDTH).reshape(u_ref.shape)
    gs_ref[...] = jax.nn.sigmoid(proj(3 * W + SSM_WIDTH, w_ref.shape[1])).astype(BF16)


def _chunk_major_spec(tm, per_b):
    return pl.BlockSpec((tm // SSM_CHUNK, None, SSM_CHUNK, SSM_WIDTH), lambda i: (i % per_b, i // per_b, 0, 0))


def _in_proj(x2, mod, norm_g, w_in, seq):
    T, D = x2.shape
    tm = TOKEN_TILE
    per_b = seq // tm
    n_gate = w_in.shape[1] - 3 * ATTN_WIDTH - SSM_WIDTH
    row = lambda i: (i, 0)
    return pl.pallas_call(
        _proj_kernel,
        grid=(T // tm,),
        in_specs=[pl.BlockSpec((tm, D), row),
                  pl.BlockSpec((None, 6, D), lambda i: (i // per_b, 0, 0)),
                  pl.BlockSpec((1, D), lambda i: (0, 0)),
                  pl.BlockSpec(w_in.shape, lambda i: (0, 0))],
        out_specs=[pl.BlockSpec((tm, ATTN_WIDTH), row)] * 3
        + [_chunk_major_spec(tm, per_b), pl.BlockSpec((tm, n_gate), row)],
        out_shape=[jax.ShapeDtypeStruct((T, ATTN_WIDTH), BF16)] * 3
        + [jax.ShapeDtypeStruct((seq // SSM_CHUNK, T // seq, SSM_CHUNK, SSM_WIDTH), F32),
           jax.ShapeDtypeStruct((T, n_gate), BF16)],
        compiler_params=_params("parallel"),
        name="in_proj",
    )(x2, mod, norm_g.reshape(1, D), w_in.astype(BF16))


def _rel_bucket(dist):
    max_exact = REL_BUCKETS // 2
    n = jnp.maximum(dist, 0)
    log_ratio = jnp.log(jnp.maximum(n, 1).astype(F32) / max_exact) / math.log(REL_MAX_DISTANCE / max_exact)
    large = max_exact + (log_ratio * (REL_BUCKETS - max_exact)).astype(I32)
    large = jnp.minimum(large, REL_BUCKETS - 1)
    return jnp.where(n < max_exact, n, large)


def _attn_bias_tiles(rel_bias, blk):
    assert blk >= REL_MAX_DISTANCE
    n_heads = rel_bias.shape[1]
    far = rel_bias[REL_BUCKETS - 1].astype(F32)
    m = jnp.arange(2 * blk)
    signed = jnp.where(m < blk, m, m - 2 * blk)
    tiles = []
    for kind in range(2):
        dist = kind * blk + signed
        tab = jnp.where(dist >= 0, (rel_bias[_rel_bucket(dist)].astype(F32).T - far[:, None]) * LOG2E, NEG_INF)
        skew = jnp.tile(tab, (1, blk))[:, :blk * (2 * blk - 1)].reshape(n_heads, blk, 2 * blk - 1)
        tiles.append(skew[:, :, :blk])
    return jnp.stack(tiles, axis=1)


def _attn_kernel(lam_ref, q_ref, k_ref, v_ref, bias_ref, g_ref, o_ref, vt_scr, *bufs, blk, heads):
    i = pl.program_id(2)
    n_kv = vt_scr.shape[1]
    V = ATTN_V_DIM
    ns = 2 * heads
    acc = bufs[0:ns]
    sbuf = tuple(bufs[(1 + r) * ns:(2 + r) * ns] for r in range(3))
    pbufs = tuple(bufs[(4 + r) * ns:(5 + r) * ns] for r in range(3))

    @pl.when(i == 0)
    def _():
        for hd in range(heads):
            for jb in range(n_kv):
                vt_scr[hd, jb, 0:V, :] = v_ref[jb * blk:(jb + 1) * blk, hd * V:(hd + 1) * V].astype(F32).T.astype(BF16)
                vt_scr[hd, jb, V:, :] = jnp.ones((ATTN_ONES_ROWS, blk), BF16)

    qt_maps = []
    for hd in range(heads):
        qt = q_ref[:, hd * V:(hd + 1) * V].astype(F32).T
        feat = lax.broadcasted_iota(I32, qt.shape, 0)
        qt_maps += [jnp.where(feat < ATTN_HEAD_DIM, qt, 0.0).astype(BF16),
                    jnp.where(feat >= ATTN_HEAD_DIM, qt, 0.0).astype(BF16)]

    def scores(j, st):
        hd = st // 2
        kj = k_ref[pl.ds(pl.multiple_of(j * blk, blk), blk), hd * V:(hd + 1) * V]
        return jnp.dot(kj, qt_maps[st], preferred_element_type=F32)

    n_chunks = blk // ATTN_ROW_CHUNK

    def rows(c):
        return slice(c * ATTN_ROW_CHUNK, (c + 1) * ATTN_ROW_CHUNK)

    def fold8(x):
        return x.reshape(ATTN_ROW_CHUNK // 8, 8, blk)

    def block(j, carry, pos, lookahead, bias_kind):
        src, dst, pbuf = sbuf[pos], sbuf[(pos + 2) % 3], pbufs[pos]
        out = []
        for st in range(ns):
            hd = st // 2
            if lookahead:
                dst[st][...] = scores(j + 2, st)

            def chunk(c):
                s = src[st][rows(c), :]
                return s if bias_kind is None else s + bias_ref[hd, bias_kind, rows(c), :]

            m_old = carry[st]
            m8 = jnp.max(fold8(chunk(0)), axis=0)
            for c in range(1, n_chunks):
                m8 = jnp.maximum(m8, jnp.max(fold8(chunk(c)), axis=0))
            m_new = jnp.maximum(m_old, jnp.max(m8, axis=0, keepdims=True))
            alpha = jnp.exp2(m_old - m_new)
            for c in range(n_chunks):
                pbuf[st][rows(c), :] = jnp.exp2(chunk(c) - m_new).astype(BF16)
            acc[st][...] = alpha * acc[st][...] + jnp.dot(vt_scr[hd, j], pbuf[st][...],
                                                          preferred_element_type=F32)
            out.append(m_new)
        return tuple(out)

    def far_triple(t, carry):
        for r in range(3):
            carry = block(3 * t + r, carry, r, True, None)
        return carry

    def far_single(j, carry):
        carry = block(j, carry, 0, True, None)
        for st in range(ns):
            sbuf[0][st][...] = sbuf[1][st][...]
        for st in range(ns):
            sbuf[1][st][...] = sbuf[2][st][...]
        return carry

    def near_pair(_, carry):
        return block(i, block(i - 1, carry, 0, False, 1), 1, False, 0)

    def near_single(_, carry):
        return block(i, carry, 0, False, 0)

    for st in range(ns):
        acc[st][...] = jnp.zeros(acc[st].shape, F32)
        sbuf[0][st][...] = scores(0, st)
        sbuf[1][st][...] = scores(jnp.minimum(i, 1), st)
    m0 = jnp.full((1, blk), -jnp.inf, F32)
    n_far = jnp.maximum(i - 1, 0)
    carry = lax.fori_loop(0, n_far // 3, far_triple, (m0,) * ns)
    carry = lax.fori_loop(n_far - n_far % 3, n_far, far_single, carry)
    carry = lax.fori_loop(0, jnp.minimum(i, 1), near_pair, carry)
    lax.fori_loop(0, 1 - jnp.minimum(i, 1), near_single, carry)

    lam = (jnp.exp(jnp.sum(lam_ref[0:1, :] * lam_ref[1:2, :], axis=-1, keepdims=True))
           - jnp.exp(jnp.sum(lam_ref[2:3, :] * lam_ref[3:4, :], axis=-1, keepdims=True)) + LAMBDA_INIT)
    for hd in range(heads):
        a1, a2 = acc[2 * hd], acc[2 * hd + 1]
        ot = a1[0:V, :] / a1[V:V + 1, :] - lam * (a2[0:V, :] / a2[V:V + 1, :])
        ot = ot * lax.rsqrt(jnp.mean(ot * ot, axis=0, keepdims=True) + SUBLN_EPS)
        o_ref[:, hd * V:(hd + 1) * V] = (ot.T * (g_ref[...] * (1.0 - LAMBDA_INIT))).astype(BF16)


def _diff_attn(q, k, v, rel_bias, lam_vecs, subln_g, batch, seq):
    T = q.shape[0]
    blk = ATTN_BLOCK
    nq = seq // blk
    bias = _attn_bias_tiles(rel_bias, blk)
    hp = ATTN_HEADS_PER_STEP
    ns = 2 * hp
    width = hp * ATTN_V_DIM
    acc_rows = ATTN_V_DIM + ATTN_ONES_ROWS
    return pl.pallas_call(
        functools.partial(_attn_kernel, blk=blk, heads=hp),
        grid=(batch, ATTN_HEADS // hp, nq),
        in_specs=[pl.BlockSpec((4, ATTN_HEAD_DIM), lambda b, h, i: (0, 0)),
                  pl.BlockSpec((blk, width), lambda b, h, i: (b * nq + i, h)),
                  pl.BlockSpec((seq, width), lambda b, h, i: (b, h)),
                  pl.BlockSpec((seq, width), lambda b, h, i: (b, h)),
                  pl.BlockSpec((hp, 2, blk, blk), lambda b, h, i: (h, 0, 0, 0)),
                  pl.BlockSpec((1, ATTN_V_DIM), lambda b, h, i: (0, 0))],
        out_specs=pl.BlockSpec((blk, width), lambda b, h, i: (b * nq + i, h)),
        out_shape=jax.ShapeDtypeStruct((T, ATTN_WIDTH), BF16),
        scratch_shapes=[pltpu.VMEM((hp, nq, acc_rows, blk), BF16)]
        + [pltpu.VMEM((acc_rows, blk), F32)] * ns + [pltpu.VMEM((blk, blk), F32)] * (3 * ns)
        + [pltpu.VMEM((blk, blk), BF16)] * (3 * ns),
        compiler_params=_params("parallel", "parallel", "arbitrary"),
        name="diff_attn",
    )(lam_vecs, q, k, v, bias, subln_g.reshape(1, ATTN_V_DIM))


def _s5_tables(lam_re, lam_im, log_step, b_re, b_im, c_re, c_im, d_skip):
    L = SSM_CHUNK
    G, P = lam_re.shape
    H = SSM_GROUP_CH
    lr = jnp.minimum(lam_re.astype(F32), SSM_EIG_MAX_RE)
    li = lam_im.astype(F32)
    step = jnp.exp(log_step.astype(F32))[:, None]
    mag = jnp.exp(lr * step)
    ang = li * step
    a_re = mag * jnp.cos(ang)
    a_im = mag * jnp.sin(ang)
    den = lr * lr + li * li
    num_re = a_re - 1.0
    coef_re = (num_re * lr + a_im * li) / den
    coef_im = (a_im * lr - num_re * li) / den
    br = b_re.astype(F32)
    bi = b_im.astype(F32)
    bb_re = coef_re[..., None] * br - coef_im[..., None] * bi
    bb_im = coef_re[..., None] * bi + coef_im[..., None] * br
    pw_re, pw_im = [jnp.ones_like(a_re)], [jnp.zeros_like(a_re)]
    for _ in range(L):
        pr, pi = pw_re[-1], pw_im[-1]
        pw_re.append(pr * a_re - pi * a_im)
        pw_im.append(pr * a_im + pi * a_re)
    pw_re = jnp.stack(pw_re)
    pw_im = jnp.stack(pw_im)
    cr = c_re.astype(F32)[None]
    ci = c_im.astype(F32)[None]
    cp_re = cr * pw_re[:, :, None, :] - ci * pw_im[:, :, None, :]
    cp_im = cr * pw_im[:, :, None, :] + ci * pw_re[:, :, None, :]
    kern = (jnp.einsum('tghp,gpk->tghk', cp_re[:L], bb_re, precision=HIGHEST)
            - jnp.einsum('tghp,gpk->tghk', cp_im[:L], bb_im, precision=HIGHEST))
    steps = jnp.arange(L)
    place = (steps[None, :, None] - steps[:, None, None] == steps[None, None, :]).astype(F32)
    m_tab = jnp.einsum('stu,ughk->gskth', place, kern, precision=HIGHEST).reshape(G, L * H, L * H)
    rev_re = pw_re[L - 1::-1][:, :, None, :]
    rev_im = pw_im[L - 1::-1][:, :, None, :]
    bbt_re = jnp.transpose(bb_re, (0, 2, 1))[None]
    bbt_im = jnp.transpose(bb_im, (0, 2, 1))[None]
    bst_re = jnp.transpose(rev_re * bbt_re - rev_im * bbt_im, (1, 0, 2, 3)).reshape(G, L * H, P)
    bst_im = jnp.transpose(rev_re * bbt_im + rev_im * bbt_re, (1, 0, 2, 3)).reshape(G, L * H, P)
    cst_re = jnp.transpose(cp_re[1:], (1, 3, 0, 2)).reshape(G, P, L * H)
    cst_im = -jnp.transpose(cp_im[1:], (1, 3, 0, 2)).reshape(G, P, L * H)
    a_chunk = jnp.stack([jnp.concatenate([pw_re[L], pw_re[L]], axis=-1),
                         jnp.concatenate([-pw_im[L], pw_im[L]], axis=-1)], axis=1)
    d_tab = jnp.tile(d_skip.astype(F32), (1, L)).reshape(G, 1, L * H)
    bst = jnp.concatenate([bst_re, bst_im], axis=-1)
    bst_swapped = jnp.concatenate([bst_im, bst_re], axis=-1)
    cst = jnp.concatenate([cst_re, cst_im], axis=1)
    return m_tab.astype(BF16), bst.astype(BF16), bst_swapped.astype(BF16), cst.astype(BF16), a_chunk, d_tab


def _gelu_tanh(x):
    return 0.5 * x * (1.0 + jnp.tanh(math.sqrt(2.0 / math.pi) * (x + 0.044715 * (x * x * x))))


def _lane_block_transpose(arrs):
    n = len(arrs)
    width = arrs[0].shape[1]
    blk_id = lax.broadcasted_iota(I32, arrs[0].shape, 1) // SSM_GROUP_CH
    k = n // 2
    while k >= 1:
        keep = (blk_id & k) == 0
        nxt = list(arrs)
        for r in range(n):
            if r & k == 0:
                a, b = arrs[r], arrs[r + k]
                nxt[r] = jnp.where(keep, a, pltpu.roll(b, k * SSM_GROUP_CH, axis=1))
                nxt[r + k] = jnp.where(keep, pltpu.roll(a, width - k * SSM_GROUP_CH, axis=1), b)
        arrs = nxt
        k //= 2
    return arrs


def _s5_kernel(u_ref, m_ref, bst_ref, bsts_ref, cst_ref, a_ref, d_ref, o_ref,
               us_scr, z_scr, zs_scr, y_scr, st_scr, *, batch):
    L, H = SSM_CHUNK, SSM_GROUP_CH
    n_grp = us_scr.shape[0]
    R = us_scr.shape[1]
    half = LANES // H

    @pl.when(pl.program_id(1) == 0)
    def _():
        st_scr[...] = jnp.zeros(st_scr.shape, F32)

    for hh in range(L // half):
        slabs = [pltpu.bitcast(u_ref[pl.ds(hh * half + s, R, stride=L), :].astype(BF16), I32) for s in range(half)]
        for gi, arr in enumerate(_lane_block_transpose(slabs)):
            us_scr[gi, :, hh * LANES:(hh + 1) * LANES] = pltpu.bitcast(arr, BF16)
    for gi in range(n_grp):
        u = us_scr[gi]
        z_scr[gi] = jnp.dot(u, bst_ref[gi], preferred_element_type=F32)
        zs_scr[gi] = jnp.dot(u, bsts_ref[gi], preferred_element_type=F32)
        y_scr[gi] = jnp.dot(u, m_ref[gi], preferred_element_type=F32) + u.astype(F32) * d_ref[gi]

    def step(c, state):
        sl = pl.ds(pl.multiple_of(c * batch, batch), batch)
        out = []
        for gi in range(n_grp):
            x, xs = state[2 * gi], state[2 * gi + 1]
            p, q = a_ref[gi, 0:1, :], a_ref[gi, 1:2, :]
            z = z_scr[gi, sl, :]
            z_scr[gi, sl, :] = x
            out += [p * x + q * xs + z, p * xs - q * x + zs_scr[gi, sl, :]]
        return tuple(out)

    state = lax.fori_loop(0, R // batch, step, tuple(st_scr[k] for k in range(2 * n_grp)))
    for k in range(2 * n_grp):
        st_scr[k] = state[k]

    for gi in range(n_grp):
        y = y_scr[gi] + jnp.dot(z_scr[gi].astype(BF16), cst_ref[gi], preferred_element_type=F32)
        y_scr[gi] = _gelu_tanh(y)
    for hh in range(L // half):
        cols = [pltpu.bitcast(y_scr[gi, :, hh * LANES:(hh + 1) * LANES].astype(BF16), I32) for gi in range(n_grp)]
        for s, arr in enumerate(_lane_block_transpose(cols)):
            o_ref[pl.ds(hh * half + s, R, stride=L), :] = pltpu.bitcast(arr, BF16).astype(F32)


def _s5_branch(u, tables, batch, seq):
    L, G, H, P = SSM_CHUNK, SSM_GROUPS, SSM_GROUP_CH, SSM_STATE
    n_chunks = seq // L
    gpt = LANES // H
    cr = S5_CHUNKS_PER_STEP
    R = cr * batch
    LH = L * H
    m_tab, bst, bst_swapped, cst, a_chunk, d_tab = tables
    tile = lambda o, c: (o, 0, 0)
    return pl.pallas_call(
        functools.partial(_s5_kernel, batch=batch),
        grid=(G // gpt, n_chunks // cr),
        in_specs=[pl.BlockSpec((R * L, LANES), lambda o, c: (c, o)),
                  pl.BlockSpec((gpt, LH, LH), tile),
                  pl.BlockSpec((gpt, LH, 2 * P), tile),
                  pl.BlockSpec((gpt, LH, 2 * P), tile),
                  pl.BlockSpec((gpt, 2 * P, LH), tile),
                  pl.BlockSpec((gpt, 2, 2 * P), tile),
                  pl.BlockSpec((gpt, 1, LH), tile)],
        out_specs=pl.BlockSpec((R * L, LANES), lambda o, c: (c, o)),
        out_shape=jax.ShapeDtypeStruct(u.shape, F32),
        scratch_shapes=[pltpu.VMEM((gpt, R, LH), BF16), pltpu.VMEM((gpt, R, 2 * P), F32),
                        pltpu.VMEM((gpt, R, 2 * P), F32), pltpu.VMEM((gpt, R, LH), F32),
                        pltpu.VMEM((2 * gpt, batch, 2 * P), F32)],
        compiler_params=_params("parallel", "arbitrary"),
        name="s5",
    )(u, m_tab, bst, bst_swapped, cst, a_chunk, d_tab)


def _merge_kernel(x_ref, ya_ref, ys_ref, gs_ref, mod_ref, wglu_ref, pa_ref, ps_ref, wout_ref, g2_ref,
                  wr_ref, br_ref, x1_ref, h2_ref, ri_ref, rw_ref, cnt_ref):
    tm, D = x_ref.shape
    ys = ys_ref[...].reshape(tm, SSM_WIDTH).astype(BF16)
    gl = jnp.dot(ys, wglu_ref[...], preferred_element_type=F32)
    y_ssm = gl[:, :SSM_WIDTH] * jax.nn.sigmoid(gl[:, SSM_WIDTH:])
    p_attn = jnp.dot(ya_ref[...], pa_ref[...], preferred_element_type=F32)
    p_ssm = jnp.dot(y_ssm.astype(BF16), ps_ref[...], preferred_element_type=F32)
    merged = gs_ref[:, :D].astype(F32) * p_attn + gs_ref[:, D:].astype(F32) * p_ssm
    mixed = jnp.dot(merged.astype(BF16), wout_ref[...], preferred_element_type=F32)
    x1 = x_ref[...] + mod_ref[2:3, :] * mixed
    x1_ref[...] = x1
    h2 = _rms(x1, NORM_EPS) * g2_ref[...] * (1.0 + mod_ref[4:5, :]) + mod_ref[3:4, :]
    h2_hi = h2.astype(BF16)
    h2_ref[...] = h2_hi

    h2_lo = (h2 - h2_hi.astype(F32)).astype(BF16)
    hi_prod = jnp.dot(h2_hi, wr_ref[...], preferred_element_type=F32)
    lo_prod = jnp.dot(h2_lo, wr_ref[:, 0:ROUTER_LANES], preferred_element_type=F32)
    lg_tok = hi_prod[:, 0:ROUTER_LANES] + hi_prod[:, ROUTER_LANES:] + lo_prod
    logits = lg_tok.T[0:ROUTER_ROWS, :] + br_ref[...]
    NG, EPG = MOE_GROUPS, MOE_EXPERTS_PER_GROUP
    lg = logits[0:NG, :]
    g_iota = lax.broadcasted_iota(I32, lg.shape, 0)
    lg_max = jnp.max(lg, axis=0, keepdims=True)
    grp = jnp.min(jnp.where(lg == lg_max, g_iota, NG), axis=0, keepdims=True)
    p_grp = 1.0 / jnp.sum(jnp.exp(lg - lg_max), axis=0, keepdims=True)
    le = logits[NG:NG + EPG, :]
    for g in range(1, NG):
        le = jnp.where(grp == g, logits[NG + g * EPG:NG + (g + 1) * EPG, :], le)
    e_iota = lax.broadcasted_iota(I32, le.shape, 0)
    v1 = jnp.max(le, axis=0, keepdims=True)
    i1 = jnp.min(jnp.where(le == v1, e_iota, EPG), axis=0, keepdims=True)
    le2 = jnp.where(e_iota == i1, -jnp.inf, le)
    v2 = jnp.max(le2, axis=0, keepdims=True)
    i2 = jnp.min(jnp.where(le2 == v2, e_iota, EPG), axis=0, keepdims=True)
    e21 = jnp.exp(v2 - v1)
    w1 = p_grp / (1.0 + e21)
    w2 = p_grp * e21 / (1.0 + e21)
    eid1 = grp * EPG + i1
    eid2 = grp * EPG + i2

    x_iota = lax.broadcasted_iota(I32, (MOE_EXPERTS, tm), 0)
    hot1 = x_iota == eid1
    hot2 = x_iota == eid2
    hot = jnp.logical_or(hot1, hot2).astype(F32)
    before = (lax.broadcasted_iota(I32, (tm, tm), 0) < lax.broadcasted_iota(I32, (tm, tm), 1))
    prior = jnp.dot(hot.astype(BF16), before.astype(BF16), preferred_element_type=F32)
    rank1 = jnp.sum(jnp.where(hot1, prior, 0.0), axis=0, keepdims=True)
    rank2 = jnp.sum(jnp.where(hot2, prior, 0.0), axis=0, keepdims=True)
    cnt_ref[...] = jnp.sum(hot, axis=1, keepdims=True).astype(I32)

    zi = jnp.zeros((4, tm), I32)
    ri_ref[...] = jnp.concatenate([eid1, eid2, rank1.astype(I32), rank2.astype(I32), zi], axis=0)
    rw_ref[...] = jnp.concatenate([w1, w2, jnp.zeros((6, tm), F32)], axis=0)


def _merge_route(x2, ya, ys, gs, mod, w_glu, w_pa, w_ps, w_out, norm2_g, w_rg, b_rg, w_re, b_re, seq):
    T, D = x2.shape
    tm = TOKEN_TILE
    per_b = seq // tm
    wr = jnp.concatenate([w_rg, jnp.transpose(w_re, (1, 0, 2)).reshape(D, MOE_EXPERTS),
                          jnp.zeros((D, ROUTER_LANES - MOE_GROUPS - MOE_EXPERTS), F32)], axis=1).astype(F32)
    wr_hi = wr.astype(BF16)
    wr_lo = (wr - wr_hi.astype(F32)).astype(BF16)
    br = jnp.concatenate([b_rg, b_re.reshape(-1),
                          jnp.zeros((ROUTER_ROWS - MOE_GROUPS - MOE_EXPERTS,), F32)]).reshape(ROUTER_ROWS, 1)
    row = lambda i: (i, 0)
    col = lambda i: (0, i)
    full = lambda i: (0, 0)
    return pl.pallas_call(
        _merge_kernel,
        grid=(T // tm,),
        in_specs=[pl.BlockSpec((tm, D), row),
                  pl.BlockSpec((tm, ATTN_WIDTH), row),
                  _chunk_major_spec(tm, per_b),
                  pl.BlockSpec((tm, 2 * D), row),
                  pl.BlockSpec((None, 6, D), lambda i: (i // per_b, 0, 0)),
                  pl.BlockSpec(w_glu.shape, full),
                  pl.BlockSpec(w_pa.shape, full),
                  pl.BlockSpec(w_ps.shape, full),
                  pl.BlockSpec(w_out.shape, full),
                  pl.BlockSpec((1, D), full),
                  pl.BlockSpec((D, 2 * ROUTER_LANES), full),
                  pl.BlockSpec((ROUTER_ROWS, 1), full)],
        out_specs=[pl.BlockSpec((tm, D), row), pl.BlockSpec((tm, D), row),
                   pl.BlockSpec((8, tm), col), pl.BlockSpec((8, tm), col),
                   pl.BlockSpec((None, MOE_EXPERTS, 1), lambda i: (i, 0, 0))],
        out_shape=[jax.ShapeDtypeStruct((T, D), F32), jax.ShapeDtypeStruct((T, D), BF16),
                   jax.ShapeDtypeStruct((8, T), I32), jax.ShapeDtypeStruct((8, T), F32),
                   jax.ShapeDtypeStruct((T // tm, MOE_EXPERTS, 1), I32)],
        compiler_params=_params("parallel"),
        name="merge_route",
    )(x2, ya, ys, gs, mod, w_glu.astype(BF16), w_pa.astype(BF16), w_ps.astype(BF16), w_out.astype(BF16),
      norm2_g.reshape(1, D), jnp.concatenate([wr_hi, wr_lo], axis=1), br)


def _pack_pairs(x):
    W = x.shape[1] // 2
    lo = lax.bitcast_convert_type(x[:, :W], I32)
    hi = lax.bitcast_convert_type(x[:, W:], I32)
    return lax.shift_right_logical(lo, 16) | hi


def _unpack_pairs(w):
    lo = lax.bitcast_convert_type(lax.shift_left(w, 16), F32)
    hi = lax.bitcast_convert_type(w & jnp.int32(-65536), F32)
    return jnp.concatenate([lo.astype(BF16), hi.astype(BF16)], axis=1)


def _tile_positions(ri_ref, seg_ref):
    tm = ri_ref.shape[1]
    x_iota = lax.broadcasted_iota(I32, (MOE_EXPERTS, tm), 0)
    seg = seg_ref[...].astype(F32)
    pos = []
    for k in range(2):
        start = jnp.sum(jnp.where(x_iota == ri_ref[k:k + 1, :], seg, 0.0), axis=0, keepdims=True)
        pos.append(start + ri_ref[2 + k:3 + k, :].astype(F32))
    return pos


def _segment_copies(meta, tile, make_copy):
    chunks, tile_chunks = meta
    base = tile * TILE_CHUNKS
    n = tile_chunks[tile]

    def issue(k):
        word = chunks[base + k]
        src = (word & (2 ** CHUNK_SLOT_BITS - 1)) * SEG_ALIGN
        dst = (word >> CHUNK_SLOT_BITS) * SEG_ALIGN
        make_copy(pl.multiple_of(src, SEG_ALIGN), pl.multiple_of(dst, SEG_ALIGN))

    def group(g, carry):
        for r in range(CHUNK_ISSUE_UNROLL):
            issue(g * CHUNK_ISSUE_UNROLL + r)
        return carry

    def single(k, carry):
        issue(k)
        return carry

    n_groups = n // CHUNK_ISSUE_UNROLL
    lax.fori_loop(0, n_groups, group, 0)
    lax.fori_loop(n_groups * CHUNK_ISSUE_UNROLL, n, single, 0)


def _wait_chunks(n, wait_chunk, wait_bulk):
    has_bulk = n >= MIN_TILE_CHUNKS

    @pl.when(has_bulk)
    def _():
        wait_bulk()

    def body(c, carry):
        wait_chunk()
        return carry
    lax.fori_loop(jnp.where(has_bulk, MIN_TILE_CHUNKS, 0), n, body, 0)


def _dispatch_kernel(chunks, tile_chunks, pad_row, pad_chunks, n_used,
                     h_ref, ri_ref, rw_ref, seg_ref, xs_ref, pw_ref, zbuf, zeros_scr, sem, pad_sem, tail_sem):
    i = pl.program_id(0)
    n_tiles = pl.num_programs(0)
    slot = i % 2
    tm = h_ref.shape[0]
    pos1, pos2 = _tile_positions(ri_ref, seg_ref)
    pw_ref[...] = jnp.concatenate([pos1, pos2, rw_ref[0:2, :], jnp.zeros((4, tm), F32)], axis=0)
    r_iota = lax.broadcasted_iota(I32, (TILE_SLOTS, tm), 0).astype(F32)
    onehot = jnp.logical_or(r_iota == pos1, r_iota == pos2).astype(BF16)
    zbuf[slot] = _pack_pairs(jnp.dot(onehot, h_ref[...], preferred_element_type=F32))

    def chunk_copy(buf_slot, src, dst):
        return pltpu.make_async_copy(zbuf.at[buf_slot, pl.ds(src, SEG_ALIGN)],
                                     xs_ref.at[pl.ds(dst, SEG_ALIGN)], sem.at[buf_slot])

    _segment_copies((chunks, tile_chunks), i, lambda src, dst: chunk_copy(slot, src, dst).start())

    def wait_tile(tile, buf_slot):
        bulk = pltpu.make_async_copy(zbuf.at[buf_slot, pl.ds(0, MIN_TILE_CHUNKS * SEG_ALIGN)],
                                     xs_ref.at[pl.ds(0, MIN_TILE_CHUNKS * SEG_ALIGN)], sem.at[buf_slot])
        _wait_chunks(tile_chunks[tile], chunk_copy(buf_slot, 0, 0).wait, bulk.wait)

    @pl.when(i > 0)
    def _():
        wait_tile(i - 1, 1 - slot)

    @pl.when(i == n_tiles - 1)
    def _():
        zeros_scr[...] = jnp.zeros(zeros_scr.shape, I32)

        def pad_copy(dst):
            return pltpu.make_async_copy(zeros_scr.at[pl.ds(0, SEG_ALIGN)], xs_ref.at[pl.ds(dst, SEG_ALIGN)], pad_sem)

        def tail_copy(dst):
            return pltpu.make_async_copy(zeros_scr, xs_ref.at[pl.ds(dst, MOE_BLOCK)], tail_sem)

        def per_expert(e, total):
            def per_chunk(c, carry):
                pad_copy(pl.multiple_of(pad_row[e] + c * SEG_ALIGN, SEG_ALIGN)).start()
                return carry
            lax.fori_loop(0, pad_chunks[e], per_chunk, 0)
            return total + pad_chunks[e]

        n_pad_copies = lax.fori_loop(0, MOE_EXPERTS, per_expert, 0)
        n_blocks = xs_ref.shape[0] // MOE_BLOCK

        def tail_start(b, carry):
            tail_copy(pl.multiple_of(b * MOE_BLOCK, MOE_BLOCK)).start()
            return carry
        lax.fori_loop(n_used[0], n_blocks, tail_start, 0)
        wait_tile(i, slot)

        def wait_pad(c, carry):
            pad_copy(0).wait()
            return carry
        lax.fori_loop(0, n_pad_copies, wait_pad, 0)

        def wait_tail(b, carry):
            tail_copy(0).wait()
            return carry
        lax.fori_loop(n_used[0], n_blocks, wait_tail, 0)


def _dispatch(meta, h2, ri, rw, seg_start, n_rows):
    T, D = h2.shape
    tm = TOKEN_TILE
    col = lambda i, *_: (0, i)
    return pl.pallas_call(
        _dispatch_kernel,
        grid_spec=pltpu.PrefetchScalarGridSpec(
            num_scalar_prefetch=5,
            grid=(T // tm,),
            in_specs=[pl.BlockSpec((tm, D), lambda i, *_: (i, 0)),
                      pl.BlockSpec((8, tm), col),
                      pl.BlockSpec((8, tm), col),
                      pl.BlockSpec((None, MOE_EXPERTS, 1), lambda i, *_: (i, 0, 0))],
            out_specs=[pl.BlockSpec(memory_space=pl.ANY), pl.BlockSpec((8, tm), col)],
            scratch_shapes=[pltpu.VMEM((2, TILE_SLOTS, D // 2), I32), pltpu.VMEM((MOE_BLOCK, D // 2), I32),
                            pltpu.SemaphoreType.DMA((2,)), pltpu.SemaphoreType.DMA(()),
                            pltpu.SemaphoreType.DMA(())]),
        out_shape=[jax.ShapeDtypeStruct((n_rows, D // 2), I32), jax.ShapeDtypeStruct((8, T), F32)],
        compiler_params=_params("arbitrary"),
        name="moe_dispatch",
    )(*meta, h2, ri, rw, seg_start)


def _expert_kernel(be_ref, nb_ref, x_ref, wi_ref, wo_ref, o_ref, wi_bf, wo_bf):
    i = pl.program_id(0)
    F = wo_ref.shape[0]
    in_use = i < nb_ref[0]

    @pl.when(jnp.logical_and(in_use, jnp.logical_or(i == 0, be_ref[i] != be_ref[jnp.maximum(i - 1, 0)])))
    def _():
        wi_bf[...] = wi_ref[...].astype(BF16)
        wo_bf[...] = wo_ref[...].astype(BF16)

    @pl.when(in_use)
    def _():
        hid = jnp.dot(_unpack_pairs(x_ref[...]), wi_bf[...], preferred_element_type=F32)
        a = hid[:, :F]
        act = a * jax.nn.sigmoid(a) * hid[:, F:]
        y = jnp.dot(act.astype(BF16), wo_bf[...], preferred_element_type=F32)
        o_ref[...] = _pack_pairs(y.astype(BF16).astype(F32))

    @pl.when(jnp.logical_not(in_use))
    def _():
        o_ref[...] = jnp.zeros(o_ref.shape, I32)


def _experts(block_e, n_used, xs, w_e_in, w_e_out):
    n_rows, half_d = xs.shape
    D = 2 * half_d
    F = w_e_out.shape[1]
    blk = lambda i, be, nb: jnp.maximum(jnp.minimum(i, nb[0] - 1), 0)
    return pl.pallas_call(
        _expert_kernel,
        grid_spec=pltpu.PrefetchScalarGridSpec(
            num_scalar_prefetch=2,
            grid=(n_rows // MOE_BLOCK,),
            in_specs=[pl.BlockSpec((MOE_BLOCK, half_d), lambda i, be, nb: (blk(i, be, nb), 0)),
                      pl.BlockSpec((None, D, 2 * F), lambda i, be, nb: (be[blk(i, be, nb)], 0, 0)),
                      pl.BlockSpec((None, F, D), lambda i, be, nb: (be[blk(i, be, nb)], 0, 0))],
            out_specs=pl.BlockSpec((MOE_BLOCK, half_d), lambda i, be, nb: (i, 0)),
            scratch_shapes=[pltpu.VMEM((D, 2 * F), BF16), pltpu.VMEM((F, D), BF16)]),
        out_shape=jax.ShapeDtypeStruct((n_rows, half_d), I32),
        compiler_params=_params("arbitrary"),
        name="experts",
    )(block_e, n_used, xs, w_e_in, w_e_out)


def _final_kernel(chunks, tile_chunks, x1_ref, pw_ref, mod_ref, g_ref, ys_ref, o_ref, ybuf, sem):
    i = pl.program_id(0)
    n_tiles = pl.num_programs(0)
    slot = i % 2
    tm = x1_ref.shape[0]

    def chunk_copy(buf_slot, src, dst):
        return pltpu.make_async_copy(ys_ref.at[pl.ds(dst, SEG_ALIGN)],
                                     ybuf.at[buf_slot, pl.ds(src, SEG_ALIGN)], sem.at[buf_slot])

    def fetch(tile, buf_slot):
        _segment_copies((chunks, tile_chunks), tile, lambda src, dst: chunk_copy(buf_slot, src, dst).start())

    @pl.when(i == 0)
    def _():
        ybuf[...] = jnp.zeros(ybuf.shape, I32)
        fetch(0, 0)

    @pl.when(i + 1 < n_tiles)
    def _():
        fetch(i + 1, 1 - slot)

    bulk = pltpu.make_async_copy(ys_ref.at[pl.ds(0, MIN_TILE_CHUNKS * SEG_ALIGN)],
                                 ybuf.at[slot, pl.ds(0, MIN_TILE_CHUNKS * SEG_ALIGN)], sem.at[slot])
    _wait_chunks(tile_chunks[i], chunk_copy(slot, 0, 0).wait, bulk.wait)

    s_iota = lax.broadcasted_iota(I32, (tm, TILE_SLOTS), 1).astype(F32)
    comb = (jnp.where(s_iota == pw_ref[:, 0:1], pw_ref[:, 2:3], 0.0)
            + jnp.where(s_iota == pw_ref[:, 1:2], pw_ref[:, 3:4], 0.0))
    moe = jnp.dot(comb.astype(BF16), _unpack_pairs(ybuf[slot]), preferred_element_type=F32)
    x2 = x1_ref[...] + mod_ref[5:6, :] * moe
    o_ref[...] = _rms(x2, NORM_EPS) * g_ref[...]


def _final(meta, x1, pw_tok, mod, final_g, ys, seq):
    T, D = x1.shape
    tm = TOKEN_TILE
    per_b = seq // tm
    row = lambda i, *_: (i, 0)
    return pl.pallas_call(
        _final_kernel,
        grid_spec=pltpu.PrefetchScalarGridSpec(
            num_scalar_prefetch=2,
            grid=(T // tm,),
            in_specs=[pl.BlockSpec((tm, D), row),
                      pl.BlockSpec((tm, 8), row),
                      pl.BlockSpec((None, 6, D), lambda i, *_: (i // per_b, 0, 0)),
                      pl.BlockSpec((1, D), lambda i, *_: (0, 0)),
                      pl.BlockSpec(memory_space=pl.ANY)],
            out_specs=pl.BlockSpec((tm, D), row),
            scratch_shapes=[pltpu.VMEM((2, TILE_SLOTS, D // 2), I32), pltpu.SemaphoreType.DMA((2,))]),
        out_shape=jax.ShapeDtypeStruct((T, D), F32),
        compiler_params=_params("arbitrary"),
        name="final",
    )(*meta, x1, pw_tok, mod, final_g.reshape(1, D), ys)


def _round_up(x, m):
    return (x + m - 1) // m * m


def _moe_layout(tile_counts):
    n_tiles = tile_counts.shape[0]
    seg = _round_up(tile_counts, SEG_ALIGN)
    seg_start = jnp.cumsum(seg, axis=1) - seg
    tile_base = jnp.cumsum(seg, axis=0) - seg
    used = jnp.sum(seg, axis=0)
    region = _round_up(used, MOE_BLOCK)
    region_end = jnp.cumsum(region)
    region_start = region_end - region
    dst_row = region_start[None, :] + tile_base
    n_chunk = seg // SEG_ALIGN
    n_assign = 2 * n_tiles * TOKEN_TILE
    n_rows = _round_up(n_assign + n_tiles * MOE_EXPERTS * (SEG_ALIGN - 1) + MOE_EXPERTS * (MOE_BLOCK - 1), MOE_BLOCK)
    block_start = jnp.arange(n_rows // MOE_BLOCK, dtype=I32) * MOE_BLOCK
    block_e = jnp.minimum(jnp.sum(block_start[:, None] >= region_end[None, :], axis=1), MOE_EXPERTS - 1)
    i32 = lambda a: a.reshape(-1).astype(I32)
    chunk_end = jnp.cumsum(n_chunk, axis=1)
    chunk_start = chunk_end - n_chunk
    k = jnp.arange(TILE_CHUNKS)[None, :, None]
    mine = (chunk_start[:, None, :] <= k) & (k < chunk_end[:, None, :])
    pick = lambda a: jnp.sum(jnp.where(mine, (a // SEG_ALIGN - chunk_start)[:, None, :] + k, 0), axis=2)
    word = pick(seg_start) | (pick(dst_row) << CHUNK_SLOT_BITS)
    meta = (i32(word), i32(chunk_end[:, -1]))
    pad = (i32(region_start + used), i32((region - used) // SEG_ALIGN))
    n_used = (region_end[-1:] // MOE_BLOCK).astype(I32)
    return meta, pad, seg_start.astype(I32)[:, :, None], block_e.astype(I32), n_used, n_rows


def kernel(x, c, w_ada, b_ada, norm1_g, w_in, rel_bias, lambda_q1, lambda_k1, lambda_q2, lambda_k2, subln_g, ssm_lambda_re, ssm_lambda_im, ssm_log_step, ssm_b_re, ssm_b_im, ssm_c_re, ssm_c_im, ssm_d, w_glu, w_proj_attn, w_proj_ssm, w_out, norm2_g, w_router_group, b_router_group, w_router_expert, b_router_expert, w_expert_in, w_expert_out, final_g):
    B, S, D = x.shape
    T = B * S
    x2 = x.reshape(T, D)
    mod = _ada_mod(c, w_ada[0], b_ada[0]).reshape(B, 6, D)
    q, k, v, u, gs = _in_proj(x2, mod, norm1_g[0], w_in[0], S)
    lam_vecs = jnp.stack([lambda_q1[0], lambda_k1[0], lambda_q2[0], lambda_k2[0]]).astype(F32)
    y_attn = _diff_attn(q, k, v, rel_bias, lam_vecs, subln_g[0], B, S)
    tables = _s5_tables(ssm_lambda_re[0], ssm_lambda_im[0], ssm_log_step[0], ssm_b_re[0], ssm_b_im[0],
                        ssm_c_re[0], ssm_c_im[0], ssm_d[0])
    y_s5 = _s5_branch(u.reshape(-1, SSM_WIDTH), tables, B, S).reshape(u.shape)
    x1, h2, ri, rw, tile_counts = _merge_route(
        x2, y_attn, y_s5, gs, mod, w_glu[0], w_proj_attn[0], w_proj_ssm[0], w_out[0], norm2_g[0],
        w_router_group[0], b_router_group[0], w_router_expert[0], b_router_expert[0], S)
    meta, pad, seg_start, block_e, n_used, n_rows = _moe_layout(tile_counts[:, :, 0])
    xs, pw = _dispatch(meta + pad + (n_used,), h2, ri, rw, seg_start, n_rows)
    ys = _experts(block_e, n_used, xs, w_expert_in[0], w_expert_out[0])
    out = _final(meta, x1, pw.T, mod, final_g, ys, S)
    return out.reshape(B, S, D)
```

```python
import functools
import math

import jax
import jax.numpy as jnp
from jax import lax
from jax.experimental import pallas as pl
from jax.experimental.pallas import tpu as pltpu

F32 = jnp.float32
BF16 = jnp.bfloat16
I32 = jnp.int32
HIGHEST = lax.Precision.HIGHEST

LANES = 128
MXU_TILE = 256

ATTN_HEADS = 4
ATTN_HEAD_DIM = 64
ATTN_V_DIM = 2 * ATTN_HEAD_DIM
ATTN_WIDTH = ATTN_HEADS * ATTN_V_DIM
NEG_INF = -1e30
REL_BUCKETS = 32
REL_MAX_DISTANCE = 128
SSM_GROUP_CH = 16
SSM_WIDTH = 512
SSM_GROUPS = SSM_WIDTH // SSM_GROUP_CH
SSM_STATE = 64
SSM_EIG_MAX_RE = -1e-4
SSM_CHUNK = 16
S5_CHUNKS_PER_STEP = 32
MOE_GROUPS = 4
MOE_EXPERTS_PER_GROUP = 8
MOE_EXPERTS = MOE_GROUPS * MOE_EXPERTS_PER_GROUP
MOE_BLOCK = 512
SEG_ALIGN = 8
NORM_EPS = 1e-6
SUBLN_EPS = 1e-5
LAMBDA_INIT = 0.8 - 0.6 * math.exp(-0.3 * 0)

ATTN_BLOCK = 256
ATTN_HEADS_PER_STEP = 4
ATTN_ROW_CHUNK = 32
ATTN_ONES_ROWS = 16
LOG2E = math.log2(math.e)
TOKEN_TILE = 512
ROUTER_ROWS = 40
ROUTER_LANES = LANES
TILE_SLOTS = -(-(2 * TOKEN_TILE + MOE_EXPERTS * (SEG_ALIGN - 1)) // MXU_TILE) * MXU_TILE
TILE_CHUNKS = TILE_SLOTS // SEG_ALIGN
MIN_TILE_CHUNKS = 2 * TOKEN_TILE // SEG_ALIGN
CHUNK_ISSUE_UNROLL = 8
CHUNK_SLOT_BITS = 8
assert TILE_CHUNKS <= 2 ** CHUNK_SLOT_BITS
VMEM_LIMIT = 56 << 20


def _params(*sem):
    return pltpu.CompilerParams(dimension_semantics=sem, vmem_limit_bytes=VMEM_LIMIT)


def _rms(x, eps):
    return x * lax.rsqrt(jnp.mean(x * x, axis=-1, keepdims=True) + eps)


def _mod_kernel(c_ref, w_ref, b_ref, o_ref):
    c = c_ref[...]
    c_act = c * jax.nn.sigmoid(c)
    o_ref[...] = jnp.dot(c_act, w_ref[...], preferred_element_type=F32, precision=HIGHEST) + b_ref[...]


def _ada_mod(c, w_ada, b_ada):
    B, D = c.shape
    N = w_ada.shape[1]
    tn = 1024
    return pl.pallas_call(
        _mod_kernel,
        grid=(N // tn,),
        in_specs=[pl.BlockSpec((B, D), lambda j: (0, 0)),
                  pl.BlockSpec((D, tn), lambda j: (0, j)),
                  pl.BlockSpec((1, tn), lambda j: (0, j))],
        out_specs=pl.BlockSpec((B, tn), lambda j: (0, j)),
        out_shape=jax.ShapeDtypeStruct((B, N), F32),
        compiler_params=_params("arbitrary"),
        name="ada_mod",
    )(c, w_ada, b_ada.reshape(1, N))


def _proj_kernel(x_ref, mod_ref, g_ref, w_ref, q_ref, k_ref, v_ref, u_ref, gs_ref):
    y = _rms(x_ref[...], NORM_EPS) * g_ref[...]
    h = (y * (1.0 + mod_ref[1:2, :]) + mod_ref[0:1, :]).astype(BF16)
    W = ATTN_WIDTH

    def proj(lo, hi):
        return jnp.dot(h, w_ref[:, lo:hi], preferred_element_type=F32)

    q_ref[...] = (proj(0, W) * (ATTN_HEAD_DIM ** -0.5 * LOG2E)).astype(BF16)
    k_ref[...] = proj(W, 2 * W).astype(BF16)
    v_ref[...] = proj(2 * W, 3 * W).astype(BF16)
    u_ref[...] = proj(3 * W, 3 * W + SSM_WIDTH).reshape(u_ref.shape)
    gs_ref[...] = jax.nn.sigmoid(proj(3 * W + SSM_WIDTH, w_ref.shape[1])).astype(BF16)


def _chunk_major_spec(tm, per_b):
    return pl.BlockSpec((tm // SSM_CHUNK, None, SSM_CHUNK, SSM_WIDTH), lambda i: (i % per_b, i // per_b, 0, 0))


def _in_proj(x2, mod, norm_g, w_in, seq):
    T, D = x2.shape
    tm = TOKEN_TILE
    per_b = seq // tm
    n_gate = w_in.shape[1] - 3 * ATTN_WIDTH - SSM_WIDTH
    row = lambda i: (i, 0)
    return pl.pallas_call(
        _proj_kernel,
        grid=(T // tm,),
        in_specs=[pl.BlockSpec((tm, D), row),
                  pl.BlockSpec((None, 6, D), lambda i: (i // per_b, 0, 0)),
                  pl.BlockSpec((1, D), lambda i: (0, 0)),
                  pl.BlockSpec(w_in.shape, lambda i: (0, 0))],
        out_specs=[pl.BlockSpec((tm, ATTN_WIDTH), row)] * 3
        + [_chunk_major_spec(tm, per_b), pl.BlockSpec((tm, n_gate), row)],
        out_shape=[jax.ShapeDtypeStruct((T, ATTN_WIDTH), BF16)] * 3
        + [jax.ShapeDtypeStruct((seq // SSM_CHUNK, T // seq, SSM_CHUNK, SSM_WIDTH), F32),
           jax.ShapeDtypeStruct((T, n_gate), BF16)],
        compiler_params=_params("parallel"),
        name="in_proj",
    )(x2, mod, norm_g.reshape(1, D), w_in.astype(BF16))


def _rel_bucket(dist):
    max_exact = REL_BUCKETS // 2
    n = jnp.maximum(dist, 0)
    log_ratio = jnp.log(jnp.maximum(n, 1).astype(F32) / max_exact) / math.log(REL_MAX_DISTANCE / max_exact)
    large = max_exact + (log_ratio * (REL_BUCKETS - max_exact)).astype(I32)
    large = jnp.minimum(large, REL_BUCKETS - 1)
    return jnp.where(n < max_exact, n, large)


def _attn_bias_tiles(rel_bias, blk):
    assert blk >= REL_MAX_DISTANCE
    n_heads = rel_bias.shape[1]
    far = rel_bias[REL_BUCKETS - 1].astype(F32)
    m = jnp.arange(2 * blk)
    signed = jnp.where(m < blk, m, m - 2 * blk)
    tiles = []
    for kind in range(2):
        dist = kind * blk + signed
        tab = jnp.where(dist >= 0, (rel_bias[_rel_bucket(dist)].astype(F32).T - far[:, None]) * LOG2E, NEG_INF)
        skew = jnp.tile(tab, (1, blk))[:, :blk * (2 * blk - 1)].reshape(n_heads, blk, 2 * blk - 1)
        tiles.append(skew[:, :, :blk])
    return jnp.stack(tiles, axis=1)


def _attn_kernel(lam_ref, q_ref, k_ref, v_ref, bias_ref, g_ref, o_ref, vt_scr, *bufs, blk, heads):
    i = pl.program_id(2)
    n_kv = vt_scr.shape[1]
    V = ATTN_V_DIM
    ns = 2 * heads
    acc = bufs[0:ns]
    sbuf = tuple(bufs[(1 + r) * ns:(2 + r) * ns] for r in range(3))
    pbufs = tuple(bufs[(4 + r) * ns:(5 + r) * ns] for r in range(3))

    @pl.when(i == 0)
    def _():
        for hd in range(heads):
            for jb in range(n_kv):
                vt_scr[hd, jb, 0:V, :] = v_ref[jb * blk:(jb + 1) * blk, hd * V:(hd + 1) * V].astype(F32).T.astype(BF16)
                vt_scr[hd, jb, V:, :] = jnp.ones((ATTN_ONES_ROWS, blk), BF16)

    qt_maps = []
    for hd in range(heads):
        qt = q_ref[:, hd * V:(hd + 1) * V].astype(F32).T
        feat = lax.broadcasted_iota(I32, qt.shape, 0)
        qt_maps += [jnp.where(feat < ATTN_HEAD_DIM, qt, 0.0).astype(BF16),
                    jnp.where(feat >= ATTN_HEAD_DIM, qt, 0.0).astype(BF16)]

    def scores(j, st):
        hd = st // 2
        kj = k_ref[pl.ds(pl.multiple_of(j * blk, blk), blk), hd * V:(hd + 1) * V]
        return jnp.dot(kj, qt_maps[st], preferred_element_type=F32)

    n_chunks = blk // ATTN_ROW_CHUNK

    def rows(c):
        return slice(c * ATTN_ROW_CHUNK, (c + 1) * ATTN_ROW_CHUNK)

    def fold8(x):
        return x.reshape(ATTN_ROW_CHUNK // 8, 8, blk)

    def block(j, carry, pos, lookahead, bias_kind):
        src, dst, pbuf = sbuf[pos], sbuf[(pos + 2) % 3], pbufs[pos]
        out = []
        for st in range(ns):
            hd = st // 2
            if lookahead:
                dst[st][...] = scores(j + 2, st)

            def chunk(c):
                s = src[st][rows(c), :]
                return s if bias_kind is None else s + bias_ref[hd, bias_kind, rows(c), :]

            m_old = carry[st]
            m8 = jnp.max(fold8(chunk(0)), axis=0)
            for c in range(1, n_chunks):
                m8 = jnp.maximum(m8, jnp.max(fold8(chunk(c)), axis=0))
            m_new = jnp.maximum(m_old, jnp.max(m8, axis=0, keepdims=True))
            alpha = jnp.exp2(m_old - m_new)
            for c in range(n_chunks):
                pbuf[st][rows(c), :] = jnp.exp2(chunk(c) - m_new).astype(BF16)
            acc[st][...] = alpha * acc[st][...] + jnp.dot(vt_scr[hd, j], pbuf[st][...],
                                                          preferred_element_type=F32)
            out.append(m_new)
        return tuple(out)

    def far_triple(t, carry):
        for r in range(3):
            carry = block(3 * t + r, carry, r, True, None)
        return carry

    def far_single(j, carry):
        carry = block(j, carry, 0, True, None)
        for st in range(ns):
            sbuf[0][st][...] = sbuf[1][st][...]
        for st in range(ns):
            sbuf[1][st][...] = sbuf[2][st][...]
        return carry

    def near_pair(_, carry):
        return block(i, block(i - 1, carry, 0, False, 1), 1, False, 0)

    def near_single(_, carry):
        return block(i, carry, 0, False, 0)

    for st in range(ns):
        acc[st][...] = jnp.zeros(acc[st].shape, F32)
        sbuf[0][st][...] = scores(0, st)
        sbuf[1][st][...] = scores(jnp.minimum(i, 1), st)
    m0 = jnp.full((1, blk), -jnp.inf, F32)
    n_far = jnp.maximum(i - 1, 0)
    carry = lax.fori_loop(0, n_far // 3, far_triple, (m0,) * ns)
    carry = lax.fori_loop(n_far - n_far % 3, n_far, far_single, carry)
    carry = lax.fori_loop(0, jnp.minimum(i, 1), near_pair, carry)
    lax.fori_loop(0, 1 - jnp.minimum(i, 1), near_single, carry)

    lam = (jnp.exp(jnp.sum(lam_ref[0:1, :] * lam_ref[1:2, :], axis=-1, keepdims=True))
           - jnp.exp(jnp.sum(lam_ref[2:3, :] * lam_ref[3:4, :], axis=-1, keepdims=True)) + LAMBDA_INIT)
    for hd in range(heads):
        a1, a2 = acc[2 * hd], acc[2 * hd + 1]
        ot = a1[0:V, :] / a1[V:V + 1, :] - lam * (a2[0:V, :] / a2[V:V + 1, :])
        ot = ot * lax.rsqrt(jnp.mean(ot * ot, axis=0, keepdims=True) + SUBLN_EPS)
        o_ref[:, hd * V:(hd + 1) * V] = (ot.T * (g_ref[...] * (1.0 - LAMBDA_INIT))).astype(BF16)


def _diff_attn(q, k, v, rel_bias, lam_vecs, subln_g, batch, seq):
    T = q.shape[0]
    blk = ATTN_BLOCK
    nq = seq // blk
    bias = _attn_bias_tiles(rel_bias, blk)
    hp = ATTN_HEADS_PER_STEP
    ns = 2 * hp
    width = hp * ATTN_V_DIM
    acc_rows = ATTN_V_DIM + ATTN_ONES_ROWS
    return pl.pallas_call(
        functools.partial(_attn_kernel, blk=blk, heads=hp),
        grid=(batch, ATTN_HEADS // hp, nq),
        in_specs=[pl.BlockSpec((4, ATTN_HEAD_DIM), lambda b, h, i: (0, 0)),
                  pl.BlockSpec((blk, width), lambda b, h, i: (b * nq + i, h)),
                  pl.BlockSpec((seq, width), lambda b, h, i: (b, h)),
                  pl.BlockSpec((seq, width), lambda b, h, i: (b, h)),
                  pl.BlockSpec((hp, 2, blk, blk), lambda b, h, i: (h, 0, 0, 0)),
                  pl.BlockSpec((1, ATTN_V_DIM), lambda b, h, i: (0, 0))],
        out_specs=pl.BlockSpec((blk, width), lambda b, h, i: (b * nq + i, h)),
        out_shape=jax.ShapeDtypeStruct((T, ATTN_WIDTH), BF16),
        scratch_shapes=[pltpu.VMEM((hp, nq, acc_rows, blk), BF16)]
        + [pltpu.VMEM((acc_rows, blk), F32)] * ns + [pltpu.VMEM((blk, blk), F32)] * (3 * ns)
        + [pltpu.VMEM((blk, blk), BF16)] * (3 * ns),
        compiler_params=_params("parallel", "parallel", "arbitrary"),
        name="diff_attn",
    )(lam_vecs, q, k, v, bias, subln_g.reshape(1, ATTN_V_DIM))


def _s5_tables(lam_re, lam_im, log_step, b_re, b_im, c_re, c_im, d_skip):
    L = SSM_CHUNK
    G, P = lam_re.shape
    H = SSM_GROUP_CH
    lr = jnp.minimum(lam_re.astype(F32), SSM_EIG_MAX_RE)
    li = lam_im.astype(F32)
    step = jnp.exp(log_step.astype(F32))[:, None]
    mag = jnp.exp(lr * step)
    ang = li * step
    a_re = mag * jnp.cos(ang)
    a_im = mag * jnp.sin(ang)
    den = lr * lr + li * li
    num_re = a_re - 1.0
    coef_re = (num_re * lr + a_im * li) / den
    coef_im = (a_im * lr - num_re * li) / den
    br = b_re.astype(F32)
    bi = b_im.astype(F32)
    bb_re = coef_re[..., None] * br - coef_im[..., None] * bi
    bb_im = coef_re[..., None] * bi + coef_im[..., None] * br
    pw_re, pw_im = [jnp.ones_like(a_re)], [jnp.zeros_like(a_re)]
    for _ in range(L):
        pr, pi = pw_re[-1], pw_im[-1]
        pw_re.append(pr * a_re - pi * a_im)
        pw_im.append(pr * a_im + pi * a_re)
    pw_re = jnp.stack(pw_re)
    pw_im = jnp.stack(pw_im)
    cr = c_re.astype(F32)[None]
    ci = c_im.astype(F32)[None]
    cp_re = cr * pw_re[:, :, None, :] - ci * pw_im[:, :, None, :]
    cp_im = cr * pw_im[:, :, None, :] + ci * pw_re[:, :, None, :]
    kern = jnp.einsum('tghp,gpk->tghk', jnp.concatenate([cp_re[:L], -cp_im[:L]], axis=-1),
                      jnp.concatenate([bb_re, bb_im], axis=1), precision=HIGHEST)
    steps = jnp.arange(L)
    place = (steps[None, :, None] - steps[:, None, None] == steps[None, None, :]).astype(F32)
    m_tab = jnp.einsum('stu,ughk->gskth', place, kern, precision=HIGHEST).reshape(G, L * H, L * H)
    rev_re = pw_re[L - 1::-1][:, :, None, :]
    rev_im = pw_im[L - 1::-1][:, :, None, :]
    bbt_re = jnp.transpose(bb_re, (0, 2, 1))[None]
    bbt_im = jnp.transpose(bb_im, (0, 2, 1))[None]
    bst_re = jnp.transpose(rev_re * bbt_re - rev_im * bbt_im, (1, 0, 2, 3)).reshape(G, L * H, P)
    bst_im = jnp.transpose(rev_re * bbt_im + rev_im * bbt_re, (1, 0, 2, 3)).reshape(G, L * H, P)
    cst_re = jnp.transpose(cp_re[1:], (1, 3, 0, 2)).reshape(G, P, L * H)
    cst_im = -jnp.transpose(cp_im[1:], (1, 3, 0, 2)).reshape(G, P, L * H)
    a_chunk = jnp.stack([jnp.concatenate([pw_re[L], pw_re[L]], axis=-1),
                         jnp.concatenate([-pw_im[L], pw_im[L]], axis=-1)], axis=1)
    d_tab = jnp.tile(d_skip.astype(F32), (1, L)).reshape(G, 1, L * H)
    bst = jnp.concatenate([bst_re, bst_im], axis=-1)
    bst_swapped = jnp.concatenate([bst_im, bst_re], axis=-1)
    cst = jnp.concatenate([cst_re, cst_im], axis=1)
    return m_tab.astype(BF16), bst.astype(BF16), bst_swapped.astype(BF16), cst.astype(BF16), a_chunk, d_tab


def _gelu_tanh(x):
    return 0.5 * x * (1.0 + jnp.tanh(math.sqrt(2.0 / math.pi) * (x + 0.044715 * (x * x * x))))


def _lane_block_transpose(arrs):
    n = len(arrs)
    width = arrs[0].shape[1]
    blk_id = lax.broadcasted_iota(I32, arrs[0].shape, 1) // SSM_GROUP_CH
    k = n // 2
    while k >= 1:
        keep = (blk_id & k) == 0
        nxt = list(arrs)
        for r in range(n):
            if r & k == 0:
                a, b = arrs[r], arrs[r + k]
                nxt[r] = jnp.where(keep, a, pltpu.roll(b, k * SSM_GROUP_CH, axis=1))
                nxt[r + k] = jnp.where(keep, pltpu.roll(a, width - k * SSM_GROUP_CH, axis=1), b)
        arrs = nxt
        k //= 2
    return arrs


def _s5_kernel(u_ref, m_ref, bst_ref, bsts_ref, cst_ref, a_ref, d_ref, o_ref,
               us_scr, z_scr, zs_scr, y_scr, st_scr, *, batch):
    L, H = SSM_CHUNK, SSM_GROUP_CH
    n_grp = us_scr.shape[0]
    R = us_scr.shape[1]
    half = LANES // H

    @pl.when(pl.program_id(1) == 0)
    def _():
        st_scr[...] = jnp.zeros(st_scr.shape, F32)

    for hh in range(L // half):
        slabs = [pltpu.bitcast(u_ref[pl.ds(hh * half + s, R, stride=L), :].astype(BF16), I32) for s in range(half)]
        for gi, arr in enumerate(_lane_block_transpose(slabs)):
            us_scr[gi, :, hh * LANES:(hh + 1) * LANES] = pltpu.bitcast(arr, BF16)
    for gi in range(n_grp):
        u = us_scr[gi]
        z_scr[gi] = jnp.dot(u, bst_ref[gi], preferred_element_type=F32)
        zs_scr[gi] = jnp.dot(u, bsts_ref[gi], preferred_element_type=F32)
        y_scr[gi] = jnp.dot(u, m_ref[gi], preferred_element_type=F32) + u.astype(F32) * d_ref[gi]

    def step(c, state):
        sl = pl.ds(pl.multiple_of(c * batch, batch), batch)
        out = []
        for gi in range(n_grp):
            x, xs = state[2 * gi], state[2 * gi + 1]
            p, q = a_ref[gi, 0:1, :], a_ref[gi, 1:2, :]
            z = z_scr[gi, sl, :]
            z_scr[gi, sl, :] = x
            out += [p * x + q * xs + z, p * xs - q * x + zs_scr[gi, sl, :]]
        return tuple(out)

    state = lax.fori_loop(0, R // batch, step, tuple(st_scr[k] for k in range(2 * n_grp)))
    for k in range(2 * n_grp):
        st_scr[k] = state[k]

    for gi in range(n_grp):
        y = y_scr[gi] + jnp.dot(z_scr[gi].astype(BF16), cst_ref[gi], preferred_element_type=F32)
        y_scr[gi] = _gelu_tanh(y)
    for hh in range(L // half):
        cols = [pltpu.bitcast(y_scr[gi, :, hh * LANES:(hh + 1) * LANES].astype(BF16), I32) for gi in range(n_grp)]
        for s, arr in enumerate(_lane_block_transpose(cols)):
            o_ref[pl.ds(hh * half + s, R, stride=L), :] = pltpu.bitcast(arr, BF16).astype(F32)


def _s5_branch(u, tables, batch, seq):
    L, G, H, P = SSM_CHUNK, SSM_GROUPS, SSM_GROUP_CH, SSM_STATE
    n_chunks = seq // L
    gpt = LANES // H
    cr = S5_CHUNKS_PER_STEP
    R = cr * batch
    LH = L * H
    m_tab, bst, bst_swapped, cst, a_chunk, d_tab = tables
    tile = lambda o, c: (o, 0, 0)
    return pl.pallas_call(
        functools.partial(_s5_kernel, batch=batch),
        grid=(G // gpt, n_chunks // cr),
        in_specs=[pl.BlockSpec((R * L, LANES), lambda o, c: (c, o)),
                  pl.BlockSpec((gpt, LH, LH), tile),
                  pl.BlockSpec((gpt, LH, 2 * P), tile),
                  pl.BlockSpec((gpt, LH, 2 * P), tile),
                  pl.BlockSpec((gpt, 2 * P, LH), tile),
                  pl.BlockSpec((gpt, 2, 2 * P), tile),
                  pl.BlockSpec((gpt, 1, LH), tile)],
        out_specs=pl.BlockSpec((R * L, LANES), lambda o, c: (c, o)),
        out_shape=jax.ShapeDtypeStruct(u.shape, F32),
        scratch_shapes=[pltpu.VMEM((gpt, R, LH), BF16), pltpu.VMEM((gpt, R, 2 * P), F32),
                        pltpu.VMEM((gpt, R, 2 * P), F32), pltpu.VMEM((gpt, R, LH), F32),
                        pltpu.VMEM((2 * gpt, batch, 2 * P), F32)],
        compiler_params=_params("parallel", "arbitrary"),
        name="s5",
    )(u, m_tab, bst, bst_swapped, cst, a_chunk, d_tab)


def _merge_kernel(x_ref, ya_ref, ys_ref, gs_ref, mod_ref, wglu_ref, pa_ref, ps_ref, wout_ref, g2_ref,
                  wr_ref, br_ref, x1_ref, h2_ref, ri_ref, rw_ref, cnt_ref):
    tm, D = x_ref.shape
    ys = ys_ref[...].reshape(tm, SSM_WIDTH).astype(BF16)
    gl = jnp.dot(ys, wglu_ref[...], preferred_element_type=F32)
    y_ssm = gl[:, :SSM_WIDTH] * jax.nn.sigmoid(gl[:, SSM_WIDTH:])
    p_attn = jnp.dot(ya_ref[...], pa_ref[...], preferred_element_type=F32)
    p_ssm = jnp.dot(y_ssm.astype(BF16), ps_ref[...], preferred_element_type=F32)
    merged = gs_ref[:, :D].astype(F32) * p_attn + gs_ref[:, D:].astype(F32) * p_ssm
    mixed = jnp.dot(merged.astype(BF16), wout_ref[...], preferred_element_type=F32)
    x1 = x_ref[...] + mod_ref[2:3, :] * mixed
    x1_ref[...] = x1
    h2 = _rms(x1, NORM_EPS) * g2_ref[...] * (1.0 + mod_ref[4:5, :]) + mod_ref[3:4, :]
    h2_hi = h2.astype(BF16)
    h2_ref[...] = h2_hi

    h2_lo = (h2 - h2_hi.astype(F32)).astype(BF16)
    hi_prod = jnp.dot(h2_hi, wr_ref[...], preferred_element_type=F32)
    lo_prod = jnp.dot(h2_lo, wr_ref[:, 0:ROUTER_LANES], preferred_element_type=F32)
    lg_tok = hi_prod[:, 0:ROUTER_LANES] + hi_prod[:, ROUTER_LANES:] + lo_prod
    logits = lg_tok.T[0:ROUTER_ROWS, :] + br_ref[...]
    NG, EPG = MOE_GROUPS, MOE_EXPERTS_PER_GROUP
    lg = logits[0:NG, :]
    g_iota = lax.broadcasted_iota(I32, lg.shape, 0)
    lg_max = jnp.max(lg, axis=0, keepdims=True)
    grp = jnp.min(jnp.where(lg == lg_max, g_iota, NG), axis=0, keepdims=True)
    p_grp = 1.0 / jnp.sum(jnp.exp(lg - lg_max), axis=0, keepdims=True)
    le = logits[NG:NG + EPG, :]
    for g in range(1, NG):
        le = jnp.where(grp == g, logits[NG + g * EPG:NG + (g + 1) * EPG, :], le)
    e_iota = lax.broadcasted_iota(I32, le.shape, 0)
    v1 = jnp.max(le, axis=0, keepdims=True)
    i1 = jnp.min(jnp.where(le == v1, e_iota, EPG), axis=0, keepdims=True)
    le2 = jnp.where(e_iota == i1, -jnp.inf, le)
    v2 = jnp.max(le2, axis=0, keepdims=True)
    i2 = jnp.min(jnp.where(le2 == v2, e_iota, EPG), axis=0, keepdims=True)
    e21 = jnp.exp(v2 - v1)
    w1 = p_grp / (1.0 + e21)
    w2 = p_grp * e21 / (1.0 + e21)
    eid1 = grp * EPG + i1
    eid2 = grp * EPG + i2

    x_iota = lax.broadcasted_iota(I32, (MOE_EXPERTS, tm), 0)
    hot1 = x_iota == eid1
    hot2 = x_iota == eid2
    hot = jnp.logical_or(hot1, hot2).astype(F32)
    before = (lax.broadcasted_iota(I32, (tm, tm), 0) < lax.broadcasted_iota(I32, (tm, tm), 1))
    prior = jnp.dot(hot.astype(BF16), before.astype(BF16), preferred_element_type=F32)
    rank1 = jnp.sum(jnp.where(hot1, prior, 0.0), axis=0, keepdims=True)
    rank2 = jnp.sum(jnp.where(hot2, prior, 0.0), axis=0, keepdims=True)
    cnt_ref[...] = jnp.sum(hot, axis=1, keepdims=True).astype(I32)

    zi = jnp.zeros((4, tm), I32)
    ri_ref[...] = jnp.concatenate([eid1, eid2, rank1.astype(I32), rank2.astype(I32), zi], axis=0)
    rw_ref[...] = jnp.concatenate([w1, w2, jnp.zeros((6, tm), F32)], axis=0)


def _merge_route(x2, ya, ys, gs, mod, w_glu, w_pa, w_ps, w_out, norm2_g, w_rg, b_rg, w_re, b_re, seq):
    T, D = x2.shape
    tm = TOKEN_TILE
    per_b = seq // tm
    wr = jnp.concatenate([w_rg, jnp.transpose(w_re, (1, 0, 2)).reshape(D, MOE_EXPERTS),
                          jnp.zeros((D, ROUTER_LANES - MOE_GROUPS - MOE_EXPERTS), F32)], axis=1).astype(F32)
    wr_hi = wr.astype(BF16)
    wr_lo = (wr - wr_hi.astype(F32)).astype(BF16)
    br = jnp.concatenate([b_rg, b_re.reshape(-1),
                          jnp.zeros((ROUTER_ROWS - MOE_GROUPS - MOE_EXPERTS,), F32)]).reshape(ROUTER_ROWS, 1)
    row = lambda i: (i, 0)
    col = lambda i: (0, i)
    full = lambda i: (0, 0)
    return pl.pallas_call(
        _merge_kernel,
        grid=(T // tm,),
        in_specs=[pl.BlockSpec((tm, D), row),
                  pl.BlockSpec((tm, ATTN_WIDTH), row),
                  _chunk_major_spec(tm, per_b),
                  pl.BlockSpec((tm, 2 * D), row),
                  pl.BlockSpec((None, 6, D), lambda i: (i // per_b, 0, 0)),
                  pl.BlockSpec(w_glu.shape, full),
                  pl.BlockSpec(w_pa.shape, full),
                  pl.BlockSpec(w_ps.shape, full),
                  pl.BlockSpec(w_out.shape, full),
                  pl.BlockSpec((1, D), full),
                  pl.BlockSpec((D, 2 * ROUTER_LANES), full),
                  pl.BlockSpec((ROUTER_ROWS, 1), full)],
        out_specs=[pl.BlockSpec((tm, D), row), pl.BlockSpec((tm, D), row),
                   pl.BlockSpec((8, tm), col), pl.BlockSpec((8, tm), col),
                   pl.BlockSpec((None, MOE_EXPERTS, 1), lambda i: (i, 0, 0))],
        out_shape=[jax.ShapeDtypeStruct((T, D), F32), jax.ShapeDtypeStruct((T, D), BF16),
                   jax.ShapeDtypeStruct((8, T), I32), jax.ShapeDtypeStruct((8, T), F32),
                   jax.ShapeDtypeStruct((T // tm, MOE_EXPERTS, 1), I32)],
        compiler_params=_params("parallel"),
        name="merge_route",
    )(x2, ya, ys, gs, mod, w_glu.astype(BF16), w_pa.astype(BF16), w_ps.astype(BF16), w_out.astype(BF16),
      norm2_g.reshape(1, D), jnp.concatenate([wr_hi, wr_lo], axis=1), br)


def _pack_pairs(x):
    W = x.shape[1] // 2
    lo = lax.bitcast_convert_type(x[:, :W], I32)
    hi = lax.bitcast_convert_type(x[:, W:], I32)
    return lax.shift_right_logical(lo, 16) | hi


def _unpack_pairs(w):
    lo = lax.bitcast_convert_type(lax.shift_left(w, 16), F32)
    hi = lax.bitcast_convert_type(w & jnp.int32(-65536), F32)
    return jnp.concatenate([lo.astype(BF16), hi.astype(BF16)], axis=1)


def _tile_positions(ri_ref, seg_ref):
    tm = ri_ref.shape[1]
    x_iota = lax.broadcasted_iota(I32, (MOE_EXPERTS, tm), 0)
    seg = seg_ref[...].astype(F32)
    pos = []
    for k in range(2):
        start = jnp.sum(jnp.where(x_iota == ri_ref[k:k + 1, :], seg, 0.0), axis=0, keepdims=True)
        pos.append(start + ri_ref[2 + k:3 + k, :].astype(F32))
    return pos


def _segment_copies(meta, tile, make_copy):
    chunks, tile_chunks = meta
    base = tile * TILE_CHUNKS
    n = tile_chunks[tile]

    def issue(k):
        word = chunks[base + k]
        src = (word & (2 ** CHUNK_SLOT_BITS - 1)) * SEG_ALIGN
        dst = (word >> CHUNK_SLOT_BITS) * SEG_ALIGN
        make_copy(pl.multiple_of(src, SEG_ALIGN), pl.multiple_of(dst, SEG_ALIGN))

    def group(g, carry):
        for r in range(CHUNK_ISSUE_UNROLL):
            issue(g * CHUNK_ISSUE_UNROLL + r)
        return carry

    def single(k, carry):
        issue(k)
        return carry

    n_groups = n // CHUNK_ISSUE_UNROLL
    lax.fori_loop(0, n_groups, group, 0)
    lax.fori_loop(n_groups * CHUNK_ISSUE_UNROLL, n, single, 0)


def _wait_chunks(n, wait_chunk, wait_bulk):
    has_bulk = n >= MIN_TILE_CHUNKS

    @pl.when(has_bulk)
    def _():
        wait_bulk()

    def body(c, carry):
        wait_chunk()
        return carry
    lax.fori_loop(jnp.where(has_bulk, MIN_TILE_CHUNKS, 0), n, body, 0)


def _dispatch_kernel(chunks, tile_chunks, pad_row, pad_chunks, n_used,
                     h_ref, ri_ref, rw_ref, seg_ref, xs_ref, pw_ref, zbuf, zeros_scr, sem, pad_sem, tail_sem):
    i = pl.program_id(0)
    n_tiles = pl.num_programs(0)
    slot = i % 2
    tm = h_ref.shape[0]
    pos1, pos2 = _tile_positions(ri_ref, seg_ref)
    pw_ref[...] = jnp.concatenate([pos1, pos2, rw_ref[0:2, :], jnp.zeros((4, tm), F32)], axis=0)
    r_iota = lax.broadcasted_iota(I32, (TILE_SLOTS, tm), 0).astype(F32)
    onehot = jnp.logical_or(r_iota == pos1, r_iota == pos2).astype(BF16)
    zbuf[slot] = _pack_pairs(jnp.dot(onehot, h_ref[...], preferred_element_type=F32))

    def chunk_copy(buf_slot, src, dst):
        return pltpu.make_async_copy(zbuf.at[buf_slot, pl.ds(src, SEG_ALIGN)],
                                     xs_ref.at[pl.ds(dst, SEG_ALIGN)], sem.at[buf_slot])

    _segment_copies((chunks, tile_chunks), i, lambda src, dst: chunk_copy(slot, src, dst).start())

    def wait_tile(tile, buf_slot):
        bulk = pltpu.make_async_copy(zbuf.at[buf_slot, pl.ds(0, MIN_TILE_CHUNKS * SEG_ALIGN)],
                                     xs_ref.at[pl.ds(0, MIN_TILE_CHUNKS * SEG_ALIGN)], sem.at[buf_slot])
        _wait_chunks(tile_chunks[tile], chunk_copy(buf_slot, 0, 0).wait, bulk.wait)

    @pl.when(i > 0)
    def _():
        wait_tile(i - 1, 1 - slot)

    @pl.when(i == n_tiles - 1)
    def _():
        zeros_scr[...] = jnp.zeros(zeros_scr.shape, I32)

        def pad_copy(dst):
            return pltpu.make_async_copy(zeros_scr.at[pl.ds(0, SEG_ALIGN)], xs_ref.at[pl.ds(dst, SEG_ALIGN)], pad_sem)

        def tail_copy(dst):
            return pltpu.make_async_copy(zeros_scr, xs_ref.at[pl.ds(dst, MOE_BLOCK)], tail_sem)

        def per_expert(e, total):
            def per_chunk(c, carry):
                pad_copy(pl.multiple_of(pad_row[e] + c * SEG_ALIGN, SEG_ALIGN)).start()
                return carry
            lax.fori_loop(0, pad_chunks[e], per_chunk, 0)
            return total + pad_chunks[e]

        n_pad_copies = lax.fori_loop(0, MOE_EXPERTS, per_expert, 0)
        n_blocks = xs_ref.shape[0] // MOE_BLOCK

        def tail_start(b, carry):
            tail_copy(pl.multiple_of(b * MOE_BLOCK, MOE_BLOCK)).start()
            return carry
        lax.fori_loop(n_used[0], n_blocks, tail_start, 0)
        wait_tile(i, slot)

        def wait_pad(c, carry):
            pad_copy(0).wait()
            return carry
        lax.fori_loop(0, n_pad_copies, wait_pad, 0)

        def wait_tail(b, carry):
            tail_copy(0).wait()
            return carry
        lax.fori_loop(n_used[0], n_blocks, wait_tail, 0)


def _dispatch(meta, h2, ri, rw, seg_start, n_rows):
    T, D = h2.shape
    tm = TOKEN_TILE
    col = lambda i, *_: (0, i)
    return pl.pallas_call(
        _dispatch_kernel,
        grid_spec=pltpu.PrefetchScalarGridSpec(
            num_scalar_prefetch=5,
            grid=(T // tm,),
            in_specs=[pl.BlockSpec((tm, D), lambda i, *_: (i, 0)),
                      pl.BlockSpec((8, tm), col),
                      pl.BlockSpec((8, tm), col),
                      pl.BlockSpec((None, MOE_EXPERTS, 1), lambda i, *_: (i, 0, 0))],
            out_specs=[pl.BlockSpec(memory_space=pl.ANY), pl.BlockSpec((8, tm), col)],
            scratch_shapes=[pltpu.VMEM((2, TILE_SLOTS, D // 2), I32), pltpu.VMEM((MOE_BLOCK, D // 2), I32),
                            pltpu.SemaphoreType.DMA((2,)), pltpu.SemaphoreType.DMA(()),
                            pltpu.SemaphoreType.DMA(())]),
        out_shape=[jax.ShapeDtypeStruct((n_rows, D // 2), I32), jax.ShapeDtypeStruct((8, T), F32)],
        compiler_params=_params("arbitrary"),
        name="moe_dispatch",
    )(*meta, h2, ri, rw, seg_start)


def _expert_kernel(be_ref, nb_ref, x_ref, wi_ref, wo_ref, o_ref, wi_bf, wo_bf):
    i = pl.program_id(0)
    F = wo_ref.shape[0]
    in_use = i < nb_ref[0]

    @pl.when(jnp.logical_and(in_use, jnp.logical_or(i == 0, be_ref[i] != be_ref[jnp.maximum(i - 1, 0)])))
    def _():
        wi_bf[...] = wi_ref[...].astype(BF16)
        wo_bf[...] = wo_ref[...].astype(BF16)

    @pl.when(in_use)
    def _():
        hid = jnp.dot(_unpack_pairs(x_ref[...]), wi_bf[...], preferred_element_type=F32)
        a = hid[:, :F]
        act = a * jax.nn.sigmoid(a) * hid[:, F:]
        y = jnp.dot(act.astype(BF16), wo_bf[...], preferred_element_type=F32)
        o_ref[...] = _pack_pairs(y.astype(BF16).astype(F32))

    @pl.when(jnp.logical_not(in_use))
    def _():
        o_ref[...] = jnp.zeros(o_ref.shape, I32)


def _experts(block_e, n_used, xs, w_e_in, w_e_out):
    n_rows, half_d = xs.shape
    D = 2 * half_d
    F = w_e_out.shape[1]
    blk = lambda i, be, nb: jnp.maximum(jnp.minimum(i, nb[0] - 1), 0)
    return pl.pallas_call(
        _expert_kernel,
        grid_spec=pltpu.PrefetchScalarGridSpec(
            num_scalar_prefetch=2,
            grid=(n_rows // MOE_BLOCK,),
            in_specs=[pl.BlockSpec((MOE_BLOCK, half_d), lambda i, be, nb: (blk(i, be, nb), 0)),
                      pl.BlockSpec((None, D, 2 * F), lambda i, be, nb: (be[blk(i, be, nb)], 0, 0)),
                      pl.BlockSpec((None, F, D), lambda i, be, nb: (be[blk(i, be, nb)], 0, 0))],
            out_specs=pl.BlockSpec((MOE_BLOCK, half_d), lambda i, be, nb: (i, 0)),
            scratch_shapes=[pltpu.VMEM((D, 2 * F), BF16), pltpu.VMEM((F, D), BF16)]),
        out_shape=jax.ShapeDtypeStruct((n_rows, half_d), I32),
        compiler_params=_params("arbitrary"),
        name="experts",
    )(block_e, n_used, xs, w_e_in, w_e_out)


def _final_kernel(chunks, tile_chunks, x1_ref, pw_ref, mod_ref, g_ref, ys_ref, o_ref, ybuf, sem):
    i = pl.program_id(0)
    n_tiles = pl.num_programs(0)
    slot = i % 2
    tm = x1_ref.shape[0]

    def chunk_copy(buf_slot, src, dst):
        return pltpu.make_async_copy(ys_ref.at[pl.ds(dst, SEG_ALIGN)],
                                     ybuf.at[buf_slot, pl.ds(src, SEG_ALIGN)], sem.at[buf_slot])

    def fetch(tile, buf_slot):
        _segment_copies((chunks, tile_chunks), tile, lambda src, dst: chunk_copy(buf_slot, src, dst).start())

    @pl.when(i == 0)
    def _():
        ybuf[...] = jnp.zeros(ybuf.shape, I32)
        fetch(0, 0)

    @pl.when(i + 1 < n_tiles)
    def _():
        fetch(i + 1, 1 - slot)

    bulk = pltpu.make_async_copy(ys_ref.at[pl.ds(0, MIN_TILE_CHUNKS * SEG_ALIGN)],
                                 ybuf.at[slot, pl.ds(0, MIN_TILE_CHUNKS * SEG_ALIGN)], sem.at[slot])
    _wait_chunks(tile_chunks[i], chunk_copy(slot, 0, 0).wait, bulk.wait)

    s_iota = lax.broadcasted_iota(I32, (tm, TILE_SLOTS), 1).astype(F32)
    comb = (jnp.where(s_iota == pw_ref[:, 0:1], pw_ref[:, 2:3], 0.0)
            + jnp.where(s_iota == pw_ref[:, 1:2], pw_ref[:, 3:4], 0.0))
    moe = jnp.dot(comb.astype(BF16), _unpack_pairs(ybuf[slot]), preferred_element_type=F32)
    x2 = x1_ref[...] + mod_ref[5:6, :] * moe
    o_ref[...] = _rms(x2, NORM_EPS) * g_ref[...]


def _final(meta, x1, pw_tok, mod, final_g, ys, seq):
    T, D = x1.shape
    tm = TOKEN_TILE
    per_b = seq // tm
    row = lambda i, *_: (i, 0)
    return pl.pallas_call(
        _final_kernel,
        grid_spec=pltpu.PrefetchScalarGridSpec(
            num_scalar_prefetch=2,
            grid=(T // tm,),
            in_specs=[pl.BlockSpec((tm, D), row),
                      pl.BlockSpec((tm, 8), row),
                      pl.BlockSpec((None, 6, D), lambda i, *_: (i // per_b, 0, 0)),
                      pl.BlockSpec((1, D), lambda i, *_: (0, 0)),
                      pl.BlockSpec(memory_space=pl.ANY)],
            out_specs=pl.BlockSpec((tm, D), row),
            scratch_shapes=[pltpu.VMEM((2, TILE_SLOTS, D // 2), I32), pltpu.SemaphoreType.DMA((2,))]),
        out_shape=jax.ShapeDtypeStruct((T, D), F32),
        compiler_params=_params("arbitrary"),
        name="final",
    )(*meta, x1, pw_tok, mod, final_g.reshape(1, D), ys)


def _round_up(x, m):
    return (x + m - 1) // m * m


def _moe_layout(tile_counts):
    n_tiles = tile_counts.shape[0]
    seg = _round_up(tile_counts, SEG_ALIGN)
    seg_start = jnp.cumsum(seg, axis=1) - seg
    tile_base = jnp.cumsum(seg, axis=0) - seg
    used = jnp.sum(seg, axis=0)
    region = _round_up(used, MOE_BLOCK)
    region_end = jnp.cumsum(region)
    region_start = region_end - region
    dst_row = region_start[None, :] + tile_base
    n_chunk = seg // SEG_ALIGN
    n_assign = 2 * n_tiles * TOKEN_TILE
    n_rows = _round_up(n_assign + n_tiles * MOE_EXPERTS * (SEG_ALIGN - 1) + MOE_EXPERTS * (MOE_BLOCK - 1), MOE_BLOCK)
    block_start = jnp.arange(n_rows // MOE_BLOCK, dtype=I32) * MOE_BLOCK
    block_e = jnp.minimum(jnp.sum(block_start[:, None] >= region_end[None, :], axis=1), MOE_EXPERTS - 1)
    i32 = lambda a: a.reshape(-1).astype(I32)
    chunk_end = jnp.cumsum(n_chunk, axis=1)
    chunk_start = chunk_end - n_chunk
    k = jnp.arange(TILE_CHUNKS)[None, :, None]
    mine = (chunk_start[:, None, :] <= k) & (k < chunk_end[:, None, :])
    pick = lambda a: jnp.sum(jnp.where(mine, (a // SEG_ALIGN - chunk_start)[:, None, :] + k, 0), axis=2)
    word = pick(seg_start) | (pick(dst_row) << CHUNK_SLOT_BITS)
    meta = (i32(word), i32(chunk_end[:, -1]))
    pad = (i32(region_start + used), i32((region - used) // SEG_ALIGN))
    n_used = (region_end[-1:] // MOE_BLOCK).astype(I32)
    return meta, pad, seg_start.astype(I32)[:, :, None], block_e.astype(I32), n_used, n_rows


def kernel(x, c, w_ada, b_ada, norm1_g, w_in, rel_bias, lambda_q1, lambda_k1, lambda_q2, lambda_k2, subln_g, ssm_lambda_re, ssm_lambda_im, ssm_log_step, ssm_b_re, ssm_b_im, ssm_c_re, ssm_c_im, ssm_d, w_glu, w_proj_attn, w_proj_ssm, w_out, norm2_g, w_router_group, b_router_group, w_router_expert, b_router_expert, w_expert_in, w_expert_out, final_g):
    B, S, D = x.shape
    T = B * S
    x2 = x.reshape(T, D)
    mod = _ada_mod(c, w_ada[0], b_ada[0]).reshape(B, 6, D)
    q, k, v, u, gs = _in_proj(x2, mod, norm1_g[0], w_in[0], S)
    lam_vecs = jnp.stack([lambda_q1[0], lambda_k1[0], lambda_q2[0], lambda_k2[0]]).astype(F32)
    y_attn = _diff_attn(q, k, v, rel_bias, lam_vecs, subln_g[0], B, S)
    tables = _s5_tables(ssm_lambda_re[0], ssm_lambda_im[0], ssm_log_step[0], ssm_b_re[0], ssm_b_im[0],
                        ssm_c_re[0], ssm_c_im[0], ssm_d[0])
    y_s5 = _s5_branch(u.reshape(-1, SSM_WIDTH), tables, B, S).reshape(u.shape)
    x1, h2, ri, rw, tile_counts = _merge_route(
        x2, y_attn, y_s5, gs, mod, w_glu[0], w_proj_attn[0], w_proj_ssm[0], w_out[0], norm2_g[0],
        w_router_group[0], b_router_group[0], w_router_expert[0], b_router_expert[0], S)
    meta, pad, seg_start, block_e, n_used, n_rows = _moe_layout(tile_counts[:, :, 0])
    xs, pw = _dispatch(meta + pad + (n_used,), h2, ri, rw, seg_start, n_rows)
    ys = _experts(block_e, n_used, xs, w_expert_in[0], w_expert_out[0])
    out = _final(meta, x1, pw.T, mod, final_g, ys, S)
    return out.reshape(B, S, D)
```

```python
import functools
import math

import jax
import jax.numpy as jnp
from jax import lax
from jax.experimental import pallas as pl
from jax.experimental.pallas import tpu as pltpu

F32 = jnp.float32
BF16 = jnp.bfloat16
I32 = jnp.int32
HIGHEST = lax.Precision.HIGHEST

LANES = 128
MXU_TILE = 256

ATTN_HEADS = 4
ATTN_HEAD_DIM = 64
ATTN_V_DIM = 2 * ATTN_HEAD_DIM
ATTN_WIDTH = ATTN_HEADS * ATTN_V_DIM
NEG_INF = -1e30
REL_BUCKETS = 32
REL_MAX_DISTANCE = 128
SSM_GROUP_CH = 16
SSM_WIDTH = 512
SSM_GROUPS = SSM_WIDTH // SSM_GROUP_CH
SSM_STATE = 64
SSM_EIG_MAX_RE = -1e-4
SSM_CHUNK = 16
S5_CHUNKS_PER_STEP = 32
MOE_GROUPS = 4
MOE_EXPERTS_PER_GROUP = 8
MOE_EXPERTS = MOE_GROUPS * MOE_EXPERTS_PER_GROUP
MOE_BLOCK = 512
SEG_ALIGN = 8
NORM_EPS = 1e-6
SUBLN_EPS = 1e-5
LAMBDA_INIT = 0.8 - 0.6 * math.exp(-0.3 * 0)

ATTN_BLOCK = 256
ATTN_HEADS_PER_STEP = 4
ATTN_ROW_CHUNK = 32
ATTN_ONES_ROWS = 16
LOG2E = math.log2(math.e)
TOKEN_TILE = 512
ROUTER_ROWS = 40
ROUTER_LANES = LANES
TILE_SLOTS = -(-(2 * TOKEN_TILE + MOE_EXPERTS * (SEG_ALIGN - 1)) // MXU_TILE) * MXU_TILE
TILE_CHUNKS = TILE_SLOTS // SEG_ALIGN
MIN_TILE_CHUNKS = 2 * TOKEN_TILE // SEG_ALIGN
CHUNK_ISSUE_UNROLL = 8
CHUNK_SLOT_BITS = 8
assert TILE_CHUNKS <= 2 ** CHUNK_SLOT_BITS
VMEM_LIMIT = 56 << 20


def _params(*sem):
    return pltpu.CompilerParams(dimension_semantics=sem, vmem_limit_bytes=VMEM_LIMIT)


def _rms(x, eps):
    return x * lax.rsqrt(jnp.mean(x * x, axis=-1, keepdims=True) + eps)


def _mod_kernel(c_ref, w_ref, b_ref, o_ref):
    c = c_ref[...]
    c_act = c * jax.nn.sigmoid(c)
    o_ref[...] = jnp.dot(c_act, w_ref[...], preferred_element_type=F32, precision=HIGHEST) + b_ref[...]


def _ada_mod(c, w_ada, b_ada):
    B, D = c.shape
    N = w_ada.shape[1]
    tn = 1024
    return pl.pallas_call(
        _mod_kernel,
        grid=(N // tn,),
        in_specs=[pl.BlockSpec((B, D), lambda j: (0, 0)),
                  pl.BlockSpec((D, tn), lambda j: (0, j)),
                  pl.BlockSpec((1, tn), lambda j: (0, j))],
        out_specs=pl.BlockSpec((B, tn), lambda j: (0, j)),
        out_shape=jax.ShapeDtypeStruct((B, N), F32),
        compiler_params=_params("arbitrary"),
        name="ada_mod",
    )(c, w_ada, b_ada.reshape(1, N))


def _proj_kernel(x_ref, mod_ref, g_ref, w_ref, q_ref, k_ref, v_ref, u_ref, gs_ref):
    y = _rms(x_ref[...], NORM_EPS) * g_ref[...]
    h = (y * (1.0 + mod_ref[1:2, :]) + mod_ref[0:1, :]).astype(BF16)
    W = ATTN_WIDTH

    def proj(lo, hi):
        return jnp.dot(h, w_ref[:, lo:hi], preferred_element_type=F32)

    q_ref[...] = (proj(0, W) * (ATTN_HEAD_DIM ** -0.5 * LOG2E)).astype(BF16)
    k_ref[...] = proj(W, 2 * W).astype(BF16)
    v_ref[...] = proj(2 * W, 3 * W).astype(BF16)
    u_ref[...] = proj(3 * W, 3 * W + SSM_WIDTH).reshape(u_ref.shape)
    gs_ref[...] = jax.nn.sigmoid(proj(3 * W + SSM_WIDTH, w_ref.shape[1])).astype(BF16)


def _chunk_major_spec(tm, per_b):
    return pl.BlockSpec((tm // SSM_CHUNK, None, SSM_CHUNK, SSM_WIDTH), lambda i: (i % per_b, i // per_b, 0, 0))


def _in_proj(x2, mod, norm_g, w_in, seq):
    T, D = x2.shape
    tm = TOKEN_TILE
    per_b = seq // tm
    n_gate = w_in.shape[1] - 3 * ATTN_WIDTH - SSM_WIDTH
    row = lambda i: (i, 0)
    return pl.pallas_call(
        _proj_kernel,
        grid=(T // tm,),
        in_specs=[pl.BlockSpec((tm, D), row),
                  pl.BlockSpec((None, 6, D), lambda i: (i // per_b, 0, 0)),
                  pl.BlockSpec((1, D), lambda i: (0, 0)),
                  pl.BlockSpec(w_in.shape, lambda i: (0, 0))],
        out_specs=[pl.BlockSpec((tm, ATTN_WIDTH), row)] * 3
        + [_chunk_major_spec(tm, per_b), pl.BlockSpec((tm, n_gate), row)],
        out_shape=[jax.ShapeDtypeStruct((T, ATTN_WIDTH), BF16)] * 3
        + [jax.ShapeDtypeStruct((seq // SSM_CHUNK, T // seq, SSM_CHUNK, SSM_WIDTH), F32),
           jax.ShapeDtypeStruct((T, n_gate), BF16)],
        compiler_params=_params("parallel"),
        name="in_proj",
    )(x2, mod, norm_g.reshape(1, D), w_in.astype(BF16))


def _rel_bucket(dist):
    max_exact = REL_BUCKETS // 2
    n = jnp.maximum(dist, 0)
    log_ratio = jnp.log(jnp.maximum(n, 1).astype(F32) / max_exact) / math.log(REL_MAX_DISTANCE / max_exact)
    large = max_exact + (log_ratio * (REL_BUCKETS - max_exact)).astype(I32)
    large = jnp.minimum(large, REL_BUCKETS - 1)
    return jnp.where(n < max_exact, n, large)


def _attn_bias_tiles(rel_bias, blk):
    assert blk >= REL_MAX_DISTANCE
    n_heads = rel_bias.shape[1]
    far = rel_bias[REL_BUCKETS - 1].astype(F32)
    m = jnp.arange(2 * blk)
    signed = jnp.where(m < blk, m, m - 2 * blk)
    tiles = []
    for kind in range(2):
        dist = kind * blk + signed
        tab = jnp.where(dist >= 0, (rel_bias[_rel_bucket(dist)].astype(F32).T - far[:, None]) * LOG2E, NEG_INF)
        skew = jnp.tile(tab, (1, blk))[:, :blk * (2 * blk - 1)].reshape(n_heads, blk, 2 * blk - 1)
        tiles.append(skew[:, :, :blk])
    return jnp.stack(tiles, axis=1)


def _attn_kernel(lam_ref, q_ref, k_ref, v_ref, bias_ref, g_ref, o_ref, vt_scr, *bufs, blk, heads):
    i = pl.program_id(2)
    n_kv = vt_scr.shape[1]
    V = ATTN_V_DIM
    ns = 2 * heads
    acc = bufs[0:ns]
    sbuf = tuple(bufs[(1 + r) * ns:(2 + r) * ns] for r in range(3))
    pbufs = tuple(bufs[(4 + r) * ns:(5 + r) * ns] for r in range(3))

    @pl.when(i == 0)
    def _():
        for hd in range(heads):
            for jb in range(n_kv):
                vt_scr[hd, jb, 0:V, :] = v_ref[jb * blk:(jb + 1) * blk, hd * V:(hd + 1) * V].astype(F32).T.astype(BF16)
                vt_scr[hd, jb, V:, :] = jnp.ones((ATTN_ONES_ROWS, blk), BF16)

    qt_maps = []
    for hd in range(heads):
        qt = q_ref[:, hd * V:(hd + 1) * V].astype(F32).T
        feat = lax.broadcasted_iota(I32, qt.shape, 0)
        qt_maps += [jnp.where(feat < ATTN_HEAD_DIM, qt, 0.0).astype(BF16),
                    jnp.where(feat >= ATTN_HEAD_DIM, qt, 0.0).astype(BF16)]

    def scores(j, st):
        hd = st // 2
        kj = k_ref[pl.ds(pl.multiple_of(j * blk, blk), blk), hd * V:(hd + 1) * V]
        return jnp.dot(kj, qt_maps[st], preferred_element_type=F32)

    n_chunks = blk // ATTN_ROW_CHUNK

    def rows(c):
        return slice(c * ATTN_ROW_CHUNK, (c + 1) * ATTN_ROW_CHUNK)

    def fold8(x):
        return x.reshape(ATTN_ROW_CHUNK // 8, 8, blk)

    def block(j, carry, pos, lookahead, bias_kind):
        src, dst, pbuf = sbuf[pos], sbuf[(pos + 2) % 3], pbufs[pos]
        out = []
        for st in range(ns):
            hd = st // 2
            if lookahead:
                dst[st][...] = scores(j + 2, st)

            def chunk(c):
                s = src[st][rows(c), :]
                return s if bias_kind is None else s + bias_ref[hd, bias_kind, rows(c), :]

            m_old = carry[st]
            m8 = jnp.max(fold8(chunk(0)), axis=0)
            for c in range(1, n_chunks):
                m8 = jnp.maximum(m8, jnp.max(fold8(chunk(c)), axis=0))
            m_new = jnp.maximum(m_old, jnp.max(m8, axis=0, keepdims=True))
            alpha = jnp.exp2(m_old - m_new)
            for c in range(n_chunks):
                pbuf[st][rows(c), :] = jnp.exp2(chunk(c) - m_new).astype(BF16)
            acc[st][...] = alpha * acc[st][...] + jnp.dot(vt_scr[hd, j], pbuf[st][...],
                                                          preferred_element_type=F32)
            out.append(m_new)
        return tuple(out)

    def far_triple(t, carry):
        for r in range(3):
            carry = block(3 * t + r, carry, r, True, None)
        return carry

    def far_single(j, carry):
        carry = block(j, carry, 0, True, None)
        for st in range(ns):
            sbuf[0][st][...] = sbuf[1][st][...]
        for st in range(ns):
            sbuf[1][st][...] = sbuf[2][st][...]
        return carry

    def near_pair(_, carry):
        return block(i, block(i - 1, carry, 0, False, 1), 1, False, 0)

    def near_single(_, carry):
        return block(i, carry, 0, False, 0)

    for st in range(ns):
        acc[st][...] = jnp.zeros(acc[st].shape, F32)
        sbuf[0][st][...] = scores(0, st)
        sbuf[1][st][...] = scores(jnp.minimum(i, 1), st)
    m0 = jnp.full((1, blk), -jnp.inf, F32)
    n_far = jnp.maximum(i - 1, 0)
    carry = lax.fori_loop(0, n_far // 3, far_triple, (m0,) * ns)
    carry = lax.fori_loop(n_far - n_far % 3, n_far, far_single, carry)
    carry = lax.fori_loop(0, jnp.minimum(i, 1), near_pair, carry)
    lax.fori_loop(0, 1 - jnp.minimum(i, 1), near_single, carry)

    lam = (jnp.exp(jnp.sum(lam_ref[0:1, :] * lam_ref[1:2, :], axis=-1, keepdims=True))
           - jnp.exp(jnp.sum(lam_ref[2:3, :] * lam_ref[3:4, :], axis=-1, keepdims=True)) + LAMBDA_INIT)
    for hd in range(heads):
        a1, a2 = acc[2 * hd], acc[2 * hd + 1]
        ot = a1[0:V, :] / a1[V:V + 1, :] - lam * (a2[0:V, :] / a2[V:V + 1, :])
        ot = ot * lax.rsqrt(jnp.mean(ot * ot, axis=0, keepdims=True) + SUBLN_EPS)
        o_ref[:, hd * V:(hd + 1) * V] = (ot.T * (g_ref[...] * (1.0 - LAMBDA_INIT))).astype(BF16)


def _diff_attn(q, k, v, rel_bias, lam_vecs, subln_g, batch, seq):
    T = q.shape[0]
    blk = ATTN_BLOCK
    nq = seq // blk
    bias = _attn_bias_tiles(rel_bias, blk)
    hp = ATTN_HEADS_PER_STEP
    ns = 2 * hp
    width = hp * ATTN_V_DIM
    acc_rows = ATTN_V_DIM + ATTN_ONES_ROWS
    return pl.pallas_call(
        functools.partial(_attn_kernel, blk=blk, heads=hp),
        grid=(batch, ATTN_HEADS // hp, nq),
        in_specs=[pl.BlockSpec((4, ATTN_HEAD_DIM), lambda b, h, i: (0, 0)),
                  pl.BlockSpec((blk, width), lambda b, h, i: (b * nq + i, h)),
                  pl.BlockSpec((seq, width), lambda b, h, i: (b, h)),
                  pl.BlockSpec((seq, width), lambda b, h, i: (b, h)),
                  pl.BlockSpec((hp, 2, blk, blk), lambda b, h, i: (h, 0, 0, 0)),
                  pl.BlockSpec((1, ATTN_V_DIM), lambda b, h, i: (0, 0))],
        out_specs=pl.BlockSpec((blk, width), lambda b, h, i: (b * nq + i, h)),
        out_shape=jax.ShapeDtypeStruct((T, ATTN_WIDTH), BF16),
        scratch_shapes=[pltpu.VMEM((hp, nq, acc_rows, blk), BF16)]
        + [pltpu.VMEM((acc_rows, blk), F32)] * ns + [pltpu.VMEM((blk, blk), F32)] * (3 * ns)
        + [pltpu.VMEM((blk, blk), BF16)] * (3 * ns),
        compiler_params=_params("parallel", "parallel", "arbitrary"),
        name="diff_attn",
    )(lam_vecs, q, k, v, bias, subln_g.reshape(1, ATTN_V_DIM))


def _s5_tables(lam_re, lam_im, log_step, b_re, b_im, c_re, c_im, d_skip):
    L = SSM_CHUNK
    G, P = lam_re.shape
    H = SSM_GROUP_CH
    lr = jnp.minimum(lam_re.astype(F32), SSM_EIG_MAX_RE)
    li = lam_im.astype(F32)
    step = jnp.exp(log_step.astype(F32))[:, None]
    mag = jnp.exp(lr * step)
    ang = li * step
    a_re = mag * jnp.cos(ang)
    a_im = mag * jnp.sin(ang)
    den = lr * lr + li * li
    num_re = a_re - 1.0
    coef_re = (num_re * lr + a_im * li) / den
    coef_im = (a_im * lr - num_re * li) / den
    br = b_re.astype(F32)
    bi = b_im.astype(F32)
    bb_re = coef_re[..., None] * br - coef_im[..., None] * bi
    bb_im = coef_re[..., None] * bi + coef_im[..., None] * br
    pw_re, pw_im = [jnp.ones_like(a_re)], [jnp.zeros_like(a_re)]
    for _ in range(L):
        pr, pi = pw_re[-1], pw_im[-1]
        pw_re.append(pr * a_re - pi * a_im)
        pw_im.append(pr * a_im + pi * a_re)
    pw_re = jnp.stack(pw_re)
    pw_im = jnp.stack(pw_im)
    cr = c_re.astype(F32)[None]
    ci = c_im.astype(F32)[None]
    cp_re = cr * pw_re[:, :, None, :] - ci * pw_im[:, :, None, :]
    cp_im = cr * pw_im[:, :, None, :] + ci * pw_re[:, :, None, :]
    kern = jnp.einsum('tghp,gpk->tghk', jnp.concatenate([cp_re[:L], -cp_im[:L]], axis=-1),
                      jnp.concatenate([bb_re, bb_im], axis=1), precision=HIGHEST)
    steps = jnp.arange(L)
    place = (steps[None, :, None] - steps[:, None, None] == steps[None, None, :]).astype(F32)
    m_tab = jnp.einsum('stu,ughk->gskth', place, kern, precision=HIGHEST).reshape(G, L * H, L * H)
    rev_re = pw_re[L - 1::-1][:, :, None, :]
    rev_im = pw_im[L - 1::-1][:, :, None, :]
    bbt_re = jnp.transpose(bb_re, (0, 2, 1))[None]
    bbt_im = jnp.transpose(bb_im, (0, 2, 1))[None]
    bst_re = jnp.transpose(rev_re * bbt_re - rev_im * bbt_im, (1, 0, 2, 3)).reshape(G, L * H, P)
    bst_im = jnp.transpose(rev_re * bbt_im + rev_im * bbt_re, (1, 0, 2, 3)).reshape(G, L * H, P)
    cst_re = jnp.transpose(cp_re[1:], (1, 3, 0, 2)).reshape(G, P, L * H)
    cst_im = -jnp.transpose(cp_im[1:], (1, 3, 0, 2)).reshape(G, P, L * H)
    a_chunk = jnp.stack([jnp.concatenate([pw_re[L], pw_re[L]], axis=-1),
                         jnp.concatenate([-pw_im[L], pw_im[L]], axis=-1)], axis=1)
    d_tab = jnp.tile(d_skip.astype(F32), (1, L)).reshape(G, 1, L * H)
    bst = jnp.concatenate([bst_re, bst_im], axis=-1)
    bst_swapped = jnp.concatenate([bst_im, bst_re], axis=-1)
    cst = jnp.concatenate([cst_re, cst_im], axis=1)
    return m_tab.astype(BF16), bst.astype(BF16), bst_swapped.astype(BF16), cst.astype(BF16), a_chunk, d_tab


def _gelu_tanh(x):
    return 0.5 * x * (1.0 + jnp.tanh(math.sqrt(2.0 / math.pi) * (x + 0.044715 * (x * x * x))))


def _lane_block_transpose(arrs):
    n = len(arrs)
    width = arrs[0].shape[1]
    blk_id = lax.broadcasted_iota(I32, arrs[0].shape, 1) // SSM_GROUP_CH
    k = n // 2
    while k >= 1:
        keep = (blk_id & k) == 0
        nxt = list(arrs)
        for r in range(n):
            if r & k == 0:
                a, b = arrs[r], arrs[r + k]
                nxt[r] = jnp.where(keep, a, pltpu.roll(b, k * SSM_GROUP_CH, axis=1))
                nxt[r + k] = jnp.where(keep, pltpu.roll(a, width - k * SSM_GROUP_CH, axis=1), b)
        arrs = nxt
        k //= 2
    return arrs


def _s5_kernel(u_ref, m_ref, bst_ref, bsts_ref, cst_ref, a_ref, d_ref, o_ref,
               us_scr, z_scr, zs_scr, y_scr, st_scr, *, batch):
    L, H = SSM_CHUNK, SSM_GROUP_CH
    n_grp = us_scr.shape[0]
    R = us_scr.shape[1]
    half = LANES // H

    @pl.when(pl.program_id(1) == 0)
    def _():
        st_scr[...] = jnp.zeros(st_scr.shape, F32)

    for hh in range(L // half):
        slabs = [pltpu.bitcast(u_ref[pl.ds(hh * half + s, R, stride=L), :].astype(BF16), I32) for s in range(half)]
        for gi, arr in enumerate(_lane_block_transpose(slabs)):
            us_scr[gi, :, hh * LANES:(hh + 1) * LANES] = pltpu.bitcast(arr, BF16)
    for gi in range(n_grp):
        u = us_scr[gi]
        z_scr[gi] = jnp.dot(u, bst_ref[gi], preferred_element_type=F32)
        zs_scr[gi] = jnp.dot(u, bsts_ref[gi], preferred_element_type=F32)
        y_scr[gi] = jnp.dot(u, m_ref[gi], preferred_element_type=F32) + u.astype(F32) * d_ref[gi]

    def step(c, state):
        sl = pl.ds(pl.multiple_of(c * batch, batch), batch)
        out = []
        for gi in range(n_grp):
            x, xs = state[2 * gi], state[2 * gi + 1]
            p, q = a_ref[gi, 0:1, :], a_ref[gi, 1:2, :]
            z = z_scr[gi, sl, :]
            z_scr[gi, sl, :] = x
            out += [p * x + q * xs + z, p * xs - q * x + zs_scr[gi, sl, :]]
        return tuple(out)

    state = lax.fori_loop(0, R // batch, step, tuple(st_scr[k] for k in range(2 * n_grp)))
    for k in range(2 * n_grp):
        st_scr[k] = state[k]

    for gi in range(n_grp):
        y = y_scr[gi] + jnp.dot(z_scr[gi].astype(BF16), cst_ref[gi], preferred_element_type=F32)
        y_scr[gi] = _gelu_tanh(y)
    for hh in range(L // half):
        cols = [pltpu.bitcast(y_scr[gi, :, hh * LANES:(hh + 1) * LANES].astype(BF16), I32) for gi in range(n_grp)]
        for s, arr in enumerate(_lane_block_transpose(cols)):
            o_ref[pl.ds(hh * half + s, R, stride=L), :] = pltpu.bitcast(arr, BF16).astype(F32)


def _s5_branch(u, tables, batch, seq):
    L, G, H, P = SSM_CHUNK, SSM_GROUPS, SSM_GROUP_CH, SSM_STATE
    n_chunks = seq // L
    gpt = LANES // H
    cr = S5_CHUNKS_PER_STEP
    R = cr * batch
    LH = L * H
    m_tab, bst, bst_swapped, cst, a_chunk, d_tab = tables
    tile = lambda o, c: (o, 0, 0)
    return pl.pallas_call(
        functools.partial(_s5_kernel, batch=batch),
        grid=(G // gpt, n_chunks // cr),
        in_specs=[pl.BlockSpec((R * L, LANES), lambda o, c: (c, o)),
                  pl.BlockSpec((gpt, LH, LH), tile),
                  pl.BlockSpec((gpt, LH, 2 * P), tile),
                  pl.BlockSpec((gpt, LH, 2 * P), tile),
                  pl.BlockSpec((gpt, 2 * P, LH), tile),
                  pl.BlockSpec((gpt, 2, 2 * P), tile),
                  pl.BlockSpec((gpt, 1, LH), tile)],
        out_specs=pl.BlockSpec((R * L, LANES), lambda o, c: (c, o)),
        out_shape=jax.ShapeDtypeStruct(u.shape, F32),
        scratch_shapes=[pltpu.VMEM((gpt, R, LH), BF16), pltpu.VMEM((gpt, R, 2 * P), F32),
                        pltpu.VMEM((gpt, R, 2 * P), F32), pltpu.VMEM((gpt, R, LH), F32),
                        pltpu.VMEM((2 * gpt, batch, 2 * P), F32)],
        compiler_params=_params("parallel", "arbitrary"),
        name="s5",
    )(u, m_tab, bst, bst_swapped, cst, a_chunk, d_tab)


def _merge_kernel(x_ref, ya_ref, ys_ref, gs_ref, mod_ref, wglu_ref, pa_ref, ps_ref, wout_ref, g2_ref,
                  wr_ref, br_ref, x1_ref, h2_ref, ri_ref, rw_ref, cnt_ref):
    tm, D = x_ref.shape
    ys = ys_ref[...].reshape(tm, SSM_WIDTH).astype(BF16)
    gl = jnp.dot(ys, wglu_ref[...], preferred_element_type=F32)
    y_ssm = gl[:, :SSM_WIDTH] * jax.nn.sigmoid(gl[:, SSM_WIDTH:])
    p_attn = jnp.dot(ya_ref[...], pa_ref[...], preferred_element_type=F32)
    p_ssm = jnp.dot(y_ssm.astype(BF16), ps_ref[...], preferred_element_type=F32)
    merged = gs_ref[:, :D].astype(F32) * p_attn + gs_ref[:, D:].astype(F32) * p_ssm
    mixed = jnp.dot(merged.astype(BF16), wout_ref[...], preferred_element_type=F32)
    x1 = x_ref[...] + mod_ref[2:3, :] * mixed
    x1_ref[...] = x1
    h2 = _rms(x1, NORM_EPS) * g2_ref[...] * (1.0 + mod_ref[4:5, :]) + mod_ref[3:4, :]
    h2_hi = h2.astype(BF16)
    h2_ref[...] = h2_hi

    h2_lo = (h2 - h2_hi.astype(F32)).astype(BF16)
    hi_prod = jnp.dot(h2_hi, wr_ref[...], preferred_element_type=F32)
    lo_prod = jnp.dot(h2_lo, wr_ref[:, 0:ROUTER_LANES], preferred_element_type=F32)
    lg_tok = hi_prod[:, 0:ROUTER_LANES] + hi_prod[:, ROUTER_LANES:] + lo_prod
    logits = lg_tok.T[0:ROUTER_ROWS, :] + br_ref[...]
    NG, EPG = MOE_GROUPS, MOE_EXPERTS_PER_GROUP
    lg = logits[0:NG, :]
    g_iota = lax.broadcasted_iota(I32, lg.shape, 0)
    lg_max = jnp.max(lg, axis=0, keepdims=True)
    grp = jnp.min(jnp.where(lg == lg_max, g_iota, NG), axis=0, keepdims=True)
    p_grp = 1.0 / jnp.sum(jnp.exp(lg - lg_max), axis=0, keepdims=True)
    le = logits[NG:NG + EPG, :]
    for g in range(1, NG):
        le = jnp.where(grp == g, logits[NG + g * EPG:NG + (g + 1) * EPG, :], le)
    e_iota = lax.broadcasted_iota(I32, le.shape, 0)
    v1 = jnp.max(le, axis=0, keepdims=True)
    i1 = jnp.min(jnp.where(le == v1, e_iota, EPG), axis=0, keepdims=True)
    le2 = jnp.where(e_iota == i1, -jnp.inf, le)
    v2 = jnp.max(le2, axis=0, keepdims=True)
    i2 = jnp.min(jnp.where(le2 == v2, e_iota, EPG), axis=0, keepdims=True)
    e21 = jnp.exp(v2 - v1)
    w1 = p_grp / (1.0 + e21)
    w2 = p_grp * e21 / (1.0 + e21)
    eid1 = grp * EPG + i1
    eid2 = grp * EPG + i2

    x_iota = lax.broadcasted_iota(I32, (MOE_EXPERTS, tm), 0)
    hot1 = x_iota == eid1
    hot2 = x_iota == eid2
    hot = jnp.logical_or(hot1, hot2).astype(F32)
    before = (lax.broadcasted_iota(I32, (tm, tm), 0) < lax.broadcasted_iota(I32, (tm, tm), 1))
    prior = jnp.dot(hot.astype(BF16), before.astype(BF16), preferred_element_type=F32)
    rank1 = jnp.sum(jnp.where(hot1, prior, 0.0), axis=0, keepdims=True)
    rank2 = jnp.sum(jnp.where(hot2, prior, 0.0), axis=0, keepdims=True)
    cnt_ref[...] = jnp.sum(hot, axis=1, keepdims=True).astype(I32)

    zi = jnp.zeros((4, tm), I32)
    ri_ref[...] = jnp.concatenate([eid1, eid2, rank1.astype(I32), rank2.astype(I32), zi], axis=0)
    rw_ref[...] = jnp.concatenate([w1, w2, jnp.zeros((6, tm), F32)], axis=0)


def _merge_route(x2, ya, ys, gs, mod, w_glu, w_pa, w_ps, w_out, norm2_g, w_rg, b_rg, w_re, b_re, seq):
    T, D = x2.shape
    tm = TOKEN_TILE
    per_b = seq // tm
    wr = jnp.concatenate([w_rg, jnp.transpose(w_re, (1, 0, 2)).reshape(D, MOE_EXPERTS),
                          jnp.zeros((D, ROUTER_LANES - MOE_GROUPS - MOE_EXPERTS), F32)], axis=1).astype(F32)
    wr_hi = wr.astype(BF16)
    wr_lo = (wr - wr_hi.astype(F32)).astype(BF16)
    br = jnp.concatenate([b_rg, b_re.reshape(-1),
                          jnp.zeros((ROUTER_ROWS - MOE_GROUPS - MOE_EXPERTS,), F32)]).reshape(ROUTER_ROWS, 1)
    row = lambda i: (i, 0)
    col = lambda i: (0, i)
    full = lambda i: (0, 0)
    return pl.pallas_call(
        _merge_kernel,
        grid=(T // tm,),
        in_specs=[pl.BlockSpec((tm, D), row),
                  pl.BlockSpec((tm, ATTN_WIDTH), row),
                  _chunk_major_spec(tm, per_b),
                  pl.BlockSpec((tm, 2 * D), row),
                  pl.BlockSpec((None, 6, D), lambda i: (i // per_b, 0, 0)),
                  pl.BlockSpec(w_glu.shape, full),
                  pl.BlockSpec(w_pa.shape, full),
                  pl.BlockSpec(w_ps.shape, full),
                  pl.BlockSpec(w_out.shape, full),
                  pl.BlockSpec((1, D), full),
                  pl.BlockSpec((D, 2 * ROUTER_LANES), full),
                  pl.BlockSpec((ROUTER_ROWS, 1), full)],
        out_specs=[pl.BlockSpec((tm, D), row), pl.BlockSpec((tm, D), row),
                   pl.BlockSpec((8, tm), col), pl.BlockSpec((8, tm), col),
                   pl.BlockSpec((None, MOE_EXPERTS, 1), lambda i: (i, 0, 0))],
        out_shape=[jax.ShapeDtypeStruct((T, D), F32), jax.ShapeDtypeStruct((T, D), BF16),
                   jax.ShapeDtypeStruct((8, T), I32), jax.ShapeDtypeStruct((8, T), F32),
                   jax.ShapeDtypeStruct((T // tm, MOE_EXPERTS, 1), I32)],
        compiler_params=_params("parallel"),
        name="merge_route",
    )(x2, ya, ys, gs, mod, w_glu.astype(BF16), w_pa.astype(BF16), w_ps.astype(BF16), w_out.astype(BF16),
      norm2_g.reshape(1, D), jnp.concatenate([wr_hi, wr_lo], axis=1), br)


def _pack_pairs(x):
    W = x.shape[1] // 2
    lo = lax.bitcast_convert_type(x[:, :W], I32)
    hi = lax.bitcast_convert_type(x[:, W:], I32)
    return lax.shift_right_logical(lo, 16) | hi


def _unpack_pairs(w):
    lo = lax.bitcast_convert_type(lax.shift_left(w, 16), F32)
    hi = lax.bitcast_convert_type(w & jnp.int32(-65536), F32)
    return jnp.concatenate([lo.astype(BF16), hi.astype(BF16)], axis=1)


def _tile_positions(ri_ref, seg_ref):
    tm = ri_ref.shape[1]
    x_iota = lax.broadcasted_iota(I32, (MOE_EXPERTS, tm), 0)
    seg = seg_ref[...].astype(F32)
    pos = []
    for k in range(2):
        start = jnp.sum(jnp.where(x_iota == ri_ref[k:k + 1, :], seg, 0.0), axis=0, keepdims=True)
        pos.append(start + ri_ref[2 + k:3 + k, :].astype(F32))
    return pos


def _segment_copies(meta, tile, make_copy):
    chunks, tile_chunks = meta
    base = tile * TILE_CHUNKS
    n = tile_chunks[tile]

    def issue(k):
        word = chunks[base + k]
        make_copy(word & (2 ** CHUNK_SLOT_BITS - 1), word >> CHUNK_SLOT_BITS)

    def group(g, carry):
        for r in range(CHUNK_ISSUE_UNROLL):
            issue(g * CHUNK_ISSUE_UNROLL + r)
        return carry

    def single(k, carry):
        issue(k)
        return carry

    n_groups = n // CHUNK_ISSUE_UNROLL
    lax.fori_loop(0, n_groups, group, 0)
    lax.fori_loop(n_groups * CHUNK_ISSUE_UNROLL, n, single, 0)


def _wait_chunks(n, wait_chunk, wait_bulk):
    has_bulk = n >= MIN_TILE_CHUNKS

    @pl.when(has_bulk)
    def _():
        wait_bulk()

    def body(c, carry):
        wait_chunk()
        return carry
    lax.fori_loop(jnp.where(has_bulk, MIN_TILE_CHUNKS, 0), n, body, 0)


def _dispatch_kernel(chunks, tile_chunks, pad_row, pad_chunks, n_used,
                     h_ref, ri_ref, rw_ref, seg_ref, xs_ref, pw_ref, zbuf, zeros_scr, sem, pad_sem, tail_sem):
    i = pl.program_id(0)
    n_tiles = pl.num_programs(0)
    slot = i % 2
    tm = h_ref.shape[0]
    pos1, pos2 = _tile_positions(ri_ref, seg_ref)
    pw_ref[...] = jnp.concatenate([pos1, pos2, rw_ref[0:2, :], jnp.zeros((4, tm), F32)], axis=0)
    r_iota = lax.broadcasted_iota(I32, (TILE_SLOTS, tm), 0).astype(F32)
    onehot = jnp.logical_or(r_iota == pos1, r_iota == pos2).astype(BF16)
    packed = _pack_pairs(jnp.dot(onehot, h_ref[...], preferred_element_type=F32))
    zbuf[slot] = packed.reshape(zbuf.shape[1:])

    def chunk_copy(buf_slot, src, dst):
        return pltpu.make_async_copy(zbuf.at[buf_slot, src], xs_ref.at[dst], sem.at[buf_slot])

    _segment_copies((chunks, tile_chunks), i, lambda src, dst: chunk_copy(slot, src, dst).start())

    def wait_tile(tile, buf_slot):
        bulk = pltpu.make_async_copy(zbuf.at[buf_slot, pl.ds(0, MIN_TILE_CHUNKS)],
                                     xs_ref.at[pl.ds(0, MIN_TILE_CHUNKS)], sem.at[buf_slot])
        _wait_chunks(tile_chunks[tile], chunk_copy(buf_slot, 0, 0).wait, bulk.wait)

    @pl.when(i > 0)
    def _():
        wait_tile(i - 1, 1 - slot)

    @pl.when(i == n_tiles - 1)
    def _():
        zeros_scr[...] = jnp.zeros(zeros_scr.shape, I32)

        block_chunks = zeros_scr.shape[0]

        def pad_copy(dst):
            return pltpu.make_async_copy(zeros_scr.at[0], xs_ref.at[dst], pad_sem)

        def tail_copy(dst):
            return pltpu.make_async_copy(zeros_scr, xs_ref.at[pl.ds(dst, block_chunks)], tail_sem)

        def per_expert(e, total):
            def per_chunk(c, carry):
                pad_copy(pad_row[e] + c).start()
                return carry
            lax.fori_loop(0, pad_chunks[e], per_chunk, 0)
            return total + pad_chunks[e]

        n_pad_copies = lax.fori_loop(0, MOE_EXPERTS, per_expert, 0)
        n_blocks = xs_ref.shape[0] // block_chunks

        def tail_start(b, carry):
            tail_copy(b * block_chunks).start()
            return carry
        lax.fori_loop(n_used[0], n_blocks, tail_start, 0)
        wait_tile(i, slot)

        def wait_pad(c, carry):
            pad_copy(0).wait()
            return carry
        lax.fori_loop(0, n_pad_copies, wait_pad, 0)

        def wait_tail(b, carry):
            tail_copy(0).wait()
            return carry
        lax.fori_loop(n_used[0], n_blocks, wait_tail, 0)


def _dispatch(meta, h2, ri, rw, seg_start, n_rows):
    T, D = h2.shape
    tm = TOKEN_TILE
    col = lambda i, *_: (0, i)
    return pl.pallas_call(
        _dispatch_kernel,
        grid_spec=pltpu.PrefetchScalarGridSpec(
            num_scalar_prefetch=5,
            grid=(T // tm,),
            in_specs=[pl.BlockSpec((tm, D), lambda i, *_: (i, 0)),
                      pl.BlockSpec((8, tm), col),
                      pl.BlockSpec((8, tm), col),
                      pl.BlockSpec((None, MOE_EXPERTS, 1), lambda i, *_: (i, 0, 0))],
            out_specs=[pl.BlockSpec(memory_space=pl.ANY), pl.BlockSpec((8, tm), col)],
            scratch_shapes=[pltpu.VMEM((2, TILE_CHUNKS, SEG_ALIGN, D // 2), I32),
                            pltpu.VMEM((MOE_BLOCK // SEG_ALIGN, SEG_ALIGN, D // 2), I32),
                            pltpu.SemaphoreType.DMA((2,)), pltpu.SemaphoreType.DMA(()),
                            pltpu.SemaphoreType.DMA(())]),
        out_shape=[jax.ShapeDtypeStruct((n_rows // SEG_ALIGN, SEG_ALIGN, D // 2), I32),
                   jax.ShapeDtypeStruct((8, T), F32)],
        compiler_params=_params("arbitrary"),
        name="moe_dispatch",
    )(*meta, h2, ri, rw, seg_start)


def _expert_kernel(be_ref, nb_ref, x_ref, wi_ref, wo_ref, o_ref, wi_bf, wo_bf):
    i = pl.program_id(0)
    F = wo_ref.shape[0]
    in_use = i < nb_ref[0]

    @pl.when(jnp.logical_and(in_use, jnp.logical_or(i == 0, be_ref[i] != be_ref[jnp.maximum(i - 1, 0)])))
    def _():
        wi_bf[...] = wi_ref[...].astype(BF16)
        wo_bf[...] = wo_ref[...].astype(BF16)

    @pl.when(in_use)
    def _():
        rows, half_d = o_ref.shape[0] * o_ref.shape[1], o_ref.shape[2]
        hid = jnp.dot(_unpack_pairs(x_ref[...].reshape(rows, half_d)), wi_bf[...], preferred_element_type=F32)
        a = hid[:, :F]
        act = a * jax.nn.sigmoid(a) * hid[:, F:]
        y = jnp.dot(act.astype(BF16), wo_bf[...], preferred_element_type=F32)
        o_ref[...] = _pack_pairs(y.astype(BF16).astype(F32)).reshape(o_ref.shape)

    @pl.when(jnp.logical_not(in_use))
    def _():
        o_ref[...] = jnp.zeros(o_ref.shape, I32)


def _experts(block_e, n_used, xs, w_e_in, w_e_out):
    n_chunks, _, half_d = xs.shape
    D = 2 * half_d
    F = w_e_out.shape[1]
    block_chunks = MOE_BLOCK // SEG_ALIGN
    blk = lambda i, be, nb: jnp.maximum(jnp.minimum(i, nb[0] - 1), 0)
    return pl.pallas_call(
        _expert_kernel,
        grid_spec=pltpu.PrefetchScalarGridSpec(
            num_scalar_prefetch=2,
            grid=(n_chunks // block_chunks,),
            in_specs=[pl.BlockSpec((block_chunks, SEG_ALIGN, half_d), lambda i, be, nb: (blk(i, be, nb), 0, 0)),
                      pl.BlockSpec((None, D, 2 * F), lambda i, be, nb: (be[blk(i, be, nb)], 0, 0)),
                      pl.BlockSpec((None, F, D), lambda i, be, nb: (be[blk(i, be, nb)], 0, 0))],
            out_specs=pl.BlockSpec((block_chunks, SEG_ALIGN, half_d), lambda i, be, nb: (i, 0, 0)),
            scratch_shapes=[pltpu.VMEM((D, 2 * F), BF16), pltpu.VMEM((F, D), BF16)]),
        out_shape=jax.ShapeDtypeStruct(xs.shape, I32),
        compiler_params=_params("arbitrary"),
        name="experts",
    )(block_e, n_used, xs, w_e_in, w_e_out)


def _final_kernel(chunks, tile_chunks, x1_ref, pw_ref, mod_ref, g_ref, ys_ref, o_ref, ybuf, sem):
    i = pl.program_id(0)
    n_tiles = pl.num_programs(0)
    slot = i % 2
    tm = x1_ref.shape[0]

    def chunk_copy(buf_slot, src, dst):
        return pltpu.make_async_copy(ys_ref.at[dst], ybuf.at[buf_slot, src], sem.at[buf_slot])

    def fetch(tile, buf_slot):
        _segment_copies((chunks, tile_chunks), tile, lambda src, dst: chunk_copy(buf_slot, src, dst).start())

    @pl.when(i == 0)
    def _():
        ybuf[...] = jnp.zeros(ybuf.shape, I32)
        fetch(0, 0)

    @pl.when(i + 1 < n_tiles)
    def _():
        fetch(i + 1, 1 - slot)

    bulk = pltpu.make_async_copy(ys_ref.at[pl.ds(0, MIN_TILE_CHUNKS)],
                                 ybuf.at[slot, pl.ds(0, MIN_TILE_CHUNKS)], sem.at[slot])
    _wait_chunks(tile_chunks[i], chunk_copy(slot, 0, 0).wait, bulk.wait)

    s_iota = lax.broadcasted_iota(I32, (tm, TILE_SLOTS), 1).astype(F32)
    comb = (jnp.where(s_iota == pw_ref[:, 0:1], pw_ref[:, 2:3], 0.0)
            + jnp.where(s_iota == pw_ref[:, 1:2], pw_ref[:, 3:4], 0.0))
    sorted_rows = _unpack_pairs(ybuf[slot].reshape(TILE_SLOTS, ybuf.shape[3]))
    moe = jnp.dot(comb.astype(BF16), sorted_rows, preferred_element_type=F32)
    x2 = x1_ref[...] + mod_ref[5:6, :] * moe
    o_ref[...] = _rms(x2, NORM_EPS) * g_ref[...]


def _final(meta, x1, pw_tok, mod, final_g, ys, seq):
    T, D = x1.shape
    tm = TOKEN_TILE
    per_b = seq // tm
    row = lambda i, *_: (i, 0)
    return pl.pallas_call(
        _final_kernel,
        grid_spec=pltpu.PrefetchScalarGridSpec(
            num_scalar_prefetch=2,
            grid=(T // tm,),
            in_specs=[pl.BlockSpec((tm, D), row),
                      pl.BlockSpec((tm, 8), row),
                      pl.BlockSpec((None, 6, D), lambda i, *_: (i // per_b, 0, 0)),
                      pl.BlockSpec((1, D), lambda i, *_: (0, 0)),
                      pl.BlockSpec(memory_space=pl.ANY)],
            out_specs=pl.BlockSpec((tm, D), row),
            scratch_shapes=[pltpu.VMEM((2, TILE_CHUNKS, SEG_ALIGN, D // 2), I32), pltpu.SemaphoreType.DMA((2,))]),
        out_shape=jax.ShapeDtypeStruct((T, D), F32),
        compiler_params=_params("arbitrary"),
        name="final",
    )(*meta, x1, pw_tok, mod, final_g.reshape(1, D), ys)


def _round_up(x, m):
    return (x + m - 1) // m * m


def _moe_layout(tile_counts):
    n_tiles = tile_counts.shape[0]
    seg = _round_up(tile_counts, SEG_ALIGN)
    seg_start = jnp.cumsum(seg, axis=1) - seg
    tile_base = jnp.cumsum(seg, axis=0) - seg
    used = jnp.sum(seg, axis=0)
    region = _round_up(used, MOE_BLOCK)
    region_end = jnp.cumsum(region)
    region_start = region_end - region
    dst_row = region_start[None, :] + tile_base
    n_chunk = seg // SEG_ALIGN
    n_assign = 2 * n_tiles * TOKEN_TILE
    n_rows = _round_up(n_assign + n_tiles * MOE_EXPERTS * (SEG_ALIGN - 1) + MOE_EXPERTS * (MOE_BLOCK - 1), MOE_BLOCK)
    block_start = jnp.arange(n_rows // MOE_BLOCK, dtype=I32) * MOE_BLOCK
    block_e = jnp.minimum(jnp.sum(block_start[:, None] >= region_end[None, :], axis=1), MOE_EXPERTS - 1)
    i32 = lambda a: a.reshape(-1).astype(I32)
    chunk_end = jnp.cumsum(n_chunk, axis=1)
    chunk_start = chunk_end - n_chunk
    k = jnp.arange(TILE_CHUNKS)[None, :, None]
    mine = (chunk_start[:, None, :] <= k) & (k < chunk_end[:, None, :])
    pick = lambda a: jnp.sum(jnp.where(mine, (a // SEG_ALIGN - chunk_start)[:, None, :] + k, 0), axis=2)
    word = pick(seg_start) | (pick(dst_row) << CHUNK_SLOT_BITS)
    meta = (i32(word), i32(chunk_end[:, -1]))
    pad = (i32((region_start + used) // SEG_ALIGN), i32((region - used) // SEG_ALIGN))
    n_used = (region_end[-1:] // MOE_BLOCK).astype(I32)
    return meta, pad, seg_start.astype(I32)[:, :, None], block_e.astype(I32), n_used, n_rows


def kernel(x, c, w_ada, b_ada, norm1_g, w_in, rel_bias, lambda_q1, lambda_k1, lambda_q2, lambda_k2, subln_g, ssm_lambda_re, ssm_lambda_im, ssm_log_step, ssm_b_re, ssm_b_im, ssm_c_re, ssm_c_im, ssm_d, w_glu, w_proj_attn, w_proj_ssm, w_out, norm2_g, w_router_group, b_router_group, w_router_expert, b_router_expert, w_expert_in, w_expert_out, final_g):
    B, S, D = x.shape
    T = B * S
    x2 = x.reshape(T, D)
    mod = _ada_mod(c, w_ada[0], b_ada[0]).reshape(B, 6, D)
    q, k, v, u, gs = _in_proj(x2, mod, norm1_g[0], w_in[0], S)
    lam_vecs = jnp.stack([lambda_q1[0], lambda_k1[0], lambda_q2[0], lambda_k2[0]]).astype(F32)
    y_attn = _diff_attn(q, k, v, rel_bias, lam_vecs, subln_g[0], B, S)
    tables = _s5_tables(ssm_lambda_re[0], ssm_lambda_im[0], ssm_log_step[0], ssm_b_re[0], ssm_b_im[0],
                        ssm_c_re[0], ssm_c_im[0], ssm_d[0])
    y_s5 = _s5_branch(u.reshape(-1, SSM_WIDTH), tables, B, S).reshape(u.shape)
    x1, h2, ri, rw, tile_counts = _merge_route(
        x2, y_attn, y_s5, gs, mod, w_glu[0], w_proj_attn[0], w_proj_ssm[0], w_out[0], norm2_g[0],
        w_router_group[0], b_router_group[0], w_router_expert[0], b_router_expert[0], S)
    meta, pad, seg_start, block_e, n_used, n_rows = _moe_layout(tile_counts[:, :, 0])
    xs, pw = _dispatch(meta + pad + (n_used,), h2, ri, rw, seg_start, n_rows)
    ys = _experts(block_e, n_used, xs, w_expert_in[0], w_expert_out[0])
    out = _final(meta, x1, pw.T, mod, final_g, ys, S)
    return out.reshape(B, S, D)
```

```python
import functools
import math

import jax
import jax.numpy as jnp
from jax import lax
from jax.experimental import pallas as pl
from jax.experimental.pallas import tpu as pltpu

F32 = jnp.float32
BF16 = jnp.bfloat16
I32 = jnp.int32
HIGHEST = lax.Precision.HIGHEST

LANES = 128
MXU_TILE = 256

ATTN_HEADS = 4
ATTN_HEAD_DIM = 64
ATTN_V_DIM = 2 * ATTN_HEAD_DIM
ATTN_WIDTH = ATTN_HEADS * ATTN_V_DIM
NEG_INF = -1e30
REL_BUCKETS = 32
REL_MAX_DISTANCE = 128
SSM_GROUP_CH = 16
SSM_WIDTH = 512
SSM_GROUPS = SSM_WIDTH // SSM_GROUP_CH
SSM_STATE = 64
SSM_EIG_MAX_RE = -1e-4
SSM_CHUNK = 16
S5_CHUNKS_PER_STEP = 32
MOE_GROUPS = 4
MOE_EXPERTS_PER_GROUP = 8
MOE_EXPERTS = MOE_GROUPS * MOE_EXPERTS_PER_GROUP
MOE_BLOCK = 512
SEG_ALIGN = 8
NORM_EPS = 1e-6
SUBLN_EPS = 1e-5
LAMBDA_INIT = 0.8 - 0.6 * math.exp(-0.3 * 0)

ATTN_BLOCK = 256
ATTN_HEADS_PER_STEP = 4
ATTN_ROW_CHUNK = 32
ATTN_ONES_ROWS = 16
LOG2E = math.log2(math.e)
TOKEN_TILE = 512
ROUTER_ROWS = 40
ROUTER_LANES = LANES
TILE_SLOTS = -(-(2 * TOKEN_TILE + MOE_EXPERTS * (SEG_ALIGN - 1)) // MXU_TILE) * MXU_TILE
TILE_CHUNKS = TILE_SLOTS // SEG_ALIGN
MIN_TILE_CHUNKS = 2 * TOKEN_TILE // SEG_ALIGN
CHUNK_ISSUE_UNROLL = 8
CHUNK_SLOT_BITS = 8
assert TILE_CHUNKS <= 2 ** CHUNK_SLOT_BITS
VMEM_LIMIT = 56 << 20


def _params(*sem):
    return pltpu.CompilerParams(dimension_semantics=sem, vmem_limit_bytes=VMEM_LIMIT)


def _rms(x, eps):
    return x * lax.rsqrt(jnp.mean(x * x, axis=-1, keepdims=True) + eps)


def _mod_kernel(c_ref, w_ref, b_ref, o_ref):
    c = c_ref[...]
    c_act = c * jax.nn.sigmoid(c)
    o_ref[...] = jnp.dot(c_act, w_ref[...], preferred_element_type=F32, precision=HIGHEST) + b_ref[...]


def _ada_mod(c, w_ada, b_ada):
    B, D = c.shape
    N = w_ada.shape[1]
    tn = 1024
    return pl.pallas_call(
        _mod_kernel,
        grid=(N // tn,),
        in_specs=[pl.BlockSpec((B, D), lambda j: (0, 0)),
                  pl.BlockSpec((D, tn), lambda j: (0, j)),
                  pl.BlockSpec((1, tn), lambda j: (0, j))],
        out_specs=pl.BlockSpec((B, tn), lambda j: (0, j)),
        out_shape=jax.ShapeDtypeStruct((B, N), F32),
        compiler_params=_params("arbitrary"),
        name="ada_mod",
    )(c, w_ada, b_ada.reshape(1, N))


def _proj_kernel(x_ref, mod_ref, g_ref, w_ref, q_ref, k_ref, v_ref, u_ref, gs_ref):
    y = _rms(x_ref[...], NORM_EPS) * g_ref[...]
    h = (y * (1.0 + mod_ref[1:2, :]) + mod_ref[0:1, :]).astype(BF16)
    W = ATTN_WIDTH

    def proj(lo, hi):
        return jnp.dot(h, w_ref[:, lo:hi], preferred_element_type=F32)

    q_ref[...] = (proj(0, W) * (ATTN_HEAD_DIM ** -0.5 * LOG2E)).astype(BF16)
    k_ref[...] = proj(W, 2 * W).astype(BF16)
    v_ref[...] = proj(2 * W, 3 * W).astype(BF16)
    u_ref[...] = proj(3 * W, 3 * W + SSM_WIDTH).reshape(u_ref.shape)
    gs_ref[...] = jax.nn.sigmoid(proj(3 * W + SSM_WIDTH, w_ref.shape[1])).astype(BF16)


def _chunk_major_spec(tm, per_b):
    return pl.BlockSpec((tm // SSM_CHUNK, None, SSM_CHUNK, SSM_WIDTH), lambda i: (i % per_b, i // per_b, 0, 0))


def _in_proj(x2, mod, norm_g, w_in, seq):
    T, D = x2.shape
    tm = TOKEN_TILE
    per_b = seq // tm
    n_gate = w_in.shape[1] - 3 * ATTN_WIDTH - SSM_WIDTH
    row = lambda i: (i, 0)
    return pl.pallas_call(
        _proj_kernel,
        grid=(T // tm,),
        in_specs=[pl.BlockSpec((tm, D), row),
                  pl.BlockSpec((None, 6, D), lambda i: (i // per_b, 0, 0)),
                  pl.BlockSpec((1, D), lambda i: (0, 0)),
                  pl.BlockSpec(w_in.shape, lambda i: (0, 0))],
        out_specs=[pl.BlockSpec((tm, ATTN_WIDTH), row)] * 3
        + [_chunk_major_spec(tm, per_b), pl.BlockSpec((tm, n_gate), row)],
        out_shape=[jax.ShapeDtypeStruct((T, ATTN_WIDTH), BF16)] * 3
        + [jax.ShapeDtypeStruct((seq // SSM_CHUNK, T // seq, SSM_CHUNK, SSM_WIDTH), F32),
           jax.ShapeDtypeStruct((T, n_gate), BF16)],
        compiler_params=_params("parallel"),
        name="in_proj",
    )(x2, mod, norm_g.reshape(1, D), w_in.astype(BF16))


def _rel_bucket(dist):
    max_exact = REL_BUCKETS // 2
    n = jnp.maximum(dist, 0)
    log_ratio = jnp.log(jnp.maximum(n, 1).astype(F32) / max_exact) / math.log(REL_MAX_DISTANCE / max_exact)
    large = max_exact + (log_ratio * (REL_BUCKETS - max_exact)).astype(I32)
    large = jnp.minimum(large, REL_BUCKETS - 1)
    return jnp.where(n < max_exact, n, large)


def _attn_bias_tiles(rel_bias, blk):
    assert blk >= REL_MAX_DISTANCE
    n_heads = rel_bias.shape[1]
    far = rel_bias[REL_BUCKETS - 1].astype(F32)
    m = jnp.arange(2 * blk)
    signed = jnp.where(m < blk, m, m - 2 * blk)
    tiles = []
    for kind in range(2):
        dist = kind * blk + signed
        tab = jnp.where(dist >= 0, (rel_bias[_rel_bucket(dist)].astype(F32).T - far[:, None]) * LOG2E, NEG_INF)
        skew = jnp.tile(tab, (1, blk))[:, :blk * (2 * blk - 1)].reshape(n_heads, blk, 2 * blk - 1)
        tiles.append(skew[:, :, :blk])
    return jnp.stack(tiles, axis=1)


def _attn_kernel(lam_ref, q_ref, k_ref, v_ref, bias_ref, g_ref, o_ref, vt_scr, *bufs, blk, heads):
    i = pl.program_id(2)
    n_kv = vt_scr.shape[1]
    V = ATTN_V_DIM
    ns = 2 * heads
    acc = bufs[0:ns]
    sbuf = tuple(bufs[(1 + r) * ns:(2 + r) * ns] for r in range(3))
    pbufs = tuple(bufs[(4 + r) * ns:(5 + r) * ns] for r in range(3))

    @pl.when(i == 0)
    def _():
        for hd in range(heads):
            for jb in range(n_kv):
                vt_scr[hd, jb, 0:V, :] = v_ref[jb * blk:(jb + 1) * blk, hd * V:(hd + 1) * V].astype(F32).T.astype(BF16)
                vt_scr[hd, jb, V:, :] = jnp.ones((ATTN_ONES_ROWS, blk), BF16)

    qt_maps = []
    for hd in range(heads):
        qt = q_ref[:, hd * V:(hd + 1) * V].astype(F32).T
        feat = lax.broadcasted_iota(I32, qt.shape, 0)
        qt_maps += [jnp.where(feat < ATTN_HEAD_DIM, qt, 0.0).astype(BF16),
                    jnp.where(feat >= ATTN_HEAD_DIM, qt, 0.0).astype(BF16)]

    def scores(j, st):
        hd = st // 2
        kj = k_ref[pl.ds(pl.multiple_of(j * blk, blk), blk), hd * V:(hd + 1) * V]
        return jnp.dot(kj, qt_maps[st], preferred_element_type=F32)

    n_chunks = blk // ATTN_ROW_CHUNK

    def rows(c):
        return slice(c * ATTN_ROW_CHUNK, (c + 1) * ATTN_ROW_CHUNK)

    def fold8(x):
        return x.reshape(ATTN_ROW_CHUNK // 8, 8, blk)

    def block(j, carry, pos, lookahead, bias_kind):
        src, dst, pbuf = sbuf[pos], sbuf[(pos + 2) % 3], pbufs[pos]
        out = []
        for st in range(ns):
            hd = st // 2
            if lookahead:
                dst[st][...] = scores(j + 2, st)

            def chunk(c):
                s = src[st][rows(c), :]
                return s if bias_kind is None else s + bias_ref[hd, bias_kind, rows(c), :]

            m_old = carry[st]
            m8 = jnp.max(fold8(chunk(0)), axis=0)
            for c in range(1, n_chunks):
                m8 = jnp.maximum(m8, jnp.max(fold8(chunk(c)), axis=0))
            m_new = jnp.maximum(m_old, jnp.max(m8, axis=0, keepdims=True))
            alpha = jnp.exp2(m_old - m_new)
            for c in range(n_chunks):
                pbuf[st][rows(c), :] = jnp.exp2(chunk(c) - m_new).astype(BF16)
            acc[st][...] = alpha * acc[st][...] + jnp.dot(vt_scr[hd, j], pbuf[st][...],
                                                          preferred_element_type=F32)
            out.append(m_new)
        return tuple(out)

    def far_triple(t, carry):
        for r in range(3):
            carry = block(3 * t + r, carry, r, True, None)
        return carry

    def far_single(j, carry):
        carry = block(j, carry, 0, True, None)
        for st in range(ns):
            sbuf[0][st][...] = sbuf[1][st][...]
        for st in range(ns):
            sbuf[1][st][...] = sbuf[2][st][...]
        return carry

    def near_pair(_, carry):
        return block(i, block(i - 1, carry, 0, False, 1), 1, False, 0)

    def near_single(_, carry):
        return block(i, carry, 0, False, 0)

    for st in range(ns):
        acc[st][...] = jnp.zeros(acc[st].shape, F32)
        sbuf[0][st][...] = scores(0, st)
        sbuf[1][st][...] = scores(jnp.minimum(i, 1), st)
    m0 = jnp.full((1, blk), -jnp.inf, F32)
    n_far = jnp.maximum(i - 1, 0)
    carry = lax.fori_loop(0, n_far // 3, far_triple, (m0,) * ns)
    carry = lax.fori_loop(n_far - n_far % 3, n_far, far_single, carry)
    carry = lax.fori_loop(0, jnp.minimum(i, 1), near_pair, carry)
    lax.fori_loop(0, 1 - jnp.minimum(i, 1), near_single, carry)

    lam = (jnp.exp(jnp.sum(lam_ref[0:1, :] * lam_ref[1:2, :], axis=-1, keepdims=True))
           - jnp.exp(jnp.sum(lam_ref[2:3, :] * lam_ref[3:4, :], axis=-1, keepdims=True)) + LAMBDA_INIT)
    for hd in range(heads):
        a1, a2 = acc[2 * hd], acc[2 * hd + 1]
        ot = a1[0:V, :] / a1[V:V + 1, :] - lam * (a2[0:V, :] / a2[V:V + 1, :])
        ot = ot * lax.rsqrt(jnp.mean(ot * ot, axis=0, keepdims=True) + SUBLN_EPS)
        o_ref[:, hd * V:(hd + 1) * V] = (ot.T * (g_ref[...] * (1.0 - LAMBDA_INIT))).astype(BF16)


def _diff_attn(q, k, v, rel_bias, lam_vecs, subln_g, batch, seq):
    T = q.shape[0]
    blk = ATTN_BLOCK
    nq = seq // blk
    bias = _attn_bias_tiles(rel_bias, blk)
    hp = ATTN_HEADS_PER_STEP
    ns = 2 * hp
    width = hp * ATTN_V_DIM
    acc_rows = ATTN_V_DIM + ATTN_ONES_ROWS
    return pl.pallas_call(
        functools.partial(_attn_kernel, blk=blk, heads=hp),
        grid=(batch, ATTN_HEADS // hp, nq),
        in_specs=[pl.BlockSpec((4, ATTN_HEAD_DIM), lambda b, h, i: (0, 0)),
                  pl.BlockSpec((blk, width), lambda b, h, i: (b * nq + i, h)),
                  pl.BlockSpec((seq, width), lambda b, h, i: (b, h)),
                  pl.BlockSpec((seq, width), lambda b, h, i: (b, h)),
                  pl.BlockSpec((hp, 2, blk, blk), lambda b, h, i: (h, 0, 0, 0)),
                  pl.BlockSpec((1, ATTN_V_DIM), lambda b, h, i: (0, 0))],
        out_specs=pl.BlockSpec((blk, width), lambda b, h, i: (b * nq + i, h)),
        out_shape=jax.ShapeDtypeStruct((T, ATTN_WIDTH), BF16),
        scratch_shapes=[pltpu.VMEM((hp, nq, acc_rows, blk), BF16)]
        + [pltpu.VMEM((acc_rows, blk), F32)] * ns + [pltpu.VMEM((blk, blk), F32)] * (3 * ns)
        + [pltpu.VMEM((blk, blk), BF16)] * (3 * ns),
        compiler_params=_params("parallel", "parallel", "arbitrary"),
        name="diff_attn",
    )(lam_vecs, q, k, v, bias, subln_g.reshape(1, ATTN_V_DIM))


def _s5_tables(lam_re, lam_im, log_step, b_re, b_im, c_re, c_im, d_skip):
    L = SSM_CHUNK
    G, P = lam_re.shape
    H = SSM_GROUP_CH
    lr = jnp.minimum(lam_re.astype(F32), SSM_EIG_MAX_RE)
    li = lam_im.astype(F32)
    step = jnp.exp(log_step.astype(F32))[:, None]
    mag = jnp.exp(lr * step)
    ang = li * step
    a_re = mag * jnp.cos(ang)
    a_im = mag * jnp.sin(ang)
    den = lr * lr + li * li
    num_re = a_re - 1.0
    coef_re = (num_re * lr + a_im * li) / den
    coef_im = (a_im * lr - num_re * li) / den
    br = b_re.astype(F32)
    bi = b_im.astype(F32)
    bb_re = coef_re[..., None] * br - coef_im[..., None] * bi
    bb_im = coef_re[..., None] * bi + coef_im[..., None] * br
    pw_re, pw_im = [jnp.ones_like(a_re)], [jnp.zeros_like(a_re)]
    for _ in range(L):
        pr, pi = pw_re[-1], pw_im[-1]
        pw_re.append(pr * a_re - pi * a_im)
        pw_im.append(pr * a_im + pi * a_re)
    pw_re = jnp.stack(pw_re)
    pw_im = jnp.stack(pw_im)
    cr = c_re.astype(F32)[None]
    ci = c_im.astype(F32)[None]
    cp_re = cr * pw_re[:, :, None, :] - ci * pw_im[:, :, None, :]
    cp_im = cr * pw_im[:, :, None, :] + ci * pw_re[:, :, None, :]
    kern = jnp.einsum('tghp,gpk->tghk', jnp.concatenate([cp_re[:L], -cp_im[:L]], axis=-1),
                      jnp.concatenate([bb_re, bb_im], axis=1), precision=HIGHEST)
    kern_t = jnp.transpose(kern, (1, 3, 0, 2))
    m_rows = [jnp.pad(kern_t[:, :, :L - s, :], ((0, 0), (0, 0), (s, 0), (0, 0))) for s in range(L)]
    m_tab = jnp.stack(m_rows, axis=1).reshape(G, L * H, L * H)
    rev_re = pw_re[L - 1::-1][:, :, None, :]
    rev_im = pw_im[L - 1::-1][:, :, None, :]
    bbt_re = jnp.transpose(bb_re, (0, 2, 1))[None]
    bbt_im = jnp.transpose(bb_im, (0, 2, 1))[None]
    bst_re = jnp.transpose(rev_re * bbt_re - rev_im * bbt_im, (1, 0, 2, 3)).reshape(G, L * H, P)
    bst_im = jnp.transpose(rev_re * bbt_im + rev_im * bbt_re, (1, 0, 2, 3)).reshape(G, L * H, P)
    cst_re = jnp.transpose(cp_re[1:], (1, 3, 0, 2)).reshape(G, P, L * H)
    cst_im = -jnp.transpose(cp_im[1:], (1, 3, 0, 2)).reshape(G, P, L * H)
    a_chunk = jnp.stack([jnp.concatenate([pw_re[L], pw_re[L]], axis=-1),
                         jnp.concatenate([-pw_im[L], pw_im[L]], axis=-1)], axis=1)
    d_tab = jnp.tile(d_skip.astype(F32), (1, L)).reshape(G, 1, L * H)
    bst = jnp.concatenate([bst_re, bst_im], axis=-1)
    bst_swapped = jnp.concatenate([bst_im, bst_re], axis=-1)
    cst = jnp.concatenate([cst_re, cst_im], axis=1)
    return m_tab.astype(BF16), bst.astype(BF16), bst_swapped.astype(BF16), cst.astype(BF16), a_chunk, d_tab


def _gelu_tanh(x):
    return 0.5 * x * (1.0 + jnp.tanh(math.sqrt(2.0 / math.pi) * (x + 0.044715 * (x * x * x))))


def _lane_block_transpose(arrs):
    n = len(arrs)
    width = arrs[0].shape[1]
    blk_id = lax.broadcasted_iota(I32, arrs[0].shape, 1) // SSM_GROUP_CH
    k = n // 2
    while k >= 1:
        keep = (blk_id & k) == 0
        nxt = list(arrs)
        for r in range(n):
            if r & k == 0:
                a, b = arrs[r], arrs[r + k]
                nxt[r] = jnp.where(keep, a, pltpu.roll(b, k * SSM_GROUP_CH, axis=1))
                nxt[r + k] = jnp.where(keep, pltpu.roll(a, width - k * SSM_GROUP_CH, axis=1), b)
        arrs = nxt
        k //= 2
    return arrs


def _s5_kernel(u_ref, m_ref, bst_ref, bsts_ref, cst_ref, a_ref, d_ref, o_ref,
               us_scr, z_scr, zs_scr, y_scr, st_scr, *, batch):
    L, H = SSM_CHUNK, SSM_GROUP_CH
    n_grp = us_scr.shape[0]
    R = us_scr.shape[1]
    half = LANES // H

    @pl.when(pl.program_id(1) == 0)
    def _():
        st_scr[...] = jnp.zeros(st_scr.shape, F32)

    for hh in range(L // half):
        slabs = [pltpu.bitcast(u_ref[pl.ds(hh * half + s, R, stride=L), :].astype(BF16), I32) for s in range(half)]
        for gi, arr in enumerate(_lane_block_transpose(slabs)):
            us_scr[gi, :, hh * LANES:(hh + 1) * LANES] = pltpu.bitcast(arr, BF16)
    for gi in range(n_grp):
        u = us_scr[gi]
        z_scr[gi] = jnp.dot(u, bst_ref[gi], preferred_element_type=F32)
        zs_scr[gi] = jnp.dot(u, bsts_ref[gi], preferred_element_type=F32)
        y_scr[gi] = jnp.dot(u, m_ref[gi], preferred_element_type=F32) + u.astype(F32) * d_ref[gi]

    def step(c, state):
        sl = pl.ds(pl.multiple_of(c * batch, batch), batch)
        out = []
        for gi in range(n_grp):
            x, xs = state[2 * gi], state[2 * gi + 1]
            p, q = a_ref[gi, 0:1, :], a_ref[gi, 1:2, :]
            z = z_scr[gi, sl, :]
            z_scr[gi, sl, :] = x
            out += [p * x + q * xs + z, p * xs - q * x + zs_scr[gi, sl, :]]
        return tuple(out)

    state = lax.fori_loop(0, R // batch, step, tuple(st_scr[k] for k in range(2 * n_grp)))
    for k in range(2 * n_grp):
        st_scr[k] = state[k]

    for gi in range(n_grp):
        y = y_scr[gi] + jnp.dot(z_scr[gi].astype(BF16), cst_ref[gi], preferred_element_type=F32)
        y_scr[gi] = _gelu_tanh(y)
    for hh in range(L // half):
        cols = [pltpu.bitcast(y_scr[gi, :, hh * LANES:(hh + 1) * LANES].astype(BF16), I32) for gi in range(n_grp)]
        for s, arr in enumerate(_lane_block_transpose(cols)):
            o_ref[pl.ds(hh * half + s, R, stride=L), :] = pltpu.bitcast(arr, BF16).astype(F32)


def _s5_branch(u, tables, batch, seq):
    L, G, H, P = SSM_CHUNK, SSM_GROUPS, SSM_GROUP_CH, SSM_STATE
    n_chunks = seq // L
    gpt = LANES // H
    cr = S5_CHUNKS_PER_STEP
    R = cr * batch
    LH = L * H
    m_tab, bst, bst_swapped, cst, a_chunk, d_tab = tables
    tile = lambda o, c: (o, 0, 0)
    return pl.pallas_call(
        functools.partial(_s5_kernel, batch=batch),
        grid=(G // gpt, n_chunks // cr),
        in_specs=[pl.BlockSpec((R * L, LANES), lambda o, c: (c, o)),
                  pl.BlockSpec((gpt, LH, LH), tile),
                  pl.BlockSpec((gpt, LH, 2 * P), tile),
                  pl.BlockSpec((gpt, LH, 2 * P), tile),
                  pl.BlockSpec((gpt, 2 * P, LH), tile),
                  pl.BlockSpec((gpt, 2, 2 * P), tile),
                  pl.BlockSpec((gpt, 1, LH), tile)],
        out_specs=pl.BlockSpec((R * L, LANES), lambda o, c: (c, o)),
        out_shape=jax.ShapeDtypeStruct(u.shape, F32),
        scratch_shapes=[pltpu.VMEM((gpt, R, LH), BF16), pltpu.VMEM((gpt, R, 2 * P), F32),
                        pltpu.VMEM((gpt, R, 2 * P), F32), pltpu.VMEM((gpt, R, LH), F32),
                        pltpu.VMEM((2 * gpt, batch, 2 * P), F32)],
        compiler_params=_params("parallel", "arbitrary"),
        name="s5",
    )(u, m_tab, bst, bst_swapped, cst, a_chunk, d_tab)


def _merge_kernel(x_ref, ya_ref, ys_ref, gs_ref, mod_ref, wglu_ref, pa_ref, ps_ref, wout_ref, g2_ref,
                  wr_ref, br_ref, x1_ref, h2_ref, ri_ref, rw_ref, cnt_ref):
    tm, D = x_ref.shape
    ys = ys_ref[...].reshape(tm, SSM_WIDTH).astype(BF16)
    gl = jnp.dot(ys, wglu_ref[...], preferred_element_type=F32)
    y_ssm = gl[:, :SSM_WIDTH] * jax.nn.sigmoid(gl[:, SSM_WIDTH:])
    p_attn = jnp.dot(ya_ref[...], pa_ref[...], preferred_element_type=F32)
    p_ssm = jnp.dot(y_ssm.astype(BF16), ps_ref[...], preferred_element_type=F32)
    merged = gs_ref[:, :D].astype(F32) * p_attn + gs_ref[:, D:].astype(F32) * p_ssm
    mixed = jnp.dot(merged.astype(BF16), wout_ref[...], preferred_element_type=F32)
    x1 = x_ref[...] + mod_ref[2:3, :] * mixed
    x1_ref[...] = x1
    h2 = _rms(x1, NORM_EPS) * g2_ref[...] * (1.0 + mod_ref[4:5, :]) + mod_ref[3:4, :]
    h2_hi = h2.astype(BF16)
    h2_ref[...] = h2_hi

    h2_lo = (h2 - h2_hi.astype(F32)).astype(BF16)
    hi_prod = jnp.dot(h2_hi, wr_ref[...], preferred_element_type=F32)
    lo_prod = jnp.dot(h2_lo, wr_ref[:, 0:ROUTER_LANES], preferred_element_type=F32)
    lg_tok = hi_prod[:, 0:ROUTER_LANES] + hi_prod[:, ROUTER_LANES:] + lo_prod
    logits = lg_tok.T[0:ROUTER_ROWS, :] + br_ref[...]
    NG, EPG = MOE_GROUPS, MOE_EXPERTS_PER_GROUP
    lg = logits[0:NG, :]
    g_iota = lax.broadcasted_iota(I32, lg.shape, 0)
    lg_max = jnp.max(lg, axis=0, keepdims=True)
    grp = jnp.min(jnp.where(lg == lg_max, g_iota, NG), axis=0, keepdims=True)
    p_grp = 1.0 / jnp.sum(jnp.exp(lg - lg_max), axis=0, keepdims=True)
    le = logits[NG:NG + EPG, :]
    for g in range(1, NG):
        le = jnp.where(grp == g, logits[NG + g * EPG:NG + (g + 1) * EPG, :], le)
    e_iota = lax.broadcasted_iota(I32, le.shape, 0)
    v1 = jnp.max(le, axis=0, keepdims=True)
    i1 = jnp.min(jnp.where(le == v1, e_iota, EPG), axis=0, keepdims=True)
    le2 = jnp.where(e_iota == i1, -jnp.inf, le)
    v2 = jnp.max(le2, axis=0, keepdims=True)
    i2 = jnp.min(jnp.where(le2 == v2, e_iota, EPG), axis=0, keepdims=True)
    e21 = jnp.exp(v2 - v1)
    w1 = p_grp / (1.0 + e21)
    w2 = p_grp * e21 / (1.0 + e21)
    eid1 = grp * EPG + i1
    eid2 = grp * EPG + i2

    x_iota = lax.broadcasted_iota(I32, (MOE_EXPERTS, tm), 0)
    hot1 = x_iota == eid1
    hot2 = x_iota == eid2
    hot = jnp.logical_or(hot1, hot2).astype(F32)
    before = (lax.broadcasted_iota(I32, (tm, tm), 0) < lax.broadcasted_iota(I32, (tm, tm), 1))
    prior = jnp.dot(hot.astype(BF16), before.astype(BF16), preferred_element_type=F32)
    rank1 = jnp.sum(jnp.where(hot1, prior, 0.0), axis=0, keepdims=True)
    rank2 = jnp.sum(jnp.where(hot2, prior, 0.0), axis=0, keepdims=True)
    cnt_ref[...] = jnp.sum(hot, axis=1, keepdims=True).astype(I32)

    zi = jnp.zeros((4, tm), I32)
    ri_ref[...] = jnp.concatenate([eid1, eid2, rank1.astype(I32), rank2.astype(I32), zi], axis=0)
    rw_ref[...] = jnp.concatenate([w1, w2, jnp.zeros((6, tm), F32)], axis=0)


def _merge_route(x2, ya, ys, gs, mod, w_glu, w_pa, w_ps, w_out, norm2_g, w_rg, b_rg, w_re, b_re, seq):
    T, D = x2.shape
    tm = TOKEN_TILE
    per_b = seq // tm
    wr = jnp.concatenate([w_rg, jnp.transpose(w_re, (1, 0, 2)).reshape(D, MOE_EXPERTS),
                          jnp.zeros((D, ROUTER_LANES - MOE_GROUPS - MOE_EXPERTS), F32)], axis=1).astype(F32)
    wr_hi = wr.astype(BF16)
    wr_lo = (wr - wr_hi.astype(F32)).astype(BF16)
    br = jnp.concatenate([b_rg, b_re.reshape(-1),
                          jnp.zeros((ROUTER_ROWS - MOE_GROUPS - MOE_EXPERTS,), F32)]).reshape(ROUTER_ROWS, 1)
    row = lambda i: (i, 0)
    col = lambda i: (0, i)
    full = lambda i: (0, 0)
    return pl.pallas_call(
        _merge_kernel,
        grid=(T // tm,),
        in_specs=[pl.BlockSpec((tm, D), row),
                  pl.BlockSpec((tm, ATTN_WIDTH), row),
                  _chunk_major_spec(tm, per_b),
                  pl.BlockSpec((tm, 2 * D), row),
                  pl.BlockSpec((None, 6, D), lambda i: (i // per_b, 0, 0)),
                  pl.BlockSpec(w_glu.shape, full),
                  pl.BlockSpec(w_pa.shape, full),
                  pl.BlockSpec(w_ps.shape, full),
                  pl.BlockSpec(w_out.shape, full),
                  pl.BlockSpec((1, D), full),
                  pl.BlockSpec((D, 2 * ROUTER_LANES), full),
                  pl.BlockSpec((ROUTER_ROWS, 1), full)],
        out_specs=[pl.BlockSpec((tm, D), row), pl.BlockSpec((tm, D), row),
                   pl.BlockSpec((8, tm), col), pl.BlockSpec((8, tm), col),
                   pl.BlockSpec((None, MOE_EXPERTS, 1), lambda i: (i, 0, 0))],
        out_shape=[jax.ShapeDtypeStruct((T, D), F32), jax.ShapeDtypeStruct((T, D), BF16),
                   jax.ShapeDtypeStruct((8, T), I32), jax.ShapeDtypeStruct((8, T), F32),
                   jax.ShapeDtypeStruct((T // tm, MOE_EXPERTS, 1), I32)],
        compiler_params=_params("parallel"),
        name="merge_route",
    )(x2, ya, ys, gs, mod, w_glu.astype(BF16), w_pa.astype(BF16), w_ps.astype(BF16), w_out.astype(BF16),
      norm2_g.reshape(1, D), jnp.concatenate([wr_hi, wr_lo], axis=1), br)


def _pack_pairs(x):
    W = x.shape[1] // 2
    lo = lax.bitcast_convert_type(x[:, :W], I32)
    hi = lax.bitcast_convert_type(x[:, W:], I32)
    return lax.shift_right_logical(lo, 16) | hi


def _unpack_pairs(w):
    lo = lax.bitcast_convert_type(lax.shift_left(w, 16), F32)
    hi = lax.bitcast_convert_type(w & jnp.int32(-65536), F32)
    return jnp.concatenate([lo.astype(BF16), hi.astype(BF16)], axis=1)


def _tile_positions(ri_ref, seg_ref):
    tm = ri_ref.shape[1]
    x_iota = lax.broadcasted_iota(I32, (MOE_EXPERTS, tm), 0)
    seg = seg_ref[...].astype(F32)
    pos = []
    for k in range(2):
        start = jnp.sum(jnp.where(x_iota == ri_ref[k:k + 1, :], seg, 0.0), axis=0, keepdims=True)
        pos.append(start + ri_ref[2 + k:3 + k, :].astype(F32))
    return pos


def _segment_copies(meta, tile, make_copy):
    chunks, tile_chunks = meta
    base = tile * TILE_CHUNKS
    n = tile_chunks[tile]

    def issue(k):
        word = chunks[base + k]
        make_copy(word & (2 ** CHUNK_SLOT_BITS - 1), word >> CHUNK_SLOT_BITS)

    def group(g, carry):
        for r in range(CHUNK_ISSUE_UNROLL):
            issue(g * CHUNK_ISSUE_UNROLL + r)
        return carry

    def single(k, carry):
        issue(k)
        return carry

    n_groups = n // CHUNK_ISSUE_UNROLL
    lax.fori_loop(0, n_groups, group, 0)
    lax.fori_loop(n_groups * CHUNK_ISSUE_UNROLL, n, single, 0)


def _wait_chunks(n, wait_chunk, wait_bulk):
    has_bulk = n >= MIN_TILE_CHUNKS

    @pl.when(has_bulk)
    def _():
        wait_bulk()

    def body(c, carry):
        wait_chunk()
        return carry
    lax.fori_loop(jnp.where(has_bulk, MIN_TILE_CHUNKS, 0), n, body, 0)


def _dispatch_kernel(chunks, tile_chunks, pad_row, pad_chunks, n_used,
                     h_ref, ri_ref, rw_ref, seg_ref, xs_ref, pw_ref, zbuf, zeros_scr, sem, pad_sem, tail_sem):
    i = pl.program_id(0)
    n_tiles = pl.num_programs(0)
    slot = i % 2
    tm = h_ref.shape[0]
    pos1, pos2 = _tile_positions(ri_ref, seg_ref)
    pw_ref[...] = jnp.concatenate([pos1, pos2, rw_ref[0:2, :], jnp.zeros((4, tm), F32)], axis=0)
    r_iota = lax.broadcasted_iota(I32, (TILE_SLOTS, tm), 0).astype(F32)
    onehot = jnp.logical_or(r_iota == pos1, r_iota == pos2).astype(BF16)
    packed = _pack_pairs(jnp.dot(onehot, h_ref[...], preferred_element_type=F32))
    zbuf[slot] = packed.reshape(zbuf.shape[1:])

    def chunk_copy(buf_slot, src, dst):
        return pltpu.make_async_copy(zbuf.at[buf_slot, src], xs_ref.at[dst], sem.at[buf_slot])

    _segment_copies((chunks, tile_chunks), i, lambda src, dst: chunk_copy(slot, src, dst).start())

    def wait_tile(tile, buf_slot):
        bulk = pltpu.make_async_copy(zbuf.at[buf_slot, pl.ds(0, MIN_TILE_CHUNKS)],
                                     xs_ref.at[pl.ds(0, MIN_TILE_CHUNKS)], sem.at[buf_slot])
        _wait_chunks(tile_chunks[tile], chunk_copy(buf_slot, 0, 0).wait, bulk.wait)

    @pl.when(i > 0)
    def _():
        wait_tile(i - 1, 1 - slot)

    @pl.when(i == n_tiles - 1)
    def _():
        zeros_scr[...] = jnp.zeros(zeros_scr.shape, I32)

        block_chunks = zeros_scr.shape[0]

        def pad_copy(dst):
            return pltpu.make_async_copy(zeros_scr.at[0], xs_ref.at[dst], pad_sem)

        def tail_copy(dst):
            return pltpu.make_async_copy(zeros_scr, xs_ref.at[pl.ds(dst, block_chunks)], tail_sem)

        def per_expert(e, total):
            def per_chunk(c, carry):
                pad_copy(pad_row[e] + c).start()
                return carry
            lax.fori_loop(0, pad_chunks[e], per_chunk, 0)
            return total + pad_chunks[e]

        n_pad_copies = lax.fori_loop(0, MOE_EXPERTS, per_expert, 0)
        n_blocks = xs_ref.shape[0] // block_chunks

        def tail_start(b, carry):
            tail_copy(b * block_chunks).start()
            return carry
        lax.fori_loop(n_used[0], n_blocks, tail_start, 0)
        wait_tile(i, slot)

        def wait_pad(c, carry):
            pad_copy(0).wait()
            return carry
        lax.fori_loop(0, n_pad_copies, wait_pad, 0)

        def wait_tail(b, carry):
            tail_copy(0).wait()
            return carry
        lax.fori_loop(n_used[0], n_blocks, wait_tail, 0)


def _dispatch(meta, h2, ri, rw, seg_start, n_rows):
    T, D = h2.shape
    tm = TOKEN_TILE
    col = lambda i, *_: (0, i)
    return pl.pallas_call(
        _dispatch_kernel,
        grid_spec=pltpu.PrefetchScalarGridSpec(
            num_scalar_prefetch=5,
            grid=(T // tm,),
            in_specs=[pl.BlockSpec((tm, D), lambda i, *_: (i, 0)),
                      pl.BlockSpec((8, tm), col),
                      pl.BlockSpec((8, tm), col),
                      pl.BlockSpec((None, MOE_EXPERTS, 1), lambda i, *_: (i, 0, 0))],
            out_specs=[pl.BlockSpec(memory_space=pl.ANY), pl.BlockSpec((8, tm), col)],
            scratch_shapes=[pltpu.VMEM((2, TILE_CHUNKS, SEG_ALIGN, D // 2), I32),
                            pltpu.VMEM((MOE_BLOCK // SEG_ALIGN, SEG_ALIGN, D // 2), I32),
                            pltpu.SemaphoreType.DMA((2,)), pltpu.SemaphoreType.DMA(()),
                            pltpu.SemaphoreType.DMA(())]),
        out_shape=[jax.ShapeDtypeStruct((n_rows // SEG_ALIGN, SEG_ALIGN, D // 2), I32),
                   jax.ShapeDtypeStruct((8, T), F32)],
        compiler_params=_params("arbitrary"),
        name="moe_dispatch",
    )(*meta, h2, ri, rw, seg_start)


def _expert_kernel(be_ref, nb_ref, x_ref, wi_ref, wo_ref, o_ref, wi_bf, wo_bf):
    i = pl.program_id(0)
    F = wo_ref.shape[0]
    in_use = i < nb_ref[0]

    @pl.when(jnp.logical_and(in_use, jnp.logical_or(i == 0, be_ref[i] != be_ref[jnp.maximum(i - 1, 0)])))
    def _():
        wi_bf[...] = wi_ref[...].astype(BF16)
        wo_bf[...] = wo_ref[...].astype(BF16)

    @pl.when(in_use)
    def _():
        rows, half_d = o_ref.shape[0] * o_ref.shape[1], o_ref.shape[2]
        hid = jnp.dot(_unpack_pairs(x_ref[...].reshape(rows, half_d)), wi_bf[...], preferred_element_type=F32)
        a = hid[:, :F]
        act = a * jax.nn.sigmoid(a) * hid[:, F:]
        y = jnp.dot(act.astype(BF16), wo_bf[...], preferred_element_type=F32)
        o_ref[...] = _pack_pairs(y.astype(BF16).astype(F32)).reshape(o_ref.shape)

    @pl.when(jnp.logical_not(in_use))
    def _():
        o_ref[...] = jnp.zeros(o_ref.shape, I32)


def _experts(block_e, n_used, xs, w_e_in, w_e_out):
    n_chunks, _, half_d = xs.shape
    D = 2 * half_d
    F = w_e_out.shape[1]
    block_chunks = MOE_BLOCK // SEG_ALIGN
    blk = lambda i, be, nb: jnp.maximum(jnp.minimum(i, nb[0] - 1), 0)
    return pl.pallas_call(
        _expert_kernel,
        grid_spec=pltpu.PrefetchScalarGridSpec(
            num_scalar_prefetch=2,
            grid=(n_chunks // block_chunks,),
            in_specs=[pl.BlockSpec((block_chunks, SEG_ALIGN, half_d), lambda i, be, nb: (blk(i, be, nb), 0, 0)),
                      pl.BlockSpec((None, D, 2 * F), lambda i, be, nb: (be[blk(i, be, nb)], 0, 0)),
                      pl.BlockSpec((None, F, D), lambda i, be, nb: (be[blk(i, be, nb)], 0, 0))],
            out_specs=pl.BlockSpec((block_chunks, SEG_ALIGN, half_d), lambda i, be, nb: (i, 0, 0)),
            scratch_shapes=[pltpu.VMEM((D, 2 * F), BF16), pltpu.VMEM((F, D), BF16)]),
        out_shape=jax.ShapeDtypeStruct(xs.shape, I32),
        compiler_params=_params("arbitrary"),
        name="experts",
    )(block_e, n_used, xs, w_e_in, w_e_out)


def _final_kernel(chunks, tile_chunks, x1_ref, pw_ref, mod_ref, g_ref, ys_ref, o_ref, ybuf, sem):
    i = pl.program_id(0)
    n_tiles = pl.num_programs(0)
    slot = i % 2
    tm = x1_ref.shape[0]

    def chunk_copy(buf_slot, src, dst):
        return pltpu.make_async_copy(ys_ref.at[dst], ybuf.at[buf_slot, src], sem.at[buf_slot])

    def fetch(tile, buf_slot):
        _segment_copies((chunks, tile_chunks), tile, lambda src, dst: chunk_copy(buf_slot, src, dst).start())

    @pl.when(i == 0)
    def _():
        ybuf[...] = jnp.zeros(ybuf.shape, I32)
        fetch(0, 0)

    @pl.when(i + 1 < n_tiles)
    def _():
        fetch(i + 1, 1 - slot)

    bulk = pltpu.make_async_copy(ys_ref.at[pl.ds(0, MIN_TILE_CHUNKS)],
                                 ybuf.at[slot, pl.ds(0, MIN_TILE_CHUNKS)], sem.at[slot])
    _wait_chunks(tile_chunks[i], chunk_copy(slot, 0, 0).wait, bulk.wait)

    s_iota = lax.broadcasted_iota(I32, (tm, TILE_SLOTS), 1).astype(F32)
    comb = (jnp.where(s_iota == pw_ref[:, 0:1], pw_ref[:, 2:3], 0.0)
            + jnp.where(s_iota == pw_ref[:, 1:2], pw_ref[:, 3:4], 0.0))
    sorted_rows = _unpack_pairs(ybuf[slot].reshape(TILE_SLOTS, ybuf.shape[3]))
    moe = jnp.dot(comb.astype(BF16), sorted_rows, preferred_element_type=F32)
    x2 = x1_ref[...] + mod_ref[5:6, :] * moe
    o_ref[...] = _rms(x2, NORM_EPS) * g_ref[...]


def _final(meta, x1, pw_tok, mod, final_g, ys, seq):
    T, D = x1.shape
    tm = TOKEN_TILE
    per_b = seq // tm
    row = lambda i, *_: (i, 0)
    return pl.pallas_call(
        _final_kernel,
        grid_spec=pltpu.PrefetchScalarGridSpec(
            num_scalar_prefetch=2,
            grid=(T // tm,),
            in_specs=[pl.BlockSpec((tm, D), row),
                      pl.BlockSpec((tm, 8), row),
                      pl.BlockSpec((None, 6, D), lambda i, *_: (i // per_b, 0, 0)),
                      pl.BlockSpec((1, D), lambda i, *_: (0, 0)),
                      pl.BlockSpec(memory_space=pl.ANY)],
            out_specs=pl.BlockSpec((tm, D), row),
            scratch_shapes=[pltpu.VMEM((2, TILE_CHUNKS, SEG_ALIGN, D // 2), I32), pltpu.SemaphoreType.DMA((2,))]),
        out_shape=jax.ShapeDtypeStruct((T, D), F32),
        compiler_params=_params("arbitrary"),
        name="final",
    )(*meta, x1, pw_tok, mod, final_g.reshape(1, D), ys)


def _round_up(x, m):
    return (x + m - 1) // m * m


def _moe_layout(tile_counts):
    n_tiles = tile_counts.shape[0]
    seg = _round_up(tile_counts, SEG_ALIGN)
    seg_start = jnp.cumsum(seg, axis=1) - seg
    tile_base = jnp.cumsum(seg, axis=0) - seg
    used = jnp.sum(seg, axis=0)
    region = _round_up(used, MOE_BLOCK)
    region_end = jnp.cumsum(region)
    region_start = region_end - region
    dst_row = region_start[None, :] + tile_base
    n_chunk = seg // SEG_ALIGN
    n_assign = 2 * n_tiles * TOKEN_TILE
    n_rows = _round_up(n_assign + n_tiles * MOE_EXPERTS * (SEG_ALIGN - 1) + MOE_EXPERTS * (MOE_BLOCK - 1), MOE_BLOCK)
    block_start = jnp.arange(n_rows // MOE_BLOCK, dtype=I32) * MOE_BLOCK
    block_e = jnp.minimum(jnp.sum(block_start[:, None] >= region_end[None, :], axis=1), MOE_EXPERTS - 1)
    i32 = lambda a: a.reshape(-1).astype(I32)
    chunk_end = jnp.cumsum(n_chunk, axis=1)
    chunk_start = chunk_end - n_chunk
    k = jnp.arange(TILE_CHUNKS)[None, :, None]
    mine = (chunk_start[:, None, :] <= k) & (k < chunk_end[:, None, :])
    pick = lambda a: jnp.sum(jnp.where(mine, (a // SEG_ALIGN - chunk_start)[:, None, :] + k, 0), axis=2)
    word = pick(seg_start) | (pick(dst_row) << CHUNK_SLOT_BITS)
    meta = (i32(word), i32(chunk_end[:, -1]))
    pad = (i32((region_start + used) // SEG_ALIGN), i32((region - used) // SEG_ALIGN))
    n_used = (region_end[-1:] // MOE_BLOCK).astype(I32)
    return meta, pad, seg_start.astype(I32)[:, :, None], block_e.astype(I32), n_used, n_rows


def kernel(x, c, w_ada, b_ada, norm1_g, w_in, rel_bias, lambda_q1, lambda_k1, lambda_q2, lambda_k2, subln_g, ssm_lambda_re, ssm_lambda_im, ssm_log_step, ssm_b_re, ssm_b_im, ssm_c_re, ssm_c_im, ssm_d, w_glu, w_proj_attn, w_proj_ssm, w_out, norm2_g, w_router_group, b_router_group, w_router_expert, b_router_expert, w_expert_in, w_expert_out, final_g):
    B, S, D = x.shape
    T = B * S
    x2 = x.reshape(T, D)
    mod = _ada_mod(c, w_ada[0], b_ada[0]).reshape(B, 6, D)
    q, k, v, u, gs = _in_proj(x2, mod, norm1_g[0], w_in[0], S)
    lam_vecs = jnp.stack([lambda_q1[0], lambda_k1[0], lambda_q2[0], lambda_k2[0]]).astype(F32)
    y_attn = _diff_attn(q, k, v, rel_bias, lam_vecs, subln_g[0], B, S)
    tables = _s5_tables(ssm_lambda_re[0], ssm_lambda_im[0], ssm_log_step[0], ssm_b_re[0], ssm_b_im[0],
                        ssm_c_re[0], ssm_c_im[0], ssm_d[0])
    y_s5 = _s5_branch(u.reshape(-1, SSM_WIDTH), tables, B, S).reshape(u.shape)
    x1, h2, ri, rw, tile_counts = _merge_route(
        x2, y_attn, y_s5, gs, mod, w_glu[0], w_proj_attn[0], w_proj_ssm[0], w_out[0], norm2_g[0],
        w_router_group[0], b_router_group[0], w_router_expert[0], b_router_expert[0], S)
    meta, pad, seg_start, block_e, n_used, n_rows = _moe_layout(tile_counts[:, :, 0])
    xs, pw = _dispatch(meta + pad + (n_used,), h2, ri, rw, seg_start, n_rows)
    ys = _experts(block_e, n_used, xs, w_expert_in[0], w_expert_out[0])
    out = _final(meta, x1, pw.T, mod, final_g, ys, S)
    return out.reshape(B, S, D)
```

```python
import functools
import math

import jax
import jax.numpy as jnp
from jax import lax
from jax.experimental import pallas as pl
from jax.experimental.pallas import tpu as pltpu

F32 = jnp.float32
BF16 = jnp.bfloat16
I32 = jnp.int32
HIGHEST = lax.Precision.HIGHEST

LANES = 128
MXU_TILE = 256

ATTN_HEADS = 4
ATTN_HEAD_DIM = 64
ATTN_V_DIM = 2 * ATTN_HEAD_DIM
ATTN_WIDTH = ATTN_HEADS * ATTN_V_DIM
NEG_INF = -1e30
REL_BUCKETS = 32
REL_MAX_DISTANCE = 128
SSM_GROUP_CH = 16
SSM_WIDTH = 512
SSM_GROUPS = SSM_WIDTH // SSM_GROUP_CH
SSM_STATE = 64
SSM_EIG_MAX_RE = -1e-4
SSM_CHUNK = 16
S5_CHUNKS_PER_STEP = 32
MOE_GROUPS = 4
MOE_EXPERTS_PER_GROUP = 8
MOE_EXPERTS = MOE_GROUPS * MOE_EXPERTS_PER_GROUP
MOE_BLOCK = 512
SEG_ALIGN = 8
NORM_EPS = 1e-6
SUBLN_EPS = 1e-5
LAMBDA_INIT = 0.8 - 0.6 * math.exp(-0.3 * 0)

ATTN_BLOCK = 256
ATTN_HEADS_PER_STEP = 4
ATTN_ROW_CHUNK = 32
ATTN_ONES_ROWS = 16
LOG2E = math.log2(math.e)
TOKEN_TILE = 512
ROUTER_ROWS = 40
ROUTER_LANES = LANES
TILE_SLOTS = -(-(2 * TOKEN_TILE + MOE_EXPERTS * (SEG_ALIGN - 1)) // MXU_TILE) * MXU_TILE
TILE_CHUNKS = TILE_SLOTS // SEG_ALIGN
MIN_TILE_CHUNKS = 2 * TOKEN_TILE // SEG_ALIGN
CHUNK_ISSUE_UNROLL = 8
CHUNK_SLOT_BITS = 8
assert TILE_CHUNKS <= 2 ** CHUNK_SLOT_BITS
VMEM_LIMIT = 56 << 20


def _params(*sem):
    return pltpu.CompilerParams(dimension_semantics=sem, vmem_limit_bytes=VMEM_LIMIT)


def _rms(x, eps):
    return x * lax.rsqrt(jnp.mean(x * x, axis=-1, keepdims=True) + eps)


def _mod_kernel(c_ref, w_ref, b_ref, o_ref):
    c = c_ref[...]
    c_act = c * jax.nn.sigmoid(c)
    o_ref[...] = jnp.dot(c_act, w_ref[...], preferred_element_type=F32, precision=HIGHEST) + b_ref[...]


def _ada_mod(c, w_ada, b_ada):
    B, D = c.shape
    N = w_ada.shape[1]
    tn = 1024
    return pl.pallas_call(
        _mod_kernel,
        grid=(N // tn,),
        in_specs=[pl.BlockSpec((B, D), lambda j: (0, 0)),
                  pl.BlockSpec((D, tn), lambda j: (0, j)),
                  pl.BlockSpec((1, tn), lambda j: (0, j))],
        out_specs=pl.BlockSpec((B, tn), lambda j: (0, j)),
        out_shape=jax.ShapeDtypeStruct((B, N), F32),
        compiler_params=_params("arbitrary"),
        name="ada_mod",
    )(c, w_ada, b_ada.reshape(1, N))


def _proj_kernel(x_ref, mod_ref, g_ref, w_ref, q_ref, k_ref, v_ref, u_ref, gs_ref):
    y = _rms(x_ref[...], NORM_EPS) * g_ref[...]
    h = (y * (1.0 + mod_ref[1:2, :]) + mod_ref[0:1, :]).astype(BF16)
    W = ATTN_WIDTH

    def proj(lo, hi):
        return jnp.dot(h, w_ref[:, lo:hi], preferred_element_type=F32)

    q_ref[...] = (proj(0, W) * (ATTN_HEAD_DIM ** -0.5 * LOG2E)).astype(BF16)
    k_ref[...] = proj(W, 2 * W).astype(BF16)
    v_ref[...] = proj(2 * W, 3 * W).astype(BF16)
    u_ref[...] = proj(3 * W, 3 * W + SSM_WIDTH).reshape(u_ref.shape)
    gs_ref[...] = jax.nn.sigmoid(proj(3 * W + SSM_WIDTH, w_ref.shape[1])).astype(BF16)


def _chunk_major_spec(tm, per_b):
    return pl.BlockSpec((tm // SSM_CHUNK, None, SSM_CHUNK, SSM_WIDTH), lambda i: (i % per_b, i // per_b, 0, 0))


def _in_proj(x2, mod, norm_g, w_in, seq):
    T, D = x2.shape
    tm = TOKEN_TILE
    per_b = seq // tm
    n_gate = w_in.shape[1] - 3 * ATTN_WIDTH - SSM_WIDTH
    row = lambda i: (i, 0)
    return pl.pallas_call(
        _proj_kernel,
        grid=(T // tm,),
        in_specs=[pl.BlockSpec((tm, D), row),
                  pl.BlockSpec((None, 6, D), lambda i: (i // per_b, 0, 0)),
                  pl.BlockSpec((1, D), lambda i: (0, 0)),
                  pl.BlockSpec(w_in.shape, lambda i: (0, 0))],
        out_specs=[pl.BlockSpec((tm, ATTN_WIDTH), row)] * 3
        + [_chunk_major_spec(tm, per_b), pl.BlockSpec((tm, n_gate), row)],
        out_shape=[jax.ShapeDtypeStruct((T, ATTN_WIDTH), BF16)] * 3
        + [jax.ShapeDtypeStruct((seq // SSM_CHUNK, T // seq, SSM_CHUNK, SSM_WIDTH), F32),
           jax.ShapeDtypeStruct((T, n_gate), BF16)],
        compiler_params=_params("parallel"),
        name="in_proj",
    )(x2, mod, norm_g.reshape(1, D), w_in.astype(BF16))


def _rel_bucket(dist):
    max_exact = REL_BUCKETS // 2
    n = jnp.maximum(dist, 0)
    log_ratio = jnp.log(jnp.maximum(n, 1).astype(F32) / max_exact) / math.log(REL_MAX_DISTANCE / max_exact)
    large = max_exact + (log_ratio * (REL_BUCKETS - max_exact)).astype(I32)
    large = jnp.minimum(large, REL_BUCKETS - 1)
    return jnp.where(n < max_exact, n, large)


def _attn_bias_tiles(rel_bias, blk):
    assert blk >= REL_MAX_DISTANCE
    n_heads = rel_bias.shape[1]
    far = rel_bias[REL_BUCKETS - 1].astype(F32)
    m = jnp.arange(2 * blk)
    signed = jnp.where(m < blk, m, m - 2 * blk)
    tiles = []
    for kind in range(2):
        dist = kind * blk + signed
        tab = jnp.where(dist >= 0, (rel_bias[_rel_bucket(dist)].astype(F32).T - far[:, None]) * LOG2E, NEG_INF)
        skew = jnp.tile(tab, (1, blk))[:, :blk * (2 * blk - 1)].reshape(n_heads, blk, 2 * blk - 1)
        tiles.append(skew[:, :, :blk])
    return jnp.stack(tiles, axis=1)


def _attn_kernel(lam_ref, q_ref, k_ref, v_ref, bias_ref, g_ref, o_ref, vt_scr, *bufs, blk, heads):
    i = pl.program_id(2)
    n_kv = vt_scr.shape[1]
    V = ATTN_V_DIM
    ns = 2 * heads
    acc = bufs[0:ns]
    sbuf = tuple(bufs[(1 + r) * ns:(2 + r) * ns] for r in range(3))
    pbufs = tuple(bufs[(4 + r) * ns:(5 + r) * ns] for r in range(3))

    @pl.when(i == 0)
    def _():
        for hd in range(heads):
            for jb in range(n_kv):
                vt_scr[hd, jb, 0:V, :] = v_ref[jb * blk:(jb + 1) * blk, hd * V:(hd + 1) * V].astype(F32).T.astype(BF16)
                vt_scr[hd, jb, V:, :] = jnp.ones((ATTN_ONES_ROWS, blk), BF16)

    qt_maps = []
    for hd in range(heads):
        qt = q_ref[:, hd * V:(hd + 1) * V].astype(F32).T
        feat = lax.broadcasted_iota(I32, qt.shape, 0)
        qt_maps += [jnp.where(feat < ATTN_HEAD_DIM, qt, 0.0).astype(BF16),
                    jnp.where(feat >= ATTN_HEAD_DIM, qt, 0.0).astype(BF16)]

    def scores(j, st):
        hd = st // 2
        kj = k_ref[pl.ds(pl.multiple_of(j * blk, blk), blk), hd * V:(hd + 1) * V]
        return jnp.dot(kj, qt_maps[st], preferred_element_type=F32)

    n_chunks = blk // ATTN_ROW_CHUNK

    def rows(c):
        return slice(c * ATTN_ROW_CHUNK, (c + 1) * ATTN_ROW_CHUNK)

    def fold8(x):
        return x.reshape(ATTN_ROW_CHUNK // 8, 8, blk)

    def block(j, carry, pos, lookahead, bias_kind):
        src, dst, pbuf = sbuf[pos], sbuf[(pos + 2) % 3], pbufs[pos]
        out = []
        for st in range(ns):
            hd = st // 2
            if lookahead:
                dst[st][...] = scores(j + 2, st)

            def chunk(c):
                s = src[st][rows(c), :]
                return s if bias_kind is None else s + bias_ref[hd, bias_kind, rows(c), :]

            m_old = carry[st]
            m8 = jnp.max(fold8(chunk(0)), axis=0)
            for c in range(1, n_chunks):
                m8 = jnp.maximum(m8, jnp.max(fold8(chunk(c)), axis=0))
            m_new = jnp.maximum(m_old, jnp.max(m8, axis=0, keepdims=True))
            alpha = jnp.exp2(m_old - m_new)
            for c in range(n_chunks):
                pbuf[st][rows(c), :] = jnp.exp2(chunk(c) - m_new).astype(BF16)
            acc[st][...] = alpha * acc[st][...] + jnp.dot(vt_scr[hd, j], pbuf[st][...],
                                                          preferred_element_type=F32)
            out.append(m_new)
        return tuple(out)

    def far_triple(t, carry):
        for r in range(3):
            carry = block(3 * t + r, carry, r, True, None)
        return carry

    def far_single(j, carry):
        carry = block(j, carry, 0, True, None)
        for st in range(ns):
            sbuf[0][st][...] = sbuf[1][st][...]
        for st in range(ns):
            sbuf[1][st][...] = sbuf[2][st][...]
        return carry

    def near_pair(_, carry):
        return block(i, block(i - 1, carry, 0, False, 1), 1, False, 0)

    def near_single(_, carry):
        return block(i, carry, 0, False, 0)

    for st in range(ns):
        acc[st][...] = jnp.zeros(acc[st].shape, F32)
        sbuf[0][st][...] = scores(0, st)
        sbuf[1][st][...] = scores(jnp.minimum(i, 1), st)
    m0 = jnp.full((1, blk), -jnp.inf, F32)
    n_far = jnp.maximum(i - 1, 0)
    carry = lax.fori_loop(0, n_far // 3, far_triple, (m0,) * ns)
    carry = lax.fori_loop(n_far - n_far % 3, n_far, far_single, carry)
    carry = lax.fori_loop(0, jnp.minimum(i, 1), near_pair, carry)
    lax.fori_loop(0, 1 - jnp.minimum(i, 1), near_single, carry)

    lam = (jnp.exp(jnp.sum(lam_ref[0:1, :] * lam_ref[1:2, :], axis=-1, keepdims=True))
           - jnp.exp(jnp.sum(lam_ref[2:3, :] * lam_ref[3:4, :], axis=-1, keepdims=True)) + LAMBDA_INIT)
    for hd in range(heads):
        a1, a2 = acc[2 * hd], acc[2 * hd + 1]
        ot = a1[0:V, :] / a1[V:V + 1, :] - lam * (a2[0:V, :] / a2[V:V + 1, :])
        ot = ot * lax.rsqrt(jnp.mean(ot * ot, axis=0, keepdims=True) + SUBLN_EPS)
        o_ref[:, hd * V:(hd + 1) * V] = (ot.T * (g_ref[...] * (1.0 - LAMBDA_INIT))).astype(BF16)


def _diff_attn(q, k, v, rel_bias, lam_vecs, subln_g, batch, seq):
    T = q.shape[0]
    blk = ATTN_BLOCK
    nq = seq // blk
    bias = _attn_bias_tiles(rel_bias, blk)
    hp = ATTN_HEADS_PER_STEP
    ns = 2 * hp
    width = hp * ATTN_V_DIM
    acc_rows = ATTN_V_DIM + ATTN_ONES_ROWS
    return pl.pallas_call(
        functools.partial(_attn_kernel, blk=blk, heads=hp),
        grid=(batch, ATTN_HEADS // hp, nq),
        in_specs=[pl.BlockSpec((4, ATTN_HEAD_DIM), lambda b, h, i: (0, 0)),
                  pl.BlockSpec((blk, width), lambda b, h, i: (b * nq + i, h)),
                  pl.BlockSpec((seq, width), lambda b, h, i: (b, h)),
                  pl.BlockSpec((seq, width), lambda b, h, i: (b, h)),
                  pl.BlockSpec((hp, 2, blk, blk), lambda b, h, i: (h, 0, 0, 0)),
                  pl.BlockSpec((1, ATTN_V_DIM), lambda b, h, i: (0, 0))],
        out_specs=pl.BlockSpec((blk, width), lambda b, h, i: (b * nq + i, h)),
        out_shape=jax.ShapeDtypeStruct((T, ATTN_WIDTH), BF16),
        scratch_shapes=[pltpu.VMEM((hp, nq, acc_rows, blk), BF16)]
        + [pltpu.VMEM((acc_rows, blk), F32)] * ns + [pltpu.VMEM((blk, blk), F32)] * (3 * ns)
        + [pltpu.VMEM((blk, blk), BF16)] * (3 * ns),
        compiler_params=_params("parallel", "parallel", "arbitrary"),
        name="diff_attn",
    )(lam_vecs, q, k, v, bias, subln_g.reshape(1, ATTN_V_DIM))


def _s5_tables(lam_re, lam_im, log_step, b_re, b_im, c_re, c_im, d_skip):
    L = SSM_CHUNK
    G, P = lam_re.shape
    H = SSM_GROUP_CH
    lr = jnp.minimum(lam_re.astype(F32), SSM_EIG_MAX_RE)
    li = lam_im.astype(F32)
    step = jnp.exp(log_step.astype(F32))[:, None]
    mag = jnp.exp(lr * step)
    ang = li * step
    a_re = mag * jnp.cos(ang)
    a_im = mag * jnp.sin(ang)
    den = lr * lr + li * li
    num_re = a_re - 1.0
    coef_re = (num_re * lr + a_im * li) / den
    coef_im = (a_im * lr - num_re * li) / den
    br = b_re.astype(F32)
    bi = b_im.astype(F32)
    bb_re = coef_re[..., None] * br - coef_im[..., None] * bi
    bb_im = coef_re[..., None] * bi + coef_im[..., None] * br
    pw_re, pw_im = [jnp.ones_like(a_re)], [jnp.zeros_like(a_re)]
    for _ in range(L):
        pr, pi = pw_re[-1], pw_im[-1]
        pw_re.append(pr * a_re - pi * a_im)
        pw_im.append(pr * a_im + pi * a_re)
    pw_re = jnp.stack(pw_re)
    pw_im = jnp.stack(pw_im)
    cr = c_re.astype(F32)[None]
    ci = c_im.astype(F32)[None]
    cp_re = cr * pw_re[:, :, None, :] - ci * pw_im[:, :, None, :]
    cp_im = cr * pw_im[:, :, None, :] + ci * pw_re[:, :, None, :]
    kern = jnp.einsum('tghp,gpk->tghk', jnp.concatenate([cp_re[:L], -cp_im[:L]], axis=-1),
                      jnp.concatenate([bb_re, bb_im], axis=1), precision=HIGHEST)
    steps = jnp.arange(L)
    place = (steps[None, :, None] - steps[:, None, None] == steps[None, None, :]).astype(F32)
    m_tab = jnp.einsum('stu,ughk->gskth', place, kern, precision=HIGHEST).reshape(G, L * H, L * H)
    rev_re = pw_re[L - 1::-1][:, :, None, :]
    rev_im = pw_im[L - 1::-1][:, :, None, :]
    bbt_re = jnp.transpose(bb_re, (0, 2, 1))[None]
    bbt_im = jnp.transpose(bb_im, (0, 2, 1))[None]
    bst_re = jnp.transpose(rev_re * bbt_re - rev_im * bbt_im, (1, 0, 2, 3)).reshape(G, L * H, P)
    bst_im = jnp.transpose(rev_re * bbt_im + rev_im * bbt_re, (1, 0, 2, 3)).reshape(G, L * H, P)
    cst_re = jnp.transpose(cp_re[1:], (1, 3, 0, 2)).reshape(G, P, L * H)
    cst_im = -jnp.transpose(cp_im[1:], (1, 3, 0, 2)).reshape(G, P, L * H)
    a_chunk = jnp.stack([jnp.concatenate([pw_re[L], pw_re[L]], axis=-1),
                         jnp.concatenate([-pw_im[L], pw_im[L]], axis=-1)], axis=1)
    d_tab = jnp.tile(d_skip.astype(F32), (1, L)).reshape(G, 1, L * H)
    bst = jnp.concatenate([bst_re, bst_im], axis=-1)
    bst_swapped = jnp.concatenate([bst_im, bst_re], axis=-1)
    cst = jnp.concatenate([cst_re, cst_im], axis=1)
    return m_tab.astype(BF16), bst.astype(BF16), bst_swapped.astype(BF16), cst.astype(BF16), a_chunk, d_tab


def _gelu_tanh(x):
    return 0.5 * x * (1.0 + jnp.tanh(math.sqrt(2.0 / math.pi) * (x + 0.044715 * (x * x * x))))


def _lane_block_transpose(arrs):
    n = len(arrs)
    width = arrs[0].shape[1]
    blk_id = lax.broadcasted_iota(I32, arrs[0].shape, 1) // SSM_GROUP_CH
    k = n // 2
    while k >= 1:
        keep = (blk_id & k) == 0
        nxt = list(arrs)
        for r in range(n):
            if r & k == 0:
                a, b = arrs[r], arrs[r + k]
                nxt[r] = jnp.where(keep, a, pltpu.roll(b, k * SSM_GROUP_CH, axis=1))
                nxt[r + k] = jnp.where(keep, pltpu.roll(a, width - k * SSM_GROUP_CH, axis=1), b)
        arrs = nxt
        k //= 2
    return arrs


def _s5_kernel(u_ref, m_ref, bst_ref, bsts_ref, cst_ref, a_ref, d_ref, o_ref,
               us_scr, z_scr, zs_scr, y_scr, st_scr, *, batch):
    L, H = SSM_CHUNK, SSM_GROUP_CH
    n_grp = us_scr.shape[0]
    R = us_scr.shape[1]
    half = LANES // H

    @pl.when(pl.program_id(1) == 0)
    def _():
        st_scr[...] = jnp.zeros(st_scr.shape, F32)

    for hh in range(L // half):
        slabs = [pltpu.bitcast(u_ref[pl.ds(hh * half + s, R, stride=L), :].astype(BF16), I32) for s in range(half)]
        for gi, arr in enumerate(_lane_block_transpose(slabs)):
            us_scr[gi, :, hh * LANES:(hh + 1) * LANES] = pltpu.bitcast(arr, BF16)
    for gi in range(n_grp):
        u = us_scr[gi]
        z_scr[gi] = jnp.dot(u, bst_ref[gi], preferred_element_type=F32)
        zs_scr[gi] = jnp.dot(u, bsts_ref[gi], preferred_element_type=F32)
        y_scr[gi] = jnp.dot(u, m_ref[gi], preferred_element_type=F32) + u.astype(F32) * d_ref[gi]

    def step(c, state):
        sl = pl.ds(pl.multiple_of(c * batch, batch), batch)
        out = []
        for gi in range(n_grp):
            x, xs = state[2 * gi], state[2 * gi + 1]
            p, q = a_ref[gi, 0:1, :], a_ref[gi, 1:2, :]
            z = z_scr[gi, sl, :]
            z_scr[gi, sl, :] = x
            out += [p * x + q * xs + z, p * xs - q * x + zs_scr[gi, sl, :]]
        return tuple(out)

    state = lax.fori_loop(0, R // batch, step, tuple(st_scr[k] for k in range(2 * n_grp)))
    for k in range(2 * n_grp):
        st_scr[k] = state[k]

    for gi in range(n_grp):
        y = y_scr[gi] + jnp.dot(z_scr[gi].astype(BF16), cst_ref[gi], preferred_element_type=F32)
        y_scr[gi] = _gelu_tanh(y)
    for hh in range(L // half):
        cols = [pltpu.bitcast(y_scr[gi, :, hh * LANES:(hh + 1) * LANES].astype(BF16), I32) for gi in range(n_grp)]
        for s, arr in enumerate(_lane_block_transpose(cols)):
            o_ref[pl.ds(hh * half + s, R, stride=L), :] = pltpu.bitcast(arr, BF16).astype(F32)


def _s5_branch(u, tables, batch, seq):
    L, G, H, P = SSM_CHUNK, SSM_GROUPS, SSM_GROUP_CH, SSM_STATE
    n_chunks = seq // L
    gpt = LANES // H
    cr = S5_CHUNKS_PER_STEP
    R = cr * batch
    LH = L * H
    m_tab, bst, bst_swapped, cst, a_chunk, d_tab = tables
    tile = lambda o, c: (o, 0, 0)
    return pl.pallas_call(
        functools.partial(_s5_kernel, batch=batch),
        grid=(G // gpt, n_chunks // cr),
        in_specs=[pl.BlockSpec((R * L, LANES), lambda o, c: (c, o)),
                  pl.BlockSpec((gpt, LH, LH), tile),
                  pl.BlockSpec((gpt, LH, 2 * P), tile),
                  pl.BlockSpec((gpt, LH, 2 * P), tile),
                  pl.BlockSpec((gpt, 2 * P, LH), tile),
                  pl.BlockSpec((gpt, 2, 2 * P), tile),
                  pl.BlockSpec((gpt, 1, LH), tile)],
        out_specs=pl.BlockSpec((R * L, LANES), lambda o, c: (c, o)),
        out_shape=jax.ShapeDtypeStruct(u.shape, F32),
        scratch_shapes=[pltpu.VMEM((gpt, R, LH), BF16), pltpu.VMEM((gpt, R, 2 * P), F32),
                        pltpu.VMEM((gpt, R, 2 * P), F32), pltpu.VMEM((gpt, R, LH), F32),
                        pltpu.VMEM((2 * gpt, batch, 2 * P), F32)],
        compiler_params=_params("parallel", "arbitrary"),
        name="s5",
    )(u, m_tab, bst, bst_swapped, cst, a_chunk, d_tab)


def _merge_kernel(x_ref, ya_ref, ys_ref, gs_ref, mod_ref, wglu_ref, pa_ref, ps_ref, wout_ref, g2_ref,
                  wr_ref, br_ref, x1_ref, h2_ref, ri_ref, rw_ref, cnt_ref):
    tm, D = x_ref.shape
    ys = ys_ref[...].reshape(tm, SSM_WIDTH).astype(BF16)
    gl = jnp.dot(ys, wglu_ref[...], preferred_element_type=F32)
    y_ssm = gl[:, :SSM_WIDTH] * jax.nn.sigmoid(gl[:, SSM_WIDTH:])
    p_attn = jnp.dot(ya_ref[...], pa_ref[...], preferred_element_type=F32)
    p_ssm = jnp.dot(y_ssm.astype(BF16), ps_ref[...], preferred_element_type=F32)
    merged = gs_ref[:, :D].astype(F32) * p_attn + gs_ref[:, D:].astype(F32) * p_ssm
    mixed = jnp.dot(merged.astype(BF16), wout_ref[...], preferred_element_type=F32)
    x1 = x_ref[...] + mod_ref[2:3, :] * mixed
    x1_ref[...] = x1
    h2 = _rms(x1, NORM_EPS) * g2_ref[...] * (1.0 + mod_ref[4:5, :]) + mod_ref[3:4, :]
    h2_hi = h2.astype(BF16)
    h2_ref[...] = h2_hi

    h2_lo = (h2 - h2_hi.astype(F32)).astype(BF16)
    hi_prod = jnp.dot(h2_hi, wr_ref[...], preferred_element_type=F32)
    lo_prod = jnp.dot(h2_lo, wr_ref[:, 0:ROUTER_LANES], preferred_element_type=F32)
    lg_tok = hi_prod[:, 0:ROUTER_LANES] + hi_prod[:, ROUTER_LANES:] + lo_prod
    logits = lg_tok.T[0:ROUTER_ROWS, :] + br_ref[...]
    NG, EPG = MOE_GROUPS, MOE_EXPERTS_PER_GROUP
    lg = logits[0:NG, :]
    g_iota = lax.broadcasted_iota(I32, lg.shape, 0)
    lg_max = jnp.max(lg, axis=0, keepdims=True)
    grp = jnp.min(jnp.where(lg == lg_max, g_iota, NG), axis=0, keepdims=True)
    p_grp = 1.0 / jnp.sum(jnp.exp(lg - lg_max), axis=0, keepdims=True)
    le = logits[NG:NG + EPG, :]
    for g in range(1, NG):
        le = jnp.where(grp == g, logits[NG + g * EPG:NG + (g + 1) * EPG, :], le)
    e_iota = lax.broadcasted_iota(I32, le.shape, 0)
    v1 = jnp.max(le, axis=0, keepdims=True)
    i1 = jnp.min(jnp.where(le == v1, e_iota, EPG), axis=0, keepdims=True)
    le2 = jnp.where(e_iota == i1, -jnp.inf, le)
    v2 = jnp.max(le2, axis=0, keepdims=True)
    i2 = jnp.min(jnp.where(le2 == v2, e_iota, EPG), axis=0, keepdims=True)
    e21 = jnp.exp(v2 - v1)
    w1 = p_grp / (1.0 + e21)
    w2 = p_grp * e21 / (1.0 + e21)
    eid1 = grp * EPG + i1
    eid2 = grp * EPG + i2

    x_iota = lax.broadcasted_iota(I32, (MOE_EXPERTS, tm), 0)
    hot1 = x_iota == eid1
    hot2 = x_iota == eid2
    hot = jnp.logical_or(hot1, hot2).astype(F32)
    before = (lax.broadcasted_iota(I32, (tm, tm), 0) < lax.broadcasted_iota(I32, (tm, tm), 1))
    prior = jnp.dot(hot.astype(BF16), before.astype(BF16), preferred_element_type=F32)
    rank1 = jnp.sum(jnp.where(hot1, prior, 0.0), axis=0, keepdims=True)
    rank2 = jnp.sum(jnp.where(hot2, prior, 0.0), axis=0, keepdims=True)
    cnt_ref[...] = jnp.sum(hot, axis=1, keepdims=True).astype(I32)

    zi = jnp.zeros((4, tm), I32)
    ri_ref[...] = jnp.concatenate([eid1, eid2, rank1.astype(I32), rank2.astype(I32), zi], axis=0)
    rw_ref[...] = jnp.concatenate([w1, w2, jnp.zeros((6, tm), F32)], axis=0)


def _merge_route(x2, ya, ys, gs, mod, w_glu, w_pa, w_ps, w_out, norm2_g, w_rg, b_rg, w_re, b_re, seq):
    T, D = x2.shape
    tm = TOKEN_TILE
    per_b = seq // tm
    wr = jnp.concatenate([w_rg, jnp.transpose(w_re, (1, 0, 2)).reshape(D, MOE_EXPERTS),
                          jnp.zeros((D, ROUTER_LANES - MOE_GROUPS - MOE_EXPERTS), F32)], axis=1).astype(F32)
    wr_hi = wr.astype(BF16)
    wr_lo = (wr - wr_hi.astype(F32)).astype(BF16)
    br = jnp.concatenate([b_rg, b_re.reshape(-1),
                          jnp.zeros((ROUTER_ROWS - MOE_GROUPS - MOE_EXPERTS,), F32)]).reshape(ROUTER_ROWS, 1)
    row = lambda i: (i, 0)
    col = lambda i: (0, i)
    full = lambda i: (0, 0)
    return pl.pallas_call(
        _merge_kernel,
        grid=(T // tm,),
        in_specs=[pl.BlockSpec((tm, D), row),
                  pl.BlockSpec((tm, ATTN_WIDTH), row),
                  _chunk_major_spec(tm, per_b),
                  pl.BlockSpec((tm, 2 * D), row),
                  pl.BlockSpec((None, 6, D), lambda i: (i // per_b, 0, 0)),
                  pl.BlockSpec(w_glu.shape, full),
                  pl.BlockSpec(w_pa.shape, full),
                  pl.BlockSpec(w_ps.shape, full),
                  pl.BlockSpec(w_out.shape, full),
                  pl.BlockSpec((1, D), full),
                  pl.BlockSpec((D, 2 * ROUTER_LANES), full),
                  pl.BlockSpec((ROUTER_ROWS, 1), full)],
        out_specs=[pl.BlockSpec((tm, D), row), pl.BlockSpec((tm, D), row),
                   pl.BlockSpec((8, tm), col), pl.BlockSpec((8, tm), col),
                   pl.BlockSpec((None, MOE_EXPERTS, 1), lambda i: (i, 0, 0))],
        out_shape=[jax.ShapeDtypeStruct((T, D), F32), jax.ShapeDtypeStruct((T, D), BF16),
                   jax.ShapeDtypeStruct((8, T), I32), jax.ShapeDtypeStruct((8, T), F32),
                   jax.ShapeDtypeStruct((T // tm, MOE_EXPERTS, 1), I32)],
        compiler_params=_params("parallel"),
        name="merge_route",
    )(x2, ya, ys, gs, mod, w_glu.astype(BF16), w_pa.astype(BF16), w_ps.astype(BF16), w_out.astype(BF16),
      norm2_g.reshape(1, D), jnp.concatenate([wr_hi, wr_lo], axis=1), br)


def _pack_pairs(x):
    W = x.shape[1] // 2
    lo = lax.bitcast_convert_type(x[:, :W], I32)
    hi = lax.bitcast_convert_type(x[:, W:], I32)
    return lax.shift_right_logical(lo, 16) | hi


def _unpack_pairs(w):
    lo = lax.bitcast_convert_type(lax.shift_left(w, 16), F32)
    hi = lax.bitcast_convert_type(w & jnp.int32(-65536), F32)
    return jnp.concatenate([lo.astype(BF16), hi.astype(BF16)], axis=1)


def _tile_positions(ri_ref, seg_ref):
    tm = ri_ref.shape[1]
    x_iota = lax.broadcasted_iota(I32, (MOE_EXPERTS, tm), 0)
    seg = seg_ref[...].astype(F32)
    pos = []
    for k in range(2):
        start = jnp.sum(jnp.where(x_iota == ri_ref[k:k + 1, :], seg, 0.0), axis=0, keepdims=True)
        pos.append(start + ri_ref[2 + k:3 + k, :].astype(F32))
    return pos


def _segment_copies(meta, tile, make_copy):
    chunks, tile_chunks = meta
    base = tile * TILE_CHUNKS
    n = tile_chunks[tile]

    def issue(k, priority):
        word = chunks[base + k]
        make_copy(word & (2 ** CHUNK_SLOT_BITS - 1), word >> CHUNK_SLOT_BITS, priority)

    def group(g, carry):
        for r in range(CHUNK_ISSUE_UNROLL):
            issue(g * CHUNK_ISSUE_UNROLL + r, r % 2)
        return carry

    def single(k, carry):
        issue(k, 0)
        return carry

    n_groups = n // CHUNK_ISSUE_UNROLL
    lax.fori_loop(0, n_groups, group, 0)
    lax.fori_loop(n_groups * CHUNK_ISSUE_UNROLL, n, single, 0)


def _wait_chunks(n, wait_chunk, wait_bulk):
    has_bulk = n >= MIN_TILE_CHUNKS

    @pl.when(has_bulk)
    def _():
        wait_bulk()

    def body(c, carry):
        wait_chunk()
        return carry
    lax.fori_loop(jnp.where(has_bulk, MIN_TILE_CHUNKS, 0), n, body, 0)


def _dispatch_kernel(chunks, tile_chunks, pad_row, pad_chunks, n_used,
                     h_ref, ri_ref, rw_ref, seg_ref, xs_ref, pw_ref, zbuf, zeros_scr, sem, pad_sem, tail_sem):
    i = pl.program_id(0)
    n_tiles = pl.num_programs(0)
    slot = i % 2
    tm = h_ref.shape[0]
    pos1, pos2 = _tile_positions(ri_ref, seg_ref)
    pw_ref[...] = jnp.concatenate([pos1, pos2, rw_ref[0:2, :], jnp.zeros((4, tm), F32)], axis=0)
    r_iota = lax.broadcasted_iota(I32, (TILE_SLOTS, tm), 0).astype(F32)
    onehot = jnp.logical_or(r_iota == pos1, r_iota == pos2).astype(BF16)
    packed = _pack_pairs(jnp.dot(onehot, h_ref[...], preferred_element_type=F32))
    zbuf[slot] = packed.reshape(zbuf.shape[1:])

    def chunk_copy(buf_slot, src, dst):
        return pltpu.make_async_copy(zbuf.at[buf_slot, src], xs_ref.at[dst], sem.at[buf_slot])

    _segment_copies((chunks, tile_chunks), i, lambda src, dst, prio: chunk_copy(slot, src, dst).start(prio))

    def wait_tile(tile, buf_slot):
        bulk = pltpu.make_async_copy(zbuf.at[buf_slot, pl.ds(0, MIN_TILE_CHUNKS)],
                                     xs_ref.at[pl.ds(0, MIN_TILE_CHUNKS)], sem.at[buf_slot])
        _wait_chunks(tile_chunks[tile], chunk_copy(buf_slot, 0, 0).wait, bulk.wait)

    @pl.when(i > 0)
    def _():
        wait_tile(i - 1, 1 - slot)

    @pl.when(i == n_tiles - 1)
    def _():
        zeros_scr[...] = jnp.zeros(zeros_scr.shape, I32)

        block_chunks = zeros_scr.shape[0]

        def pad_copy(dst):
            return pltpu.make_async_copy(zeros_scr.at[0], xs_ref.at[dst], pad_sem)

        def tail_copy(dst):
            return pltpu.make_async_copy(zeros_scr, xs_ref.at[pl.ds(dst, block_chunks)], tail_sem)

        def per_expert(e, total):
            def per_chunk(c, carry):
                pad_copy(pad_row[e] + c).start()
                return carry
            lax.fori_loop(0, pad_chunks[e], per_chunk, 0)
            return total + pad_chunks[e]

        n_pad_copies = lax.fori_loop(0, MOE_EXPERTS, per_expert, 0)
        n_blocks = xs_ref.shape[0] // block_chunks

        def tail_start(b, carry):
            tail_copy(b * block_chunks).start()
            return carry
        lax.fori_loop(n_used[0], n_blocks, tail_start, 0)
        wait_tile(i, slot)

        def wait_pad(c, carry):
            pad_copy(0).wait()
            return carry
        lax.fori_loop(0, n_pad_copies, wait_pad, 0)

        def wait_tail(b, carry):
            tail_copy(0).wait()
            return carry
        lax.fori_loop(n_used[0], n_blocks, wait_tail, 0)


def _dispatch(meta, h2, ri, rw, seg_start, n_rows):
    T, D = h2.shape
    tm = TOKEN_TILE
    col = lambda i, *_: (0, i)
    return pl.pallas_call(
        _dispatch_kernel,
        grid_spec=pltpu.PrefetchScalarGridSpec(
            num_scalar_prefetch=5,
            grid=(T // tm,),
            in_specs=[pl.BlockSpec((tm, D), lambda i, *_: (i, 0)),
                      pl.BlockSpec((8, tm), col),
                      pl.BlockSpec((8, tm), col),
                      pl.BlockSpec((None, MOE_EXPERTS, 1), lambda i, *_: (i, 0, 0))],
            out_specs=[pl.BlockSpec(memory_space=pl.ANY), pl.BlockSpec((8, tm), col)],
            scratch_shapes=[pltpu.VMEM((2, TILE_CHUNKS, SEG_ALIGN, D // 2), I32),
                            pltpu.VMEM((MOE_BLOCK // SEG_ALIGN, SEG_ALIGN, D // 2), I32),
                            pltpu.SemaphoreType.DMA((2,)), pltpu.SemaphoreType.DMA(()),
                            pltpu.SemaphoreType.DMA(())]),
        out_shape=[jax.ShapeDtypeStruct((n_rows // SEG_ALIGN, SEG_ALIGN, D // 2), I32),
                   jax.ShapeDtypeStruct((8, T), F32)],
        compiler_params=_params("arbitrary"),
        name="moe_dispatch",
    )(*meta, h2, ri, rw, seg_start)


def _expert_kernel(be_ref, nb_ref, x_ref, wi_ref, wo_ref, o_ref, wi_bf, wo_bf):
    i = pl.program_id(0)
    F = wo_ref.shape[0]
    in_use = i < nb_ref[0]

    @pl.when(jnp.logical_and(in_use, jnp.logical_or(i == 0, be_ref[i] != be_ref[jnp.maximum(i - 1, 0)])))
    def _():
        wi_bf[...] = wi_ref[...].astype(BF16)
        wo_bf[...] = wo_ref[...].astype(BF16)

    @pl.when(in_use)
    def _():
        rows, half_d = o_ref.shape[0] * o_ref.shape[1], o_ref.shape[2]
        hid = jnp.dot(_unpack_pairs(x_ref[...].reshape(rows, half_d)), wi_bf[...], preferred_element_type=F32)
        a = hid[:, :F]
        act = a * jax.nn.sigmoid(a) * hid[:, F:]
        y = jnp.dot(act.astype(BF16), wo_bf[...], preferred_element_type=F32)
        o_ref[...] = _pack_pairs(y.astype(BF16).astype(F32)).reshape(o_ref.shape)

    @pl.when(jnp.logical_not(in_use))
    def _():
        o_ref[...] = jnp.zeros(o_ref.shape, I32)


def _experts(block_e, n_used, xs, w_e_in, w_e_out):
    n_chunks, _, half_d = xs.shape
    D = 2 * half_d
    F = w_e_out.shape[1]
    block_chunks = MOE_BLOCK // SEG_ALIGN
    blk = lambda i, be, nb: jnp.maximum(jnp.minimum(i, nb[0] - 1), 0)
    return pl.pallas_call(
        _expert_kernel,
        grid_spec=pltpu.PrefetchScalarGridSpec(
            num_scalar_prefetch=2,
            grid=(n_chunks // block_chunks,),
            in_specs=[pl.BlockSpec((block_chunks, SEG_ALIGN, half_d), lambda i, be, nb: (blk(i, be, nb), 0, 0)),
                      pl.BlockSpec((None, D, 2 * F), lambda i, be, nb: (be[blk(i, be, nb)], 0, 0)),
                      pl.BlockSpec((None, F, D), lambda i, be, nb: (be[blk(i, be, nb)], 0, 0))],
            out_specs=pl.BlockSpec((block_chunks, SEG_ALIGN, half_d), lambda i, be, nb: (i, 0, 0)),
            scratch_shapes=[pltpu.VMEM((D, 2 * F), BF16), pltpu.VMEM((F, D), BF16)]),
        out_shape=jax.ShapeDtypeStruct(xs.shape, I32),
        compiler_params=_params("arbitrary"),
        name="experts",
    )(block_e, n_used, xs, w_e_in, w_e_out)


def _final_kernel(chunks, tile_chunks, x1_ref, pw_ref, mod_ref, g_ref, ys_ref, o_ref, ybuf, sem):
    i = pl.program_id(0)
    n_tiles = pl.num_programs(0)
    slot = i % 2
    tm = x1_ref.shape[0]

    def chunk_copy(buf_slot, src, dst):
        return pltpu.make_async_copy(ys_ref.at[dst], ybuf.at[buf_slot, src], sem.at[buf_slot])

    def fetch(tile, buf_slot):
        _segment_copies((chunks, tile_chunks), tile,
                        lambda src, dst, prio: chunk_copy(buf_slot, src, dst).start(prio))

    @pl.when(i == 0)
    def _():
        ybuf[...] = jnp.zeros(ybuf.shape, I32)
        fetch(0, 0)

    @pl.when(i + 1 < n_tiles)
    def _():
        fetch(i + 1, 1 - slot)

    bulk = pltpu.make_async_copy(ys_ref.at[pl.ds(0, MIN_TILE_CHUNKS)],
                                 ybuf.at[slot, pl.ds(0, MIN_TILE_CHUNKS)], sem.at[slot])
    _wait_chunks(tile_chunks[i], chunk_copy(slot, 0, 0).wait, bulk.wait)

    s_iota = lax.broadcasted_iota(I32, (tm, TILE_SLOTS), 1).astype(F32)
    comb = (jnp.where(s_iota == pw_ref[:, 0:1], pw_ref[:, 2:3], 0.0)
            + jnp.where(s_iota == pw_ref[:, 1:2], pw_ref[:, 3:4], 0.0))
    sorted_rows = _unpack_pairs(ybuf[slot].reshape(TILE_SLOTS, ybuf.shape[3]))
    moe = jnp.dot(comb.astype(BF16), sorted_rows, preferred_element_type=F32)
    x2 = x1_ref[...] + mod_ref[5:6, :] * moe
    o_ref[...] = _rms(x2, NORM_EPS) * g_ref[...]


def _final(meta, x1, pw_tok, mod, final_g, ys, seq):
    T, D = x1.shape
    tm = TOKEN_TILE
    per_b = seq // tm
    row = lambda i, *_: (i, 0)
    return pl.pallas_call(
        _final_kernel,
        grid_spec=pltpu.PrefetchScalarGridSpec(
            num_scalar_prefetch=2,
            grid=(T // tm,),
            in_specs=[pl.BlockSpec((tm, D), row),
                      pl.BlockSpec((tm, 8), row),
                      pl.BlockSpec((None, 6, D), lambda i, *_: (i // per_b, 0, 0)),
                      pl.BlockSpec((1, D), lambda i, *_: (0, 0)),
                      pl.BlockSpec(memory_space=pl.ANY)],
            out_specs=pl.BlockSpec((tm, D), row),
            scratch_shapes=[pltpu.VMEM((2, TILE_CHUNKS, SEG_ALIGN, D // 2), I32), pltpu.SemaphoreType.DMA((2,))]),
        out_shape=jax.ShapeDtypeStruct((T, D), F32),
        compiler_params=_params("arbitrary"),
        name="final",
    )(*meta, x1, pw_tok, mod, final_g.reshape(1, D), ys)


def _round_up(x, m):
    return (x + m - 1) // m * m


def _moe_layout(tile_counts):
    n_tiles = tile_counts.shape[0]
    seg = _round_up(tile_counts, SEG_ALIGN)
    seg_start = jnp.cumsum(seg, axis=1) - seg
    tile_base = jnp.cumsum(seg, axis=0) - seg
    used = jnp.sum(seg, axis=0)
    region = _round_up(used, MOE_BLOCK)
    region_end = jnp.cumsum(region)
    region_start = region_end - region
    dst_row = region_start[None, :] + tile_base
    n_chunk = seg // SEG_ALIGN
    n_assign = 2 * n_tiles * TOKEN_TILE
    n_rows = _round_up(n_assign + n_tiles * MOE_EXPERTS * (SEG_ALIGN - 1) + MOE_EXPERTS * (MOE_BLOCK - 1), MOE_BLOCK)
    block_start = jnp.arange(n_rows // MOE_BLOCK, dtype=I32) * MOE_BLOCK
    block_e = jnp.minimum(jnp.sum(block_start[:, None] >= region_end[None, :], axis=1), MOE_EXPERTS - 1)
    i32 = lambda a: a.reshape(-1).astype(I32)
    chunk_end = jnp.cumsum(n_chunk, axis=1)
    chunk_start = chunk_end - n_chunk
    k = jnp.arange(TILE_CHUNKS)[None, :, None]
    mine = (chunk_start[:, None, :] <= k) & (k < chunk_end[:, None, :])
    pick = lambda a: jnp.sum(jnp.where(mine, (a // SEG_ALIGN - chunk_start)[:, None, :] + k, 0), axis=2)
    word = pick(seg_start) | (pick(dst_row) << CHUNK_SLOT_BITS)
    meta = (i32(word), i32(chunk_end[:, -1]))
    pad = (i32((region_start + used) // SEG_ALIGN), i32((region - used) // SEG_ALIGN))
    n_used = (region_end[-1:] // MOE_BLOCK).astype(I32)
    return meta, pad, seg_start.astype(I32)[:, :, None], block_e.astype(I32), n_used, n_rows


def kernel(x, c, w_ada, b_ada, norm1_g, w_in, rel_bias, lambda_q1, lambda_k1, lambda_q2, lambda_k2, subln_g, ssm_lambda_re, ssm_lambda_im, ssm_log_step, ssm_b_re, ssm_b_im, ssm_c_re, ssm_c_im, ssm_d, w_glu, w_proj_attn, w_proj_ssm, w_out, norm2_g, w_router_group, b_router_group, w_router_expert, b_router_expert, w_expert_in, w_expert_out, final_g):
    B, S, D = x.shape
    T = B * S
    x2 = x.reshape(T, D)
    mod = _ada_mod(c, w_ada[0], b_ada[0]).reshape(B, 6, D)
    q, k, v, u, gs = _in_proj(x2, mod, norm1_g[0], w_in[0], S)
    lam_vecs = jnp.stack([lambda_q1[0], lambda_k1[0], lambda_q2[0], lambda_k2[0]]).astype(F32)
    y_attn = _diff_attn(q, k, v, rel_bias, lam_vecs, subln_g[0], B, S)
    tables = _s5_tables(ssm_lambda_re[0], ssm_lambda_im[0], ssm_log_step[0], ssm_b_re[0], ssm_b_im[0],
                        ssm_c_re[0], ssm_c_im[0], ssm_d[0])
    y_s5 = _s5_branch(u.reshape(-1, SSM_WIDTH), tables, B, S).reshape(u.shape)
    x1, h2, ri, rw, tile_counts = _merge_route(
        x2, y_attn, y_s5, gs, mod, w_glu[0], w_proj_attn[0], w_proj_ssm[0], w_out[0], norm2_g[0],
        w_router_group[0], b_router_group[0], w_router_expert[0], b_router_expert[0], S)
    meta, pad, seg_start, block_e, n_used, n_rows = _moe_layout(tile_counts[:, :, 0])
    xs, pw = _dispatch(meta + pad + (n_used,), h2, ri, rw, seg_start, n_rows)
    ys = _experts(block_e, n_used, xs, w_expert_in[0], w_expert_out[0])
    out = _final(meta, x1, pw.T, mod, final_g, ys, S)
    return out.reshape(B, S, D)
```
